```python
import math
import jax, jax.numpy as jnp
from jax import lax
import numpy as np

D_MODEL = 1024
BATCH = 8
SEQ = 2048
DEPTH = 2
DEC_BATCH = 128
DEC_SEQ = 4
PAST_LEN = 8192
PAGE_SIZE = 128

N_A_LAYERS = DEPTH // 2
N_B_LAYERS = DEPTH - N_A_LAYERS

GLA_HEADS = 4
GLA_DV = 192
GLA_DK = GLA_DV // 2
GLA_GATE_RANK = 16
GLA_GATE_NORM = 16.0
GLA_CHUNK = 64
GLA_QK = GLA_HEADS * GLA_DK
GLA_V = GLA_HEADS * GLA_DV

HEAD_DIM = 64
SWA_Q_HEADS = 12
SWA_KV_HEADS = 3
WINDOW = 128
ROT_DIM = HEAD_DIM // 4
ROPE_THETA = 500000.0
SWA_Q = SWA_Q_HEADS * HEAD_DIM
KV_SHARED = 2 * SWA_KV_HEADS * HEAD_DIM

MEM_TOKENS = 256
MEM_HEADS = 4
MEM_HEAD_DIM = 64
MEM_Q = MEM_HEADS * MEM_HEAD_DIM

FFN_DIM = 2816
EPS = 1e-6

A_IN = 2 * GLA_QK + 2 * GLA_V + GLA_GATE_RANK + MEM_Q
A_SPLITS = [GLA_QK, 2 * GLA_QK, 2 * GLA_QK + GLA_V, 2 * GLA_QK + 2 * GLA_V,
            2 * GLA_QK + 2 * GLA_V + GLA_GATE_RANK]
A_MIX = GLA_V + MEM_Q
B_IN = SWA_Q + MEM_Q
B_MIX = SWA_Q + MEM_Q

kernel_name = 'yoco_gla_swa_sink_memxattn_step'


def rmsnorm(x, g):
    xf = x.astype(jnp.float32)
    y = xf * lax.rsqrt(jnp.mean(xf * xf, axis=-1, keepdims=True) + EPS)
    return (y * g.astype(jnp.float32)).astype(x.dtype)


def swiglu_half(x, g, w_gu, w_down):
    h = rmsnorm(x, g)
    gate, up = jnp.split(h @ w_gu, 2, axis=-1)
    return 0.5 * ((jax.nn.silu(gate) * up) @ w_down)


def rope_partial(x, pos):
    half = ROT_DIM // 2
    inv_freq = jnp.exp(-math.log(ROPE_THETA) * jnp.arange(0, ROT_DIM, 2, dtype=jnp.float32) / ROT_DIM)
    ang = pos[:, None] * inv_freq[None, :]
    cos = jnp.cos(ang)[None, :, None, :].astype(x.dtype)
    sin = jnp.sin(ang)[None, :, None, :].astype(x.dtype)
    x1 = x[..., :half]
    x2 = x[..., half:ROT_DIM]
    return jnp.concatenate([x1 * cos - x2 * sin, x2 * cos + x1 * sin, x[..., ROT_DIM:]], axis=-1)


def mem_kv(mem, g, w):
    B, M, _ = mem.shape
    kv = (rmsnorm(mem, g) @ w).reshape(B, M, 2, MEM_HEADS, MEM_HEAD_DIM)
    return kv[:, :, 0], kv[:, :, 1]


def mem_attention(q, mk, mv):
    s = jnp.einsum('blhd,bmhd->bhlm', q, mk).astype(jnp.float32) * MEM_HEAD_DIM ** -0.5
    p = jax.nn.softmax(s, axis=-1).astype(mv.dtype)
    return jnp.einsum('bhlm,bmhd->blhd', p, mv)


def sink_softmax(s, sinks):
    sink = jnp.broadcast_to(sinks.astype(jnp.float32)[:, :, None, None], s.shape[:-1] + (1,))
    p = jax.nn.softmax(jnp.concatenate([s, sink], axis=-1), axis=-1)
    return p[..., :-1]


def gla_chunked(q, k, v, log_a):
    B, L, H, DK = q.shape
    DV = v.shape[-1]
    C = GLA_CHUNK
    nc = L // C

    def to_chunks(t):
        return t.reshape(B, nc, C, H, t.shape[-1]).transpose(1, 0, 3, 2, 4).astype(jnp.float32)

    qc, kc, vc, gc = to_chunks(q), to_chunks(k), to_chunks(v), to_chunks(log_a)
    causal = jnp.tril(jnp.ones((C, C), dtype=bool))

    def step(S, inp):
        qi, ki, vi, gi = inp
        b = jnp.cumsum(gi, axis=2)
        diff = b[:, :, :, None, :] - b[:, :, None, :, :]
        decay = jnp.exp(jnp.where(causal[:, :, None], diff, -jnp.inf))
        A = jnp.einsum('bhid,bhjd,bhijd->bhij', qi, ki, decay)
        o = jnp.einsum('bhij,bhjv->bhiv', A, vi) + jnp.einsum('bhid,bhdv->bhiv', qi * jnp.exp(b), S)
        b_last = b[:, :, -1:, :]
        S = jnp.exp(b_last[:, :, 0, :])[..., None] * S + jnp.einsum(
            'bhjd,bhjv->bhdv', ki * jnp.exp(b_last - b), vi)
        return S, o

    S0 = jnp.zeros((B, H, DK, DV), jnp.float32)
    S, o = lax.scan(step, S0, (qc, kc, vc, gc))
    return o.transpose(1, 0, 3, 2, 4).reshape(B, L, H, DV), S


def gla_recurrent(q, k, v, log_a, S0):
    def step(S, inp):
        qt, kt, vt, gt = inp
        S = jnp.exp(gt)[..., None] * S + kt[..., :, None] * vt[..., None, :]
        return S, jnp.einsum('bhk,bhkv->bhv', qt, S)

    xs = tuple(t.astype(jnp.float32).transpose(1, 0, 2, 3) for t in (q, k, v, log_a))
    S, o = lax.scan(step, S0.astype(jnp.float32), xs)
    return o.transpose(1, 0, 2, 3), S


def swa_prompt(q, k, v, sinks):
    B, L, HQ, HD = q.shape
    KVH = k.shape[2]
    G = HQ // KVH
    BLK = WINDOW
    nb = L // BLK
    qb = q.reshape(B, nb, BLK, KVH, G, HD)

    def band(t):
        tb = t.reshape(B, nb, BLK, KVH, HD)
        prev = jnp.concatenate([jnp.zeros_like(tb[:, :1]), tb[:, :-1]], axis=1)
        return jnp.concatenate([prev, tb], axis=2)

    kb, vb = band(k), band(v)
    s = jnp.einsum('bnqkgd,bnskd->bnkgqs', qb, kb).astype(jnp.float32) * HD ** -0.5
    blk = jnp.arange(nb)[:, None, None]
    qpos = blk * BLK + jnp.arange(BLK)[None, :, None]
    kpos = (blk - 1) * BLK + jnp.arange(2 * BLK)[None, None, :]
    d = qpos - kpos
    mask = (d >= 0) & (d < WINDOW) & (kpos >= 0)
    s = jnp.where(mask[None, :, None, None], s, -jnp.inf)
    p = sink_softmax(s, sinks.reshape(KVH, G))
    o = jnp.einsum('bnkgqs,bnskd->bnqkgd', p.astype(vb.dtype), vb)
    return o.reshape(B, L, HQ * HD)


def swa_sample(q, k_all, v_all, sinks):
    B, Lq, HQ, HD = q.shape
    KVH = k_all.shape[2]
    G = HQ // KVH
    W = k_all.shape[1] - Lq
    qg = q.reshape(B, Lq, KVH, G, HD)
    s = jnp.einsum('bqkgd,bskd->bkgqs', qg, k_all).astype(jnp.float32) * HD ** -0.5
    d = (W + jnp.arange(Lq))[:, None] - jnp.arange(W + Lq)[None, :]
    mask = (d >= 0) & (d < WINDOW)
    s = jnp.where(mask, s, -jnp.inf)
    p = sink_softmax(s, sinks.reshape(KVH, G))
    o = jnp.einsum('bkgqs,bskd->bqkgd', p.astype(v_all.dtype), v_all)
    return o.reshape(B, Lq, HQ * HD)


def trunk(x, pos, mem_k, mem_v, gla_s0, buf_k, buf_v, w, decode):
    B, L, _ = x.shape
    gla_states = []
    for l in range(DEPTH):
        if l == N_A_LAYERS:
            kv = (rmsnorm(x, w['kv_norm']) @ w['w_kv']).reshape(B, L, 2, SWA_KV_HEADS, HEAD_DIM)
            k_sh = rope_partial(kv[:, :, 0], pos)
            v_sh = kv[:, :, 1]
            if decode:
                k_att = jnp.concatenate([buf_k.astype(k_sh.dtype), k_sh], axis=1)
                v_att = jnp.concatenate([buf_v.astype(v_sh.dtype), v_sh], axis=1)
                n_keep = buf_k.shape[1]
            else:
                k_att, v_att = k_sh, v_sh
                n_keep = min(WINDOW, L)
            swa_k_new = k_att[:, -n_keep:]
            swa_v_new = v_att[:, -n_keep:]

        x = x + swiglu_half(x, w['ffn1_norm'][l], w['ffn1_w_gu'][l], w['ffn1_w_down'][l])
        h = rmsnorm(x, w['mix_norm'][l])
        if l < N_A_LAYERS:
            a = l
            q, k, v, r, g_lr, mq = jnp.split(h @ w['a_w_in'][a], A_SPLITS, axis=-1)
            q = q.reshape(B, L, GLA_HEADS, GLA_DK) * GLA_DK ** -0.5
            k = k.reshape(B, L, GLA_HEADS, GLA_DK)
            v = v.reshape(B, L, GLA_HEADS, GLA_DV)
            log_a = jax.nn.log_sigmoid((g_lr @ w['a_w_gate'][a] + w['a_b_gate'][a]).astype(jnp.float32))
            log_a = (log_a / GLA_GATE_NORM).reshape(B, L, GLA_HEADS, GLA_DK)
            if decode:
                o, S = gla_recurrent(q, k, v, log_a, gla_s0[a])
            else:
                o, S = gla_chunked(q, k, v, log_a)
            gla_states.append(S.astype(x.dtype))
            o = rmsnorm(o.astype(x.dtype), w['a_out_norm'][a]) * jax.nn.silu(r.reshape(B, L, GLA_HEADS, GLA_DV))
            tok = o.reshape(B, L, GLA_V)
            w_out = w['a_w_out'][a]
        else:
            bi = l - N_A_LAYERS
            qs, mq = jnp.split(h @ w['b_w_in'][bi], [SWA_Q], axis=-1)
            qs = rope_partial(qs.reshape(B, L, SWA_Q_HEADS, HEAD_DIM), pos)
            if decode:
                tok = swa_sample(qs, k_att, v_att, w['b_sinks'][bi])
            else:
                tok = swa_prompt(qs, k_sh, v_sh, w['b_sinks'][bi])
            w_out = w['b_w_out'][bi]
        mo = mem_attention(mq.reshape(B, L, MEM_HEADS, MEM_HEAD_DIM), mem_k[l], mem_v[l]).reshape(B, L, MEM_Q)
        x = x + jnp.concatenate([tok, mo], axis=-1) @ w_out
        x = x + swiglu_half(x, w['ffn2_norm'][l], w['ffn2_w_gu'][l], w['ffn2_w_down'][l])
    y = rmsnorm(x, w['final_norm'])
    return y, jnp.stack(gla_states), swa_k_new, swa_v_new


def setup_inputs(seed: int = 0) -> dict:
    key = jax.random.key(seed)
    ks = iter(jax.random.split(key, 48))

    def nrm(shape, scale=1.0):
        return jax.random.normal(next(ks), shape, jnp.float32) * scale

    def gain(shape):
        return 1.0 + 0.05 * nrm(shape)

    w_buf = min(WINDOW, PAST_LEN)
    return {
        'x_prompt': nrm((BATCH, SEQ, D_MODEL)),
        'x_sample': nrm((DEC_BATCH, DEC_SEQ, D_MODEL)),
        'state_gla': nrm((N_A_LAYERS, DEC_BATCH, GLA_HEADS, GLA_DK, GLA_DV)),
        'cache_swa_k': nrm((DEC_BATCH, w_buf, SWA_KV_HEADS, HEAD_DIM)),
        'cache_swa_v': nrm((DEC_BATCH, w_buf, SWA_KV_HEADS, HEAD_DIM)),
        'cache_mem_k': nrm((DEPTH, DEC_BATCH, MEM_TOKENS, MEM_HEADS, MEM_HEAD_DIM)),
        'cache_mem_v': nrm((DEPTH, DEC_BATCH, MEM_TOKENS, MEM_HEADS, MEM_HEAD_DIM)),
        'mem_prompt': nrm((BATCH, MEM_TOKENS, D_MODEL)),
        'ffn1_norm': gain((DEPTH, D_MODEL)),
        'ffn1_w_gu': nrm((DEPTH, D_MODEL, 2 * FFN_DIM), D_MODEL ** -0.5),
        'ffn1_w_down': nrm((DEPTH, FFN_DIM, D_MODEL), FFN_DIM ** -0.5),
        'mix_norm': gain((DEPTH, D_MODEL)),
        'ffn2_norm': gain((DEPTH, D_MODEL)),
        'ffn2_w_gu': nrm((DEPTH, D_MODEL, 2 * FFN_DIM), D_MODEL ** -0.5),
        'ffn2_w_down': nrm((DEPTH, FFN_DIM, D_MODEL), FFN_DIM ** -0.5),
        'mem_norm': gain((DEPTH, D_MODEL)),
        'mem_w_kv': nrm((DEPTH, D_MODEL, 2 * MEM_Q), D_MODEL ** -0.5),
        'a_w_in': nrm((N_A_LAYERS, D_MODEL, A_IN), D_MODEL ** -0.5),
        'a_w_gate': nrm((N_A_LAYERS, GLA_GATE_RANK, GLA_QK), GLA_GATE_RANK ** -0.5),
        'a_b_gate': nrm((N_A_LAYERS, GLA_QK), 0.1),
        'a_out_norm': gain((N_A_LAYERS, GLA_DV)),
        'a_w_out': nrm((N_A_LAYERS, A_MIX, D_MODEL), A_MIX ** -0.5),
        'kv_norm': gain((D_MODEL,)),
        'w_kv': nrm((D_MODEL, KV_SHARED), D_MODEL ** -0.5),
        'b_w_in': nrm((N_B_LAYERS, D_MODEL, B_IN), D_MODEL ** -0.5),
        'b_sinks': nrm((N_B_LAYERS, SWA_Q_HEADS), 0.5),
        'b_w_out': nrm((N_B_LAYERS, B_MIX, D_MODEL), B_MIX ** -0.5),
        'final_norm': gain((D_MODEL,)),
    }


def reference(x_prompt, x_sample, state_gla, cache_swa_k, cache_swa_v, cache_mem_k, cache_mem_v,
              mem_prompt, ffn1_norm, ffn1_w_gu, ffn1_w_down, mix_norm, ffn2_norm, ffn2_w_gu,
              ffn2_w_down, mem_norm, mem_w_kv, a_w_in, a_w_gate, a_b_gate, a_out_norm, a_w_out,
              kv_norm, w_kv, b_w_in, b_sinks, b_w_out, final_norm):
    w = dict(ffn1_norm=ffn1_norm, ffn1_w_gu=ffn1_w_gu, ffn1_w_down=ffn1_w_down, mix_norm=mix_norm,
             ffn2_norm=ffn2_norm, ffn2_w_gu=ffn2_w_gu, ffn2_w_down=ffn2_w_down,
             a_w_in=a_w_in, a_w_gate=a_w_gate, a_b_gate=a_b_gate, a_out_norm=a_out_norm,
             a_w_out=a_w_out, kv_norm=kv_norm, w_kv=w_kv, b_w_in=b_w_in, b_sinks=b_sinks,
             b_w_out=b_w_out, final_norm=final_norm)

    mks, mvs = [], []
    for l in range(DEPTH):
        mk, mv = mem_kv(mem_prompt, mem_norm[l], mem_w_kv[l])
        mks.append(mk)
        mvs.append(mv)
    mem_k_prompt = jnp.stack(mks)
    mem_v_prompt = jnp.stack(mvs)

    pos_p = jnp.arange(x_prompt.shape[1], dtype=jnp.float32)
    y_prompt, gla_prompt, swa_k_prompt, swa_v_prompt = trunk(
        x_prompt, pos_p, mem_k_prompt, mem_v_prompt, None, None, None, w, False)

    pos_s = PAST_LEN + jnp.arange(x_sample.shape[1], dtype=jnp.float32)
    y_sample, gla_sample, swa_k_sample, swa_v_sample = trunk(
        x_sample, pos_s, cache_mem_k, cache_mem_v, state_gla, cache_swa_k, cache_swa_v, w, True)

    return (y_prompt, y_sample, gla_prompt, gla_sample, swa_k_prompt, swa_v_prompt,
            swa_k_sample, swa_v_sample, mem_k_prompt, mem_v_prompt)
```

```python
import functools
import math

import jax
import jax.numpy as jnp
from jax import lax
from jax.experimental import pallas as pl
from jax.experimental.pallas import tpu as pltpu

F32 = jnp.float32
BF = jnp.bfloat16

D_MODEL = 1024
FFN_DIM = 2816
EPS = 1e-6

GLA_HEADS = 4
GLA_DK = 96
GLA_DV = 192
GLA_DKP = 128
GLA_DVP = 256
GLA_RANK = 16
GLA_RANKP = 128
GLA_GATE_NORM = 16.0
GLA_CHUNK = 256

HEAD_DIM = 64
HEAD_DIMP = 128
SWA_Q_HEADS = 12
SWA_KV_HEADS = 3
SWA_GROUP = SWA_Q_HEADS // SWA_KV_HEADS
WINDOW = 128
ROT_DIM = 16
ROPE_THETA = 500000.0
PAST_LEN = 8192

MEM_TOKENS = 256
MEM_HEADS = 4
MEM_HEAD_DIM = 64
MEM_Q = MEM_HEADS * MEM_HEAD_DIM

FFN_TF = 256
FFN_CHUNKS = FFN_DIM // FFN_TF

VMEM_LIMIT = 56 * 1024 * 1024


def _params(*sem):
    return pltpu.CompilerParams(dimension_semantics=sem, vmem_limit_bytes=VMEM_LIMIT)


def _resident(shape):
    nd = len(shape)
    return pl.BlockSpec(shape, lambda *_: (0,) * nd, pipeline_mode=pl.Buffered(1))


def _rows(tm, width):
    return pl.BlockSpec((tm, width), lambda i: (i, 0))


def _rms(x, g):
    ms = jnp.mean(x * x, axis=-1, keepdims=True)
    return x * lax.rsqrt(ms + EPS) * g


def _silu(x):
    return x * (1.0 / (1.0 + jnp.exp(-x)))


def _dot(a, b):
    return jnp.dot(a, b, preferred_element_type=F32)


def _dot_nt(a, b):
    return lax.dot_general(a, b, (((1,), (1,)), ((), ())), preferred_element_type=F32)


def _dot_tn(a, b):
    return lax.dot_general(a, b, (((0,), (0,)), ((), ())), preferred_element_type=F32)


def _ffn_kernel(x_ref, g_ref, wgu_ref, wd_ref, *rest, final_norm):
    if final_norm:
        fg_ref, o_ref = rest
    else:
        (o_ref,) = rest
    x = x_ref[...]
    h = _rms(x, g_ref[...]).astype(BF)
    acc = jnp.zeros(x.shape, F32)
    for c in range(FFN_CHUNKS):
        gu = _dot(h, wgu_ref[c])
        a = (_silu(gu[:, :FFN_TF]) * gu[:, FFN_TF:]).astype(BF)
        acc = acc + _dot(a, wd_ref[c])
    y = x + 0.5 * acc
    if final_norm:
        y = _rms(y, fg_ref[...])
    o_ref[...] = y


def _ffn(x, g, wgu3, wd3, final_g=None, *, tm):
    m = x.shape[0]
    final_norm = final_g is not None
    in_specs = [_rows(tm, D_MODEL), _resident((1, D_MODEL)),
                _resident(wgu3.shape), _resident(wd3.shape)]
    args = [x, g, wgu3, wd3]
    if final_norm:
        in_specs.append(_resident((1, D_MODEL)))
        args.append(final_g)
    return pl.pallas_call(
        functools.partial(_ffn_kernel, final_norm=final_norm),
        grid=(m // tm,),
        in_specs=in_specs,
        out_specs=_rows(tm, D_MODEL),
        out_shape=jax.ShapeDtypeStruct((m, D_MODEL), F32),
        compiler_params=_params("parallel"),
        name="ffn_final" if final_norm else "ffn",
    )(*args)


_A_Q = 0
_A_K = _A_Q + GLA_HEADS * GLA_DKP
_A_V = _A_K + GLA_HEADS * GLA_DKP
_A_R = _A_V + GLA_HEADS * GLA_DVP
_A_G = _A_R + GLA_HEADS * GLA_DVP
_A_M = _A_G + GLA_RANKP
_A_END = _A_M + MEM_Q


def _proj_a_kernel(x_ref, g_ref, w_ref, wg_ref, bg_ref,
                   q_ref, k_ref, la_ref, v_ref, r_ref, mq_ref):
    h = _rms(x_ref[...], g_ref[...]).astype(BF)

    def mm(a, b):
        return _dot(h, w_ref[:, a:b])

    q_ref[...] = mm(_A_Q, _A_K) * (GLA_DK ** -0.5)
    k_ref[...] = mm(_A_K, _A_V)
    v_ref[...] = mm(_A_V, _A_R)
    r_ref[...] = mm(_A_R, _A_G)
    z = _dot(mm(_A_G, _A_M).astype(BF), wg_ref[...]) + bg_ref[...]
    la_ref[...] = (jnp.minimum(z, 0.0) - jnp.log1p(jnp.exp(-jnp.abs(z)))) * (1.0 / GLA_GATE_NORM)
    mq_ref[...] = mm(_A_M, _A_END)


def _proj_a(x, g, w, wg, bg, *, tm):
    m = x.shape[0]
    qk = GLA_HEADS * GLA_DKP
    vr = GLA_HEADS * GLA_DVP
    widths = (qk, qk, qk, vr, vr, MEM_Q)
    return pl.pallas_call(
        _proj_a_kernel,
        grid=(m // tm,),
        in_specs=[_rows(tm, D_MODEL), _resident((1, D_MODEL)), _resident(w.shape),
                  _resident(wg.shape), _resident(bg.shape)],
        out_specs=[_rows(tm, n) for n in widths],
        out_shape=[jax.ShapeDtypeStruct((m, n), F32) for n in widths],
        compiler_params=_params("parallel"),
        name="proj_a",
    )(x, g, w, wg, bg)


def _rope(x, c, s1, s2):
    return (x * c + pltpu.roll(x, HEAD_DIMP - ROT_DIM // 2, 1) * s1
            + pltpu.roll(x, ROT_DIM // 2, 1) * s2)


def _proj_b_kernel(x_ref, g_ref, w_ref, c_ref, s1_ref, s2_ref, q_ref, mq_ref):
    h = _rms(x_ref[...], g_ref[...]).astype(BF)
    c, s1, s2 = c_ref[...], s1_ref[...], s2_ref[...]
    for hh in range(SWA_Q_HEADS):
        q = _dot(h, w_ref[:, hh * HEAD_DIMP:(hh + 1) * HEAD_DIMP])
        q_ref[hh] = _rope(q, c, s1, s2)
    mq_ref[...] = _dot(h, w_ref[:, SWA_Q_HEADS * HEAD_DIMP:])


def _proj_b(x, g, w, tabs, *, tm):
    m = x.shape[0]
    tab_blocks = tabs[0].shape[0] // tm
    tab_spec = pl.BlockSpec((tm, HEAD_DIMP), lambda i: (i % tab_blocks, 0))
    return pl.pallas_call(
        _proj_b_kernel,
        grid=(m // tm,),
        in_specs=[_rows(tm, D_MODEL), _resident((1, D_MODEL)), _resident(w.shape),
                  tab_spec, tab_spec, tab_spec],
        out_specs=[pl.BlockSpec((SWA_Q_HEADS, tm, HEAD_DIMP), lambda i: (0, i, 0)),
                   _rows(tm, MEM_Q)],
        out_shape=[jax.ShapeDtypeStruct((SWA_Q_HEADS, m, HEAD_DIMP), F32),
                   jax.ShapeDtypeStruct((m, MEM_Q), F32)],
        compiler_params=_params("parallel"),
        name="proj_b",
    )(x, g, w, *tabs)


def _proj_kv_kernel(x_ref, g_ref, w_ref, c_ref, s1_ref, s2_ref, k_ref, v_ref):
    h = _rms(x_ref[...], g_ref[...]).astype(BF)
    c, s1, s2 = c_ref[...], s1_ref[...], s2_ref[...]
    kw = SWA_KV_HEADS * HEAD_DIMP
    for hh in range(SWA_KV_HEADS):
        sl = slice(hh * HEAD_DIMP, (hh + 1) * HEAD_DIMP)
        k_ref[:, sl] = _rope(_dot(h, w_ref[:, sl]), c, s1, s2)
    v_ref[...] = _dot(h, w_ref[:, kw:])


def _proj_kv(x, g, w, tabs, *, tm):
    m = x.shape[0]
    kw = SWA_KV_HEADS * HEAD_DIMP
    tab_blocks = tabs[0].shape[0] // tm
    tab_spec = pl.BlockSpec((tm, HEAD_DIMP), lambda i: (i % tab_blocks, 0))
    return pl.pallas_call(
        _proj_kv_kernel,
        grid=(m // tm,),
        in_specs=[_rows(tm, D_MODEL), _resident((1, D_MODEL)), _resident(w.shape),
                  tab_spec, tab_spec, tab_spec],
        out_specs=[_rows(tm, kw), _rows(tm, kw)],
        out_shape=[jax.ShapeDtypeStruct((m, kw), F32)] * 2,
        compiler_params=_params("parallel"),
        name="proj_kv",
    )(x, g, w, *tabs)


def _out_a_kernel(x_ref, tok_ref, mo_ref, w_ref, o_ref):
    nt = GLA_HEADS * GLA_DVP
    o_ref[...] = (x_ref[...] + _dot(tok_ref[...].astype(BF), w_ref[:nt, :])
                  + _dot(mo_ref[...].astype(BF), w_ref[nt:, :]))


def _out_a(x, tok, mo, w, *, tm):
    m = x.shape[0]
    return pl.pallas_call(
        _out_a_kernel,
        grid=(m // tm,),
        in_specs=[_rows(tm, D_MODEL), _rows(tm, GLA_HEADS * GLA_DVP), _rows(tm, MEM_Q),
                  _resident(w.shape)],
        out_specs=_rows(tm, D_MODEL),
        out_shape=jax.ShapeDtypeStruct((m, D_MODEL), F32),
        compiler_params=_params("parallel"),
        name="out_a",
    )(x, tok, mo, w)


def _out_b_kernel(x_ref, tok_ref, mo_ref, w_ref, o_ref):
    nt = SWA_Q_HEADS * HEAD_DIMP
    tok = jnp.concatenate([tok_ref[hh] for hh in range(SWA_Q_HEADS)], axis=1).astype(BF)
    o_ref[...] = (x_ref[...] + _dot(tok, w_ref[:nt, :])
                  + _dot(mo_ref[...].astype(BF), w_ref[nt:, :]))


def _out_b(x, tok, mo, w, *, tm):
    m = x.shape[0]
    return pl.pallas_call(
        _out_b_kernel,
        grid=(m // tm,),
        in_specs=[_rows(tm, D_MODEL),
                  pl.BlockSpec((SWA_Q_HEADS, tm, HEAD_DIMP), lambda i: (0, i, 0)),
                  _rows(tm, MEM_Q), _resident(w.shape)],
        out_specs=_rows(tm, D_MODEL),
        out_shape=jax.ShapeDtypeStruct((m, D_MODEL), F32),
        compiler_params=_params("parallel"),
        name="out_b",
    )(x, tok, mo, w)


def _mem_kv_kernel(x_ref, g_ref, w_ref, k_ref, v_ref):
    x = x_ref[...]
    xn = x * lax.rsqrt(jnp.mean(x * x, axis=-1, keepdims=True) + EPS)
    for l in range(2):
        h = (xn * g_ref[l:l + 1, :]).astype(BF)
        kv = _dot(h, w_ref[:, l * 2 * MEM_Q:(l + 1) * 2 * MEM_Q])
        k_ref[l] = kv[:, :MEM_Q]
        v_ref[l] = kv[:, MEM_Q:]


def _mem_kv(mem, g, w, *, tm):
    m = mem.shape[0]
    out_spec = pl.BlockSpec((2, tm, MEM_Q), lambda i: (0, i, 0))
    return pl.pallas_call(
        _mem_kv_kernel,
        grid=(m // tm,),
        in_specs=[_rows(tm, D_MODEL), _resident(g.shape), _resident(w.shape)],
        out_specs=[out_spec, out_spec],
        out_shape=[jax.ShapeDtypeStruct((2, m, MEM_Q), F32)] * 2,
        compiler_params=_params("parallel"),
        name="mem_kv",
    )(mem, g, w)


def _gla_prompt_kernel(q_ref, k_ref, la_ref, v_ref, r_ref, gn_ref, tok_ref, st_ref, s_scr):
    c = pl.program_id(1)
    C = GLA_CHUNK

    @pl.when(c == 0)
    def _():
        s_scr[...] = jnp.zeros(s_scr.shape, F32)

    row = lax.broadcasted_iota(jnp.int32, (C, C), 0)
    col = lax.broadcasted_iota(jnp.int32, (C, C), 1)
    causal = row >= col
    ltri = jnp.where(causal, 1.0, 0.0).astype(BF)
    gn = gn_ref[...]
    for h in range(GLA_HEADS):
        sk = slice(h * GLA_DKP, (h + 1) * GLA_DKP)
        sv = slice(h * GLA_DVP, (h + 1) * GLA_DVP)
        la = la_ref[:, sk]
        hi = la.astype(BF)
        lo = (la - hi.astype(F32)).astype(BF)
        bb = _dot(ltri, jnp.concatenate([hi, lo], axis=1))
        b = bb[:, :GLA_DKP] + bb[:, GLA_DKP:]
        b_ref = b[C // 2 - 1:C // 2, :]
        b_last = b[C - 1:C, :]
        q = q_ref[:, sk]
        k = k_ref[:, sk]
        v = v_ref[:, sv].astype(BF)
        qe = (q * jnp.exp(b - b_ref)).astype(BF)
        ke = (k * jnp.exp(b_ref - b)).astype(BF)
        a = jnp.where(causal, _dot_nt(qe, ke), 0.0).astype(BF)
        st = s_scr[h]
        qb = (q * jnp.exp(b)).astype(BF)
        o = _dot(a, v) + _dot_nt(qb, st.astype(BF))
        kd = (k * jnp.exp(b_last - b)).astype(BF)
        s_scr[h] = st * jnp.exp(b_last) + _dot_tn(v, kd)
        ms = jnp.sum(o * o, axis=1, keepdims=True) * (1.0 / GLA_DV)
        on = o * lax.rsqrt(ms + EPS) * gn
        tok_ref[:, sv] = on * _silu(r_ref[:, sv])

    @pl.when(c == pl.num_programs(1) - 1)
    def _():
        st_ref[0] = s_scr[...]


def _gla_prompt(q, k, la, v, r, gn, *, batch, seq):
    m = batch * seq
    nc = seq // GLA_CHUNK
    qk = GLA_HEADS * GLA_DKP
    vr = GLA_HEADS * GLA_DVP

    def tok_map(b, c):
        return (b * nc + c, 0)

    return pl.pallas_call(
        _gla_prompt_kernel,
        grid=(batch, nc),
        in_specs=[pl.BlockSpec((GLA_CHUNK, qk), tok_map)] * 3
                 + [pl.BlockSpec((GLA_CHUNK, vr), tok_map)] * 2
                 + [_resident((1, GLA_DVP))],
        out_specs=[pl.BlockSpec((GLA_CHUNK, vr), tok_map),
                   pl.BlockSpec((1, GLA_HEADS, GLA_DVP, GLA_DKP), lambda b, c: (b, 0, 0, 0))],
        out_shape=[jax.ShapeDtypeStruct((m, vr), F32),
                   jax.ShapeDtypeStruct((batch, GLA_HEADS, GLA_DVP, GLA_DKP), F32)],
        scratch_shapes=[pltpu.VMEM((GLA_HEADS, GLA_DVP, GLA_DKP), F32)],
        compiler_params=_params("parallel", "arbitrary"),
        name="gla_prompt",
    )(q, k, la, v, r, gn)


def _mem_attn_prompt_kernel(q_ref, mk_ref, mv_ref, o_ref):
    tq = q_ref.shape[0]
    nk = MEM_HEADS * MEM_TOKENS
    q = q_ref[...].astype(BF)
    rh = lax.broadcasted_iota(jnp.int32, (nk, MEM_Q), 0) // MEM_TOKENS
    ch = lax.broadcasted_iota(jnp.int32, (nk, MEM_Q), 1) // MEM_HEAD_DIM
    diag = rh == ch
    kbd = jnp.where(diag, jnp.concatenate([mk_ref[0]] * MEM_HEADS, axis=0), 0.0).astype(BF)
    vbd = jnp.where(diag, jnp.concatenate([mv_ref[0]] * MEM_HEADS, axis=0), 0.0).astype(BF)
    s = _dot_nt(q, kbd) * (MEM_HEAD_DIM ** -0.5)
    ps, inv = [], []
    for h in range(MEM_HEADS):
        sh = s[:, h * MEM_TOKENS:(h + 1) * MEM_TOKENS]
        e = jnp.exp(sh - jnp.max(sh, axis=1, keepdims=True))
        inv.append(1.0 / jnp.sum(e, axis=1, keepdims=True))
        ps.append(e.astype(BF))
    o = _dot(jnp.concatenate(ps, axis=1), vbd)
    lane_h = lax.broadcasted_iota(jnp.int32, (tq, MEM_Q), 1) // MEM_HEAD_DIM
    scale = jnp.where(lane_h == 0, inv[0],
                      jnp.where(lane_h == 1, inv[1], jnp.where(lane_h == 2, inv[2], inv[3])))
    o_ref[...] = o * scale


def _mem_attn_prompt(mq, mk, mv, *, batch, seq, tq):
    nq = seq // tq
    kv_spec = pl.BlockSpec((1, MEM_TOKENS, MEM_Q), lambda b, i: (b, 0, 0))
    q_spec = pl.BlockSpec((tq, MEM_Q), lambda b, i: (b * nq + i, 0))
    return pl.pallas_call(
        _mem_attn_prompt_kernel,
        grid=(batch, nq),
        in_specs=[q_spec, kv_spec, kv_spec],
        out_specs=q_spec,
        out_shape=jax.ShapeDtypeStruct((batch * seq, MEM_Q), F32),
        compiler_params=_params("parallel", "parallel"),
        name="mem_attn_prompt",
    )(mq, mk, mv)


def _swa_prompt_kernel(sink_ref, q_ref, kp_ref, kc_ref, vp_ref, vc_ref, o_ref):
    n = pl.program_id(1)
    blk = WINDOW
    qi = lax.broadcasted_iota(jnp.int32, (blk, 2 * blk), 0)
    kj = lax.broadcasted_iota(jnp.int32, (blk, 2 * blk), 1)
    d = blk + qi - kj
    first_key = jnp.where(n > 0, 0, blk)
    valid = (d >= 0) & (d < WINDOW) & (kj >= first_key)
    valid = jnp.concatenate([valid] * SWA_GROUP, axis=0)
    for kh in range(SWA_KV_HEADS):
        sl = slice(kh * HEAD_DIMP, (kh + 1) * HEAD_DIMP)
        kb = jnp.concatenate([kp_ref[:, sl], kc_ref[:, sl]], axis=0).astype(BF)
        vb = jnp.concatenate([vp_ref[:, sl], vc_ref[:, sl]], axis=0).astype(BF)
        qg = q_ref[kh * SWA_GROUP:(kh + 1) * SWA_GROUP].reshape(SWA_GROUP * blk, HEAD_DIMP)
        s = _dot_nt(qg.astype(BF), kb) * (HEAD_DIM ** -0.5)
        s = jnp.where(valid, s, -jnp.inf)
        sink = jnp.concatenate(
            [jnp.full((blk, 1), sink_ref[kh * SWA_GROUP + g], F32) for g in range(SWA_GROUP)],
            axis=0)
        m = jnp.maximum(jnp.max(s, axis=1, keepdims=True), sink)
        e = jnp.exp(s - m)
        l = jnp.sum(e, axis=1, keepdims=True) + jnp.exp(sink - m)
        o = _dot(e.astype(BF), vb) * (1.0 / l)
        o_ref[kh * SWA_GROUP:(kh + 1) * SWA_GROUP] = o.reshape(SWA_GROUP, blk, HEAD_DIMP)


def _swa_prompt(sinks, q, k, v, *, batch, seq):
    nb = seq // WINDOW
    kw = SWA_KV_HEADS * HEAD_DIMP
    q_spec = pl.BlockSpec((SWA_Q_HEADS, WINDOW, HEAD_DIMP), lambda b, n: (0, b * nb + n, 0))
    prev = pl.BlockSpec((WINDOW, kw), lambda b, n: (b * nb + jnp.maximum(n - 1, 0), 0))
    cur = pl.BlockSpec((WINDOW, kw), lambda b, n: (b * nb + n, 0))
    return pl.pallas_call(
        _swa_prompt_kernel,
        grid=(batch, nb),
        in_specs=[pl.BlockSpec(memory_space=pltpu.SMEM), q_spec, prev, cur, prev, cur],
        out_specs=q_spec,
        out_shape=jax.ShapeDtypeStruct(q.shape, F32),
        compiler_params=_params("parallel", "parallel"),
        name="swa_prompt",
    )(sinks, q, k, k, v, v)


_SAMPLE_BB = 4
_DEC_SEQ = 4


def _gla_sample_kernel(s_ref, col_ref, v_ref, r_ref, gn_ref, tok_ref, so_ref):
    tok_ref[...] = jnp.zeros(tok_ref.shape, F32)
    gn = gn_ref[:, :GLA_DV]
    for bi in range(_SAMPLE_BB):
        cols = col_ref[bi]
        for h in range(GLA_HEADS):
            s = s_ref[bi, h]
            for t in range(_DEC_SEQ):
                lane = h * _DEC_SEQ + t
                qc = cols[:, lane:lane + 1]
                kc = cols[:, 16 + lane:16 + lane + 1]
                gc = cols[:, 32 + lane:32 + lane + 1]
                rw = bi * _DEC_SEQ + t
                sv = slice(h * GLA_DVP, h * GLA_DVP + GLA_DV)
                s = jnp.exp(gc) * s + kc * v_ref[rw:rw + 1, sv]
                o = jnp.sum(qc * s, axis=0, keepdims=True)
                ms = jnp.sum(o * o, axis=1, keepdims=True) * (1.0 / GLA_DV)
                on = o * lax.rsqrt(ms + EPS) * gn
                tok_ref[rw:rw + 1, sv] = on * _silu(r_ref[rw:rw + 1, sv])
            so_ref[bi, h] = s


def _gla_sample(state, cols, v, r, gn):
    nb = state.shape[0]
    vr = GLA_HEADS * GLA_DVP
    rows = _SAMPLE_BB * _DEC_SEQ
    s_spec = pl.BlockSpec((_SAMPLE_BB, GLA_HEADS, GLA_DK, GLA_DV), lambda i: (i, 0, 0, 0))
    return pl.pallas_call(
        _gla_sample_kernel,
        grid=(nb // _SAMPLE_BB,),
        in_specs=[s_spec, pl.BlockSpec((_SAMPLE_BB, GLA_DK, 128), lambda i: (i, 0, 0)),
                  _rows(rows, vr), _rows(rows, vr), _resident((1, GLA_DVP))],
        out_specs=[_rows(rows, vr), s_spec],
        out_shape=[jax.ShapeDtypeStruct((nb * _DEC_SEQ, vr), F32),
                   jax.ShapeDtypeStruct(state.shape, F32)],
        compiler_params=_params("parallel"),
        name="gla_sample",
    )(state, cols, v, r, gn)


def _mem_attn_sample_kernel(mk_ref, mv_ref, qbd_ref, o_ref):
    for bi in range(_SAMPLE_BB):
        st = _dot(mk_ref[bi].astype(BF), qbd_ref[bi].astype(BF)) * (MEM_HEAD_DIM ** -0.5)
        e = jnp.exp(st - jnp.max(st, axis=0, keepdims=True))
        p = (e * (1.0 / jnp.sum(e, axis=0, keepdims=True))).astype(BF)
        o = _dot_tn(p, mv_ref[bi].astype(BF))
        o_ref[bi] = o[:MEM_HEADS * _DEC_SEQ, :]


def _mem_attn_sample(mk, mv, qbd):
    nb = mk.shape[0]
    kv_spec = pl.BlockSpec((_SAMPLE_BB, MEM_TOKENS, MEM_Q), lambda i: (i, 0, 0))
    nr = MEM_HEADS * _DEC_SEQ
    return pl.pallas_call(
        _mem_attn_sample_kernel,
        grid=(nb // _SAMPLE_BB,),
        in_specs=[kv_spec, kv_spec, pl.BlockSpec((_SAMPLE_BB, MEM_Q, 128), lambda i: (i, 0, 0))],
        out_specs=pl.BlockSpec((_SAMPLE_BB, nr, MEM_Q), lambda i: (i, 0, 0)),
        out_shape=jax.ShapeDtypeStruct((nb, nr, MEM_Q), F32),
        compiler_params=_params("parallel"),
        name="mem_attn_sample",
    )(mk, mv, qbd)


_SWA_KEYS = WINDOW + 2 * _DEC_SEQ


def _swa_sample_kernel(k_ref, v_ref, qbd_ref, sink_ref, o_ref):
    kj = lax.broadcasted_iota(jnp.int32, (_SWA_KEYS, 128), 0)
    t = lax.broadcasted_iota(jnp.int32, (_SWA_KEYS, 128), 1) % _DEC_SEQ
    valid = (kj > t) & (kj <= WINDOW + t)
    sink = sink_ref[...]
    nq = SWA_Q_HEADS * _DEC_SEQ
    for bi in range(_SAMPLE_BB):
        st = _dot(k_ref[bi].astype(BF), qbd_ref[bi].astype(BF)) * (HEAD_DIM ** -0.5)
        st = jnp.where(valid, st, -jnp.inf)
        m = jnp.maximum(jnp.max(st, axis=0, keepdims=True), sink)
        e = jnp.exp(st - m)
        l = jnp.sum(e, axis=0, keepdims=True) + jnp.exp(sink - m)
        p = (e * (1.0 / l)).astype(BF)
        o = _dot_tn(p, v_ref[bi].astype(BF))
        o_ref[bi] = o[:nq, :]


def _swa_sample(k_all, v_all, qbd, sink_l):
    nb = k_all.shape[0]
    kw = SWA_KV_HEADS * HEAD_DIM
    nq = SWA_Q_HEADS * _DEC_SEQ
    kv_spec = pl.BlockSpec((_SAMPLE_BB, _SWA_KEYS, kw), lambda i: (i, 0, 0))
    return pl.pallas_call(
        _swa_sample_kernel,
        grid=(nb // _SAMPLE_BB,),
        in_specs=[kv_spec, kv_spec, pl.BlockSpec((_SAMPLE_BB, kw, 128), lambda i: (i, 0, 0)),
                  _resident((1, 128))],
        out_specs=pl.BlockSpec((_SAMPLE_BB, nq, kw), lambda i: (i, 0, 0)),
        out_shape=jax.ShapeDtypeStruct((nb, nq, kw), F32),
        compiler_params=_params("parallel"),
        name="swa_sample",
    )(k_all, v_all, qbd, sink_l)


def _pad_heads(w, heads, dim, dim_p, axis):
    shape = w.shape
    w = w.reshape(shape[:axis] + (heads, dim) + shape[axis + 1:])
    pad = [(0, 0)] * w.ndim
    pad[axis + 1] = (0, dim_p - dim)
    w = jnp.pad(w, pad)
    return w.reshape(shape[:axis] + (heads * dim_p,) + shape[axis + 1:])


def _prep_ffn(w_gu, w_down):
    wg = w_gu[:, :FFN_DIM].reshape(D_MODEL, FFN_CHUNKS, FFN_TF)
    wu = w_gu[:, FFN_DIM:].reshape(D_MODEL, FFN_CHUNKS, FFN_TF)
    wgu3 = jnp.concatenate([wg, wu], axis=2).transpose(1, 0, 2).astype(BF)
    wd3 = w_down.reshape(FFN_CHUNKS, FFN_TF, D_MODEL).astype(BF)
    return wgu3, wd3


def _rope_tables(pos):
    half = ROT_DIM // 2
    inv_freq = jnp.exp(-math.log(ROPE_THETA) * jnp.arange(0, ROT_DIM, 2, dtype=F32) / ROT_DIM)
    ang = pos[:, None] * inv_freq[None, :]
    cos, sin = jnp.cos(ang), jnp.sin(ang)
    n = pos.shape[0]
    rest = HEAD_DIMP - ROT_DIM
    c = jnp.concatenate([cos, cos, jnp.ones((n, rest), F32)], axis=1)
    s1 = jnp.concatenate([-sin, jnp.zeros((n, HEAD_DIMP - half), F32)], axis=1)
    s2 = jnp.concatenate([jnp.zeros((n, half), F32), sin, jnp.zeros((n, rest), F32)], axis=1)
    return c, s1, s2


def _prep_weights(p):
    w = {}
    for l in range(2):
        w["ffn1", l] = _prep_ffn(p["ffn1_w_gu"][l], p["ffn1_w_down"][l])
        w["ffn2", l] = _prep_ffn(p["ffn2_w_gu"][l], p["ffn2_w_down"][l])
    qk = GLA_HEADS * GLA_DK
    vv = GLA_HEADS * GLA_DV
    a_in = p["a_w_in"][0]
    o = 0
    wq = _pad_heads(a_in[:, o:o + qk], GLA_HEADS, GLA_DK, GLA_DKP, 1); o += qk
    wk = _pad_heads(a_in[:, o:o + qk], GLA_HEADS, GLA_DK, GLA_DKP, 1); o += qk
    wv = _pad_heads(a_in[:, o:o + vv], GLA_HEADS, GLA_DV, GLA_DVP, 1); o += vv
    wr = _pad_heads(a_in[:, o:o + vv], GLA_HEADS, GLA_DV, GLA_DVP, 1); o += vv
    wg = jnp.pad(a_in[:, o:o + GLA_RANK], ((0, 0), (0, GLA_RANKP - GLA_RANK))); o += GLA_RANK
    wm = a_in[:, o:]
    w["a_in"] = jnp.concatenate([wq, wk, wv, wr, wg, wm], axis=1).astype(BF)
    gate = _pad_heads(p["a_w_gate"][0], GLA_HEADS, GLA_DK, GLA_DKP, 1)
    w["a_gate"] = jnp.pad(gate, ((0, GLA_RANKP - GLA_RANK), (0, 0))).astype(BF)
    w["a_bgate"] = _pad_heads(p["a_b_gate"][0][None, :], GLA_HEADS, GLA_DK, GLA_DKP, 1)
    w["a_gn"] = jnp.pad(p["a_out_norm"][0], (0, GLA_DVP - GLA_DV))[None, :]
    a_out = p["a_w_out"][0]
    w["a_out"] = jnp.concatenate(
        [_pad_heads(a_out[:vv], GLA_HEADS, GLA_DV, GLA_DVP, 0), a_out[vv:]], axis=0).astype(BF)
    nq = SWA_Q_HEADS * HEAD_DIM
    b_in = p["b_w_in"][0]
    w["b_in"] = jnp.concatenate(
        [_pad_heads(b_in[:, :nq], SWA_Q_HEADS, HEAD_DIM, HEAD_DIMP, 1), b_in[:, nq:]],
        axis=1).astype(BF)
    b_out = p["b_w_out"][0]
    w["b_out"] = jnp.concatenate(
        [_pad_heads(b_out[:nq], SWA_Q_HEADS, HEAD_DIM, HEAD_DIMP, 0), b_out[nq:]],
        axis=0).astype(BF)
    nkv = SWA_KV_HEADS * HEAD_DIM
    w_kv = p["w_kv"]
    w["kv"] = jnp.concatenate(
        [_pad_heads(w_kv[:, :nkv], SWA_KV_HEADS, HEAD_DIM, HEAD_DIMP, 1),
         _pad_heads(w_kv[:, nkv:], SWA_KV_HEADS, HEAD_DIM, HEAD_DIMP, 1)], axis=1).astype(BF)
    w["mem"] = jnp.concatenate([p["mem_w_kv"][0], p["mem_w_kv"][1]], axis=1).astype(BF)
    return w


def _row(v):
    return v[None, :]


def _trunk_prompt(x, mem_k, mem_v, p, w, tabs, *, batch, seq, tm):
    x = _ffn(x, _row(p["ffn1_norm"][0]), *w["ffn1", 0], tm=tm)
    q, k, la, v, r, mq = _proj_a(x, _row(p["mix_norm"][0]), w["a_in"], w["a_gate"],
                                 w["a_bgate"], tm=tm)
    tok, st = _gla_prompt(q, k, la, v, r, w["a_gn"], batch=batch, seq=seq)
    mo = _mem_attn_prompt(mq, mem_k[0], mem_v[0], batch=batch, seq=seq, tq=tm)
    x = _out_a(x, tok, mo, w["a_out"], tm=tm)
    x = _ffn(x, _row(p["ffn2_norm"][0]), *w["ffn2", 0], tm=tm)
    k_sh, v_sh = _proj_kv(x, _row(p["kv_norm"]), w["kv"], tabs, tm=tm)
    x = _ffn(x, _row(p["ffn1_norm"][1]), *w["ffn1", 1], tm=tm)
    qs, mq = _proj_b(x, _row(p["mix_norm"][1]), w["b_in"], tabs, tm=tm)
    tok = _swa_prompt(p["b_sinks"][0], qs, k_sh, v_sh, batch=batch, seq=seq)
    mo = _mem_attn_prompt(mq, mem_k[1], mem_v[1], batch=batch, seq=seq, tq=tm)
    x = _out_b(x, tok, mo, w["b_out"], tm=tm)
    y = _ffn(x, _row(p["ffn2_norm"][1]), *w["ffn2", 1], _row(p["final_norm"]), tm=tm)
    return y, st, k_sh, v_sh


def _compact_kv(a, batch, seq):
    return a.reshape(batch, seq, SWA_KV_HEADS, HEAD_DIMP)[..., :HEAD_DIM]


def _trunk_sample(x, state, cache_k, cache_v, cache_mk, cache_mv, p, w, tabs, *, tm):
    nb, t = state.shape[0], _DEC_SEQ
    eye_m = jnp.eye(MEM_HEADS, dtype=F32)

    def mem_attn(mq, mk, mv):
        mq4 = mq.reshape(nb, t, MEM_HEADS, MEM_HEAD_DIM)
        qbd = jnp.einsum("bthd,hg->bhdgt", mq4, eye_m).reshape(nb, MEM_Q, MEM_HEADS * t)
        qbd = jnp.pad(qbd, ((0, 0), (0, 0), (0, 128 - MEM_HEADS * t)))
        o = _mem_attn_sample(mk.reshape(nb, MEM_TOKENS, MEM_Q), mv.reshape(nb, MEM_TOKENS, MEM_Q),
                             qbd)
        o = o.reshape(nb, MEM_HEADS, t, MEM_HEADS, MEM_HEAD_DIM)
        return jnp.einsum("bhtgd,hg->bthd", o, eye_m).reshape(nb * t, MEM_Q)

    x = _ffn(x, _row(p["ffn1_norm"][0]), *w["ffn1", 0], tm=tm)
    q, k, la, v, r, mq = _proj_a(x, _row(p["mix_norm"][0]), w["a_in"], w["a_gate"],
                                 w["a_bgate"], tm=tm)

    def col(a):
        return a.reshape(nb, t, GLA_HEADS, GLA_DKP)[..., :GLA_DK]

    cols = jnp.stack([col(q), col(k), col(la)], axis=0)
    cols = cols.transpose(1, 4, 0, 3, 2).reshape(nb, GLA_DK, 3 * GLA_HEADS * t)
    cols = jnp.pad(cols, ((0, 0), (0, 0), (0, 128 - 3 * GLA_HEADS * t)))
    tok, st = _gla_sample(state, cols, v, r, w["a_gn"])
    mo = mem_attn(mq, cache_mk[0], cache_mv[0])
    x = _out_a(x, tok, mo, w["a_out"], tm=tm)
    x = _ffn(x, _row(p["ffn2_norm"][0]), *w["ffn2", 0], tm=tm)
    k_sh, v_sh = _proj_kv(x, _row(p["kv_norm"]), w["kv"], tabs, tm=tm)
    kw = SWA_KV_HEADS * HEAD_DIM
    k_new = _compact_kv(k_sh, nb, t).reshape(nb, t, kw)
    v_new = _compact_kv(v_sh, nb, t).reshape(nb, t, kw)
    zpad = jnp.zeros((nb, t, kw), F32)
    k_all = jnp.concatenate([cache_k.reshape(nb, WINDOW, kw), k_new, zpad], axis=1)
    v_all = jnp.concatenate([cache_v.reshape(nb, WINDOW, kw), v_new, zpad], axis=1)
    x = _ffn(x, _row(p["ffn1_norm"][1]), *w["ffn1", 1], tm=tm)
    qs, mq = _proj_b(x, _row(p["mix_norm"][1]), w["b_in"], tabs, tm=tm)
    q5 = qs[..., :HEAD_DIM].reshape(SWA_KV_HEADS, SWA_GROUP, nb, t, HEAD_DIM)
    eye_k = jnp.eye(SWA_KV_HEADS, dtype=F32)
    qbd = jnp.einsum("kgbtd,kj->bkdjgt", q5, eye_k).reshape(nb, kw, SWA_Q_HEADS * t)
    qbd = jnp.pad(qbd, ((0, 0), (0, 0), (0, 128 - SWA_Q_HEADS * t)))
    sink_l = jnp.pad(jnp.repeat(p["b_sinks"][0], t), (0, 128 - SWA_Q_HEADS * t))[None, :]
    o = _swa_sample(k_all, v_all, qbd, sink_l)
    o = o.reshape(nb, SWA_KV_HEADS, SWA_GROUP, t, SWA_KV_HEADS, HEAD_DIM)
    tok = jnp.einsum("bkgtjd,kj->kgbtd", o, eye_k).reshape(SWA_Q_HEADS, nb * t, HEAD_DIM)
    tok = jnp.pad(tok, ((0, 0), (0, 0), (0, HEAD_DIMP - HEAD_DIM)))
    mo = mem_attn(mq, cache_mk[1], cache_mv[1])
    x = _out_b(x, tok, mo, w["b_out"], tm=tm)
    y = _ffn(x, _row(p["ffn2_norm"][1]), *w["ffn2", 1], _row(p["final_norm"]), tm=tm)
    swa_k = k_all[:, t:t + WINDOW].reshape(nb, WINDOW, SWA_KV_HEADS, HEAD_DIM)
    swa_v = v_all[:, t:t + WINDOW].reshape(nb, WINDOW, SWA_KV_HEADS, HEAD_DIM)
    return y, st, swa_k, swa_v


def kernel(x_prompt, x_sample, state_gla, cache_swa_k, cache_swa_v, cache_mem_k, cache_mem_v,
           mem_prompt, ffn1_norm, ffn1_w_gu, ffn1_w_down, mix_norm, ffn2_norm, ffn2_w_gu,
           ffn2_w_down, mem_norm, mem_w_kv, a_w_in, a_w_gate, a_b_gate, a_out_norm, a_w_out,
           kv_norm, w_kv, b_w_in, b_sinks, b_w_out, final_norm):
    p = dict(ffn1_norm=ffn1_norm, ffn1_w_gu=ffn1_w_gu, ffn1_w_down=ffn1_w_down,
             mix_norm=mix_norm, ffn2_norm=ffn2_norm, ffn2_w_gu=ffn2_w_gu,
             ffn2_w_down=ffn2_w_down, mem_w_kv=mem_w_kv, a_w_in=a_w_in, a_w_gate=a_w_gate,
             a_b_gate=a_b_gate, a_out_norm=a_out_norm, a_w_out=a_w_out, kv_norm=kv_norm,
             w_kv=w_kv, b_w_in=b_w_in, b_sinks=b_sinks, b_w_out=b_w_out, final_norm=final_norm)
    w = _prep_weights(p)
    batch, seq, _ = x_prompt.shape
    nb, t, _ = x_sample.shape

    mem_k, mem_v = _mem_kv(mem_prompt.reshape(batch * MEM_TOKENS, D_MODEL), mem_norm, w["mem"],
                           tm=512)
    mem_k = mem_k.reshape(2, batch, MEM_TOKENS, MEM_Q)
    mem_v = mem_v.reshape(2, batch, MEM_TOKENS, MEM_Q)

    tabs_p = _rope_tables(jnp.arange(seq, dtype=F32))
    y_p, st_p, k_p, v_p = _trunk_prompt(x_prompt.reshape(batch * seq, D_MODEL), mem_k, mem_v,
                                        p, w, tabs_p, batch=batch, seq=seq, tm=512)
    gla_prompt = st_p.transpose(0, 1, 3, 2)[None, :, :, :GLA_DK, :GLA_DV]
    swa_k_prompt = _compact_kv(k_p, batch, seq)[:, seq - WINDOW:]
    swa_v_prompt = _compact_kv(v_p, batch, seq)[:, seq - WINDOW:]

    tabs_s = tuple(jnp.tile(a, (nb, 1)) for a in _rope_tables(PAST_LEN + jnp.arange(t, dtype=F32)))
    y_s, st_s, swa_k_sample, swa_v_sample = _trunk_sample(
        x_sample.reshape(nb * t, D_MODEL), state_gla[0], cache_swa_k, cache_swa_v,
        cache_mem_k, cache_mem_v, p, w, tabs_s, tm=256)

    shp = (2, batch, MEM_TOKENS, MEM_HEADS, MEM_HEAD_DIM)
    return (y_p.reshape(batch, seq, D_MODEL), y_s.reshape(nb, t, D_MODEL), gla_prompt,
            st_s[None], swa_k_prompt, swa_v_prompt, swa_k_sample, swa_v_sample,
            mem_k.reshape(shp), mem_v.reshape(shp))
```

```python
import functools
import math

import jax
import jax.numpy as jnp
from jax import lax
from jax.experimental import pallas as pl
from jax.experimental.pallas import tpu as pltpu

F32 = jnp.float32
BF = jnp.bfloat16

D_MODEL = 1024
FFN_DIM = 2816
EPS = 1e-6

GLA_HEADS = 4
GLA_DK = 96
GLA_DV = 192
GLA_DKP = 128
GLA_DVP = 256
GLA_RANK = 16
GLA_RANKP = 128
GLA_GATE_NORM = 16.0
GLA_CHUNK = 256

HEAD_DIM = 64
HEAD_DIMP = 128
SWA_Q_HEADS = 12
SWA_KV_HEADS = 3
SWA_GROUP = SWA_Q_HEADS // SWA_KV_HEADS
WINDOW = 128
ROT_DIM = 16
ROPE_THETA = 500000.0
PAST_LEN = 8192

MEM_TOKENS = 256
MEM_HEADS = 4
MEM_HEAD_DIM = 64
MEM_Q = MEM_HEADS * MEM_HEAD_DIM

FFN_TF = 256
FFN_CHUNKS = FFN_DIM // FFN_TF

VMEM_LIMIT = 56 * 1024 * 1024


def _params(*sem):
    return pltpu.CompilerParams(dimension_semantics=sem, vmem_limit_bytes=VMEM_LIMIT)


def _resident(shape):
    nd = len(shape)
    return pl.BlockSpec(shape, lambda *_: (0,) * nd, pipeline_mode=pl.Buffered(1))


def _rows(tm, width):
    return pl.BlockSpec((tm, width), lambda i: (i, 0))


def _rms(x, g):
    ms = jnp.mean(x * x, axis=-1, keepdims=True)
    return x * lax.rsqrt(ms + EPS) * g


def _silu(x):
    return x * (1.0 / (1.0 + jnp.exp(-x)))


def _dot(a, b):
    return jnp.dot(a, b, preferred_element_type=F32)


def _dot_nt(a, b):
    return lax.dot_general(a, b, (((1,), (1,)), ((), ())), preferred_element_type=F32)


def _dot_tn(a, b):
    return lax.dot_general(a, b, (((0,), (0,)), ((), ())), preferred_element_type=F32)


def _ffn_kernel(x_ref, g_ref, wgu_ref, wd_ref, *rest, final_norm):
    if final_norm:
        fg_ref, o_ref = rest
    else:
        (o_ref,) = rest
    x = x_ref[...]
    h = _rms(x, g_ref[...]).astype(BF)
    acc = jnp.zeros(x.shape, F32)
    for c in range(FFN_CHUNKS):
        lo, hi = c * FFN_TF, (c + 1) * FFN_TF
        gate = _dot(h, wgu_ref[:, lo:hi])
        up = _dot(h, wgu_ref[:, FFN_DIM + lo:FFN_DIM + hi])
        a = (_silu(gate) * up).astype(BF)
        acc = acc + _dot(a, wd_ref[lo:hi, :])
    y = x + 0.5 * acc
    if final_norm:
        y = _rms(y, fg_ref[...])
    o_ref[...] = y


def _ffn(x, g, wgu, wd, final_g=None, *, tm):
    m = x.shape[0]
    final_norm = final_g is not None
    in_specs = [_rows(tm, D_MODEL), _resident((1, D_MODEL)),
                _resident(wgu.shape), _resident(wd.shape)]
    args = [x, g, wgu, wd]
    if final_norm:
        in_specs.append(_resident((1, D_MODEL)))
        args.append(final_g)
    return pl.pallas_call(
        functools.partial(_ffn_kernel, final_norm=final_norm),
        grid=(m // tm,),
        in_specs=in_specs,
        out_specs=_rows(tm, D_MODEL),
        out_shape=jax.ShapeDtypeStruct((m, D_MODEL), F32),
        compiler_params=_params("parallel"),
        name="ffn_final" if final_norm else "ffn",
    )(*args)


_A_Q = 0
_A_K = _A_Q + GLA_HEADS * GLA_DKP
_A_V = _A_K + GLA_HEADS * GLA_DKP
_A_R = _A_V + GLA_HEADS * GLA_DVP
_A_G = _A_R + GLA_HEADS * GLA_DVP
_A_M = _A_G + GLA_RANKP
_A_END = _A_M + MEM_Q


def _proj_a_kernel(x_ref, g_ref, w_ref, wg_ref, bg_ref,
                   q_ref, k_ref, la_ref, v_ref, r_ref, mq_ref):
    h = _rms(x_ref[...], g_ref[...]).astype(BF)

    def mm(a, b):
        return _dot(h, w_ref[:, a:b])

    q_ref[...] = mm(_A_Q, _A_K) * (GLA_DK ** -0.5)
    k_ref[...] = mm(_A_K, _A_V)
    v_ref[...] = mm(_A_V, _A_R).astype(v_ref.dtype)
    r_ref[...] = mm(_A_R, _A_G)
    z = _dot(mm(_A_G, _A_M).astype(BF), wg_ref[...]) + bg_ref[...]
    la_ref[...] = (jnp.minimum(z, 0.0) - jnp.log1p(jnp.exp(-jnp.abs(z)))) * (1.0 / GLA_GATE_NORM)
    mq_ref[...] = (mm(_A_M, _A_END) * (MEM_HEAD_DIM ** -0.5)).astype(BF)


def _proj_a(x, g, w, wg, bg, *, tm, v_dtype):
    m = x.shape[0]
    qk = GLA_HEADS * GLA_DKP
    vr = GLA_HEADS * GLA_DVP
    outs = ((qk, F32), (qk, F32), (qk, F32), (vr, v_dtype), (vr, F32), (MEM_Q, BF))
    return pl.pallas_call(
        _proj_a_kernel,
        grid=(m // tm,),
        in_specs=[_rows(tm, D_MODEL), _resident((1, D_MODEL)), _resident(w.shape),
                  _resident(wg.shape), _resident(bg.shape)],
        out_specs=[_rows(tm, n) for n, _ in outs],
        out_shape=[jax.ShapeDtypeStruct((m, n), dt) for n, dt in outs],
        compiler_params=_params("parallel"),
        name="proj_a",
    )(x, g, w, wg, bg)


def _rope(x, c, s1, s2):
    return (x * c + pltpu.roll(x, HEAD_DIMP - ROT_DIM // 2, 1) * s1
            + pltpu.roll(x, ROT_DIM // 2, 1) * s2)


def _proj_b_kernel(x_ref, g_ref, w_ref, c_ref, s1_ref, s2_ref, q_ref, mq_ref):
    h = _rms(x_ref[...], g_ref[...]).astype(BF)
    c, s1, s2 = c_ref[...], s1_ref[...], s2_ref[...]
    for hh in range(SWA_Q_HEADS):
        q = _dot(h, w_ref[:, hh * HEAD_DIMP:(hh + 1) * HEAD_DIMP])
        q_ref[hh] = (_rope(q, c, s1, s2) * (HEAD_DIM ** -0.5)).astype(BF)
    mq_ref[...] = (_dot(h, w_ref[:, SWA_Q_HEADS * HEAD_DIMP:]) * (MEM_HEAD_DIM ** -0.5)).astype(BF)


def _proj_b(x, g, w, tabs, *, tm):
    m = x.shape[0]
    tab_blocks = tabs[0].shape[0] // tm
    tab_spec = pl.BlockSpec((tm, HEAD_DIMP), lambda i: (i % tab_blocks, 0))
    return pl.pallas_call(
        _proj_b_kernel,
        grid=(m // tm,),
        in_specs=[_rows(tm, D_MODEL), _resident((1, D_MODEL)), _resident(w.shape),
                  tab_spec, tab_spec, tab_spec],
        out_specs=[pl.BlockSpec((SWA_Q_HEADS, tm, HEAD_DIMP), lambda i: (0, i, 0)),
                   _rows(tm, MEM_Q)],
        out_shape=[jax.ShapeDtypeStruct((SWA_Q_HEADS, m, HEAD_DIMP), BF),
                   jax.ShapeDtypeStruct((m, MEM_Q), BF)],
        compiler_params=_params("parallel"),
        name="proj_b",
    )(x, g, w, *tabs)


def _proj_kv_kernel(x_ref, g_ref, w_ref, c_ref, s1_ref, s2_ref, k_ref, v_ref):
    h = _rms(x_ref[...], g_ref[...]).astype(BF)
    c, s1, s2 = c_ref[...], s1_ref[...], s2_ref[...]
    kw = SWA_KV_HEADS * HEAD_DIMP
    for hh in range(SWA_KV_HEADS):
        sl = slice(hh * HEAD_DIMP, (hh + 1) * HEAD_DIMP)
        k_ref[:, sl] = _rope(_dot(h, w_ref[:, sl]), c, s1, s2)
    v_ref[...] = _dot(h, w_ref[:, kw:])


def _proj_kv(x, g, w, tabs, *, tm):
    m = x.shape[0]
    kw = SWA_KV_HEADS * HEAD_DIMP
    tab_blocks = tabs[0].shape[0] // tm
    tab_spec = pl.BlockSpec((tm, HEAD_DIMP), lambda i: (i % tab_blocks, 0))
    return pl.pallas_call(
        _proj_kv_kernel,
        grid=(m // tm,),
        in_specs=[_rows(tm, D_MODEL), _resident((1, D_MODEL)), _resident(w.shape),
                  tab_spec, tab_spec, tab_spec],
        out_specs=[_rows(tm, kw), _rows(tm, kw)],
        out_shape=[jax.ShapeDtypeStruct((m, kw), F32)] * 2,
        compiler_params=_params("parallel"),
        name="proj_kv",
    )(x, g, w, *tabs)


def _out_a_kernel(x_ref, tok_ref, mo_ref, w_ref, o_ref):
    nt = GLA_HEADS * GLA_DVP
    o_ref[...] = (x_ref[...] + _dot(tok_ref[...].astype(BF), w_ref[:nt, :])
                  + _dot(mo_ref[...].astype(BF), w_ref[nt:, :]))


def _out_a(x, tok, mo, w, *, tm):
    m = x.shape[0]
    return pl.pallas_call(
        _out_a_kernel,
        grid=(m // tm,),
        in_specs=[_rows(tm, D_MODEL), _rows(tm, GLA_HEADS * GLA_DVP), _rows(tm, MEM_Q),
                  _resident(w.shape)],
        out_specs=_rows(tm, D_MODEL),
        out_shape=jax.ShapeDtypeStruct((m, D_MODEL), F32),
        compiler_params=_params("parallel"),
        name="out_a",
    )(x, tok, mo, w)


def _out_b_kernel(x_ref, tok_ref, mo_ref, w_ref, o_ref):
    nt = SWA_Q_HEADS * HEAD_DIMP
    tok = jnp.concatenate([tok_ref[hh] for hh in range(SWA_Q_HEADS)], axis=1).astype(BF)
    o_ref[...] = (x_ref[...] + _dot(tok, w_ref[:nt, :])
                  + _dot(mo_ref[...].astype(BF), w_ref[nt:, :]))


def _out_b(x, tok, mo, w, *, tm):
    m = x.shape[0]
    return pl.pallas_call(
        _out_b_kernel,
        grid=(m // tm,),
        in_specs=[_rows(tm, D_MODEL),
                  pl.BlockSpec((SWA_Q_HEADS, tm, HEAD_DIMP), lambda i: (0, i, 0)),
                  _rows(tm, MEM_Q), _resident(w.shape)],
        out_specs=_rows(tm, D_MODEL),
        out_shape=jax.ShapeDtypeStruct((m, D_MODEL), F32),
        compiler_params=_params("parallel"),
        name="out_b",
    )(x, tok, mo, w)


def _mem_kv_kernel(x_ref, g_ref, w_ref, k_ref, v_ref):
    x = x_ref[...]
    xn = x * lax.rsqrt(jnp.mean(x * x, axis=-1, keepdims=True) + EPS)
    for l in range(2):
        h = (xn * g_ref[l:l + 1, :]).astype(BF)
        kv = _dot(h, w_ref[:, l * 2 * MEM_Q:(l + 1) * 2 * MEM_Q])
        k_ref[l] = kv[:, :MEM_Q]
        v_ref[l] = kv[:, MEM_Q:]


def _mem_kv(mem, g, w, *, tm):
    m = mem.shape[0]
    out_spec = pl.BlockSpec((2, tm, MEM_Q), lambda i: (0, i, 0))
    return pl.pallas_call(
        _mem_kv_kernel,
        grid=(m // tm,),
        in_specs=[_rows(tm, D_MODEL), _resident(g.shape), _resident(w.shape)],
        out_specs=[out_spec, out_spec],
        out_shape=[jax.ShapeDtypeStruct((2, m, MEM_Q), F32)] * 2,
        compiler_params=_params("parallel"),
        name="mem_kv",
    )(mem, g, w)


def _gla_prompt_kernel(q_ref, k_ref, la_ref, v_ref, r_ref, gn_ref, tok_ref, st_ref, s_scr):
    c = pl.program_id(1)
    C = GLA_CHUNK

    @pl.when(c == 0)
    def _():
        s_scr[...] = jnp.zeros(s_scr.shape, F32)

    row = lax.broadcasted_iota(jnp.int32, (C, C), 0)
    col = lax.broadcasted_iota(jnp.int32, (C, C), 1)
    causal = row >= col
    ltri = jnp.where(causal, 1.0, 0.0).astype(BF)
    gn = gn_ref[...]
    for h in range(GLA_HEADS):
        sk = slice(h * GLA_DKP, (h + 1) * GLA_DKP)
        sv = slice(h * GLA_DVP, (h + 1) * GLA_DVP)
        la = la_ref[:, sk]
        hi = la.astype(BF)
        lo = (la - hi.astype(F32)).astype(BF)
        bb = _dot(ltri, jnp.concatenate([hi, lo], axis=1))
        b = bb[:, :GLA_DKP] + bb[:, GLA_DKP:]
        b_ref = b[C // 2 - 1:C // 2, :]
        b_last = b[C - 1:C, :]
        q = q_ref[:, sk]
        k = k_ref[:, sk]
        v = v_ref[:, sv].astype(BF)
        qe = (q * jnp.exp(b - b_ref)).astype(BF)
        ke = (k * jnp.exp(b_ref - b)).astype(BF)
        a = jnp.where(causal, _dot_nt(qe, ke), 0.0).astype(BF)
        st = s_scr[h]
        qb = (q * jnp.exp(b)).astype(BF)
        o = _dot(a, v) + _dot_nt(qb, st.astype(BF))
        kd = (k * jnp.exp(b_last - b)).astype(BF)
        s_scr[h] = st * jnp.exp(b_last) + _dot_tn(v, kd)
        ms = jnp.sum(o * o, axis=1, keepdims=True) * (1.0 / GLA_DV)
        on = o * lax.rsqrt(ms + EPS) * gn
        tok_ref[:, sv] = (on * _silu(r_ref[:, sv])).astype(BF)

    @pl.when(c == pl.num_programs(1) - 1)
    def _():
        st_ref[0] = s_scr[...]


def _gla_prompt(q, k, la, v, r, gn, *, batch, seq):
    m = batch * seq
    nc = seq // GLA_CHUNK
    qk = GLA_HEADS * GLA_DKP
    vr = GLA_HEADS * GLA_DVP

    def tok_map(b, c):
        return (b * nc + c, 0)

    return pl.pallas_call(
        _gla_prompt_kernel,
        grid=(batch, nc),
        in_specs=[pl.BlockSpec((GLA_CHUNK, qk), tok_map)] * 3
                 + [pl.BlockSpec((GLA_CHUNK, vr), tok_map)] * 2
                 + [_resident((1, GLA_DVP))],
        out_specs=[pl.BlockSpec((GLA_CHUNK, vr), tok_map),
                   pl.BlockSpec((1, GLA_HEADS, GLA_DVP, GLA_DKP), lambda b, c: (b, 0, 0, 0))],
        out_shape=[jax.ShapeDtypeStruct((m, vr), BF),
                   jax.ShapeDtypeStruct((batch, GLA_HEADS, GLA_DVP, GLA_DKP), F32)],
        scratch_shapes=[pltpu.VMEM((GLA_HEADS, GLA_DVP, GLA_DKP), F32)],
        compiler_params=_params("parallel", "arbitrary"),
        name="gla_prompt",
    )(q, k, la, v, r, gn)


def _mem_attn_prompt_kernel(q_ref, mk_ref, mv_ref, o_ref):
    tq = q_ref.shape[0]
    nk = MEM_HEADS * MEM_TOKENS
    q = q_ref[...]
    rh = lax.broadcasted_iota(jnp.int32, (nk, MEM_Q), 0) // MEM_TOKENS
    ch = lax.broadcasted_iota(jnp.int32, (nk, MEM_Q), 1) // MEM_HEAD_DIM
    diag = rh == ch
    kbd = jnp.where(diag, jnp.concatenate([mk_ref[0]] * MEM_HEADS, axis=0), 0.0).astype(BF)
    vbd = jnp.where(diag, jnp.concatenate([mv_ref[0]] * MEM_HEADS, axis=0), 0.0).astype(BF)
    s = _dot_nt(q, kbd)
    ps, inv = [], []
    for h in range(MEM_HEADS):
        sh = s[:, h * MEM_TOKENS:(h + 1) * MEM_TOKENS]
        e = jnp.exp(sh - jnp.max(sh, axis=1, keepdims=True))
        inv.append(1.0 / jnp.sum(e, axis=1, keepdims=True))
        ps.append(e.astype(BF))
    o = _dot(jnp.concatenate(ps, axis=1), vbd)
    lane_h = lax.broadcasted_iota(jnp.int32, (tq, MEM_Q), 1) // MEM_HEAD_DIM
    scale = jnp.where(lane_h == 0, inv[0],
                      jnp.where(lane_h == 1, inv[1], jnp.where(lane_h == 2, inv[2], inv[3])))
    o_ref[...] = (o * scale).astype(BF)


def _mem_attn_prompt(mq, mk, mv, *, batch, seq, tq):
    nq = seq // tq
    kv_spec = pl.BlockSpec((1, MEM_TOKENS, MEM_Q), lambda b, i: (b, 0, 0))
    q_spec = pl.BlockSpec((tq, MEM_Q), lambda b, i: (b * nq + i, 0))
    return pl.pallas_call(
        _mem_attn_prompt_kernel,
        grid=(batch, nq),
        in_specs=[q_spec, kv_spec, kv_spec],
        out_specs=q_spec,
        out_shape=jax.ShapeDtypeStruct((batch * seq, MEM_Q), BF),
        compiler_params=_params("parallel", "parallel"),
        name="mem_attn_prompt",
    )(mq, mk, mv)


def _swa_prompt_kernel(sink_ref, q_ref, kp_ref, kc_ref, vp_ref, vc_ref, o_ref):
    n = pl.program_id(1)
    blk = WINDOW
    qi = lax.broadcasted_iota(jnp.int32, (blk, 2 * blk), 0)
    kj = lax.broadcasted_iota(jnp.int32, (blk, 2 * blk), 1)
    d = blk + qi - kj
    first_key = jnp.where(n > 0, 0, blk)
    valid = (d >= 0) & (d < WINDOW) & (kj >= first_key)
    bias = jnp.where(valid, 0.0, -jnp.inf)
    for kh in range(SWA_KV_HEADS):
        sl = slice(kh * HEAD_DIMP, (kh + 1) * HEAD_DIMP)
        kb = jnp.concatenate([kp_ref[:, sl], kc_ref[:, sl]], axis=0).astype(BF)
        vb = jnp.concatenate([vp_ref[:, sl], vc_ref[:, sl]], axis=0).astype(BF)
        for g in range(SWA_GROUP):
            hh = kh * SWA_GROUP + g
            s = _dot_nt(q_ref[hh], kb) + bias
            sink = sink_ref[hh]
            m = jnp.maximum(jnp.max(s, axis=1, keepdims=True), sink)
            e = jnp.exp(s - m)
            l = jnp.sum(e, axis=1, keepdims=True) + jnp.exp(sink - m)
            o_ref[hh] = (_dot(e.astype(BF), vb) * (1.0 / l)).astype(BF)


def _swa_prompt(sinks, q, k, v, *, batch, seq):
    nb = seq // WINDOW
    kw = SWA_KV_HEADS * HEAD_DIMP
    q_spec = pl.BlockSpec((SWA_Q_HEADS, WINDOW, HEAD_DIMP), lambda b, n: (0, b * nb + n, 0))
    prev = pl.BlockSpec((WINDOW, kw), lambda b, n: (b * nb + jnp.maximum(n - 1, 0), 0))
    cur = pl.BlockSpec((WINDOW, kw), lambda b, n: (b * nb + n, 0))
    return pl.pallas_call(
        _swa_prompt_kernel,
        grid=(batch, nb),
        in_specs=[pl.BlockSpec(memory_space=pltpu.SMEM), q_spec, prev, cur, prev, cur],
        out_specs=q_spec,
        out_shape=jax.ShapeDtypeStruct(q.shape, BF),
        compiler_params=_params("parallel", "parallel"),
        name="swa_prompt",
    )(sinks, q, k, k, v, v)


_SAMPLE_BB = 4
_DEC_SEQ = 4


def _gla_sample_kernel(s_ref, col_ref, v_ref, r_ref, gn_ref, tok_ref, so_ref):
    tok_ref[...] = jnp.zeros(tok_ref.shape, F32)
    gn = gn_ref[:, :GLA_DV]
    for bi in range(_SAMPLE_BB):
        cols = col_ref[bi]
        for h in range(GLA_HEADS):
            s = s_ref[bi, h]
            for t in range(_DEC_SEQ):
                lane = h * _DEC_SEQ + t
                qc = cols[:, lane:lane + 1]
                kc = cols[:, 16 + lane:16 + lane + 1]
                gc = cols[:, 32 + lane:32 + lane + 1]
                rw = bi * _DEC_SEQ + t
                sv = slice(h * GLA_DVP, h * GLA_DVP + GLA_DV)
                s = jnp.exp(gc) * s + kc * v_ref[rw:rw + 1, sv]
                o = jnp.sum(qc * s, axis=0, keepdims=True)
                ms = jnp.sum(o * o, axis=1, keepdims=True) * (1.0 / GLA_DV)
                on = o * lax.rsqrt(ms + EPS) * gn
                tok_ref[rw:rw + 1, sv] = on * _silu(r_ref[rw:rw + 1, sv])
            so_ref[bi, h] = s


def _gla_sample(state, cols, v, r, gn):
    nb = state.shape[0]
    vr = GLA_HEADS * GLA_DVP
    rows = _SAMPLE_BB * _DEC_SEQ
    s_spec = pl.BlockSpec((_SAMPLE_BB, GLA_HEADS, GLA_DK, GLA_DV), lambda i: (i, 0, 0, 0))
    return pl.pallas_call(
        _gla_sample_kernel,
        grid=(nb // _SAMPLE_BB,),
        in_specs=[s_spec, pl.BlockSpec((_SAMPLE_BB, GLA_DK, 128), lambda i: (i, 0, 0)),
                  _rows(rows, vr), _rows(rows, vr), _resident((1, GLA_DVP))],
        out_specs=[_rows(rows, vr), s_spec],
        out_shape=[jax.ShapeDtypeStruct((nb * _DEC_SEQ, vr), F32),
                   jax.ShapeDtypeStruct(state.shape, F32)],
        compiler_params=_params("parallel"),
        name="gla_sample",
    )(state, cols, v, r, gn)


def _mem_attn_sample_kernel(mk_ref, mv_ref, qbd_ref, o_ref):
    for bi in range(_SAMPLE_BB):
        st = _dot(mk_ref[bi].astype(BF), qbd_ref[bi].astype(BF))
        e = jnp.exp(st - jnp.max(st, axis=0, keepdims=True))
        p = (e * (1.0 / jnp.sum(e, axis=0, keepdims=True))).astype(BF)
        o = _dot_tn(p, mv_ref[bi].astype(BF))
        o_ref[bi] = o[:MEM_HEADS * _DEC_SEQ, :]


def _mem_attn_sample(mk, mv, qbd):
    nb = mk.shape[0]
    kv_spec = pl.BlockSpec((_SAMPLE_BB, MEM_TOKENS, MEM_Q), lambda i: (i, 0, 0))
    nr = MEM_HEADS * _DEC_SEQ
    return pl.pallas_call(
        _mem_attn_sample_kernel,
        grid=(nb // _SAMPLE_BB,),
        in_specs=[kv_spec, kv_spec, pl.BlockSpec((_SAMPLE_BB, MEM_Q, 128), lambda i: (i, 0, 0))],
        out_specs=pl.BlockSpec((_SAMPLE_BB, nr, MEM_Q), lambda i: (i, 0, 0)),
        out_shape=jax.ShapeDtypeStruct((nb, nr, MEM_Q), F32),
        compiler_params=_params("parallel"),
        name="mem_attn_sample",
    )(mk, mv, qbd)


_SWA_KEYS = WINDOW + 2 * _DEC_SEQ


def _swa_sample_kernel(k_ref, v_ref, qbd_ref, sink_ref, o_ref):
    kj = lax.broadcasted_iota(jnp.int32, (_SWA_KEYS, 128), 0)
    t = lax.broadcasted_iota(jnp.int32, (_SWA_KEYS, 128), 1) % _DEC_SEQ
    valid = (kj > t) & (kj <= WINDOW + t)
    sink = sink_ref[...]
    nq = SWA_Q_HEADS * _DEC_SEQ
    for bi in range(_SAMPLE_BB):
        st = _dot(k_ref[bi].astype(BF), qbd_ref[bi].astype(BF))
        st = jnp.where(valid, st, -jnp.inf)
        m = jnp.maximum(jnp.max(st, axis=0, keepdims=True), sink)
        e = jnp.exp(st - m)
        l = jnp.sum(e, axis=0, keepdims=True) + jnp.exp(sink - m)
        p = (e * (1.0 / l)).astype(BF)
        o = _dot_tn(p, v_ref[bi].astype(BF))
        o_ref[bi] = o[:nq, :]


def _swa_sample(k_all, v_all, qbd, sink_l):
    nb = k_all.shape[0]
    kw = SWA_KV_HEADS * HEAD_DIM
    nq = SWA_Q_HEADS * _DEC_SEQ
    kv_spec = pl.BlockSpec((_SAMPLE_BB, _SWA_KEYS, kw), lambda i: (i, 0, 0))
    return pl.pallas_call(
        _swa_sample_kernel,
        grid=(nb // _SAMPLE_BB,),
        in_specs=[kv_spec, kv_spec, pl.BlockSpec((_SAMPLE_BB, kw, 128), lambda i: (i, 0, 0)),
                  _resident((1, 128))],
        out_specs=pl.BlockSpec((_SAMPLE_BB, nq, kw), lambda i: (i, 0, 0)),
        out_shape=jax.ShapeDtypeStruct((nb, nq, kw), F32),
        compiler_params=_params("parallel"),
        name="swa_sample",
    )(k_all, v_all, qbd, sink_l)


def _pad_heads(w, heads, dim, dim_p, axis):
    shape = w.shape
    w = w.reshape(shape[:axis] + (heads, dim) + shape[axis + 1:])
    pad = [(0, 0)] * w.ndim
    pad[axis + 1] = (0, dim_p - dim)
    w = jnp.pad(w, pad)
    return w.reshape(shape[:axis] + (heads * dim_p,) + shape[axis + 1:])


def _prep_ffn(w_gu, w_down):
    return w_gu.astype(BF), w_down.astype(BF)


def _rope_tables(pos):
    half = ROT_DIM // 2
    inv_freq = jnp.exp(-math.log(ROPE_THETA) * jnp.arange(0, ROT_DIM, 2, dtype=F32) / ROT_DIM)
    ang = pos[:, None] * inv_freq[None, :]
    cos, sin = jnp.cos(ang), jnp.sin(ang)
    n = pos.shape[0]
    rest = HEAD_DIMP - ROT_DIM
    c = jnp.concatenate([cos, cos, jnp.ones((n, rest), F32)], axis=1)
    s1 = jnp.concatenate([-sin, jnp.zeros((n, HEAD_DIMP - half), F32)], axis=1)
    s2 = jnp.concatenate([jnp.zeros((n, half), F32), sin, jnp.zeros((n, rest), F32)], axis=1)
    return c, s1, s2


def _prep_weights(p):
    w = {}
    for l in range(2):
        w["ffn1", l] = _prep_ffn(p["ffn1_w_gu"][l], p["ffn1_w_down"][l])
        w["ffn2", l] = _prep_ffn(p["ffn2_w_gu"][l], p["ffn2_w_down"][l])
    qk = GLA_HEADS * GLA_DK
    vv = GLA_HEADS * GLA_DV
    a_in = p["a_w_in"][0]
    o = 0
    wq = _pad_heads(a_in[:, o:o + qk], GLA_HEADS, GLA_DK, GLA_DKP, 1); o += qk
    wk = _pad_heads(a_in[:, o:o + qk], GLA_HEADS, GLA_DK, GLA_DKP, 1); o += qk
    wv = _pad_heads(a_in[:, o:o + vv], GLA_HEADS, GLA_DV, GLA_DVP, 1); o += vv
    wr = _pad_heads(a_in[:, o:o + vv], GLA_HEADS, GLA_DV, GLA_DVP, 1); o += vv
    wg = jnp.pad(a_in[:, o:o + GLA_RANK], ((0, 0), (0, GLA_RANKP - GLA_RANK))); o += GLA_RANK
    wm = a_in[:, o:]
    w["a_in"] = jnp.concatenate([wq, wk, wv, wr, wg, wm], axis=1).astype(BF)
    gate = _pad_heads(p["a_w_gate"][0], GLA_HEADS, GLA_DK, GLA_DKP, 1)
    w["a_gate"] = jnp.pad(gate, ((0, GLA_RANKP - GLA_RANK), (0, 0))).astype(BF)
    w["a_bgate"] = _pad_heads(p["a_b_gate"][0][None, :], GLA_HEADS, GLA_DK, GLA_DKP, 1)
    w["a_gn"] = jnp.pad(p["a_out_norm"][0], (0, GLA_DVP - GLA_DV))[None, :]
    a_out = p["a_w_out"][0]
    w["a_out"] = jnp.concatenate(
        [_pad_heads(a_out[:vv], GLA_HEADS, GLA_DV, GLA_DVP, 0), a_out[vv:]], axis=0).astype(BF)
    nq = SWA_Q_HEADS * HEAD_DIM
    b_in = p["b_w_in"][0]
    w["b_in"] = jnp.concatenate(
        [_pad_heads(b_in[:, :nq], SWA_Q_HEADS, HEAD_DIM, HEAD_DIMP, 1), b_in[:, nq:]],
        axis=1).astype(BF)
    b_out = p["b_w_out"][0]
    w["b_out"] = jnp.concatenate(
        [_pad_heads(b_out[:nq], SWA_Q_HEADS, HEAD_DIM, HEAD_DIMP, 0), b_out[nq:]],
        axis=0).astype(BF)
    nkv = SWA_KV_HEADS * HEAD_DIM
    w_kv = p["w_kv"]
    w["kv"] = jnp.concatenate(
        [_pad_heads(w_kv[:, :nkv], SWA_KV_HEADS, HEAD_DIM, HEAD_DIMP, 1),
         _pad_heads(w_kv[:, nkv:], SWA_KV_HEADS, HEAD_DIM, HEAD_DIMP, 1)], axis=1).astype(BF)
    w["mem"] = jnp.concatenate([p["mem_w_kv"][0], p["mem_w_kv"][1]], axis=1).astype(BF)
    return w


def _row(v):
    return v[None, :]


def _trunk_prompt(x, mem_k, mem_v, p, w, tabs, *, batch, seq, tm):
    x = _ffn(x, _row(p["ffn1_norm"][0]), *w["ffn1", 0], tm=tm)
    q, k, la, v, r, mq = _proj_a(x, _row(p["mix_norm"][0]), w["a_in"], w["a_gate"],
                                 w["a_bgate"], tm=tm, v_dtype=BF)
    tok, st = _gla_prompt(q, k, la, v, r, w["a_gn"], batch=batch, seq=seq)
    mo = _mem_attn_prompt(mq, mem_k[0], mem_v[0], batch=batch, seq=seq, tq=tm)
    x = _out_a(x, tok, mo, w["a_out"], tm=tm)
    x = _ffn(x, _row(p["ffn2_norm"][0]), *w["ffn2", 0], tm=tm)
    k_sh, v_sh = _proj_kv(x, _row(p["kv_norm"]), w["kv"], tabs, tm=tm)
    x = _ffn(x, _row(p["ffn1_norm"][1]), *w["ffn1", 1], tm=tm)
    qs, mq = _proj_b(x, _row(p["mix_norm"][1]), w["b_in"], tabs, tm=tm)
    tok = _swa_prompt(p["b_sinks"][0], qs, k_sh, v_sh, batch=batch, seq=seq)
    mo = _mem_attn_prompt(mq, mem_k[1], mem_v[1], batch=batch, seq=seq, tq=tm)
    x = _out_b(x, tok, mo, w["b_out"], tm=tm)
    y = _ffn(x, _row(p["ffn2_norm"][1]), *w["ffn2", 1], _row(p["final_norm"]), tm=tm)
    return y, st, k_sh, v_sh


def _compact_kv(a, batch, seq):
    return a.reshape(batch, seq, SWA_KV_HEADS, HEAD_DIMP)[..., :HEAD_DIM]


def _trunk_sample(x, state, cache_k, cache_v, cache_mk, cache_mv, p, w, tabs, *, tm):
    nb, t = state.shape[0], _DEC_SEQ
    eye_m = jnp.eye(MEM_HEADS, dtype=F32)

    def mem_attn(mq, mk, mv):
        mq4 = mq.reshape(nb, t, MEM_HEADS, MEM_HEAD_DIM)
        qbd = jnp.einsum("bthd,hg->bhdgt", mq4, eye_m).reshape(nb, MEM_Q, MEM_HEADS * t)
        qbd = jnp.pad(qbd, ((0, 0), (0, 0), (0, 128 - MEM_HEADS * t)))
        o = _mem_attn_sample(mk.reshape(nb, MEM_TOKENS, MEM_Q), mv.reshape(nb, MEM_TOKENS, MEM_Q),
                             qbd)
        o = o.reshape(nb, MEM_HEADS, t, MEM_HEADS, MEM_HEAD_DIM)
        return jnp.einsum("bhtgd,hg->bthd", o, eye_m).reshape(nb * t, MEM_Q)

    x = _ffn(x, _row(p["ffn1_norm"][0]), *w["ffn1", 0], tm=tm)
    q, k, la, v, r, mq = _proj_a(x, _row(p["mix_norm"][0]), w["a_in"], w["a_gate"],
                                 w["a_bgate"], tm=tm, v_dtype=F32)

    def col(a):
        return a.reshape(nb, t, GLA_HEADS, GLA_DKP)[..., :GLA_DK]

    cols = jnp.stack([col(q), col(k), col(la)], axis=0)
    cols = cols.transpose(1, 4, 0, 3, 2).reshape(nb, GLA_DK, 3 * GLA_HEADS * t)
    cols = jnp.pad(cols, ((0, 0), (0, 0), (0, 128 - 3 * GLA_HEADS * t)))
    tok, st = _gla_sample(state, cols, v, r, w["a_gn"])
    mo = mem_attn(mq, cache_mk[0], cache_mv[0])
    x = _out_a(x, tok, mo, w["a_out"], tm=tm)
    x = _ffn(x, _row(p["ffn2_norm"][0]), *w["ffn2", 0], tm=tm)
    k_sh, v_sh = _proj_kv(x, _row(p["kv_norm"]), w["kv"], tabs, tm=tm)
    kw = SWA_KV_HEADS * HEAD_DIM
    k_new = _compact_kv(k_sh, nb, t).reshape(nb, t, kw)
    v_new = _compact_kv(v_sh, nb, t).reshape(nb, t, kw)
    zpad = jnp.zeros((nb, t, kw), F32)
    k_all = jnp.concatenate([cache_k.reshape(nb, WINDOW, kw), k_new, zpad], axis=1)
    v_all = jnp.concatenate([cache_v.reshape(nb, WINDOW, kw), v_new, zpad], axis=1)
    x = _ffn(x, _row(p["ffn1_norm"][1]), *w["ffn1", 1], tm=tm)
    qs, mq = _proj_b(x, _row(p["mix_norm"][1]), w["b_in"], tabs, tm=tm)
    q5 = qs[..., :HEAD_DIM].reshape(SWA_KV_HEADS, SWA_GROUP, nb, t, HEAD_DIM)
    eye_k = jnp.eye(SWA_KV_HEADS, dtype=F32)
    qbd = jnp.einsum("kgbtd,kj->bkdjgt", q5, eye_k).reshape(nb, kw, SWA_Q_HEADS * t)
    qbd = jnp.pad(qbd, ((0, 0), (0, 0), (0, 128 - SWA_Q_HEADS * t)))
    sink_l = jnp.pad(jnp.repeat(p["b_sinks"][0], t), (0, 128 - SWA_Q_HEADS * t))[None, :]
    o = _swa_sample(k_all, v_all, qbd, sink_l)
    o = o.reshape(nb, SWA_KV_HEADS, SWA_GROUP, t, SWA_KV_HEADS, HEAD_DIM)
    tok = jnp.einsum("bkgtjd,kj->kgbtd", o, eye_k).reshape(SWA_Q_HEADS, nb * t, HEAD_DIM)
    tok = jnp.pad(tok, ((0, 0), (0, 0), (0, HEAD_DIMP - HEAD_DIM)))
    mo = mem_attn(mq, cache_mk[1], cache_mv[1])
    x = _out_b(x, tok, mo, w["b_out"], tm=tm)
    y = _ffn(x, _row(p["ffn2_norm"][1]), *w["ffn2", 1], _row(p["final_norm"]), tm=tm)
    swa_k = k_all[:, t:t + WINDOW].reshape(nb, WINDOW, SWA_KV_HEADS, HEAD_DIM)
    swa_v = v_all[:, t:t + WINDOW].reshape(nb, WINDOW, SWA_KV_HEADS, HEAD_DIM)
    return y, st, swa_k, swa_v


def kernel(x_prompt, x_sample, state_gla, cache_swa_k, cache_swa_v, cache_mem_k, cache_mem_v,
           mem_prompt, ffn1_norm, ffn1_w_gu, ffn1_w_down, mix_norm, ffn2_norm, ffn2_w_gu,
           ffn2_w_down, mem_norm, mem_w_kv, a_w_in, a_w_gate, a_b_gate, a_out_norm, a_w_out,
           kv_norm, w_kv, b_w_in, b_sinks, b_w_out, final_norm):
    p = dict(ffn1_norm=ffn1_norm, ffn1_w_gu=ffn1_w_gu, ffn1_w_down=ffn1_w_down,
             mix_norm=mix_norm, ffn2_norm=ffn2_norm, ffn2_w_gu=ffn2_w_gu,
             ffn2_w_down=ffn2_w_down, mem_w_kv=mem_w_kv, a_w_in=a_w_in, a_w_gate=a_w_gate,
             a_b_gate=a_b_gate, a_out_norm=a_out_norm, a_w_out=a_w_out, kv_norm=kv_norm,
             w_kv=w_kv, b_w_in=b_w_in, b_sinks=b_sinks, b_w_out=b_w_out, final_norm=final_norm)
    w = _prep_weights(p)
    batch, seq, _ = x_prompt.shape
    nb, t, _ = x_sample.shape

    mem_k, mem_v = _mem_kv(mem_prompt.reshape(batch * MEM_TOKENS, D_MODEL), mem_norm, w["mem"],
                           tm=512)
    mem_k = mem_k.reshape(2, batch, MEM_TOKENS, MEM_Q)
    mem_v = mem_v.reshape(2, batch, MEM_TOKENS, MEM_Q)

    tabs_p = _rope_tables(jnp.arange(seq, dtype=F32))
    y_p, st_p, k_p, v_p = _trunk_prompt(x_prompt.reshape(batch * seq, D_MODEL), mem_k, mem_v,
                                        p, w, tabs_p, batch=batch, seq=seq, tm=512)
    gla_prompt = st_p.transpose(0, 1, 3, 2)[None, :, :, :GLA_DK, :GLA_DV]
    swa_k_prompt = _compact_kv(k_p, batch, seq)[:, seq - WINDOW:]
    swa_v_prompt = _compact_kv(v_p, batch, seq)[:, seq - WINDOW:]

    tabs_s = tuple(jnp.tile(a, (nb, 1)) for a in _rope_tables(PAST_LEN + jnp.arange(t, dtype=F32)))
    y_s, st_s, swa_k_sample, swa_v_sample = _trunk_sample(
        x_sample.reshape(nb * t, D_MODEL), state_gla[0], cache_swa_k, cache_swa_v,
        cache_mem_k, cache_mem_v, p, w, tabs_s, tm=256)

    shp = (2, batch, MEM_TOKENS, MEM_HEADS, MEM_HEAD_DIM)
    return (y_p.reshape(batch, seq, D_MODEL), y_s.reshape(nb, t, D_MODEL), gla_prompt,
            st_s[None], swa_k_prompt, swa_v_prompt, swa_k_sample, swa_v_sample,
            mem_k.reshape(shp), mem_v.reshape(shp))
```

```python
import functools
import math

import jax
import jax.numpy as jnp
from jax import lax
from jax.experimental import pallas as pl
from jax.experimental.pallas import tpu as pltpu

F32 = jnp.float32
BF = jnp.bfloat16

D_MODEL = 1024
FFN_DIM = 2816
EPS = 1e-6

GLA_HEADS = 4
GLA_DK = 96
GLA_DV = 192
GLA_DKP = 128
GLA_DVP = 256
GLA_RANK = 16
GLA_RANKP = 128
GLA_GATE_NORM = 16.0
GLA_CHUNK = 256

HEAD_DIM = 64
HEAD_DIMP = 128
SWA_Q_HEADS = 12
SWA_KV_HEADS = 3
SWA_GROUP = SWA_Q_HEADS // SWA_KV_HEADS
WINDOW = 128
ROT_DIM = 16
ROPE_THETA = 500000.0
PAST_LEN = 8192

MEM_TOKENS = 256
MEM_HEADS = 4
MEM_HEAD_DIM = 64
MEM_Q = MEM_HEADS * MEM_HEAD_DIM

FFN_TF = 256
FFN_CHUNKS = FFN_DIM // FFN_TF

VMEM_LIMIT = 56 * 1024 * 1024


def _params(*sem):
    return pltpu.CompilerParams(dimension_semantics=sem, vmem_limit_bytes=VMEM_LIMIT)


def _resident(shape):
    nd = len(shape)
    return pl.BlockSpec(shape, lambda *_: (0,) * nd, pipeline_mode=pl.Buffered(1))


def _rows(tm, width):
    return pl.BlockSpec((tm, width), lambda i: (i, 0))


def _rms(x, g):
    ms = jnp.mean(x * x, axis=-1, keepdims=True)
    return x * lax.rsqrt(ms + EPS) * g


def _silu(x):
    return x * (1.0 / (1.0 + jnp.exp(-x)))


def _dot(a, b):
    return jnp.dot(a, b, preferred_element_type=F32)


def _dot_nt(a, b):
    return lax.dot_general(a, b, (((1,), (1,)), ((), ())), preferred_element_type=F32)


def _dot_tn(a, b):
    return lax.dot_general(a, b, (((0,), (0,)), ((), ())), preferred_element_type=F32)


def _ffn_kernel(x_ref, g_ref, wgu_ref, wd_ref, *rest, final_norm):
    if final_norm:
        fg_ref, o_ref = rest
    else:
        (o_ref,) = rest
    x = x_ref[...]
    h = _rms(x, g_ref[...]).astype(BF)
    acc = jnp.zeros(x.shape, F32)
    for c in range(FFN_CHUNKS):
        lo, hi = c * FFN_TF, (c + 1) * FFN_TF
        gate = _dot(h, wgu_ref[:, lo:hi])
        up = _dot(h, wgu_ref[:, FFN_DIM + lo:FFN_DIM + hi])
        a = (_silu(gate) * up).astype(BF)
        acc = acc + _dot(a, wd_ref[lo:hi, :])
    y = x + 0.5 * acc
    if final_norm:
        y = _rms(y, fg_ref[...])
    o_ref[...] = y


def _ffn(x, g, wgu, wd, final_g=None, *, tm):
    m = x.shape[0]
    final_norm = final_g is not None
    in_specs = [_rows(tm, D_MODEL), _resident((1, D_MODEL)),
                _resident(wgu.shape), _resident(wd.shape)]
    args = [x, g, wgu, wd]
    if final_norm:
        in_specs.append(_resident((1, D_MODEL)))
        args.append(final_g)
    return pl.pallas_call(
        functools.partial(_ffn_kernel, final_norm=final_norm),
        grid=(m // tm,),
        in_specs=in_specs,
        out_specs=_rows(tm, D_MODEL),
        out_shape=jax.ShapeDtypeStruct((m, D_MODEL), F32),
        compiler_params=_params("parallel"),
        name="ffn_final" if final_norm else "ffn",
    )(*args)


_A_Q = 0
_A_K = _A_Q + GLA_HEADS * GLA_DKP
_A_V = _A_K + GLA_HEADS * GLA_DKP
_A_R = _A_V + GLA_HEADS * GLA_DVP
_A_G = _A_R + GLA_HEADS * GLA_DVP
_A_M = _A_G + GLA_RANKP
_A_END = _A_M + MEM_Q


def _proj_a_kernel(x_ref, g_ref, w_ref, wg_ref, bg_ref,
                   q_ref, k_ref, la_ref, v_ref, r_ref, mq_ref):
    h = _rms(x_ref[...], g_ref[...]).astype(BF)

    def mm(a, b):
        return _dot(h, w_ref[:, a:b])

    q_ref[...] = mm(_A_Q, _A_K) * (GLA_DK ** -0.5)
    k_ref[...] = mm(_A_K, _A_V)
    v_ref[...] = mm(_A_V, _A_R).astype(v_ref.dtype)
    r_ref[...] = mm(_A_R, _A_G)
    z = _dot(mm(_A_G, _A_M).astype(BF), wg_ref[...]) + bg_ref[...]
    la_ref[...] = (jnp.minimum(z, 0.0) - jnp.log1p(jnp.exp(-jnp.abs(z)))) * (1.0 / GLA_GATE_NORM)
    mq_ref[...] = (mm(_A_M, _A_END) * (MEM_HEAD_DIM ** -0.5)).astype(BF)


def _proj_a(x, g, w, wg, bg, *, tm, v_dtype):
    m = x.shape[0]
    qk = GLA_HEADS * GLA_DKP
    vr = GLA_HEADS * GLA_DVP
    outs = ((qk, F32), (qk, F32), (qk, F32), (vr, v_dtype), (vr, F32), (MEM_Q, BF))
    return pl.pallas_call(
        _proj_a_kernel,
        grid=(m // tm,),
        in_specs=[_rows(tm, D_MODEL), _resident((1, D_MODEL)), _resident(w.shape),
                  _resident(wg.shape), _resident(bg.shape)],
        out_specs=[_rows(tm, n) for n, _ in outs],
        out_shape=[jax.ShapeDtypeStruct((m, n), dt) for n, dt in outs],
        compiler_params=_params("parallel"),
        name="proj_a",
    )(x, g, w, wg, bg)


def _rope(x, c, s1, s2):
    return (x * c + pltpu.roll(x, HEAD_DIMP - ROT_DIM // 2, 1) * s1
            + pltpu.roll(x, ROT_DIM // 2, 1) * s2)


def _proj_b_kernel(x_ref, g_ref, w_ref, c_ref, s1_ref, s2_ref, q_ref, mq_ref):
    h = _rms(x_ref[...], g_ref[...]).astype(BF)
    c, s1, s2 = c_ref[...], s1_ref[...], s2_ref[...]
    for hh in range(SWA_Q_HEADS):
        q = _dot(h, w_ref[:, hh * HEAD_DIMP:(hh + 1) * HEAD_DIMP])
        q_ref[hh] = (_rope(q, c, s1, s2) * (HEAD_DIM ** -0.5)).astype(BF)
    mq_ref[...] = (_dot(h, w_ref[:, SWA_Q_HEADS * HEAD_DIMP:]) * (MEM_HEAD_DIM ** -0.5)).astype(BF)


def _proj_b(x, g, w, tabs, *, tm):
    m = x.shape[0]
    tab_blocks = tabs[0].shape[0] // tm
    tab_spec = pl.BlockSpec((tm, HEAD_DIMP), lambda i: (i % tab_blocks, 0))
    return pl.pallas_call(
        _proj_b_kernel,
        grid=(m // tm,),
        in_specs=[_rows(tm, D_MODEL), _resident((1, D_MODEL)), _resident(w.shape),
                  tab_spec, tab_spec, tab_spec],
        out_specs=[pl.BlockSpec((SWA_Q_HEADS, tm, HEAD_DIMP), lambda i: (0, i, 0)),
                   _rows(tm, MEM_Q)],
        out_shape=[jax.ShapeDtypeStruct((SWA_Q_HEADS, m, HEAD_DIMP), BF),
                   jax.ShapeDtypeStruct((m, MEM_Q), BF)],
        compiler_params=_params("parallel"),
        name="proj_b",
    )(x, g, w, *tabs)


def _proj_kv_kernel(x_ref, g_ref, w_ref, c_ref, s1_ref, s2_ref, k_ref, v_ref):
    h = _rms(x_ref[...], g_ref[...]).astype(BF)
    c, s1, s2 = c_ref[...], s1_ref[...], s2_ref[...]
    kw = SWA_KV_HEADS * HEAD_DIMP
    for hh in range(SWA_KV_HEADS):
        sl = slice(hh * HEAD_DIMP, (hh + 1) * HEAD_DIMP)
        k_ref[:, sl] = _rope(_dot(h, w_ref[:, sl]), c, s1, s2)
    v_ref[...] = _dot(h, w_ref[:, kw:])


def _proj_kv(x, g, w, tabs, *, tm):
    m = x.shape[0]
    kw = SWA_KV_HEADS * HEAD_DIMP
    tab_blocks = tabs[0].shape[0] // tm
    tab_spec = pl.BlockSpec((tm, HEAD_DIMP), lambda i: (i % tab_blocks, 0))
    return pl.pallas_call(
        _proj_kv_kernel,
        grid=(m // tm,),
        in_specs=[_rows(tm, D_MODEL), _resident((1, D_MODEL)), _resident(w.shape),
                  tab_spec, tab_spec, tab_spec],
        out_specs=[_rows(tm, kw), _rows(tm, kw)],
        out_shape=[jax.ShapeDtypeStruct((m, kw), F32)] * 2,
        compiler_params=_params("parallel"),
        name="proj_kv",
    )(x, g, w, *tabs)


def _out_a_kernel(x_ref, tok_ref, mo_ref, w_ref, o_ref):
    nt = GLA_HEADS * GLA_DVP
    o_ref[...] = (x_ref[...] + _dot(tok_ref[...].astype(BF), w_ref[:nt, :])
                  + _dot(mo_ref[...].astype(BF), w_ref[nt:, :]))


def _out_a(x, tok, mo, w, *, tm):
    m = x.shape[0]
    return pl.pallas_call(
        _out_a_kernel,
        grid=(m // tm,),
        in_specs=[_rows(tm, D_MODEL), _rows(tm, GLA_HEADS * GLA_DVP), _rows(tm, MEM_Q),
                  _resident(w.shape)],
        out_specs=_rows(tm, D_MODEL),
        out_shape=jax.ShapeDtypeStruct((m, D_MODEL), F32),
        compiler_params=_params("parallel"),
        name="out_a",
    )(x, tok, mo, w)


def _out_b_kernel(x_ref, tok_ref, mo_ref, w_ref, o_ref):
    nt = SWA_Q_HEADS * HEAD_DIMP
    tok = jnp.concatenate([tok_ref[hh] for hh in range(SWA_Q_HEADS)], axis=1).astype(BF)
    o_ref[...] = (x_ref[...] + _dot(tok, w_ref[:nt, :])
                  + _dot(mo_ref[...].astype(BF), w_ref[nt:, :]))


def _out_b(x, tok, mo, w, *, tm):
    m = x.shape[0]
    return pl.pallas_call(
        _out_b_kernel,
        grid=(m // tm,),
        in_specs=[_rows(tm, D_MODEL),
                  pl.BlockSpec((SWA_Q_HEADS, tm, HEAD_DIMP), lambda i: (0, i, 0)),
                  _rows(tm, MEM_Q), _resident(w.shape)],
        out_specs=_rows(tm, D_MODEL),
        out_shape=jax.ShapeDtypeStruct((m, D_MODEL), F32),
        compiler_params=_params("parallel"),
        name="out_b",
    )(x, tok, mo, w)


def _mem_kv_kernel(x_ref, g_ref, w_ref, k_ref, v_ref):
    x = x_ref[...]
    xn = x * lax.rsqrt(jnp.mean(x * x, axis=-1, keepdims=True) + EPS)
    for l in range(2):
        h = (xn * g_ref[l:l + 1, :]).astype(BF)
        kv = _dot(h, w_ref[:, l * 2 * MEM_Q:(l + 1) * 2 * MEM_Q])
        k_ref[l] = kv[:, :MEM_Q]
        v_ref[l] = kv[:, MEM_Q:]


def _mem_kv(mem, g, w, *, tm):
    m = mem.shape[0]
    out_spec = pl.BlockSpec((2, tm, MEM_Q), lambda i: (0, i, 0))
    return pl.pallas_call(
        _mem_kv_kernel,
        grid=(m // tm,),
        in_specs=[_rows(tm, D_MODEL), _resident(g.shape), _resident(w.shape)],
        out_specs=[out_spec, out_spec],
        out_shape=[jax.ShapeDtypeStruct((2, m, MEM_Q), F32)] * 2,
        compiler_params=_params("parallel"),
        name="mem_kv",
    )(mem, g, w)


def _gla_prompt_kernel(q_ref, k_ref, la_ref, v_ref, r_ref, gn_ref, tok_ref, st_ref, s_scr):
    c = pl.program_id(1)
    C = GLA_CHUNK

    @pl.when(c == 0)
    def _():
        s_scr[...] = jnp.zeros(s_scr.shape, F32)

    row = lax.broadcasted_iota(jnp.int32, (C, C), 0)
    col = lax.broadcasted_iota(jnp.int32, (C, C), 1)
    causal = row >= col
    ltri = jnp.where(causal, 1.0, 0.0).astype(BF)
    gn = gn_ref[...]
    for h in range(GLA_HEADS):
        sk = slice(h * GLA_DKP, (h + 1) * GLA_DKP)
        sv = slice(h * GLA_DVP, (h + 1) * GLA_DVP)
        la = la_ref[:, sk]
        hi = la.astype(BF)
        lo = (la - hi.astype(F32)).astype(BF)
        bb = _dot(ltri, jnp.concatenate([hi, lo], axis=1))
        b = bb[:, :GLA_DKP] + bb[:, GLA_DKP:]
        b_ref = b[C // 2 - 1:C // 2, :]
        b_last = b[C - 1:C, :]
        q = q_ref[:, sk]
        k = k_ref[:, sk]
        v = v_ref[:, sv].astype(BF)
        qe = (q * jnp.exp(b - b_ref)).astype(BF)
        ke = (k * jnp.exp(b_ref - b)).astype(BF)
        a = jnp.where(causal, _dot_nt(qe, ke), 0.0).astype(BF)
        st = s_scr[h]
        qb = (q * jnp.exp(b)).astype(BF)
        o = _dot(a, v) + _dot_nt(qb, st.astype(BF))
        kd = (k * jnp.exp(b_last - b)).astype(BF)
        s_scr[h] = st * jnp.exp(b_last) + _dot_tn(v, kd)
        ms = jnp.sum(o * o, axis=1, keepdims=True) * (1.0 / GLA_DV)
        on = o * lax.rsqrt(ms + EPS) * gn
        tok_ref[:, sv] = (on * _silu(r_ref[:, sv])).astype(BF)

    @pl.when(c == pl.num_programs(1) - 1)
    def _():
        st_ref[0] = s_scr[...]


def _gla_prompt(q, k, la, v, r, gn, *, batch, seq):
    m = batch * seq
    nc = seq // GLA_CHUNK
    qk = GLA_HEADS * GLA_DKP
    vr = GLA_HEADS * GLA_DVP

    def tok_map(b, c):
        return (b * nc + c, 0)

    return pl.pallas_call(
        _gla_prompt_kernel,
        grid=(batch, nc),
        in_specs=[pl.BlockSpec((GLA_CHUNK, qk), tok_map)] * 3
                 + [pl.BlockSpec((GLA_CHUNK, vr), tok_map)] * 2
                 + [_resident((1, GLA_DVP))],
        out_specs=[pl.BlockSpec((GLA_CHUNK, vr), tok_map),
                   pl.BlockSpec((1, GLA_HEADS, GLA_DVP, GLA_DKP), lambda b, c: (b, 0, 0, 0))],
        out_shape=[jax.ShapeDtypeStruct((m, vr), BF),
                   jax.ShapeDtypeStruct((batch, GLA_HEADS, GLA_DVP, GLA_DKP), F32)],
        scratch_shapes=[pltpu.VMEM((GLA_HEADS, GLA_DVP, GLA_DKP), F32)],
        compiler_params=_params("parallel", "arbitrary"),
        name="gla_prompt",
    )(q, k, la, v, r, gn)


def _mem_attn_prompt_kernel(q_ref, mk_ref, mv_ref, o_ref):
    tq = q_ref.shape[0]
    nk = MEM_HEADS * MEM_TOKENS
    q = q_ref[...]
    rh = lax.broadcasted_iota(jnp.int32, (nk, MEM_Q), 0) // MEM_TOKENS
    ch = lax.broadcasted_iota(jnp.int32, (nk, MEM_Q), 1) // MEM_HEAD_DIM
    diag = rh == ch
    kbd = jnp.where(diag, jnp.concatenate([mk_ref[0]] * MEM_HEADS, axis=0), 0.0).astype(BF)
    vbd = jnp.where(diag, jnp.concatenate([mv_ref[0]] * MEM_HEADS, axis=0), 0.0).astype(BF)
    s = _dot_nt(q, kbd)
    ps, inv = [], []
    for h in range(MEM_HEADS):
        sh = s[:, h * MEM_TOKENS:(h + 1) * MEM_TOKENS]
        e = jnp.exp(sh - jnp.max(sh, axis=1, keepdims=True))
        inv.append(1.0 / jnp.sum(e, axis=1, keepdims=True))
        ps.append(e.astype(BF))
    o = _dot(jnp.concatenate(ps, axis=1), vbd)
    lane_h = lax.broadcasted_iota(jnp.int32, (tq, MEM_Q), 1) // MEM_HEAD_DIM
    scale = jnp.where(lane_h == 0, inv[0],
                      jnp.where(lane_h == 1, inv[1], jnp.where(lane_h == 2, inv[2], inv[3])))
    o_ref[...] = (o * scale).astype(BF)


def _mem_attn_prompt(mq, mk, mv, *, batch, seq, tq):
    nq = seq // tq
    kv_spec = pl.BlockSpec((1, MEM_TOKENS, MEM_Q), lambda b, i: (b, 0, 0))
    q_spec = pl.BlockSpec((tq, MEM_Q), lambda b, i: (b * nq + i, 0))
    return pl.pallas_call(
        _mem_attn_prompt_kernel,
        grid=(batch, nq),
        in_specs=[q_spec, kv_spec, kv_spec],
        out_specs=q_spec,
        out_shape=jax.ShapeDtypeStruct((batch * seq, MEM_Q), BF),
        compiler_params=_params("parallel", "parallel"),
        name="mem_attn_prompt",
    )(mq, mk, mv)


def _swa_prompt_kernel(sink_ref, q_ref, kp_ref, kc_ref, vp_ref, vc_ref, o_ref):
    n = pl.program_id(1)
    blk = WINDOW
    qi = lax.broadcasted_iota(jnp.int32, (blk, 2 * blk), 0)
    kj = lax.broadcasted_iota(jnp.int32, (blk, 2 * blk), 1)
    d = blk + qi - kj
    first_key = jnp.where(n > 0, 0, blk)
    valid = (d >= 0) & (d < WINDOW) & (kj >= first_key)
    bias = jnp.where(valid, 0.0, -jnp.inf)
    for kh in range(SWA_KV_HEADS):
        sl = slice(kh * HEAD_DIMP, (kh + 1) * HEAD_DIMP)
        kb = jnp.concatenate([kp_ref[:, sl], kc_ref[:, sl]], axis=0).astype(BF)
        vb = jnp.concatenate([vp_ref[:, sl], vc_ref[:, sl]], axis=0).astype(BF)
        for g in range(SWA_GROUP):
            hh = kh * SWA_GROUP + g
            s = _dot_nt(q_ref[hh], kb) + bias
            sink = sink_ref[hh]
            m = jnp.maximum(jnp.max(s, axis=1, keepdims=True), sink)
            e = jnp.exp(s - m)
            l = jnp.sum(e, axis=1, keepdims=True) + jnp.exp(sink - m)
            o_ref[hh] = (_dot(e.astype(BF), vb) * (1.0 / l)).astype(BF)


def _swa_prompt(sinks, q, k, v, *, batch, seq):
    nb = seq // WINDOW
    kw = SWA_KV_HEADS * HEAD_DIMP
    q_spec = pl.BlockSpec((SWA_Q_HEADS, WINDOW, HEAD_DIMP), lambda b, n: (0, b * nb + n, 0))
    prev = pl.BlockSpec((WINDOW, kw), lambda b, n: (b * nb + jnp.maximum(n - 1, 0), 0))
    cur = pl.BlockSpec((WINDOW, kw), lambda b, n: (b * nb + n, 0))
    return pl.pallas_call(
        _swa_prompt_kernel,
        grid=(batch, nb),
        in_specs=[pl.BlockSpec(memory_space=pltpu.SMEM), q_spec, prev, cur, prev, cur],
        out_specs=q_spec,
        out_shape=jax.ShapeDtypeStruct(q.shape, BF),
        compiler_params=_params("parallel", "parallel"),
        name="swa_prompt",
    )(sinks, q, k, k, v, v)


_SAMPLE_BB = 8
_DEC_SEQ = 4
_DEC_BATCH = 128
_GLA_DK_BLK = 32


def _gla_sample_kernel(s_ref, q_ref, k_ref, la_ref, v_ref, r_ref, gn_ref, so_ref, tok_ref, o_scr):
    j = pl.program_id(1)

    @pl.when(j == 0)
    def _():
        o_scr[...] = jnp.zeros(o_scr.shape, F32)

    def body(dk, carry):
        s = s_ref[0, 0, dk]
        for t in range(_DEC_SEQ):
            a = jnp.exp(la_ref[t, 0, pl.ds(dk, 1), :])
            s = a * s + k_ref[t, 0, pl.ds(dk, 1), :] * v_ref[t, 0, :GLA_DV, :]
            o_scr[t] = o_scr[t] + q_ref[t, 0, pl.ds(dk, 1), :] * s
        so_ref[0, 0, dk] = s
        return carry

    lax.fori_loop(0, _GLA_DK_BLK, body, 0)

    @pl.when(j == pl.num_programs(1) - 1)
    def _():
        tok_ref[...] = jnp.zeros(tok_ref.shape, F32)
        for t in range(_DEC_SEQ):
            o = o_scr[t]
            ms = jnp.sum(o * o, axis=0, keepdims=True) * (1.0 / GLA_DV)
            on = o * lax.rsqrt(ms + EPS) * gn_ref[...]
            tok_ref[t, 0, :GLA_DV, :] = on * _silu(r_ref[t, 0, :GLA_DV, :])


def _gla_sample(state, q, k, la, v, r, gn):
    qk_spec = pl.BlockSpec((_DEC_SEQ, 1, _GLA_DK_BLK, _DEC_BATCH), lambda h, j: (0, h, j, 0))
    vr_spec = pl.BlockSpec((_DEC_SEQ, 1, GLA_DVP, _DEC_BATCH), lambda h, j: (0, h, 0, 0))
    s_spec = pl.BlockSpec((1, 1, _GLA_DK_BLK, GLA_DV, _DEC_BATCH), lambda h, j: (0, h, j, 0, 0))
    return pl.pallas_call(
        _gla_sample_kernel,
        grid=(GLA_HEADS, GLA_DK // _GLA_DK_BLK),
        in_specs=[s_spec, qk_spec, qk_spec, qk_spec, vr_spec, vr_spec,
                  _resident((GLA_DV, _DEC_BATCH))],
        out_specs=[s_spec, vr_spec],
        out_shape=[jax.ShapeDtypeStruct(state.shape, F32),
                   jax.ShapeDtypeStruct(v.shape, F32)],
        scratch_shapes=[pltpu.VMEM((_DEC_SEQ, GLA_DV, _DEC_BATCH), F32)],
        compiler_params=_params("parallel", "arbitrary"),
        name="gla_sample",
    )(state, q, k, la, v, r, gn)


def _mem_attn_sample_kernel(q_ref, mk_ref, mv_ref, o_ref):
    for bi in range(_SAMPLE_BB):
        s = _dot(q_ref[bi], mk_ref[0, bi].astype(BF))
        e = jnp.exp(s - jnp.max(s, axis=1, keepdims=True))
        p = (e * (1.0 / jnp.sum(e, axis=1, keepdims=True))).astype(BF)
        o_ref[bi] = _dot_nt(p, mv_ref[0, bi].astype(BF))


def _mem_attn_sample(qbd, mk_t, mv_t, layer):
    nb = qbd.shape[0]
    nr = MEM_HEADS * _DEC_SEQ
    kv_spec = pl.BlockSpec((1, _SAMPLE_BB, MEM_Q, MEM_TOKENS), lambda i: (layer, i, 0, 0))
    q_spec = pl.BlockSpec((_SAMPLE_BB, nr, MEM_Q), lambda i: (i, 0, 0))
    return pl.pallas_call(
        _mem_attn_sample_kernel,
        grid=(nb // _SAMPLE_BB,),
        in_specs=[q_spec, kv_spec, kv_spec],
        out_specs=q_spec,
        out_shape=jax.ShapeDtypeStruct((nb, nr, MEM_Q), F32),
        compiler_params=_params("parallel"),
        name="mem_attn_sample",
    )(qbd, mk_t, mv_t)


def _swa_sample_kernel(q_ref, sink_ref, kc_ref, vc_ref, kn_ref, vn_ref, o_ref, ko_ref, vo_ref):
    nq = SWA_Q_HEADS * _DEC_SEQ
    t = lax.broadcasted_iota(jnp.int32, (nq, WINDOW), 0) % _DEC_SEQ
    pos = lax.broadcasted_iota(jnp.int32, (nq, WINDOW), 1)
    new0 = WINDOW - _DEC_SEQ
    bias_c = jnp.where(pos > t, 0.0, -jnp.inf)
    bias_n = jnp.where((pos >= new0) & (pos - new0 <= t), 0.0, -jnp.inf)
    is_new = lax.broadcasted_iota(jnp.int32, (SWA_KV_HEADS * HEAD_DIM, WINDOW), 1) >= new0
    sink = sink_ref[...]
    for bi in range(_SAMPLE_BB):
        q = q_ref[bi]
        kc, vc, kn, vn = kc_ref[bi], vc_ref[bi], kn_ref[bi], vn_ref[bi]
        sc = _dot(q, kc.astype(BF)) + bias_c
        sn = _dot(q, kn.astype(BF)) + bias_n
        m = jnp.maximum(jnp.maximum(jnp.max(sc, axis=1, keepdims=True),
                                    jnp.max(sn, axis=1, keepdims=True)), sink)
        ec = jnp.exp(sc - m)
        en = jnp.exp(sn - m)
        l = (jnp.sum(ec, axis=1, keepdims=True) + jnp.sum(en, axis=1, keepdims=True)
             + jnp.exp(sink - m))
        inv = 1.0 / l
        o_ref[bi] = (_dot_nt((ec * inv).astype(BF), vc.astype(BF))
                     + _dot_nt((en * inv).astype(BF), vn.astype(BF)))
        ko_ref[bi] = jnp.where(is_new, kn, pltpu.roll(kc, new0, 1))
        vo_ref[bi] = jnp.where(is_new, vn, pltpu.roll(vc, new0, 1))


def _swa_sample(qbd, sink_col, kc, vc, kn, vn):
    nb = qbd.shape[0]
    kw = SWA_KV_HEADS * HEAD_DIM
    nq = SWA_Q_HEADS * _DEC_SEQ
    kv_spec = pl.BlockSpec((_SAMPLE_BB, kw, WINDOW), lambda i: (i, 0, 0))
    q_spec = pl.BlockSpec((_SAMPLE_BB, nq, kw), lambda i: (i, 0, 0))
    return pl.pallas_call(
        _swa_sample_kernel,
        grid=(nb // _SAMPLE_BB,),
        in_specs=[q_spec, _resident((nq, 1)), kv_spec, kv_spec, kv_spec, kv_spec],
        out_specs=[q_spec, kv_spec, kv_spec],
        out_shape=[jax.ShapeDtypeStruct((nb, nq, kw), F32),
                   jax.ShapeDtypeStruct(kc.shape, F32), jax.ShapeDtypeStruct(kc.shape, F32)],
        compiler_params=_params("parallel"),
        name="swa_sample",
    )(qbd, sink_col, kc, vc, kn, vn)


def _pad_heads(w, heads, dim, dim_p, axis):
    shape = w.shape
    w = w.reshape(shape[:axis] + (heads, dim) + shape[axis + 1:])
    pad = [(0, 0)] * w.ndim
    pad[axis + 1] = (0, dim_p - dim)
    w = jnp.pad(w, pad)
    return w.reshape(shape[:axis] + (heads * dim_p,) + shape[axis + 1:])


def _prep_ffn(w_gu, w_down):
    return w_gu.astype(BF), w_down.astype(BF)


def _rope_tables(pos):
    half = ROT_DIM // 2
    inv_freq = jnp.exp(-math.log(ROPE_THETA) * jnp.arange(0, ROT_DIM, 2, dtype=F32) / ROT_DIM)
    ang = pos[:, None] * inv_freq[None, :]
    cos, sin = jnp.cos(ang), jnp.sin(ang)
    n = pos.shape[0]
    rest = HEAD_DIMP - ROT_DIM
    c = jnp.concatenate([cos, cos, jnp.ones((n, rest), F32)], axis=1)
    s1 = jnp.concatenate([-sin, jnp.zeros((n, HEAD_DIMP - half), F32)], axis=1)
    s2 = jnp.concatenate([jnp.zeros((n, half), F32), sin, jnp.zeros((n, rest), F32)], axis=1)
    return c, s1, s2


def _prep_weights(p):
    w = {}
    for l in range(2):
        w["ffn1", l] = _prep_ffn(p["ffn1_w_gu"][l], p["ffn1_w_down"][l])
        w["ffn2", l] = _prep_ffn(p["ffn2_w_gu"][l], p["ffn2_w_down"][l])
    qk = GLA_HEADS * GLA_DK
    vv = GLA_HEADS * GLA_DV
    a_in = p["a_w_in"][0]
    o = 0
    wq = _pad_heads(a_in[:, o:o + qk], GLA_HEADS, GLA_DK, GLA_DKP, 1); o += qk
    wk = _pad_heads(a_in[:, o:o + qk], GLA_HEADS, GLA_DK, GLA_DKP, 1); o += qk
    wv = _pad_heads(a_in[:, o:o + vv], GLA_HEADS, GLA_DV, GLA_DVP, 1); o += vv
    wr = _pad_heads(a_in[:, o:o + vv], GLA_HEADS, GLA_DV, GLA_DVP, 1); o += vv
    wg = jnp.pad(a_in[:, o:o + GLA_RANK], ((0, 0), (0, GLA_RANKP - GLA_RANK))); o += GLA_RANK
    wm = a_in[:, o:]
    w["a_in"] = jnp.concatenate([wq, wk, wv, wr, wg, wm], axis=1).astype(BF)
    gate = _pad_heads(p["a_w_gate"][0], GLA_HEADS, GLA_DK, GLA_DKP, 1)
    w["a_gate"] = jnp.pad(gate, ((0, GLA_RANKP - GLA_RANK), (0, 0))).astype(BF)
    w["a_bgate"] = _pad_heads(p["a_b_gate"][0][None, :], GLA_HEADS, GLA_DK, GLA_DKP, 1)
    w["a_gn"] = jnp.pad(p["a_out_norm"][0], (0, GLA_DVP - GLA_DV))[None, :]
    a_out = p["a_w_out"][0]
    w["a_out"] = jnp.concatenate(
        [_pad_heads(a_out[:vv], GLA_HEADS, GLA_DV, GLA_DVP, 0), a_out[vv:]], axis=0).astype(BF)
    nq = SWA_Q_HEADS * HEAD_DIM
    b_in = p["b_w_in"][0]
    w["b_in"] = jnp.concatenate(
        [_pad_heads(b_in[:, :nq], SWA_Q_HEADS, HEAD_DIM, HEAD_DIMP, 1), b_in[:, nq:]],
        axis=1).astype(BF)
    b_out = p["b_w_out"][0]
    w["b_out"] = jnp.concatenate(
        [_pad_heads(b_out[:nq], SWA_Q_HEADS, HEAD_DIM, HEAD_DIMP, 0), b_out[nq:]],
        axis=0).astype(BF)
    nkv = SWA_KV_HEADS * HEAD_DIM
    w_kv = p["w_kv"]
    w["kv"] = jnp.concatenate(
        [_pad_heads(w_kv[:, :nkv], SWA_KV_HEADS, HEAD_DIM, HEAD_DIMP, 1),
         _pad_heads(w_kv[:, nkv:], SWA_KV_HEADS, HEAD_DIM, HEAD_DIMP, 1)], axis=1).astype(BF)
    w["mem"] = jnp.concatenate([p["mem_w_kv"][0], p["mem_w_kv"][1]], axis=1).astype(BF)
    return w


def _row(v):
    return v[None, :]


def _trunk_prompt(x, mem_k, mem_v, p, w, tabs, *, batch, seq, tm):
    x = _ffn(x, _row(p["ffn1_norm"][0]), *w["ffn1", 0], tm=tm)
    q, k, la, v, r, mq = _proj_a(x, _row(p["mix_norm"][0]), w["a_in"], w["a_gate"],
                                 w["a_bgate"], tm=tm, v_dtype=BF)
    tok, st = _gla_prompt(q, k, la, v, r, w["a_gn"], batch=batch, seq=seq)
    mo = _mem_attn_prompt(mq, mem_k[0], mem_v[0], batch=batch, seq=seq, tq=tm)
    x = _out_a(x, tok, mo, w["a_out"], tm=tm)
    x = _ffn(x, _row(p["ffn2_norm"][0]), *w["ffn2", 0], tm=tm)
    k_sh, v_sh = _proj_kv(x, _row(p["kv_norm"]), w["kv"], tabs, tm=tm)
    x = _ffn(x, _row(p["ffn1_norm"][1]), *w["ffn1", 1], tm=tm)
    qs, mq = _proj_b(x, _row(p["mix_norm"][1]), w["b_in"], tabs, tm=tm)
    tok = _swa_prompt(p["b_sinks"][0], qs, k_sh, v_sh, batch=batch, seq=seq)
    mo = _mem_attn_prompt(mq, mem_k[1], mem_v[1], batch=batch, seq=seq, tq=tm)
    x = _out_b(x, tok, mo, w["b_out"], tm=tm)
    y = _ffn(x, _row(p["ffn2_norm"][1]), *w["ffn2", 1], _row(p["final_norm"]), tm=tm)
    return y, st, k_sh, v_sh


def _compact_kv(a, batch, seq):
    return a.reshape(batch, seq, SWA_KV_HEADS, HEAD_DIMP)[..., :HEAD_DIM]


def _trunk_sample(x, state_t, kc_t, vc_t, mk_t, mv_t, p, w, tabs, *, tm):
    nb, t = _DEC_BATCH, _DEC_SEQ
    kw = SWA_KV_HEADS * HEAD_DIM
    mem_mask = (jnp.arange(MEM_Q) // MEM_HEAD_DIM)[None, :] == jnp.arange(MEM_HEADS)[:, None]
    kv_mask = (jnp.arange(kw) // HEAD_DIM)[None, :] == jnp.arange(SWA_KV_HEADS)[:, None]

    def mem_attn(mq, layer):
        q4 = mq.reshape(nb, 1, t, MEM_Q)
        qbd = jnp.where(mem_mask[None, :, None, :], q4, 0).reshape(nb, MEM_HEADS * t, MEM_Q)
        o = _mem_attn_sample(qbd, mk_t, mv_t, layer).reshape(nb, MEM_HEADS, t, MEM_Q)
        o = jnp.sum(jnp.where(mem_mask[None, :, None, :], o, 0.0), axis=1)
        return o.reshape(nb * t, MEM_Q)

    def lanes(a, width):
        return a.reshape(nb, t, GLA_HEADS, width).transpose(1, 2, 3, 0)

    x = _ffn(x, _row(p["ffn1_norm"][0]), *w["ffn1", 0], tm=tm)
    q, k, la, v, r, mq = _proj_a(x, _row(p["mix_norm"][0]), w["a_in"], w["a_gate"],
                                 w["a_bgate"], tm=tm, v_dtype=F32)
    gn = jnp.broadcast_to(p["a_out_norm"][0][:, None], (GLA_DV, nb))
    st, tok = _gla_sample(state_t, lanes(q, GLA_DKP), lanes(k, GLA_DKP), lanes(la, GLA_DKP),
                          lanes(v, GLA_DVP), lanes(r, GLA_DVP), gn)
    tok = tok.transpose(3, 0, 1, 2).reshape(nb * t, GLA_HEADS * GLA_DVP)
    mo = mem_attn(mq, 0)
    x = _out_a(x, tok, mo, w["a_out"], tm=tm)
    x = _ffn(x, _row(p["ffn2_norm"][0]), *w["ffn2", 0], tm=tm)
    k_sh, v_sh = _proj_kv(x, _row(p["kv_norm"]), w["kv"], tabs, tm=tm)

    def new_rows(a):
        a = _compact_kv(a, nb, t).reshape(nb, t, kw).transpose(0, 2, 1)
        return jnp.pad(a, ((0, 0), (0, 0), (WINDOW - t, 0)))

    x = _ffn(x, _row(p["ffn1_norm"][1]), *w["ffn1", 1], tm=tm)
    qs, mq = _proj_b(x, _row(p["mix_norm"][1]), w["b_in"], tabs, tm=tm)
    q5 = qs[..., :HEAD_DIM].reshape(SWA_KV_HEADS, SWA_GROUP, nb, t, HEAD_DIM).transpose(2, 0, 1, 3, 4)
    qbd = jnp.where(kv_mask[None, :, None, None, :], jnp.tile(q5, (1, 1, 1, 1, SWA_KV_HEADS)), 0)
    qbd = qbd.reshape(nb, SWA_Q_HEADS * t, kw)
    sink_col = jnp.repeat(p["b_sinks"][0], t)[:, None]
    o, swa_k, swa_v = _swa_sample(qbd, sink_col, kc_t, vc_t, new_rows(k_sh), new_rows(v_sh))
    o = o.reshape(nb, SWA_KV_HEADS, SWA_GROUP, t, kw)
    o = jnp.where(kv_mask[None, :, None, None, :], o, 0.0)
    o = o.reshape(nb, SWA_KV_HEADS, SWA_GROUP, t, SWA_KV_HEADS, HEAD_DIM).sum(axis=4)
    tok = o.transpose(1, 2, 0, 3, 4).reshape(SWA_Q_HEADS, nb * t, HEAD_DIM)
    tok = jnp.pad(tok, ((0, 0), (0, 0), (0, HEAD_DIMP - HEAD_DIM)))
    mo = mem_attn(mq, 1)
    x = _out_b(x, tok, mo, w["b_out"], tm=tm)
    y = _ffn(x, _row(p["ffn2_norm"][1]), *w["ffn2", 1], _row(p["final_norm"]), tm=tm)
    return y, st, swa_k, swa_v


def kernel(x_prompt, x_sample, state_gla, cache_swa_k, cache_swa_v, cache_mem_k, cache_mem_v,
           mem_prompt, ffn1_norm, ffn1_w_gu, ffn1_w_down, mix_norm, ffn2_norm, ffn2_w_gu,
           ffn2_w_down, mem_norm, mem_w_kv, a_w_in, a_w_gate, a_b_gate, a_out_norm, a_w_out,
           kv_norm, w_kv, b_w_in, b_sinks, b_w_out, final_norm):
    p = dict(ffn1_norm=ffn1_norm, ffn1_w_gu=ffn1_w_gu, ffn1_w_down=ffn1_w_down,
             mix_norm=mix_norm, ffn2_norm=ffn2_norm, ffn2_w_gu=ffn2_w_gu,
             ffn2_w_down=ffn2_w_down, mem_w_kv=mem_w_kv, a_w_in=a_w_in, a_w_gate=a_w_gate,
             a_b_gate=a_b_gate, a_out_norm=a_out_norm, a_w_out=a_w_out, kv_norm=kv_norm,
             w_kv=w_kv, b_w_in=b_w_in, b_sinks=b_sinks, b_w_out=b_w_out, final_norm=final_norm)
    w = _prep_weights(p)
    batch, seq, _ = x_prompt.shape
    nb, t, _ = x_sample.shape

    mem_k, mem_v = _mem_kv(mem_prompt.reshape(batch * MEM_TOKENS, D_MODEL), mem_norm, w["mem"],
                           tm=512)
    mem_k = mem_k.reshape(2, batch, MEM_TOKENS, MEM_Q)
    mem_v = mem_v.reshape(2, batch, MEM_TOKENS, MEM_Q)

    tabs_p = _rope_tables(jnp.arange(seq, dtype=F32))
    y_p, st_p, k_p, v_p = _trunk_prompt(x_prompt.reshape(batch * seq, D_MODEL), mem_k, mem_v,
                                        p, w, tabs_p, batch=batch, seq=seq, tm=512)
    gla_prompt = st_p.transpose(0, 1, 3, 2)[None, :, :, :GLA_DK, :GLA_DV]
    swa_k_prompt = _compact_kv(k_p, batch, seq)[:, seq - WINDOW:]
    swa_v_prompt = _compact_kv(v_p, batch, seq)[:, seq - WINDOW:]

    tabs_s = tuple(jnp.tile(a, (nb, 1)) for a in _rope_tables(PAST_LEN + jnp.arange(t, dtype=F32)))
    kw = SWA_KV_HEADS * HEAD_DIM
    y_s, st_s, k_s, v_s = _trunk_sample(
        x_sample.reshape(nb * t, D_MODEL),
        state_gla.transpose(0, 2, 3, 4, 1),
        cache_swa_k.transpose(0, 2, 3, 1).reshape(nb, kw, WINDOW),
        cache_swa_v.transpose(0, 2, 3, 1).reshape(nb, kw, WINDOW),
        cache_mem_k.transpose(0, 1, 3, 4, 2).reshape(2, nb, MEM_Q, MEM_TOKENS),
        cache_mem_v.transpose(0, 1, 3, 4, 2).reshape(2, nb, MEM_Q, MEM_TOKENS),
        p, w, tabs_s, tm=256)
    gla_sample = st_s.transpose(0, 4, 1, 2, 3)
    swa_k_sample = k_s.reshape(nb, SWA_KV_HEADS, HEAD_DIM, WINDOW).transpose(0, 3, 1, 2)
    swa_v_sample = v_s.reshape(nb, SWA_KV_HEADS, HEAD_DIM, WINDOW).transpose(0, 3, 1, 2)

    shp = (2, batch, MEM_TOKENS, MEM_HEADS, MEM_HEAD_DIM)
    return (y_p.reshape(batch, seq, D_MODEL), y_s.reshape(nb, t, D_MODEL), gla_prompt,
            gla_sample, swa_k_prompt, swa_v_prompt, swa_k_sample, swa_v_sample,
            mem_k.reshape(shp), mem_v.reshape(shp))
```

```python
import functools
import math

import jax
import jax.numpy as jnp
from jax import lax
from jax.experimental import pallas as pl
from jax.experimental.pallas import tpu as pltpu

F32 = jnp.float32
BF = jnp.bfloat16

D_MODEL = 1024
FFN_DIM = 2816
EPS = 1e-6

GLA_HEADS = 4
GLA_DK = 96
GLA_DV = 192
GLA_DKP = 128
GLA_DVP = 256
GLA_RANK = 16
GLA_RANKP = 128
GLA_GATE_NORM = 16.0
GLA_CHUNK = 256

HEAD_DIM = 64
HEAD_DIMP = 128
SWA_Q_HEADS = 12
SWA_KV_HEADS = 3
SWA_GROUP = SWA_Q_HEADS // SWA_KV_HEADS
WINDOW = 128
ROT_DIM = 16
ROPE_THETA = 500000.0
PAST_LEN = 8192

MEM_TOKENS = 256
MEM_HEADS = 4
MEM_HEAD_DIM = 64
MEM_Q = MEM_HEADS * MEM_HEAD_DIM

FFN_TF = 256
FFN_CHUNKS = FFN_DIM // FFN_TF
TOKEN_TILE = 512

VMEM_LIMIT = 60 * 1024 * 1024


def _params(*sem):
    return pltpu.CompilerParams(dimension_semantics=sem, vmem_limit_bytes=VMEM_LIMIT)


def _resident(shape):
    nd = len(shape)
    return pl.BlockSpec(shape, lambda *_: (0,) * nd, pipeline_mode=pl.Buffered(1))


def _layer_block(shape, layer):
    nd = len(shape)
    return pl.BlockSpec((1,) + tuple(shape[1:]), lambda *_: (layer,) + (0,) * (nd - 1),
                        pipeline_mode=pl.Buffered(1))


def _rows(width):
    return pl.BlockSpec((TOKEN_TILE, width), lambda i: (i, 0))


def _prompt_rows(width, n_prompt):
    return pl.BlockSpec((TOKEN_TILE, width), lambda i: (jnp.minimum(i, n_prompt - 1), 0))


def _sample_rows(width):
    return pl.BlockSpec((TOKEN_TILE, width), lambda i: (0, 0), pipeline_mode=pl.Buffered(1))


def _rms(x, g):
    ms = jnp.mean(x * x, axis=-1, keepdims=True)
    return x * lax.rsqrt(ms + EPS) * g


def _silu(x):
    return x * (1.0 / (1.0 + jnp.exp(-x)))


def _dot(a, b):
    return jnp.dot(a, b, preferred_element_type=F32)


def _dot_nt(a, b):
    return lax.dot_general(a, b, (((1,), (1,)), ((), ())), preferred_element_type=F32)


def _dot_tn(a, b):
    return lax.dot_general(a, b, (((0,), (0,)), ((), ())), preferred_element_type=F32)


def _pick(is_prompt, p_ref, s_ref):
    return jnp.where(is_prompt, p_ref[...], s_ref[...])


def _ffn_half(x, g_ref, wgu_ref, wd_ref):
    h = _rms(x, g_ref[0]).astype(BF)
    acc = jnp.zeros(x.shape, F32)
    for c in range(FFN_CHUNKS):
        lo, hi = c * FFN_TF, (c + 1) * FFN_TF
        gate = _dot(h, wgu_ref[0, :, lo:hi])
        up = _dot(h, wgu_ref[0, :, FFN_DIM + lo:FFN_DIM + hi])
        a = (_silu(gate) * up).astype(BF)
        acc = acc + _dot(a, wd_ref[0, lo:hi, :])
    return x + 0.5 * acc


def _rope(x, c, s1, s2):
    return (x * c + pltpu.roll(x, HEAD_DIMP - ROT_DIM // 2, 1) * s1
            + pltpu.roll(x, ROT_DIM // 2, 1) * s2)


_A_Q = 0
_A_K = _A_Q + GLA_HEADS * GLA_DKP
_A_V = _A_K + GLA_HEADS * GLA_DKP
_A_R = _A_V + GLA_HEADS * GLA_DVP
_A_G = _A_R + GLA_HEADS * GLA_DVP
_A_M = _A_G + GLA_RANKP
_A_END = _A_M + MEM_Q


def _layer_a_in_kernel(xp_ref, xs_ref, g1_ref, wgu_ref, wd_ref, gm_ref, w_ref, wg_ref, bg_ref,
                       x_ref, q_ref, k_ref, la_ref, v_ref, r_ref, mq_ref, *, n_prompt):
    is_prompt = pl.program_id(0) < n_prompt
    x = _ffn_half(_pick(is_prompt, xp_ref, xs_ref), g1_ref, wgu_ref, wd_ref)
    x_ref[...] = x
    h = _rms(x, gm_ref[0]).astype(BF)

    def mm(a, b):
        return _dot(h, w_ref[:, a:b])

    q_ref[...] = mm(_A_Q, _A_K) * (GLA_DK ** -0.5)
    k_ref[...] = mm(_A_K, _A_V)
    v_ref[...] = mm(_A_V, _A_R)
    r_ref[...] = mm(_A_R, _A_G)
    z = _dot(mm(_A_G, _A_M).astype(BF), wg_ref[...]) + bg_ref[...]
    la_ref[...] = (jnp.minimum(z, 0.0) - jnp.log1p(jnp.exp(-jnp.abs(z)))) * (1.0 / GLA_GATE_NORM)
    mq_ref[...] = (mm(_A_M, _A_END) * (MEM_HEAD_DIM ** -0.5)).astype(BF)


def _layer_a_in(xp, xs, g1, wgu, wd, gm, w, wg, bg):
    n_prompt = xp.shape[0] // TOKEN_TILE
    m = xp.shape[0] + xs.shape[0]
    qk = GLA_HEADS * GLA_DKP
    vr = GLA_HEADS * GLA_DVP
    outs = ((D_MODEL, F32), (qk, F32), (qk, F32), (qk, F32), (vr, F32), (vr, F32), (MEM_Q, BF))
    return pl.pallas_call(
        functools.partial(_layer_a_in_kernel, n_prompt=n_prompt),
        grid=(m // TOKEN_TILE,),
        in_specs=[_prompt_rows(D_MODEL, n_prompt), _sample_rows(D_MODEL),
                  _layer_block(g1.shape, 0), _layer_block(wgu.shape, 0), _layer_block(wd.shape, 0),
                  _layer_block(gm.shape, 0), _resident(w.shape), _resident(wg.shape),
                  _resident(bg.shape)],
        out_specs=[_rows(n) for n, _ in outs],
        out_shape=[jax.ShapeDtypeStruct((m, n), dt) for n, dt in outs],
        compiler_params=_params("arbitrary"),
        name="layer_a_in",
    )(xp, xs, g1, wgu, wd, gm, w, wg, bg)


def _layer_a_out_kernel(x_ref, tokp_ref, toks_ref, mop_ref, mos_ref, wo_ref, g2_ref, wgu_ref,
                        wd_ref, gkv_ref, wkv_ref, c_ref, s1_ref, s2_ref,
                        xo_ref, k_ref, v_ref, *, n_prompt):
    is_prompt = pl.program_id(0) < n_prompt
    nt = GLA_HEADS * GLA_DVP
    x = (x_ref[...] + _dot(_pick(is_prompt, tokp_ref, toks_ref), wo_ref[:nt, :])
         + _dot(_pick(is_prompt, mop_ref, mos_ref), wo_ref[nt:, :]))
    x = _ffn_half(x, g2_ref, wgu_ref, wd_ref)
    xo_ref[...] = x
    h = _rms(x, gkv_ref[...]).astype(BF)
    c, s1, s2 = c_ref[...], s1_ref[...], s2_ref[...]
    kw = SWA_KV_HEADS * HEAD_DIMP
    for hh in range(SWA_KV_HEADS):
        sl = slice(hh * HEAD_DIMP, (hh + 1) * HEAD_DIMP)
        k_ref[:, sl] = _rope(_dot(h, wkv_ref[:, sl]), c, s1, s2)
    v_ref[...] = _dot(h, wkv_ref[:, kw:])


def _tab_spec(n_prompt, blocks_per_seq):
    return pl.BlockSpec((TOKEN_TILE, HEAD_DIMP),
                        lambda i: (jnp.where(i < n_prompt, i % blocks_per_seq, blocks_per_seq), 0))


def _layer_a_out(x, tokp, toks, mop, mos, wo, g2, wgu, wd, gkv, wkv, tabs, *, seq):
    m = x.shape[0]
    n_prompt = tokp.shape[0] // TOKEN_TILE
    kw = SWA_KV_HEADS * HEAD_DIMP
    nt = GLA_HEADS * GLA_DVP
    tab = _tab_spec(n_prompt, seq // TOKEN_TILE)
    return pl.pallas_call(
        functools.partial(_layer_a_out_kernel, n_prompt=n_prompt),
        grid=(m // TOKEN_TILE,),
        in_specs=[_rows(D_MODEL), _prompt_rows(nt, n_prompt), _sample_rows(nt),
                  _prompt_rows(MEM_Q, n_prompt), _sample_rows(MEM_Q), _resident(wo.shape),
                  _layer_block(g2.shape, 0), _layer_block(wgu.shape, 0), _layer_block(wd.shape, 0),
                  _resident(gkv.shape), _resident(wkv.shape), tab, tab, tab],
        out_specs=[_rows(D_MODEL), _rows(kw), _rows(kw)],
        out_shape=[jax.ShapeDtypeStruct((m, D_MODEL), F32),
                   jax.ShapeDtypeStruct((m, kw), F32), jax.ShapeDtypeStruct((m, kw), F32)],
        compiler_params=_params("arbitrary"),
        name="layer_a_out",
    )(x, tokp, toks, mop, mos, wo, g2, wgu, wd, gkv, wkv, *tabs)


def _layer_b_in_kernel(x_ref, g1_ref, wgu_ref, wd_ref, gm_ref, w_ref, c_ref, s1_ref, s2_ref,
                       xo_ref, q_ref, mq_ref):
    x = _ffn_half(x_ref[...], g1_ref, wgu_ref, wd_ref)
    xo_ref[...] = x
    h = _rms(x, gm_ref[0]).astype(BF)
    c, s1, s2 = c_ref[...], s1_ref[...], s2_ref[...]
    for hh in range(SWA_Q_HEADS):
        q = _dot(h, w_ref[:, hh * HEAD_DIMP:(hh + 1) * HEAD_DIMP])
        q_ref[hh] = (_rope(q, c, s1, s2) * (HEAD_DIM ** -0.5)).astype(BF)
    mq_ref[...] = (_dot(h, w_ref[:, SWA_Q_HEADS * HEAD_DIMP:]) * (MEM_HEAD_DIM ** -0.5)).astype(BF)


def _layer_b_in(x, g1, wgu, wd, gm, w, tabs, *, n_prompt, seq):
    m = x.shape[0]
    tab = _tab_spec(n_prompt, seq // TOKEN_TILE)
    q_spec = pl.BlockSpec((SWA_Q_HEADS, TOKEN_TILE, HEAD_DIMP), lambda i: (0, i, 0))
    return pl.pallas_call(
        _layer_b_in_kernel,
        grid=(m // TOKEN_TILE,),
        in_specs=[_rows(D_MODEL), _layer_block(g1.shape, 1), _layer_block(wgu.shape, 1),
                  _layer_block(wd.shape, 1), _layer_block(gm.shape, 1), _resident(w.shape),
                  tab, tab, tab],
        out_specs=[_rows(D_MODEL), q_spec, _rows(MEM_Q)],
        out_shape=[jax.ShapeDtypeStruct((m, D_MODEL), F32),
                   jax.ShapeDtypeStruct((SWA_Q_HEADS, m, HEAD_DIMP), BF),
                   jax.ShapeDtypeStruct((m, MEM_Q), BF)],
        compiler_params=_params("arbitrary"),
        name="layer_b_in",
    )(x, g1, wgu, wd, gm, w, *tabs)


def _layer_b_out_kernel(x_ref, tokp_ref, toks_ref, mop_ref, mos_ref, wo_ref, g2_ref, wgu_ref,
                        wd_ref, gf_ref, yp_ref, ys_ref, *, n_prompt):
    is_prompt = pl.program_id(0) < n_prompt
    nt = SWA_Q_HEADS * HEAD_DIMP
    tok = _pick(is_prompt, tokp_ref, toks_ref)
    tok = jnp.concatenate([tok[hh] for hh in range(SWA_Q_HEADS)], axis=1)
    x = (x_ref[...] + _dot(tok, wo_ref[:nt, :])
         + _dot(_pick(is_prompt, mop_ref, mos_ref), wo_ref[nt:, :]))
    y = _rms(_ffn_half(x, g2_ref, wgu_ref, wd_ref), gf_ref[...])

    @pl.when(is_prompt)
    def _():
        yp_ref[...] = y

    @pl.when(jnp.logical_not(is_prompt))
    def _():
        ys_ref[...] = y


def _layer_b_out(x, tokp, toks, mop, mos, wo, g2, wgu, wd, gf):
    m = x.shape[0]
    n_prompt = tokp.shape[1] // TOKEN_TILE
    tokp_spec = pl.BlockSpec((SWA_Q_HEADS, TOKEN_TILE, HEAD_DIMP),
                             lambda i: (0, jnp.minimum(i, n_prompt - 1), 0))
    toks_spec = pl.BlockSpec((SWA_Q_HEADS, TOKEN_TILE, HEAD_DIMP), lambda i: (0, 0, 0),
                             pipeline_mode=pl.Buffered(1))
    return pl.pallas_call(
        functools.partial(_layer_b_out_kernel, n_prompt=n_prompt),
        grid=(m // TOKEN_TILE,),
        in_specs=[_rows(D_MODEL), tokp_spec, toks_spec, _prompt_rows(MEM_Q, n_prompt),
                  _sample_rows(MEM_Q), _resident(wo.shape), _layer_block(g2.shape, 1),
                  _layer_block(wgu.shape, 1), _layer_block(wd.shape, 1), _resident(gf.shape)],
        out_specs=[_prompt_rows(D_MODEL, n_prompt),
                   pl.BlockSpec((TOKEN_TILE, D_MODEL), lambda i: (0, 0))],
        out_shape=[jax.ShapeDtypeStruct((n_prompt * TOKEN_TILE, D_MODEL), F32),
                   jax.ShapeDtypeStruct((TOKEN_TILE, D_MODEL), F32)],
        compiler_params=_params("arbitrary"),
        name="layer_b_out",
    )(x, tokp, toks, mop, mos, wo, g2, wgu, wd, gf)


def _mem_kv_kernel(x_ref, g_ref, wt_ref, k_ref, v_ref):
    x = x_ref[...]
    xn = x * lax.rsqrt(jnp.mean(x * x, axis=-1, keepdims=True) + EPS)
    for l in range(2):
        h = (xn * g_ref[l]).astype(BF)
        kvt = _dot_nt(wt_ref[l], h)
        k_ref[l, 0] = kvt[:MEM_Q, :]
        v_ref[l, 0] = kvt[MEM_Q:, :]


def _mem_kv(mem, g, wt, *, batch):
    out_spec = pl.BlockSpec((2, 1, MEM_Q, MEM_TOKENS), lambda b: (0, b, 0, 0))
    return pl.pallas_call(
        _mem_kv_kernel,
        grid=(batch,),
        in_specs=[pl.BlockSpec((MEM_TOKENS, D_MODEL), lambda b: (b, 0)), _resident(g.shape),
                  _resident(wt.shape)],
        out_specs=[out_spec, out_spec],
        out_shape=[jax.ShapeDtypeStruct((2, batch, MEM_Q, MEM_TOKENS), F32)] * 2,
        compiler_params=_params("parallel"),
        name="mem_kv",
    )(mem, g, wt)


def _gla_prompt_kernel(q_ref, k_ref, la_ref, v_ref, r_ref, gn_ref, tok_ref, st_ref, s_scr):
    c = pl.program_id(1)
    C = GLA_CHUNK

    @pl.when(c == 0)
    def _():
        s_scr[...] = jnp.zeros(s_scr.shape, F32)

    row = lax.broadcasted_iota(jnp.int32, (C, C), 0)
    col = lax.broadcasted_iota(jnp.int32, (C, C), 1)
    causal = row >= col
    ltri = jnp.where(causal, 1.0, 0.0).astype(BF)
    gn = gn_ref[...]
    for h in range(GLA_HEADS):
        sk = slice(h * GLA_DKP, (h + 1) * GLA_DKP)
        sv = slice(h * GLA_DVP, (h + 1) * GLA_DVP)
        la = la_ref[:, sk]
        hi = la.astype(BF)
        lo = (la - hi.astype(F32)).astype(BF)
        bb = _dot(ltri, jnp.concatenate([hi, lo], axis=1))
        b = bb[:, :GLA_DKP] + bb[:, GLA_DKP:]
        b_ref = b[C // 2 - 1:C // 2, :]
        b_last = b[C - 1:C, :]
        q = q_ref[:, sk]
        k = k_ref[:, sk]
        v = v_ref[:, sv].astype(BF)
        qe = (q * jnp.exp(b - b_ref)).astype(BF)
        ke = (k * jnp.exp(b_ref - b)).astype(BF)
        a = jnp.where(causal, _dot_nt(qe, ke), 0.0).astype(BF)
        st = s_scr[h]
        qb = (q * jnp.exp(b)).astype(BF)
        o = _dot(a, v) + _dot_nt(qb, st.astype(BF))
        kd = (k * jnp.exp(b_last - b)).astype(BF)
        s_scr[h] = st * jnp.exp(b_last) + _dot_tn(v, kd)
        ms = jnp.sum(o * o, axis=1, keepdims=True) * (1.0 / GLA_DV)
        on = o * lax.rsqrt(ms + EPS) * gn
        tok_ref[:, sv] = (on * _silu(r_ref[:, sv])).astype(BF)

    @pl.when(c == pl.num_programs(1) - 1)
    def _():
        st_ref[0] = s_scr[...]


def _gla_prompt(q, k, la, v, r, gn, *, batch, seq):
    m = batch * seq
    nc = seq // GLA_CHUNK
    qk = GLA_HEADS * GLA_DKP
    vr = GLA_HEADS * GLA_DVP

    def tok_map(b, c):
        return (b * nc + c, 0)

    return pl.pallas_call(
        _gla_prompt_kernel,
        grid=(batch, nc),
        in_specs=[pl.BlockSpec((GLA_CHUNK, qk), tok_map)] * 3
                 + [pl.BlockSpec((GLA_CHUNK, vr), tok_map)] * 2
                 + [_resident((1, GLA_DVP))],
        out_specs=[pl.BlockSpec((GLA_CHUNK, vr), tok_map),
                   pl.BlockSpec((1, GLA_HEADS, GLA_DVP, GLA_DKP), lambda b, c: (b, 0, 0, 0))],
        out_shape=[jax.ShapeDtypeStruct((m, vr), BF),
                   jax.ShapeDtypeStruct((batch, GLA_HEADS, GLA_DVP, GLA_DKP), F32)],
        scratch_shapes=[pltpu.VMEM((GLA_HEADS, GLA_DVP, GLA_DKP), F32)],
        compiler_params=_params("parallel", "arbitrary"),
        name="gla_prompt",
    )(q, k, la, v, r, gn)


def _mem_attn_prompt_kernel(q_ref, mk_ref, mv_ref, o_ref, kbd_scr, vbd_scr):
    tq = q_ref.shape[0]

    @pl.when(pl.program_id(1) == 0)
    def _():
        shape = (MEM_Q, MEM_HEADS * MEM_TOKENS)
        rh = lax.broadcasted_iota(jnp.int32, shape, 0) // MEM_HEAD_DIM
        ch = lax.broadcasted_iota(jnp.int32, shape, 1) // MEM_TOKENS
        diag = rh == ch
        kbd_scr[...] = jnp.where(diag, jnp.concatenate([mk_ref[0, 0]] * MEM_HEADS, axis=1),
                                 0.0).astype(BF)
        vbd_scr[...] = jnp.where(diag, jnp.concatenate([mv_ref[0, 0]] * MEM_HEADS, axis=1),
                                 0.0).astype(BF)

    s = _dot(q_ref[...], kbd_scr[...])
    ps, inv = [], []
    for h in range(MEM_HEADS):
        sh = s[:, h * MEM_TOKENS:(h + 1) * MEM_TOKENS]
        e = jnp.exp(sh - jnp.max(sh, axis=1, keepdims=True))
        inv.append(1.0 / jnp.sum(e, axis=1, keepdims=True))
        ps.append(e.astype(BF))
    o = _dot_nt(jnp.concatenate(ps, axis=1), vbd_scr[...])
    lane_h = lax.broadcasted_iota(jnp.int32, (tq, MEM_Q), 1) // MEM_HEAD_DIM
    scale = jnp.where(lane_h == 0, inv[0],
                      jnp.where(lane_h == 1, inv[1], jnp.where(lane_h == 2, inv[2], inv[3])))
    o_ref[...] = (o * scale).astype(BF)


def _mem_attn_prompt(mq, mk_t, mv_t, layer, *, batch, seq):
    tq = TOKEN_TILE
    nq = seq // tq
    kv_spec = pl.BlockSpec((1, 1, MEM_Q, MEM_TOKENS), lambda b, i: (layer, b, 0, 0))
    q_spec = pl.BlockSpec((tq, MEM_Q), lambda b, i: (b * nq + i, 0))
    bd = pltpu.VMEM((MEM_Q, MEM_HEADS * MEM_TOKENS), BF)
    return pl.pallas_call(
        _mem_attn_prompt_kernel,
        grid=(batch, nq),
        in_specs=[q_spec, kv_spec, kv_spec],
        out_specs=q_spec,
        out_shape=jax.ShapeDtypeStruct((batch * seq, MEM_Q), BF),
        scratch_shapes=[bd, bd],
        compiler_params=_params("parallel", "arbitrary"),
        name="mem_attn_prompt",
    )(mq, mk_t, mv_t)


def _swa_prompt_kernel(sink_ref, q_ref, kp_ref, kc_ref, vp_ref, vc_ref, o_ref):
    n = pl.program_id(1)
    blk = WINDOW
    qi = lax.broadcasted_iota(jnp.int32, (blk, 2 * blk), 0)
    kj = lax.broadcasted_iota(jnp.int32, (blk, 2 * blk), 1)
    d = blk + qi - kj
    first_key = jnp.where(n > 0, 0, blk)
    valid = (d >= 0) & (d < WINDOW) & (kj >= first_key)
    bias = jnp.where(valid, 0.0, -jnp.inf)
    for kh in range(SWA_KV_HEADS):
        sl = slice(kh * HEAD_DIMP, (kh + 1) * HEAD_DIMP)
        kb = jnp.concatenate([kp_ref[:, sl], kc_ref[:, sl]], axis=0).astype(BF)
        vb = jnp.concatenate([vp_ref[:, sl], vc_ref[:, sl]], axis=0).astype(BF)
        for g in range(SWA_GROUP):
            hh = kh * SWA_GROUP + g
            s = _dot_nt(q_ref[hh], kb) + bias
            sink = sink_ref[hh]
            m = jnp.maximum(jnp.max(s, axis=1, keepdims=True), sink)
            e = jnp.exp(s - m)
            l = jnp.sum(e, axis=1, keepdims=True) + jnp.exp(sink - m)
            o_ref[hh] = (_dot(e.astype(BF), vb) * (1.0 / l)).astype(BF)


def _swa_prompt(sinks, q, k, v, *, batch, seq):
    nb = seq // WINDOW
    kw = SWA_KV_HEADS * HEAD_DIMP
    q_spec = pl.BlockSpec((SWA_Q_HEADS, WINDOW, HEAD_DIMP), lambda b, n: (0, b * nb + n, 0))
    prev = pl.BlockSpec((WINDOW, kw), lambda b, n: (b * nb + jnp.maximum(n - 1, 0), 0))
    cur = pl.BlockSpec((WINDOW, kw), lambda b, n: (b * nb + n, 0))
    return pl.pallas_call(
        _swa_prompt_kernel,
        grid=(batch, nb),
        in_specs=[pl.BlockSpec(memory_space=pltpu.SMEM), q_spec, prev, cur, prev, cur],
        out_specs=q_spec,
        out_shape=jax.ShapeDtypeStruct((SWA_Q_HEADS, batch * seq, HEAD_DIMP), BF),
        compiler_params=_params("parallel", "parallel"),
        name="swa_prompt",
    )(sinks, q, k, k, v, v)


_SAMPLE_BB = 8
_DEC_SEQ = 4
_DEC_BATCH = 128
_GLA_DK_BLK = 32


def _gla_sample_kernel(s_ref, q_ref, k_ref, la_ref, v_ref, r_ref, gn_ref, so_ref, tok_ref, o_scr):
    j = pl.program_id(1)

    @pl.when(j == 0)
    def _():
        o_scr[...] = jnp.zeros(o_scr.shape, F32)

    def body(dk, carry):
        s = s_ref[0, 0, dk]
        for t in range(_DEC_SEQ):
            a = jnp.exp(la_ref[t, 0, pl.ds(dk, 1), :])
            s = a * s + k_ref[t, 0, pl.ds(dk, 1), :] * v_ref[t, 0, :GLA_DV, :]
            o_scr[t] = o_scr[t] + q_ref[t, 0, pl.ds(dk, 1), :] * s
        so_ref[0, 0, dk] = s
        return carry

    lax.fori_loop(0, _GLA_DK_BLK, body, 0)

    @pl.when(j == pl.num_programs(1) - 1)
    def _():
        tok_ref[...] = jnp.zeros(tok_ref.shape, F32)
        for t in range(_DEC_SEQ):
            o = o_scr[t]
            ms = jnp.sum(o * o, axis=0, keepdims=True) * (1.0 / GLA_DV)
            on = o * lax.rsqrt(ms + EPS) * gn_ref[...]
            tok_ref[t, 0, :GLA_DV, :] = on * _silu(r_ref[t, 0, :GLA_DV, :])


def _gla_sample(state, q, k, la, v, r, gn):
    qk_spec = pl.BlockSpec((_DEC_SEQ, 1, _GLA_DK_BLK, _DEC_BATCH), lambda h, j: (0, h, j, 0))
    vr_spec = pl.BlockSpec((_DEC_SEQ, 1, GLA_DVP, _DEC_BATCH), lambda h, j: (0, h, 0, 0))
    s_spec = pl.BlockSpec((1, 1, _GLA_DK_BLK, GLA_DV, _DEC_BATCH), lambda h, j: (0, h, j, 0, 0))
    return pl.pallas_call(
        _gla_sample_kernel,
        grid=(GLA_HEADS, GLA_DK // _GLA_DK_BLK),
        in_specs=[s_spec, qk_spec, qk_spec, qk_spec, vr_spec, vr_spec,
                  _resident((GLA_DV, _DEC_BATCH))],
        out_specs=[s_spec, vr_spec],
        out_shape=[jax.ShapeDtypeStruct(state.shape, F32),
                   jax.ShapeDtypeStruct(v.shape, F32)],
        scratch_shapes=[pltpu.VMEM((_DEC_SEQ, GLA_DV, _DEC_BATCH), F32)],
        compiler_params=_params("parallel", "arbitrary"),
        name="gla_sample",
    )(state, q, k, la, v, r, gn)


def _mem_attn_sample_kernel(q_ref, mk_ref, mv_ref, o_ref):
    for bi in range(_SAMPLE_BB):
        s = _dot(q_ref[bi], mk_ref[0, bi].astype(BF))
        e = jnp.exp(s - jnp.max(s, axis=1, keepdims=True))
        p = (e * (1.0 / jnp.sum(e, axis=1, keepdims=True))).astype(BF)
        o_ref[bi] = _dot_nt(p, mv_ref[0, bi].astype(BF))


def _mem_attn_sample(qbd, mk_t, mv_t, layer):
    nb = qbd.shape[0]
    nr = MEM_HEADS * _DEC_SEQ
    kv_spec = pl.BlockSpec((1, _SAMPLE_BB, MEM_Q, MEM_TOKENS), lambda i: (layer, i, 0, 0))
    q_spec = pl.BlockSpec((_SAMPLE_BB, nr, MEM_Q), lambda i: (i, 0, 0))
    return pl.pallas_call(
        _mem_attn_sample_kernel,
        grid=(nb // _SAMPLE_BB,),
        in_specs=[q_spec, kv_spec, kv_spec],
        out_specs=q_spec,
        out_shape=jax.ShapeDtypeStruct((nb, nr, MEM_Q), F32),
        compiler_params=_params("parallel"),
        name="mem_attn_sample",
    )(qbd, mk_t, mv_t)


def _swa_sample_kernel(q_ref, sink_ref, kc_ref, vc_ref, kn_ref, vn_ref, o_ref, ko_ref, vo_ref):
    nq = SWA_Q_HEADS * _DEC_SEQ
    t = lax.broadcasted_iota(jnp.int32, (nq, WINDOW), 0) % _DEC_SEQ
    pos = lax.broadcasted_iota(jnp.int32, (nq, WINDOW), 1)
    new0 = WINDOW - _DEC_SEQ
    bias_c = jnp.where(pos > t, 0.0, -jnp.inf)
    bias_n = jnp.where((pos >= new0) & (pos - new0 <= t), 0.0, -jnp.inf)
    is_new = lax.broadcasted_iota(jnp.int32, (SWA_KV_HEADS * HEAD_DIM, WINDOW), 1) >= new0
    sink = sink_ref[...]
    for bi in range(_SAMPLE_BB):
        q = q_ref[bi]
        kc, vc, kn, vn = kc_ref[bi], vc_ref[bi], kn_ref[bi], vn_ref[bi]
        sc = _dot(q, kc.astype(BF)) + bias_c
        sn = _dot(q, kn.astype(BF)) + bias_n
        m = jnp.maximum(jnp.maximum(jnp.max(sc, axis=1, keepdims=True),
                                    jnp.max(sn, axis=1, keepdims=True)), sink)
        ec = jnp.exp(sc - m)
        en = jnp.exp(sn - m)
        l = (jnp.sum(ec, axis=1, keepdims=True) + jnp.sum(en, axis=1, keepdims=True)
             + jnp.exp(sink - m))
        inv = 1.0 / l
        o_ref[bi] = (_dot_nt((ec * inv).astype(BF), vc.astype(BF))
                     + _dot_nt((en * inv).astype(BF), vn.astype(BF)))
        ko_ref[bi] = jnp.where(is_new, kn, pltpu.roll(kc, new0, 1))
        vo_ref[bi] = jnp.where(is_new, vn, pltpu.roll(vc, new0, 1))


def _swa_sample(qbd, sink_col, kc, vc, kn, vn):
    nb = qbd.shape[0]
    kw = SWA_KV_HEADS * HEAD_DIM
    nq = SWA_Q_HEADS * _DEC_SEQ
    kv_spec = pl.BlockSpec((_SAMPLE_BB, kw, WINDOW), lambda i: (i, 0, 0))
    q_spec = pl.BlockSpec((_SAMPLE_BB, nq, kw), lambda i: (i, 0, 0))
    return pl.pallas_call(
        _swa_sample_kernel,
        grid=(nb // _SAMPLE_BB,),
        in_specs=[q_spec, _resident((nq, 1)), kv_spec, kv_spec, kv_spec, kv_spec],
        out_specs=[q_spec, kv_spec, kv_spec],
        out_shape=[jax.ShapeDtypeStruct((nb, nq, kw), F32),
                   jax.ShapeDtypeStruct(kc.shape, F32), jax.ShapeDtypeStruct(kc.shape, F32)],
        compiler_params=_params("parallel"),
        name="swa_sample",
    )(qbd, sink_col, kc, vc, kn, vn)


def _pad_heads(w, heads, dim, dim_p, axis):
    shape = w.shape
    w = w.reshape(shape[:axis] + (heads, dim) + shape[axis + 1:])
    pad = [(0, 0)] * w.ndim
    pad[axis + 1] = (0, dim_p - dim)
    w = jnp.pad(w, pad)
    return w.reshape(shape[:axis] + (heads * dim_p,) + shape[axis + 1:])


def _rope_tables(pos):
    half = ROT_DIM // 2
    inv_freq = jnp.exp(-math.log(ROPE_THETA) * jnp.arange(0, ROT_DIM, 2, dtype=F32) / ROT_DIM)
    ang = pos[:, None] * inv_freq[None, :]
    cos, sin = jnp.cos(ang), jnp.sin(ang)
    n = pos.shape[0]
    rest = HEAD_DIMP - ROT_DIM
    c = jnp.concatenate([cos, cos, jnp.ones((n, rest), F32)], axis=1)
    s1 = jnp.concatenate([-sin, jnp.zeros((n, HEAD_DIMP - half), F32)], axis=1)
    s2 = jnp.concatenate([jnp.zeros((n, half), F32), sin, jnp.zeros((n, rest), F32)], axis=1)
    return c, s1, s2


def _prep_weights(p):
    w = {}
    for name in ("ffn1_w_gu", "ffn1_w_down", "ffn2_w_gu", "ffn2_w_down"):
        w[name] = p[name].astype(BF)
    for name in ("ffn1_norm", "ffn2_norm", "mix_norm"):
        w[name] = p[name][:, None, :]
    qk = GLA_HEADS * GLA_DK
    vv = GLA_HEADS * GLA_DV
    a_in = p["a_w_in"][0]
    o = 0
    wq = _pad_heads(a_in[:, o:o + qk], GLA_HEADS, GLA_DK, GLA_DKP, 1); o += qk
    wk = _pad_heads(a_in[:, o:o + qk], GLA_HEADS, GLA_DK, GLA_DKP, 1); o += qk
    wv = _pad_heads(a_in[:, o:o + vv], GLA_HEADS, GLA_DV, GLA_DVP, 1); o += vv
    wr = _pad_heads(a_in[:, o:o + vv], GLA_HEADS, GLA_DV, GLA_DVP, 1); o += vv
    wg = jnp.pad(a_in[:, o:o + GLA_RANK], ((0, 0), (0, GLA_RANKP - GLA_RANK))); o += GLA_RANK
    wm = a_in[:, o:]
    w["a_in"] = jnp.concatenate([wq, wk, wv, wr, wg, wm], axis=1).astype(BF)
    gate = _pad_heads(p["a_w_gate"][0], GLA_HEADS, GLA_DK, GLA_DKP, 1)
    w["a_gate"] = jnp.pad(gate, ((0, GLA_RANKP - GLA_RANK), (0, 0))).astype(BF)
    w["a_bgate"] = _pad_heads(p["a_b_gate"][0][None, :], GLA_HEADS, GLA_DK, GLA_DKP, 1)
    w["a_gn"] = jnp.pad(p["a_out_norm"][0], (0, GLA_DVP - GLA_DV))[None, :]
    a_out = p["a_w_out"][0]
    w["a_out"] = jnp.concatenate(
        [_pad_heads(a_out[:vv], GLA_HEADS, GLA_DV, GLA_DVP, 0), a_out[vv:]], axis=0).astype(BF)
    nq = SWA_Q_HEADS * HEAD_DIM
    b_in = p["b_w_in"][0]
    w["b_in"] = jnp.concatenate(
        [_pad_heads(b_in[:, :nq], SWA_Q_HEADS, HEAD_DIM, HEAD_DIMP, 1), b_in[:, nq:]],
        axis=1).astype(BF)
    b_out = p["b_w_out"][0]
    w["b_out"] = jnp.concatenate(
        [_pad_heads(b_out[:nq], SWA_Q_HEADS, HEAD_DIM, HEAD_DIMP, 0), b_out[nq:]],
        axis=0).astype(BF)
    nkv = SWA_KV_HEADS * HEAD_DIM
    w_kv = p["w_kv"]
    w["kv"] = jnp.concatenate(
        [_pad_heads(w_kv[:, :nkv], SWA_KV_HEADS, HEAD_DIM, HEAD_DIMP, 1),
         _pad_heads(w_kv[:, nkv:], SWA_KV_HEADS, HEAD_DIM, HEAD_DIMP, 1)], axis=1).astype(BF)
    w["mem_t"] = p["mem_w_kv"].transpose(0, 2, 1).astype(BF)
    return w


def _compact_kv(a, batch, seq):
    return a.reshape(batch, seq, SWA_KV_HEADS, HEAD_DIMP)[..., :HEAD_DIM]


def kernel(x_prompt, x_sample, state_gla, cache_swa_k, cache_swa_v, cache_mem_k, cache_mem_v,
           mem_prompt, ffn1_norm, ffn1_w_gu, ffn1_w_down, mix_norm, ffn2_norm, ffn2_w_gu,
           ffn2_w_down, mem_norm, mem_w_kv, a_w_in, a_w_gate, a_b_gate, a_out_norm, a_w_out,
           kv_norm, w_kv, b_w_in, b_sinks, b_w_out, final_norm):
    p = dict(ffn1_norm=ffn1_norm, ffn1_w_gu=ffn1_w_gu, ffn1_w_down=ffn1_w_down,
             mix_norm=mix_norm, ffn2_norm=ffn2_norm, ffn2_w_gu=ffn2_w_gu,
             ffn2_w_down=ffn2_w_down, mem_w_kv=mem_w_kv, a_w_in=a_w_in, a_w_gate=a_w_gate,
             a_b_gate=a_b_gate, a_out_norm=a_out_norm, a_w_out=a_w_out, w_kv=w_kv,
             b_w_in=b_w_in, b_w_out=b_w_out)
    w = _prep_weights(p)
    batch, seq, _ = x_prompt.shape
    nb, t, _ = x_sample.shape
    assert nb == _DEC_BATCH and t == _DEC_SEQ and nb * t == TOKEN_TILE
    assert seq % TOKEN_TILE == 0 and seq % GLA_CHUNK == 0
    mp = batch * seq
    kw = SWA_KV_HEADS * HEAD_DIM
    sinks = b_sinks[0]

    mem_k_t, mem_v_t = _mem_kv(mem_prompt.reshape(batch * MEM_TOKENS, D_MODEL), mem_norm[:, None, :],
                               w["mem_t"], batch=batch)
    state_t = state_gla.transpose(0, 2, 3, 4, 1)
    kc_t = cache_swa_k.transpose(0, 2, 3, 1).reshape(nb, kw, WINDOW)
    vc_t = cache_swa_v.transpose(0, 2, 3, 1).reshape(nb, kw, WINDOW)
    cmk_t = cache_mem_k.transpose(0, 1, 3, 4, 2).reshape(2, nb, MEM_Q, MEM_TOKENS)
    cmv_t = cache_mem_v.transpose(0, 1, 3, 4, 2).reshape(2, nb, MEM_Q, MEM_TOKENS)

    tabs = tuple(
        jnp.concatenate([a, jnp.tile(b, (nb, 1))], axis=0)
        for a, b in zip(_rope_tables(jnp.arange(seq, dtype=F32)),
                        _rope_tables(PAST_LEN + jnp.arange(t, dtype=F32))))

    mem_mask = (jnp.arange(MEM_Q) // MEM_HEAD_DIM)[None, :] == jnp.arange(MEM_HEADS)[:, None]
    kv_mask = (jnp.arange(kw) // HEAD_DIM)[None, :] == jnp.arange(SWA_KV_HEADS)[:, None]

    def mem_attn_sample(mq, layer):
        q4 = mq[mp:].reshape(nb, 1, t, MEM_Q)
        qbd = jnp.where(mem_mask[None, :, None, :], q4, 0).reshape(nb, MEM_HEADS * t, MEM_Q)
        o = _mem_attn_sample(qbd, cmk_t, cmv_t, layer).reshape(nb, MEM_HEADS, t, MEM_Q)
        o = jnp.sum(jnp.where(mem_mask[None, :, None, :], o, 0.0), axis=1)
        return o.reshape(nb * t, MEM_Q).astype(BF)

    def lanes(a, width):
        return a[mp:].reshape(nb, t, GLA_HEADS, width).transpose(1, 2, 3, 0)

    x, q, k, la, v, r, mq = _layer_a_in(
        x_prompt.reshape(mp, D_MODEL), x_sample.reshape(nb * t, D_MODEL), w["ffn1_norm"],
        w["ffn1_w_gu"], w["ffn1_w_down"], w["mix_norm"], w["a_in"], w["a_gate"], w["a_bgate"])
    tok_p, st_p = _gla_prompt(q, k, la, v, r, w["a_gn"], batch=batch, seq=seq)
    mo_p = _mem_attn_prompt(mq, mem_k_t, mem_v_t, 0, batch=batch, seq=seq)
    gn = jnp.broadcast_to(a_out_norm[0][:, None], (GLA_DV, nb))
    st_s, tok_s = _gla_sample(state_t, lanes(q, GLA_DKP), lanes(k, GLA_DKP), lanes(la, GLA_DKP),
                              lanes(v, GLA_DVP), lanes(r, GLA_DVP), gn)
    tok_s = tok_s.transpose(3, 0, 1, 2).reshape(nb * t, GLA_HEADS * GLA_DVP).astype(BF)
    mo_s = mem_attn_sample(mq, 0)
    x, k_sh, v_sh = _layer_a_out(x, tok_p, tok_s, mo_p, mo_s, w["a_out"], w["ffn2_norm"],
                                 w["ffn2_w_gu"], w["ffn2_w_down"], kv_norm[None, :], w["kv"],
                                 tabs, seq=seq)

    x, qs, mq = _layer_b_in(x, w["ffn1_norm"], w["ffn1_w_gu"], w["ffn1_w_down"], w["mix_norm"],
                            w["b_in"], tabs, n_prompt=mp // TOKEN_TILE, seq=seq)
    tok_p = _swa_prompt(sinks, qs, k_sh, v_sh, batch=batch, seq=seq)
    mo_p = _mem_attn_prompt(mq, mem_k_t, mem_v_t, 1, batch=batch, seq=seq)

    def new_rows(a):
        a = _compact_kv(a[mp:], nb, t).reshape(nb, t, kw).transpose(0, 2, 1)
        return jnp.pad(a, ((0, 0), (0, 0), (WINDOW - t, 0)))

    q5 = qs[:, mp:, :HEAD_DIM].reshape(SWA_KV_HEADS, SWA_GROUP, nb, t, HEAD_DIM)
    q5 = q5.transpose(2, 0, 1, 3, 4)
    qbd = jnp.where(kv_mask[None, :, None, None, :], jnp.tile(q5, (1, 1, 1, 1, SWA_KV_HEADS)), 0)
    qbd = qbd.reshape(nb, SWA_Q_HEADS * t, kw)
    o, k_s, v_s = _swa_sample(qbd, jnp.repeat(sinks, t)[:, None], kc_t, vc_t,
                              new_rows(k_sh), new_rows(v_sh))
    o = o.reshape(nb, SWA_KV_HEADS, SWA_GROUP, t, kw)
    o = jnp.where(kv_mask[None, :, None, None, :], o, 0.0)
    o = o.reshape(nb, SWA_KV_HEADS, SWA_GROUP, t, SWA_KV_HEADS, HEAD_DIM).sum(axis=4)
    tok_s = o.transpose(1, 2, 0, 3, 4).reshape(SWA_Q_HEADS, nb * t, HEAD_DIM)
    tok_s = jnp.pad(tok_s, ((0, 0), (0, 0), (0, HEAD_DIMP - HEAD_DIM))).astype(BF)
    mo_s = mem_attn_sample(mq, 1)
    y_p, y_s = _layer_b_out(x, tok_p, tok_s, mo_p, mo_s, w["b_out"], w["ffn2_norm"],
                            w["ffn2_w_gu"], w["ffn2_w_down"], final_norm[None, :])

    gla_prompt = st_p.transpose(0, 1, 3, 2)[None, :, :, :GLA_DK, :GLA_DV]
    gla_sample = st_s.transpose(0, 4, 1, 2, 3)
    swa_k_prompt = _compact_kv(k_sh[:mp], batch, seq)[:, seq - WINDOW:]
    swa_v_prompt = _compact_kv(v_sh[:mp], batch, seq)[:, seq - WINDOW:]
    swa_k_sample = k_s.reshape(nb, SWA_KV_HEADS, HEAD_DIM, WINDOW).transpose(0, 3, 1, 2)
    swa_v_sample = v_s.reshape(nb, SWA_KV_HEADS, HEAD_DIM, WINDOW).transpose(0, 3, 1, 2)
    mem_shape = (2, batch, MEM_HEADS, MEM_HEAD_DIM, MEM_TOKENS)
    mem_k_prompt = mem_k_t.reshape(mem_shape).transpose(0, 1, 4, 2, 3)
    mem_v_prompt = mem_v_t.reshape(mem_shape).transpose(0, 1, 4, 2, 3)
    return (y_p.reshape(batch, seq, D_MODEL), y_s.reshape(nb, t, D_MODEL), gla_prompt,
            gla_sample, swa_k_prompt, swa_v_prompt, swa_k_sample, swa_v_sample,
            mem_k_prompt, mem_v_prompt)
```

```python
import functools
import math

import jax
import jax.numpy as jnp
from jax import lax
from jax.experimental import pallas as pl
from jax.experimental.pallas import tpu as pltpu

F32 = jnp.float32
BF = jnp.bfloat16

D_MODEL = 1024
FFN_DIM = 2816
EPS = 1e-6

GLA_HEADS = 4
GLA_DK = 96
GLA_DV = 192
GLA_DKP = 128
GLA_DVP = 256
GLA_RANK = 16
GLA_RANKP = 128
GLA_GATE_NORM = 16.0
GLA_CHUNK = 256

HEAD_DIM = 64
HEAD_DIMP = 128
SWA_Q_HEADS = 12
SWA_KV_HEADS = 3
SWA_GROUP = SWA_Q_HEADS // SWA_KV_HEADS
WINDOW = 128
ROT_DIM = 16
ROPE_THETA = 500000.0
PAST_LEN = 8192

MEM_TOKENS = 256
MEM_HEADS = 4
MEM_HEAD_DIM = 64
MEM_Q = MEM_HEADS * MEM_HEAD_DIM

FFN_TF = 256
FFN_CHUNKS = FFN_DIM // FFN_TF
TOKEN_TILE = 512

VMEM_LIMIT = 60 * 1024 * 1024


def _params(*sem):
    return pltpu.CompilerParams(dimension_semantics=sem, vmem_limit_bytes=VMEM_LIMIT)


def _resident(shape):
    nd = len(shape)
    return pl.BlockSpec(shape, lambda *_: (0,) * nd, pipeline_mode=pl.Buffered(1))


def _layer_block(shape, layer):
    nd = len(shape)
    return pl.BlockSpec((1,) + tuple(shape[1:]), lambda *_: (layer,) + (0,) * (nd - 1),
                        pipeline_mode=pl.Buffered(1))


def _rows(width):
    return pl.BlockSpec((TOKEN_TILE, width), lambda i: (i, 0))


def _prompt_rows(width, n_prompt):
    return pl.BlockSpec((TOKEN_TILE, width), lambda i: (jnp.minimum(i, n_prompt - 1), 0))


def _sample_rows(width):
    return pl.BlockSpec((TOKEN_TILE, width), lambda i: (0, 0), pipeline_mode=pl.Buffered(1))


def _rms(x, g):
    ms = jnp.mean(x * x, axis=-1, keepdims=True)
    return x * lax.rsqrt(ms + EPS) * g


def _silu(x):
    return x * (1.0 / (1.0 + jnp.exp(-x)))


def _dot(a, b):
    return jnp.dot(a, b, preferred_element_type=F32)


def _dot_nt(a, b):
    return lax.dot_general(a, b, (((1,), (1,)), ((), ())), preferred_element_type=F32)


def _dot_tn(a, b):
    return lax.dot_general(a, b, (((0,), (0,)), ((), ())), preferred_element_type=F32)


def _pick(is_prompt, p_ref, s_ref):
    return jnp.where(is_prompt, p_ref[...], s_ref[...])


def _ffn_half(x, g_ref, wgu_ref, wd_ref):
    h = _rms(x, g_ref[0]).astype(BF)
    acc = jnp.zeros(x.shape, F32)
    for c in range(FFN_CHUNKS):
        lo, hi = c * FFN_TF, (c + 1) * FFN_TF
        gate = _dot(h, wgu_ref[0, :, lo:hi])
        up = _dot(h, wgu_ref[0, :, FFN_DIM + lo:FFN_DIM + hi])
        a = (_silu(gate) * up).astype(BF)
        acc = acc + _dot(a, wd_ref[0, lo:hi, :])
    return x + 0.5 * acc


def _rope(x, c, s1, s2):
    return (x * c + pltpu.roll(x, HEAD_DIMP - ROT_DIM // 2, 1) * s1
            + pltpu.roll(x, ROT_DIM // 2, 1) * s2)


_A_Q = 0
_A_K = _A_Q + GLA_HEADS * GLA_DKP
_A_V = _A_K + GLA_HEADS * GLA_DKP
_A_R = _A_V + GLA_HEADS * GLA_DVP
_A_G = _A_R + GLA_HEADS * GLA_DVP
_A_M = _A_G + GLA_RANKP
_A_END = _A_M + MEM_Q


def _layer_a_in_kernel(xp_ref, xs_ref, g1_ref, wgu_ref, wd_ref, gm_ref, w_ref, wg_ref, bg_ref,
                       x_ref, q_ref, k_ref, la_ref, v_ref, r_ref, mq_ref, *, n_prompt):
    is_prompt = pl.program_id(0) < n_prompt
    x = _ffn_half(_pick(is_prompt, xp_ref, xs_ref), g1_ref, wgu_ref, wd_ref)
    x_ref[...] = x
    h = _rms(x, gm_ref[0]).astype(BF)

    proj = _dot(h, w_ref[...])

    def mm(a, b):
        return proj[:, a:b]

    q_ref[...] = mm(_A_Q, _A_K) * (GLA_DK ** -0.5)
    k_ref[...] = mm(_A_K, _A_V)
    v_ref[...] = mm(_A_V, _A_R)
    r_ref[...] = mm(_A_R, _A_G)
    z = _dot(mm(_A_G, _A_M).astype(BF), wg_ref[...]) + bg_ref[...]
    la_ref[...] = (jnp.minimum(z, 0.0) - jnp.log1p(jnp.exp(-jnp.abs(z)))) * (1.0 / GLA_GATE_NORM)
    mq_ref[...] = (mm(_A_M, _A_END) * (MEM_HEAD_DIM ** -0.5)).astype(BF)


def _layer_a_in(xp, xs, g1, wgu, wd, gm, w, wg, bg):
    n_prompt = xp.shape[0] // TOKEN_TILE
    m = xp.shape[0] + xs.shape[0]
    qk = GLA_HEADS * GLA_DKP
    vr = GLA_HEADS * GLA_DVP
    outs = ((D_MODEL, F32), (qk, F32), (qk, F32), (qk, F32), (vr, F32), (vr, F32), (MEM_Q, BF))
    return pl.pallas_call(
        functools.partial(_layer_a_in_kernel, n_prompt=n_prompt),
        grid=(m // TOKEN_TILE,),
        in_specs=[_prompt_rows(D_MODEL, n_prompt), _sample_rows(D_MODEL),
                  _layer_block(g1.shape, 0), _layer_block(wgu.shape, 0), _layer_block(wd.shape, 0),
                  _layer_block(gm.shape, 0), _resident(w.shape), _resident(wg.shape),
                  _resident(bg.shape)],
        out_specs=[_rows(n) for n, _ in outs],
        out_shape=[jax.ShapeDtypeStruct((m, n), dt) for n, dt in outs],
        compiler_params=_params("arbitrary"),
        name="layer_a_in",
    )(xp, xs, g1, wgu, wd, gm, w, wg, bg)


def _layer_a_out_kernel(x_ref, tokp_ref, toks_ref, mop_ref, mos_ref, wo_ref, g2_ref, wgu_ref,
                        wd_ref, gkv_ref, wkv_ref, c_ref, s1_ref, s2_ref,
                        xo_ref, k_ref, v_ref, *, n_prompt):
    is_prompt = pl.program_id(0) < n_prompt
    nt = GLA_HEADS * GLA_DVP
    x = (x_ref[...] + _dot(_pick(is_prompt, tokp_ref, toks_ref), wo_ref[:nt, :])
         + _dot(_pick(is_prompt, mop_ref, mos_ref), wo_ref[nt:, :]))
    x = _ffn_half(x, g2_ref, wgu_ref, wd_ref)
    xo_ref[...] = x
    h = _rms(x, gkv_ref[...]).astype(BF)
    c, s1, s2 = c_ref[...], s1_ref[...], s2_ref[...]
    kw = SWA_KV_HEADS * HEAD_DIMP
    kv = _dot(h, wkv_ref[...])
    for hh in range(SWA_KV_HEADS):
        sl = slice(hh * HEAD_DIMP, (hh + 1) * HEAD_DIMP)
        k_ref[:, sl] = _rope(kv[:, sl], c, s1, s2)
    v_ref[...] = kv[:, kw:]


def _tab_spec(n_prompt, blocks_per_seq):
    return pl.BlockSpec((TOKEN_TILE, HEAD_DIMP),
                        lambda i: (jnp.where(i < n_prompt, i % blocks_per_seq, blocks_per_seq), 0))


def _layer_a_out(x, tokp, toks, mop, mos, wo, g2, wgu, wd, gkv, wkv, tabs, *, seq):
    m = x.shape[0]
    n_prompt = tokp.shape[0] // TOKEN_TILE
    kw = SWA_KV_HEADS * HEAD_DIMP
    nt = GLA_HEADS * GLA_DVP
    tab = _tab_spec(n_prompt, seq // TOKEN_TILE)
    return pl.pallas_call(
        functools.partial(_layer_a_out_kernel, n_prompt=n_prompt),
        grid=(m // TOKEN_TILE,),
        in_specs=[_rows(D_MODEL), _prompt_rows(nt, n_prompt), _sample_rows(nt),
                  _prompt_rows(MEM_Q, n_prompt), _sample_rows(MEM_Q), _resident(wo.shape),
                  _layer_block(g2.shape, 0), _layer_block(wgu.shape, 0), _layer_block(wd.shape, 0),
                  _resident(gkv.shape), _resident(wkv.shape), tab, tab, tab],
        out_specs=[_rows(D_MODEL), _rows(kw), _rows(kw)],
        out_shape=[jax.ShapeDtypeStruct((m, D_MODEL), F32),
                   jax.ShapeDtypeStruct((m, kw), F32), jax.ShapeDtypeStruct((m, kw), F32)],
        compiler_params=_params("arbitrary"),
        name="layer_a_out",
    )(x, tokp, toks, mop, mos, wo, g2, wgu, wd, gkv, wkv, *tabs)


def _layer_b_in_kernel(x_ref, g1_ref, wgu_ref, wd_ref, gm_ref, w_ref, c_ref, s1_ref, s2_ref,
                       xo_ref, q_ref, mq_ref):
    x = _ffn_half(x_ref[...], g1_ref, wgu_ref, wd_ref)
    xo_ref[...] = x
    h = _rms(x, gm_ref[0]).astype(BF)
    c, s1, s2 = c_ref[...], s1_ref[...], s2_ref[...]
    nq = SWA_Q_HEADS * HEAD_DIMP
    qm = _dot(h, w_ref[...])
    for hh in range(SWA_Q_HEADS):
        q = qm[:, hh * HEAD_DIMP:(hh + 1) * HEAD_DIMP]
        q_ref[hh] = (_rope(q, c, s1, s2) * (HEAD_DIM ** -0.5)).astype(BF)
    mq_ref[...] = (qm[:, nq:] * (MEM_HEAD_DIM ** -0.5)).astype(BF)


def _layer_b_in(x, g1, wgu, wd, gm, w, tabs, *, n_prompt, seq):
    m = x.shape[0]
    tab = _tab_spec(n_prompt, seq // TOKEN_TILE)
    q_spec = pl.BlockSpec((SWA_Q_HEADS, TOKEN_TILE, HEAD_DIMP), lambda i: (0, i, 0))
    return pl.pallas_call(
        _layer_b_in_kernel,
        grid=(m // TOKEN_TILE,),
        in_specs=[_rows(D_MODEL), _layer_block(g1.shape, 1), _layer_block(wgu.shape, 1),
                  _layer_block(wd.shape, 1), _layer_block(gm.shape, 1), _resident(w.shape),
                  tab, tab, tab],
        out_specs=[_rows(D_MODEL), q_spec, _rows(MEM_Q)],
        out_shape=[jax.ShapeDtypeStruct((m, D_MODEL), F32),
                   jax.ShapeDtypeStruct((SWA_Q_HEADS, m, HEAD_DIMP), BF),
                   jax.ShapeDtypeStruct((m, MEM_Q), BF)],
        compiler_params=_params("arbitrary"),
        name="layer_b_in",
    )(x, g1, wgu, wd, gm, w, *tabs)


def _layer_b_out_kernel(x_ref, tokp_ref, toks_ref, mop_ref, mos_ref, wo_ref, g2_ref, wgu_ref,
                        wd_ref, gf_ref, yp_ref, ys_ref, *, n_prompt):
    is_prompt = pl.program_id(0) < n_prompt
    nt = SWA_Q_HEADS * HEAD_DIMP
    tok = _pick(is_prompt, tokp_ref, toks_ref)
    tok = jnp.concatenate([tok[hh] for hh in range(SWA_Q_HEADS)], axis=1)
    x = (x_ref[...] + _dot(tok, wo_ref[:nt, :])
         + _dot(_pick(is_prompt, mop_ref, mos_ref), wo_ref[nt:, :]))
    y = _rms(_ffn_half(x, g2_ref, wgu_ref, wd_ref), gf_ref[...])

    @pl.when(is_prompt)
    def _():
        yp_ref[...] = y

    @pl.when(jnp.logical_not(is_prompt))
    def _():
        ys_ref[...] = y


def _layer_b_out(x, tokp, toks, mop, mos, wo, g2, wgu, wd, gf):
    m = x.shape[0]
    n_prompt = tokp.shape[1] // TOKEN_TILE
    tokp_spec = pl.BlockSpec((SWA_Q_HEADS, TOKEN_TILE, HEAD_DIMP),
                             lambda i: (0, jnp.minimum(i, n_prompt - 1), 0))
    toks_spec = pl.BlockSpec((SWA_Q_HEADS, TOKEN_TILE, HEAD_DIMP), lambda i: (0, 0, 0),
                             pipeline_mode=pl.Buffered(1))
    return pl.pallas_call(
        functools.partial(_layer_b_out_kernel, n_prompt=n_prompt),
        grid=(m // TOKEN_TILE,),
        in_specs=[_rows(D_MODEL), tokp_spec, toks_spec, _prompt_rows(MEM_Q, n_prompt),
                  _sample_rows(MEM_Q), _resident(wo.shape), _layer_block(g2.shape, 1),
                  _layer_block(wgu.shape, 1), _layer_block(wd.shape, 1), _resident(gf.shape)],
        out_specs=[_prompt_rows(D_MODEL, n_prompt),
                   pl.BlockSpec((TOKEN_TILE, D_MODEL), lambda i: (0, 0))],
        out_shape=[jax.ShapeDtypeStruct((n_prompt * TOKEN_TILE, D_MODEL), F32),
                   jax.ShapeDtypeStruct((TOKEN_TILE, D_MODEL), F32)],
        compiler_params=_params("arbitrary"),
        name="layer_b_out",
    )(x, tokp, toks, mop, mos, wo, g2, wgu, wd, gf)


def _mem_kv_kernel(x_ref, g_ref, wt_ref, k_ref, v_ref):
    x = x_ref[...]
    xn = x * lax.rsqrt(jnp.mean(x * x, axis=-1, keepdims=True) + EPS)
    for l in range(2):
        h = (xn * g_ref[l]).astype(BF)
        kvt = _dot_nt(wt_ref[l], h)
        k_ref[l, 0] = kvt[:MEM_Q, :]
        v_ref[l, 0] = kvt[MEM_Q:, :]


def _mem_kv(mem, g, wt, *, batch):
    out_spec = pl.BlockSpec((2, 1, MEM_Q, MEM_TOKENS), lambda b: (0, b, 0, 0))
    return pl.pallas_call(
        _mem_kv_kernel,
        grid=(batch,),
        in_specs=[pl.BlockSpec((MEM_TOKENS, D_MODEL), lambda b: (b, 0)), _resident(g.shape),
                  _resident(wt.shape)],
        out_specs=[out_spec, out_spec],
        out_shape=[jax.ShapeDtypeStruct((2, batch, MEM_Q, MEM_TOKENS), F32)] * 2,
        compiler_params=_params("parallel"),
        name="mem_kv",
    )(mem, g, wt)


def _gla_prompt_kernel(q_ref, k_ref, la_ref, v_ref, r_ref, gn_ref, tok_ref, st_ref, s_scr):
    c = pl.program_id(1)
    C = GLA_CHUNK

    @pl.when(c == 0)
    def _():
        s_scr[...] = jnp.zeros(s_scr.shape, F32)

    row = lax.broadcasted_iota(jnp.int32, (C, C), 0)
    col = lax.broadcasted_iota(jnp.int32, (C, C), 1)
    causal = row >= col
    ltri = jnp.where(causal, 1.0, 0.0).astype(BF)
    gn = gn_ref[...]
    for h in range(GLA_HEADS):
        sk = slice(h * GLA_DKP, (h + 1) * GLA_DKP)
        sv = slice(h * GLA_DVP, (h + 1) * GLA_DVP)
        la = la_ref[:, sk]
        hi = la.astype(BF)
        lo = (la - hi.astype(F32)).astype(BF)
        bb = _dot(ltri, jnp.concatenate([hi, lo], axis=1))
        b = bb[:, :GLA_DKP] + bb[:, GLA_DKP:]
        b_ref = b[C // 2 - 1:C // 2, :]
        b_last = b[C - 1:C, :]
        q = q_ref[:, sk]
        k = k_ref[:, sk]
        v = v_ref[:, sv].astype(BF)
        qe = (q * jnp.exp(b - b_ref)).astype(BF)
        ke = (k * jnp.exp(b_ref - b)).astype(BF)
        a = jnp.where(causal, _dot_nt(qe, ke), 0.0).astype(BF)
        st = s_scr[h]
        qb = (q * jnp.exp(b)).astype(BF)
        o = _dot(a, v) + _dot_nt(qb, st.astype(BF))
        kd = (k * jnp.exp(b_last - b)).astype(BF)
        s_scr[h] = st * jnp.exp(b_last) + _dot_tn(v, kd)
        ms = jnp.sum(o * o, axis=1, keepdims=True) * (1.0 / GLA_DV)
        on = o * lax.rsqrt(ms + EPS) * gn
        tok_ref[:, sv] = (on * _silu(r_ref[:, sv])).astype(BF)

    @pl.when(c == pl.num_programs(1) - 1)
    def _():
        st_ref[0] = s_scr[...]


def _gla_prompt(q, k, la, v, r, gn, *, batch, seq):
    m = batch * seq
    nc = seq // GLA_CHUNK
    qk = GLA_HEADS * GLA_DKP
    vr = GLA_HEADS * GLA_DVP

    def tok_map(b, c):
        return (b * nc + c, 0)

    return pl.pallas_call(
        _gla_prompt_kernel,
        grid=(batch, nc),
        in_specs=[pl.BlockSpec((GLA_CHUNK, qk), tok_map)] * 3
                 + [pl.BlockSpec((GLA_CHUNK, vr), tok_map)] * 2
                 + [_resident((1, GLA_DVP))],
        out_specs=[pl.BlockSpec((GLA_CHUNK, vr), tok_map),
                   pl.BlockSpec((1, GLA_HEADS, GLA_DVP, GLA_DKP), lambda b, c: (b, 0, 0, 0))],
        out_shape=[jax.ShapeDtypeStruct((m, vr), BF),
                   jax.ShapeDtypeStruct((batch, GLA_HEADS, GLA_DVP, GLA_DKP), F32)],
        scratch_shapes=[pltpu.VMEM((GLA_HEADS, GLA_DVP, GLA_DKP), F32)],
        compiler_params=_params("parallel", "arbitrary"),
        name="gla_prompt",
    )(q, k, la, v, r, gn)


def _mem_attn_prompt_kernel(q_ref, mk_ref, mv_ref, o_ref, kbd_scr, vbd_scr):
    tq = q_ref.shape[0]

    @pl.when(pl.program_id(1) == 0)
    def _():
        shape = (MEM_Q, MEM_HEADS * MEM_TOKENS)
        rh = lax.broadcasted_iota(jnp.int32, shape, 0) // MEM_HEAD_DIM
        ch = lax.broadcasted_iota(jnp.int32, shape, 1) // MEM_TOKENS
        diag = rh == ch
        kbd_scr[...] = jnp.where(diag, jnp.concatenate([mk_ref[0, 0]] * MEM_HEADS, axis=1),
                                 0.0).astype(BF)
        vbd_scr[...] = jnp.where(diag, jnp.concatenate([mv_ref[0, 0]] * MEM_HEADS, axis=1),
                                 0.0).astype(BF)

    s = _dot(q_ref[...], kbd_scr[...])
    ps, inv = [], []
    for h in range(MEM_HEADS):
        sh = s[:, h * MEM_TOKENS:(h + 1) * MEM_TOKENS]
        e = jnp.exp(sh - jnp.max(sh, axis=1, keepdims=True))
        inv.append(1.0 / jnp.sum(e, axis=1, keepdims=True))
        ps.append(e.astype(BF))
    o = _dot_nt(jnp.concatenate(ps, axis=1), vbd_scr[...])
    lane_h = lax.broadcasted_iota(jnp.int32, (tq, MEM_Q), 1) // MEM_HEAD_DIM
    scale = jnp.where(lane_h == 0, inv[0],
                      jnp.where(lane_h == 1, inv[1], jnp.where(lane_h == 2, inv[2], inv[3])))
    o_ref[...] = (o * scale).astype(BF)


def _mem_attn_prompt(mq, mk_t, mv_t, layer, *, batch, seq):
    tq = TOKEN_TILE
    nq = seq // tq
    kv_spec = pl.BlockSpec((1, 1, MEM_Q, MEM_TOKENS), lambda b, i: (layer, b, 0, 0))
    q_spec = pl.BlockSpec((tq, MEM_Q), lambda b, i: (b * nq + i, 0))
    bd = pltpu.VMEM((MEM_Q, MEM_HEADS * MEM_TOKENS), BF)
    return pl.pallas_call(
        _mem_attn_prompt_kernel,
        grid=(batch, nq),
        in_specs=[q_spec, kv_spec, kv_spec],
        out_specs=q_spec,
        out_shape=jax.ShapeDtypeStruct((batch * seq, MEM_Q), BF),
        scratch_shapes=[bd, bd],
        compiler_params=_params("parallel", "arbitrary"),
        name="mem_attn_prompt",
    )(mq, mk_t, mv_t)


def _swa_prompt_kernel(sink_ref, q_ref, kp_ref, kc_ref, vp_ref, vc_ref, o_ref):
    n = pl.program_id(1)
    blk = WINDOW
    qi = lax.broadcasted_iota(jnp.int32, (blk, 2 * blk), 0)
    kj = lax.broadcasted_iota(jnp.int32, (blk, 2 * blk), 1)
    d = blk + qi - kj
    first_key = jnp.where(n > 0, 0, blk)
    valid = (d >= 0) & (d < WINDOW) & (kj >= first_key)
    bias = jnp.where(valid, 0.0, -jnp.inf)
    for kh in range(SWA_KV_HEADS):
        sl = slice(kh * HEAD_DIMP, (kh + 1) * HEAD_DIMP)
        kb = jnp.concatenate([kp_ref[:, sl], kc_ref[:, sl]], axis=0).astype(BF)
        vb = jnp.concatenate([vp_ref[:, sl], vc_ref[:, sl]], axis=0).astype(BF)
        for g in range(SWA_GROUP):
            hh = kh * SWA_GROUP + g
            s = _dot_nt(q_ref[hh], kb) + bias
            sink = sink_ref[hh]
            m = jnp.maximum(jnp.max(s, axis=1, keepdims=True), sink)
            e = jnp.exp(s - m)
            l = jnp.sum(e, axis=1, keepdims=True) + jnp.exp(sink - m)
            o_ref[hh] = (_dot(e.astype(BF), vb) * (1.0 / l)).astype(BF)


def _swa_prompt(sinks, q, k, v, *, batch, seq):
    nb = seq // WINDOW
    kw = SWA_KV_HEADS * HEAD_DIMP
    q_spec = pl.BlockSpec((SWA_Q_HEADS, WINDOW, HEAD_DIMP), lambda b, n: (0, b * nb + n, 0))
    prev = pl.BlockSpec((WINDOW, kw), lambda b, n: (b * nb + jnp.maximum(n - 1, 0), 0))
    cur = pl.BlockSpec((WINDOW, kw), lambda b, n: (b * nb + n, 0))
    return pl.pallas_call(
        _swa_prompt_kernel,
        grid=(batch, nb),
        in_specs=[pl.BlockSpec(memory_space=pltpu.SMEM), q_spec, prev, cur, prev, cur],
        out_specs=q_spec,
        out_shape=jax.ShapeDtypeStruct((SWA_Q_HEADS, batch * seq, HEAD_DIMP), BF),
        compiler_params=_params("parallel", "parallel"),
        name="swa_prompt",
    )(sinks, q, k, k, v, v)


_SAMPLE_BB = 8
_DEC_SEQ = 4
_DEC_BATCH = 128
_GLA_DK_BLK = 32


def _gla_sample_kernel(s_ref, q_ref, k_ref, la_ref, v_ref, r_ref, gn_ref, so_ref, tok_ref, o_scr):
    j = pl.program_id(1)

    @pl.when(j == 0)
    def _():
        o_scr[...] = jnp.zeros(o_scr.shape, F32)

    def body(dk, carry):
        s = s_ref[0, 0, dk]
        for t in range(_DEC_SEQ):
            a = jnp.exp(la_ref[t, 0, pl.ds(dk, 1), :])
            s = a * s + k_ref[t, 0, pl.ds(dk, 1), :] * v_ref[t, 0, :GLA_DV, :]
            o_scr[t] = o_scr[t] + q_ref[t, 0, pl.ds(dk, 1), :] * s
        so_ref[0, 0, dk] = s
        return carry

    lax.fori_loop(0, _GLA_DK_BLK, body, 0)

    @pl.when(j == pl.num_programs(1) - 1)
    def _():
        tok_ref[...] = jnp.zeros(tok_ref.shape, F32)
        for t in range(_DEC_SEQ):
            o = o_scr[t]
            ms = jnp.sum(o * o, axis=0, keepdims=True) * (1.0 / GLA_DV)
            on = o * lax.rsqrt(ms + EPS) * gn_ref[...]
            tok_ref[t, 0, :GLA_DV, :] = on * _silu(r_ref[t, 0, :GLA_DV, :])


def _gla_sample(state, q, k, la, v, r, gn):
    qk_spec = pl.BlockSpec((_DEC_SEQ, 1, _GLA_DK_BLK, _DEC_BATCH), lambda h, j: (0, h, j, 0))
    vr_spec = pl.BlockSpec((_DEC_SEQ, 1, GLA_DVP, _DEC_BATCH), lambda h, j: (0, h, 0, 0))
    s_spec = pl.BlockSpec((1, 1, _GLA_DK_BLK, GLA_DV, _DEC_BATCH), lambda h, j: (0, h, j, 0, 0))
    return pl.pallas_call(
        _gla_sample_kernel,
        grid=(GLA_HEADS, GLA_DK // _GLA_DK_BLK),
        in_specs=[s_spec, qk_spec, qk_spec, qk_spec, vr_spec, vr_spec,
                  _resident((GLA_DV, _DEC_BATCH))],
        out_specs=[s_spec, vr_spec],
        out_shape=[jax.ShapeDtypeStruct(state.shape, F32),
                   jax.ShapeDtypeStruct(v.shape, F32)],
        scratch_shapes=[pltpu.VMEM((_DEC_SEQ, GLA_DV, _DEC_BATCH), F32)],
        compiler_params=_params("parallel", "arbitrary"),
        name="gla_sample",
    )(state, q, k, la, v, r, gn)


def _mem_attn_sample_kernel(q_ref, mk_ref, mv_ref, o_ref):
    for bi in range(_SAMPLE_BB):
        s = _dot(q_ref[bi], mk_ref[0, bi].astype(BF))
        e = jnp.exp(s - jnp.max(s, axis=1, keepdims=True))
        p = (e * (1.0 / jnp.sum(e, axis=1, keepdims=True))).astype(BF)
        o_ref[bi] = _dot_nt(p, mv_ref[0, bi].astype(BF))


def _mem_attn_sample(qbd, mk_t, mv_t, layer):
    nb = qbd.shape[0]
    nr = MEM_HEADS * _DEC_SEQ
    kv_spec = pl.BlockSpec((1, _SAMPLE_BB, MEM_Q, MEM_TOKENS), lambda i: (layer, i, 0, 0))
    q_spec = pl.BlockSpec((_SAMPLE_BB, nr, MEM_Q), lambda i: (i, 0, 0))
    return pl.pallas_call(
        _mem_attn_sample_kernel,
        grid=(nb // _SAMPLE_BB,),
        in_specs=[q_spec, kv_spec, kv_spec],
        out_specs=q_spec,
        out_shape=jax.ShapeDtypeStruct((nb, nr, MEM_Q), F32),
        compiler_params=_params("parallel"),
        name="mem_attn_sample",
    )(qbd, mk_t, mv_t)


def _swa_sample_kernel(q_ref, sink_ref, kc_ref, vc_ref, kn_ref, vn_ref, o_ref, ko_ref, vo_ref):
    nq = SWA_Q_HEADS * _DEC_SEQ
    t = lax.broadcasted_iota(jnp.int32, (nq, WINDOW), 0) % _DEC_SEQ
    pos = lax.broadcasted_iota(jnp.int32, (nq, WINDOW), 1)
    new0 = WINDOW - _DEC_SEQ
    bias_c = jnp.where(pos > t, 0.0, -jnp.inf)
    bias_n = jnp.where((pos >= new0) & (pos - new0 <= t), 0.0, -jnp.inf)
    is_new = lax.broadcasted_iota(jnp.int32, (SWA_KV_HEADS * HEAD_DIM, WINDOW), 1) >= new0
    sink = sink_ref[...]
    for bi in range(_SAMPLE_BB):
        q = q_ref[bi]
        kc, vc, kn, vn = kc_ref[bi], vc_ref[bi], kn_ref[bi], vn_ref[bi]
        sc = _dot(q, kc.astype(BF)) + bias_c
        sn = _dot(q, kn.astype(BF)) + bias_n
        m = jnp.maximum(jnp.maximum(jnp.max(sc, axis=1, keepdims=True),
                                    jnp.max(sn, axis=1, keepdims=True)), sink)
        ec = jnp.exp(sc - m)
        en = jnp.exp(sn - m)
        l = (jnp.sum(ec, axis=1, keepdims=True) + jnp.sum(en, axis=1, keepdims=True)
             + jnp.exp(sink - m))
        inv = 1.0 / l
        o_ref[bi] = (_dot_nt((ec * inv).astype(BF), vc.astype(BF))
                     + _dot_nt((en * inv).astype(BF), vn.astype(BF)))
        ko_ref[bi] = jnp.where(is_new, kn, pltpu.roll(kc, new0, 1))
        vo_ref[bi] = jnp.where(is_new, vn, pltpu.roll(vc, new0, 1))


def _swa_sample(qbd, sink_col, kc, vc, kn, vn):
    nb = qbd.shape[0]
    kw = SWA_KV_HEADS * HEAD_DIM
    nq = SWA_Q_HEADS * _DEC_SEQ
    kv_spec = pl.BlockSpec((_SAMPLE_BB, kw, WINDOW), lambda i: (i, 0, 0))
    q_spec = pl.BlockSpec((_SAMPLE_BB, nq, kw), lambda i: (i, 0, 0))
    return pl.pallas_call(
        _swa_sample_kernel,
        grid=(nb // _SAMPLE_BB,),
        in_specs=[q_spec, _resident((nq, 1)), kv_spec, kv_spec, kv_spec, kv_spec],
        out_specs=[q_spec, kv_spec, kv_spec],
        out_shape=[jax.ShapeDtypeStruct((nb, nq, kw), F32),
                   jax.ShapeDtypeStruct(kc.shape, F32), jax.ShapeDtypeStruct(kc.shape, F32)],
        compiler_params=_params("parallel"),
        name="swa_sample",
    )(qbd, sink_col, kc, vc, kn, vn)


def _pad_heads(w, heads, dim, dim_p, axis):
    shape = w.shape
    w = w.reshape(shape[:axis] + (heads, dim) + shape[axis + 1:])
    pad = [(0, 0)] * w.ndim
    pad[axis + 1] = (0, dim_p - dim)
    w = jnp.pad(w, pad)
    return w.reshape(shape[:axis] + (heads * dim_p,) + shape[axis + 1:])


def _rope_tables(pos):
    half = ROT_DIM // 2
    inv_freq = jnp.exp(-math.log(ROPE_THETA) * jnp.arange(0, ROT_DIM, 2, dtype=F32) / ROT_DIM)
    ang = pos[:, None] * inv_freq[None, :]
    cos, sin = jnp.cos(ang), jnp.sin(ang)
    n = pos.shape[0]
    rest = HEAD_DIMP - ROT_DIM
    c = jnp.concatenate([cos, cos, jnp.ones((n, rest), F32)], axis=1)
    s1 = jnp.concatenate([-sin, jnp.zeros((n, HEAD_DIMP - half), F32)], axis=1)
    s2 = jnp.concatenate([jnp.zeros((n, half), F32), sin, jnp.zeros((n, rest), F32)], axis=1)
    return c, s1, s2


def _prep_weights(p):
    w = {}
    for name in ("ffn1_w_gu", "ffn1_w_down", "ffn2_w_gu", "ffn2_w_down"):
        w[name] = p[name].astype(BF)
    for name in ("ffn1_norm", "ffn2_norm", "mix_norm"):
        w[name] = p[name][:, None, :]
    qk = GLA_HEADS * GLA_DK
    vv = GLA_HEADS * GLA_DV
    a_in = p["a_w_in"][0]
    o = 0
    wq = _pad_heads(a_in[:, o:o + qk], GLA_HEADS, GLA_DK, GLA_DKP, 1); o += qk
    wk = _pad_heads(a_in[:, o:o + qk], GLA_HEADS, GLA_DK, GLA_DKP, 1); o += qk
    wv = _pad_heads(a_in[:, o:o + vv], GLA_HEADS, GLA_DV, GLA_DVP, 1); o += vv
    wr = _pad_heads(a_in[:, o:o + vv], GLA_HEADS, GLA_DV, GLA_DVP, 1); o += vv
    wg = jnp.pad(a_in[:, o:o + GLA_RANK], ((0, 0), (0, GLA_RANKP - GLA_RANK))); o += GLA_RANK
    wm = a_in[:, o:]
    w["a_in"] = jnp.concatenate([wq, wk, wv, wr, wg, wm], axis=1).astype(BF)
    gate = _pad_heads(p["a_w_gate"][0], GLA_HEADS, GLA_DK, GLA_DKP, 1)
    w["a_gate"] = jnp.pad(gate, ((0, GLA_RANKP - GLA_RANK), (0, 0))).astype(BF)
    w["a_bgate"] = _pad_heads(p["a_b_gate"][0][None, :], GLA_HEADS, GLA_DK, GLA_DKP, 1)
    w["a_gn"] = jnp.pad(p["a_out_norm"][0], (0, GLA_DVP - GLA_DV))[None, :]
    a_out = p["a_w_out"][0]
    w["a_out"] = jnp.concatenate(
        [_pad_heads(a_out[:vv], GLA_HEADS, GLA_DV, GLA_DVP, 0), a_out[vv:]], axis=0).astype(BF)
    nq = SWA_Q_HEADS * HEAD_DIM
    b_in = p["b_w_in"][0]
    w["b_in"] = jnp.concatenate(
        [_pad_heads(b_in[:, :nq], SWA_Q_HEADS, HEAD_DIM, HEAD_DIMP, 1), b_in[:, nq:]],
        axis=1).astype(BF)
    b_out = p["b_w_out"][0]
    w["b_out"] = jnp.concatenate(
        [_pad_heads(b_out[:nq], SWA_Q_HEADS, HEAD_DIM, HEAD_DIMP, 0), b_out[nq:]],
        axis=0).astype(BF)
    nkv = SWA_KV_HEADS * HEAD_DIM
    w_kv = p["w_kv"]
    w["kv"] = jnp.concatenate(
        [_pad_heads(w_kv[:, :nkv], SWA_KV_HEADS, HEAD_DIM, HEAD_DIMP, 1),
         _pad_heads(w_kv[:, nkv:], SWA_KV_HEADS, HEAD_DIM, HEAD_DIMP, 1)], axis=1).astype(BF)
    w["mem_t"] = p["mem_w_kv"].transpose(0, 2, 1).astype(BF)
    return w


def _compact_kv(a, batch, seq):
    return a.reshape(batch, seq, SWA_KV_HEADS, HEAD_DIMP)[..., :HEAD_DIM]


def kernel(x_prompt, x_sample, state_gla, cache_swa_k, cache_swa_v, cache_mem_k, cache_mem_v,
           mem_prompt, ffn1_norm, ffn1_w_gu, ffn1_w_down, mix_norm, ffn2_norm, ffn2_w_gu,
           ffn2_w_down, mem_norm, mem_w_kv, a_w_in, a_w_gate, a_b_gate, a_out_norm, a_w_out,
           kv_norm, w_kv, b_w_in, b_sinks, b_w_out, final_norm):
    p = dict(ffn1_norm=ffn1_norm, ffn1_w_gu=ffn1_w_gu, ffn1_w_down=ffn1_w_down,
             mix_norm=mix_norm, ffn2_norm=ffn2_norm, ffn2_w_gu=ffn2_w_gu,
             ffn2_w_down=ffn2_w_down, mem_w_kv=mem_w_kv, a_w_in=a_w_in, a_w_gate=a_w_gate,
             a_b_gate=a_b_gate, a_out_norm=a_out_norm, a_w_out=a_w_out, w_kv=w_kv,
             b_w_in=b_w_in, b_w_out=b_w_out)
    w = _prep_weights(p)
    batch, seq, _ = x_prompt.shape
    nb, t, _ = x_sample.shape
    assert nb == _DEC_BATCH and t == _DEC_SEQ and nb * t == TOKEN_TILE
    assert seq % TOKEN_TILE == 0 and seq % GLA_CHUNK == 0
    mp = batch * seq
    kw = SWA_KV_HEADS * HEAD_DIM
    sinks = b_sinks[0]

    mem_k_t, mem_v_t = _mem_kv(mem_prompt.reshape(batch * MEM_TOKENS, D_MODEL), mem_norm[:, None, :],
                               w["mem_t"], batch=batch)
    state_t = state_gla.transpose(0, 2, 3, 4, 1)
    kc_t = cache_swa_k.transpose(0, 2, 3, 1).reshape(nb, kw, WINDOW)
    vc_t = cache_swa_v.transpose(0, 2, 3, 1).reshape(nb, kw, WINDOW)
    cmk_t = cache_mem_k.transpose(0, 1, 3, 4, 2).reshape(2, nb, MEM_Q, MEM_TOKENS)
    cmv_t = cache_mem_v.transpose(0, 1, 3, 4, 2).reshape(2, nb, MEM_Q, MEM_TOKENS)

    tabs = tuple(
        jnp.concatenate([a, jnp.tile(b, (nb, 1))], axis=0)
        for a, b in zip(_rope_tables(jnp.arange(seq, dtype=F32)),
                        _rope_tables(PAST_LEN + jnp.arange(t, dtype=F32))))

    mem_mask = (jnp.arange(MEM_Q) // MEM_HEAD_DIM)[None, :] == jnp.arange(MEM_HEADS)[:, None]
    kv_mask = (jnp.arange(kw) // HEAD_DIM)[None, :] == jnp.arange(SWA_KV_HEADS)[:, None]

    def mem_attn_sample(mq, layer):
        q4 = mq[mp:].reshape(nb, 1, t, MEM_Q)
        qbd = jnp.where(mem_mask[None, :, None, :], q4, 0).reshape(nb, MEM_HEADS * t, MEM_Q)
        o = _mem_attn_sample(qbd, cmk_t, cmv_t, layer).reshape(nb, MEM_HEADS, t, MEM_Q)
        o = jnp.sum(jnp.where(mem_mask[None, :, None, :], o, 0.0), axis=1)
        return o.reshape(nb * t, MEM_Q).astype(BF)

    def lanes(a, width):
        return a[mp:].reshape(nb, t, GLA_HEADS, width).transpose(1, 2, 3, 0)

    x, q, k, la, v, r, mq = _layer_a_in(
        x_prompt.reshape(mp, D_MODEL), x_sample.reshape(nb * t, D_MODEL), w["ffn1_norm"],
        w["ffn1_w_gu"], w["ffn1_w_down"], w["mix_norm"], w["a_in"], w["a_gate"], w["a_bgate"])
    tok_p, st_p = _gla_prompt(q, k, la, v, r, w["a_gn"], batch=batch, seq=seq)
    mo_p = _mem_attn_prompt(mq, mem_k_t, mem_v_t, 0, batch=batch, seq=seq)
    gn = jnp.broadcast_to(a_out_norm[0][:, None], (GLA_DV, nb))
    st_s, tok_s = _gla_sample(state_t, lanes(q, GLA_DKP), lanes(k, GLA_DKP), lanes(la, GLA_DKP),
                              lanes(v, GLA_DVP), lanes(r, GLA_DVP), gn)
    tok_s = tok_s.transpose(3, 0, 1, 2).reshape(nb * t, GLA_HEADS * GLA_DVP).astype(BF)
    mo_s = mem_attn_sample(mq, 0)
    x, k_sh, v_sh = _layer_a_out(x, tok_p, tok_s, mo_p, mo_s, w["a_out"], w["ffn2_norm"],
                                 w["ffn2_w_gu"], w["ffn2_w_down"], kv_norm[None, :], w["kv"],
                                 tabs, seq=seq)

    x, qs, mq = _layer_b_in(x, w["ffn1_norm"], w["ffn1_w_gu"], w["ffn1_w_down"], w["mix_norm"],
                            w["b_in"], tabs, n_prompt=mp // TOKEN_TILE, seq=seq)
    tok_p = _swa_prompt(sinks, qs, k_sh, v_sh, batch=batch, seq=seq)
    mo_p = _mem_attn_prompt(mq, mem_k_t, mem_v_t, 1, batch=batch, seq=seq)

    def new_rows(a):
        a = _compact_kv(a[mp:], nb, t).reshape(nb, t, kw).transpose(0, 2, 1)
        return jnp.pad(a, ((0, 0), (0, 0), (WINDOW - t, 0)))

    q5 = qs[:, mp:, :HEAD_DIM].reshape(SWA_KV_HEADS, SWA_GROUP, nb, t, HEAD_DIM)
    q5 = q5.transpose(2, 0, 1, 3, 4)
    qbd = jnp.where(kv_mask[None, :, None, None, :], jnp.tile(q5, (1, 1, 1, 1, SWA_KV_HEADS)), 0)
    qbd = qbd.reshape(nb, SWA_Q_HEADS * t, kw)
    o, k_s, v_s = _swa_sample(qbd, jnp.repeat(sinks, t)[:, None], kc_t, vc_t,
                              new_rows(k_sh), new_rows(v_sh))
    o = o.reshape(nb, SWA_KV_HEADS, SWA_GROUP, t, kw)
    o = jnp.where(kv_mask[None, :, None, None, :], o, 0.0)
    o = o.reshape(nb, SWA_KV_HEADS, SWA_GROUP, t, SWA_KV_HEADS, HEAD_DIM).sum(axis=4)
    tok_s = o.transpose(1, 2, 0, 3, 4).reshape(SWA_Q_HEADS, nb * t, HEAD_DIM)
    tok_s = jnp.pad(tok_s, ((0, 0), (0, 0), (0, HEAD_DIMP - HEAD_DIM))).astype(BF)
    mo_s = mem_attn_sample(mq, 1)
    y_p, y_s = _layer_b_out(x, tok_p, tok_s, mo_p, mo_s, w["b_out"], w["ffn2_norm"],
                            w["ffn2_w_gu"], w["ffn2_w_down"], final_norm[None, :])

    gla_prompt = st_p.transpose(0, 1, 3, 2)[None, :, :, :GLA_DK, :GLA_DV]
    gla_sample = st_s.transpose(0, 4, 1, 2, 3)
    def last_window(a):
        tiles_per_seq = seq // TOKEN_TILE
        a = a.reshape(-1, TOKEN_TILE, SWA_KV_HEADS, HEAD_DIMP)
        return a[tiles_per_seq - 1:batch * tiles_per_seq:tiles_per_seq, TOKEN_TILE - WINDOW:, :,
                 :HEAD_DIM]

    swa_k_prompt = last_window(k_sh)
    swa_v_prompt = last_window(v_sh)
    swa_k_sample = k_s.reshape(nb, SWA_KV_HEADS, HEAD_DIM, WINDOW).transpose(0, 3, 1, 2)
    swa_v_sample = v_s.reshape(nb, SWA_KV_HEADS, HEAD_DIM, WINDOW).transpose(0, 3, 1, 2)
    mem_shape = (2, batch, MEM_HEADS, MEM_HEAD_DIM, MEM_TOKENS)
    mem_k_prompt = mem_k_t.reshape(mem_shape).transpose(0, 1, 4, 2, 3)
    mem_v_prompt = mem_v_t.reshape(mem_shape).transpose(0, 1, 4, 2, 3)
    return (y_p.reshape(batch, seq, D_MODEL), y_s.reshape(nb, t, D_MODEL), gla_prompt,
            gla_sample, swa_k_prompt, swa_v_prompt, swa_k_sample, swa_v_sample,
            mem_k_prompt, mem_v_prompt)
```

```python
import functools
import math

import jax
import jax.numpy as jnp
from jax import lax
from jax.experimental import pallas as pl
from jax.experimental.pallas import tpu as pltpu

F32 = jnp.float32
BF = jnp.bfloat16

D_MODEL = 1024
FFN_DIM = 2816
EPS = 1e-6

GLA_HEADS = 4
GLA_DK = 96
GLA_DV = 192
GLA_DKP = 128
GLA_DVP = 256
GLA_RANK = 16
GLA_RANKP = 128
GLA_GATE_NORM = 16.0
GLA_CHUNK = 256

HEAD_DIM = 64
HEAD_DIMP = 128
SWA_Q_HEADS = 12
SWA_KV_HEADS = 3
SWA_GROUP = SWA_Q_HEADS // SWA_KV_HEADS
WINDOW = 128
ROT_DIM = 16
ROPE_THETA = 500000.0
PAST_LEN = 8192

MEM_TOKENS = 256
MEM_HEADS = 4
MEM_HEAD_DIM = 64
MEM_Q = MEM_HEADS * MEM_HEAD_DIM

FFN_TF = 256
FFN_CHUNKS = FFN_DIM // FFN_TF
TOKEN_TILE = 512

VMEM_LIMIT = 60 * 1024 * 1024


def _params(*sem):
    return pltpu.CompilerParams(dimension_semantics=sem, vmem_limit_bytes=VMEM_LIMIT)


def _resident(shape):
    nd = len(shape)
    return pl.BlockSpec(shape, lambda *_: (0,) * nd, pipeline_mode=pl.Buffered(1))


def _layer_block(shape, layer):
    nd = len(shape)
    return pl.BlockSpec((1,) + tuple(shape[1:]), lambda *_: (layer,) + (0,) * (nd - 1),
                        pipeline_mode=pl.Buffered(1))


def _rows(width):
    return pl.BlockSpec((TOKEN_TILE, width), lambda i: (i, 0))


def _prompt_rows(width, n_prompt):
    return pl.BlockSpec((TOKEN_TILE, width), lambda i: (jnp.minimum(i, n_prompt - 1), 0))


def _sample_rows(width):
    return pl.BlockSpec((TOKEN_TILE, width), lambda i: (0, 0), pipeline_mode=pl.Buffered(1))


def _rms(x, g):
    ms = jnp.mean(x * x, axis=-1, keepdims=True)
    return x * lax.rsqrt(ms + EPS) * g


def _silu(x):
    return x * (1.0 / (1.0 + jnp.exp(-x)))


def _dot(a, b):
    return jnp.dot(a, b, preferred_element_type=F32)


def _dot_nt(a, b):
    return lax.dot_general(a, b, (((1,), (1,)), ((), ())), preferred_element_type=F32)


def _dot_tn(a, b):
    return lax.dot_general(a, b, (((0,), (0,)), ((), ())), preferred_element_type=F32)


def _pick(is_prompt, p_ref, s_ref):
    return jnp.where(is_prompt, p_ref[...], s_ref[...])


def _cast_job(w, layer, rows, grid):
    _, r, c = w.shape
    assert r % rows == 0 and rows % 16 == 0
    nblk = r // rows
    total = math.prod(grid)
    assert nblk <= total
    steps_per_block = total // nblk

    def block(*idx):
        step = idx[0]
        for dim, i in zip(grid[1:], idx[1:]):
            step = step * dim + i
        return jnp.minimum(step // steps_per_block, nblk - 1)

    in_spec = pl.BlockSpec((1, rows, c), lambda *idx: (layer, block(*idx), 0))
    out_spec = pl.BlockSpec((1, rows, c), lambda *idx: (0, block(*idx), 0))
    return in_spec, out_spec, jax.ShapeDtypeStruct((1, r, c), BF)


def _run_cast_jobs(refs):
    n = len(refs) // 2
    for src, dst in zip(refs[:n], refs[n:]):
        dst[...] = src[...].astype(BF)


def _ffn_half(x, g_ref, wgu_ref, wd_ref):
    h = _rms(x, g_ref[0]).astype(BF)
    acc = jnp.zeros(x.shape, F32)
    for c in range(FFN_CHUNKS):
        lo, hi = c * FFN_TF, (c + 1) * FFN_TF
        gate = _dot(h, wgu_ref[0, :, lo:hi])
        up = _dot(h, wgu_ref[0, :, FFN_DIM + lo:FFN_DIM + hi])
        a = (_silu(gate) * up).astype(BF)
        acc = acc + _dot(a, wd_ref[0, lo:hi, :])
    return x + 0.5 * acc


def _rope(x, c, s1, s2):
    return (x * c + pltpu.roll(x, HEAD_DIMP - ROT_DIM // 2, 1) * s1
            + pltpu.roll(x, ROT_DIM // 2, 1) * s2)


_A_Q = 0
_A_K = _A_Q + GLA_HEADS * GLA_DKP
_A_V = _A_K + GLA_HEADS * GLA_DKP
_A_R = _A_V + GLA_HEADS * GLA_DVP
_A_G = _A_R + GLA_HEADS * GLA_DVP
_A_M = _A_G + GLA_RANKP
_A_END = _A_M + MEM_Q


def _layer_a_in_kernel(xp_ref, xs_ref, g1_ref, wgu_ref, wd_ref, gm_ref, w_ref, wg_ref, bg_ref,
                       x_ref, q_ref, k_ref, la_ref, v_ref, r_ref, mq_ref, *, n_prompt):
    is_prompt = pl.program_id(0) < n_prompt
    x = _ffn_half(_pick(is_prompt, xp_ref, xs_ref), g1_ref, wgu_ref, wd_ref)
    x_ref[...] = x
    h = _rms(x, gm_ref[0]).astype(BF)

    proj = _dot(h, w_ref[...])

    def mm(a, b):
        return proj[:, a:b]

    q_ref[...] = mm(_A_Q, _A_K) * (GLA_DK ** -0.5)
    k_ref[...] = mm(_A_K, _A_V)
    v_ref[...] = mm(_A_V, _A_R)
    r_ref[...] = mm(_A_R, _A_G)
    z = _dot(mm(_A_G, _A_M).astype(BF), wg_ref[...]) + bg_ref[...]
    la_ref[...] = (jnp.minimum(z, 0.0) - jnp.log1p(jnp.exp(-jnp.abs(z)))) * (1.0 / GLA_GATE_NORM)
    mq_ref[...] = (mm(_A_M, _A_END) * (MEM_HEAD_DIM ** -0.5)).astype(BF)


def _layer_a_in(xp, xs, g1, wgu, wd, gm, w, wg, bg):
    n_prompt = xp.shape[0] // TOKEN_TILE
    m = xp.shape[0] + xs.shape[0]
    qk = GLA_HEADS * GLA_DKP
    vr = GLA_HEADS * GLA_DVP
    outs = ((D_MODEL, F32), (qk, F32), (qk, F32), (qk, F32), (vr, F32), (vr, F32), (MEM_Q, BF))
    return pl.pallas_call(
        functools.partial(_layer_a_in_kernel, n_prompt=n_prompt),
        grid=(m // TOKEN_TILE,),
        in_specs=[_prompt_rows(D_MODEL, n_prompt), _sample_rows(D_MODEL),
                  _layer_block(g1.shape, 0), _layer_block(wgu.shape, 0), _layer_block(wd.shape, 0),
                  _layer_block(gm.shape, 0), _resident(w.shape), _resident(wg.shape),
                  _resident(bg.shape)],
        out_specs=[_rows(n) for n, _ in outs],
        out_shape=[jax.ShapeDtypeStruct((m, n), dt) for n, dt in outs],
        compiler_params=_params("arbitrary"),
        name="layer_a_in",
    )(xp, xs, g1, wgu, wd, gm, w, wg, bg)


def _layer_a_out_kernel(x_ref, tokp_ref, toks_ref, mop_ref, mos_ref, wo_ref, g2_ref, wgu_ref,
                        wd_ref, gkv_ref, wkv_ref, c_ref, s1_ref, s2_ref,
                        xo_ref, k_ref, v_ref, *, n_prompt):
    is_prompt = pl.program_id(0) < n_prompt
    nt = GLA_HEADS * GLA_DVP
    x = (x_ref[...] + _dot(_pick(is_prompt, tokp_ref, toks_ref), wo_ref[:nt, :])
         + _dot(_pick(is_prompt, mop_ref, mos_ref), wo_ref[nt:, :]))
    x = _ffn_half(x, g2_ref, wgu_ref, wd_ref)
    xo_ref[...] = x
    h = _rms(x, gkv_ref[...]).astype(BF)
    c, s1, s2 = c_ref[...], s1_ref[...], s2_ref[...]
    kw = SWA_KV_HEADS * HEAD_DIMP
    kv = _dot(h, wkv_ref[...])
    for hh in range(SWA_KV_HEADS):
        sl = slice(hh * HEAD_DIMP, (hh + 1) * HEAD_DIMP)
        k_ref[:, sl] = _rope(kv[:, sl], c, s1, s2)
    v_ref[...] = kv[:, kw:]


def _tab_spec(n_prompt, blocks_per_seq):
    return pl.BlockSpec((TOKEN_TILE, HEAD_DIMP),
                        lambda i: (jnp.where(i < n_prompt, i % blocks_per_seq, blocks_per_seq), 0))


def _layer_a_out(x, tokp, toks, mop, mos, wo, g2, wgu, wd, gkv, wkv, tabs, *, seq):
    m = x.shape[0]
    n_prompt = tokp.shape[0] // TOKEN_TILE
    kw = SWA_KV_HEADS * HEAD_DIMP
    nt = GLA_HEADS * GLA_DVP
    tab = _tab_spec(n_prompt, seq // TOKEN_TILE)
    return pl.pallas_call(
        functools.partial(_layer_a_out_kernel, n_prompt=n_prompt),
        grid=(m // TOKEN_TILE,),
        in_specs=[_rows(D_MODEL), _prompt_rows(nt, n_prompt), _sample_rows(nt),
                  _prompt_rows(MEM_Q, n_prompt), _sample_rows(MEM_Q), _resident(wo.shape),
                  _layer_block(g2.shape, 0), _layer_block(wgu.shape, 0), _layer_block(wd.shape, 0),
                  _resident(gkv.shape), _resident(wkv.shape), tab, tab, tab],
        out_specs=[_rows(D_MODEL), _rows(kw), _rows(kw)],
        out_shape=[jax.ShapeDtypeStruct((m, D_MODEL), F32),
                   jax.ShapeDtypeStruct((m, kw), F32), jax.ShapeDtypeStruct((m, kw), F32)],
        compiler_params=_params("arbitrary"),
        name="layer_a_out",
    )(x, tokp, toks, mop, mos, wo, g2, wgu, wd, gkv, wkv, *tabs)


def _layer_b_in_kernel(x_ref, g1_ref, wgu_ref, wd_ref, gm_ref, w_ref, c_ref, s1_ref, s2_ref,
                       xo_ref, q_ref, mq_ref):
    x = _ffn_half(x_ref[...], g1_ref, wgu_ref, wd_ref)
    xo_ref[...] = x
    h = _rms(x, gm_ref[0]).astype(BF)
    c, s1, s2 = c_ref[...], s1_ref[...], s2_ref[...]
    nq = SWA_Q_HEADS * HEAD_DIMP
    qm = _dot(h, w_ref[...])
    for hh in range(SWA_Q_HEADS):
        q = qm[:, hh * HEAD_DIMP:(hh + 1) * HEAD_DIMP]
        q_ref[hh] = (_rope(q, c, s1, s2) * (HEAD_DIM ** -0.5)).astype(BF)
    mq_ref[...] = (qm[:, nq:] * (MEM_HEAD_DIM ** -0.5)).astype(BF)


def _layer_b_in(x, g1, wgu, wd, gm, w, tabs, *, n_prompt, seq):
    m = x.shape[0]
    tab = _tab_spec(n_prompt, seq // TOKEN_TILE)
    q_spec = pl.BlockSpec((SWA_Q_HEADS, TOKEN_TILE, HEAD_DIMP), lambda i: (0, i, 0))
    return pl.pallas_call(
        _layer_b_in_kernel,
        grid=(m // TOKEN_TILE,),
        in_specs=[_rows(D_MODEL), _layer_block(g1.shape, 1), _layer_block(wgu.shape, 0),
                  _layer_block(wd.shape, 0), _layer_block(gm.shape, 1), _resident(w.shape),
                  tab, tab, tab],
        out_specs=[_rows(D_MODEL), q_spec, _rows(MEM_Q)],
        out_shape=[jax.ShapeDtypeStruct((m, D_MODEL), F32),
                   jax.ShapeDtypeStruct((SWA_Q_HEADS, m, HEAD_DIMP), BF),
                   jax.ShapeDtypeStruct((m, MEM_Q), BF)],
        compiler_params=_params("arbitrary"),
        name="layer_b_in",
    )(x, g1, wgu, wd, gm, w, *tabs)


def _layer_b_out_kernel(x_ref, tokp_ref, toks_ref, mop_ref, mos_ref, wo_ref, g2_ref, wgu_ref,
                        wd_ref, gf_ref, yp_ref, ys_ref, *, n_prompt):
    is_prompt = pl.program_id(0) < n_prompt
    nt = SWA_Q_HEADS * HEAD_DIMP
    tok = _pick(is_prompt, tokp_ref, toks_ref)
    tok = jnp.concatenate([tok[hh] for hh in range(SWA_Q_HEADS)], axis=1)
    x = (x_ref[...] + _dot(tok, wo_ref[:nt, :])
         + _dot(_pick(is_prompt, mop_ref, mos_ref), wo_ref[nt:, :]))
    y = _rms(_ffn_half(x, g2_ref, wgu_ref, wd_ref), gf_ref[...])

    @pl.when(is_prompt)
    def _():
        yp_ref[...] = y

    @pl.when(jnp.logical_not(is_prompt))
    def _():
        ys_ref[...] = y


def _layer_b_out(x, tokp, toks, mop, mos, wo, g2, wgu, wd, gf):
    m = x.shape[0]
    n_prompt = tokp.shape[1] // TOKEN_TILE
    tokp_spec = pl.BlockSpec((SWA_Q_HEADS, TOKEN_TILE, HEAD_DIMP),
                             lambda i: (0, jnp.minimum(i, n_prompt - 1), 0))
    toks_spec = pl.BlockSpec((SWA_Q_HEADS, TOKEN_TILE, HEAD_DIMP), lambda i: (0, 0, 0),
                             pipeline_mode=pl.Buffered(1))
    return pl.pallas_call(
        functools.partial(_layer_b_out_kernel, n_prompt=n_prompt),
        grid=(m // TOKEN_TILE,),
        in_specs=[_rows(D_MODEL), tokp_spec, toks_spec, _prompt_rows(MEM_Q, n_prompt),
                  _sample_rows(MEM_Q), _resident(wo.shape), _layer_block(g2.shape, 1),
                  _layer_block(wgu.shape, 0), _layer_block(wd.shape, 0), _resident(gf.shape)],
        out_specs=[_prompt_rows(D_MODEL, n_prompt),
                   pl.BlockSpec((TOKEN_TILE, D_MODEL), lambda i: (0, 0))],
        out_shape=[jax.ShapeDtypeStruct((n_prompt * TOKEN_TILE, D_MODEL), F32),
                   jax.ShapeDtypeStruct((TOKEN_TILE, D_MODEL), F32)],
        compiler_params=_params("arbitrary"),
        name="layer_b_out",
    )(x, tokp, toks, mop, mos, wo, g2, wgu, wd, gf)


def _mem_kv_kernel(x_ref, g_ref, wt_ref, *refs):
    n = (len(refs) - 2) // 2
    k_ref, v_ref = refs[n:n + 2]
    _run_cast_jobs(refs[:n] + refs[n + 2:])
    x = x_ref[...]
    xn = x * lax.rsqrt(jnp.mean(x * x, axis=-1, keepdims=True) + EPS)
    for l in range(2):
        h = (xn * g_ref[l]).astype(BF)
        kvt = _dot_nt(wt_ref[l], h)
        k_ref[l, 0] = kvt[:MEM_Q, :]
        v_ref[l, 0] = kvt[MEM_Q:, :]


def _mem_kv(mem, g, wt, casts, *, batch):
    out_spec = pl.BlockSpec((2, 1, MEM_Q, MEM_TOKENS), lambda b: (0, b, 0, 0))
    jobs = [_cast_job(cw, layer, rows, (batch,)) for cw, layer, rows in casts]
    return pl.pallas_call(
        _mem_kv_kernel,
        grid=(batch,),
        in_specs=[pl.BlockSpec((MEM_TOKENS, D_MODEL), lambda b: (b, 0)), _resident(g.shape),
                  _resident(wt.shape)] + [j[0] for j in jobs],
        out_specs=[out_spec, out_spec] + [j[1] for j in jobs],
        out_shape=[jax.ShapeDtypeStruct((2, batch, MEM_Q, MEM_TOKENS), F32)] * 2
                  + [j[2] for j in jobs],
        compiler_params=_params("arbitrary"),
        name="mem_kv",
    )(mem, g, wt, *[c[0] for c in casts])


def _gla_prompt_kernel(q_ref, k_ref, la_ref, v_ref, r_ref, gn_ref, *refs):
    n = (len(refs) - 3) // 2
    tok_ref, st_ref = refs[n:n + 2]
    s_scr = refs[-1]
    _run_cast_jobs(refs[:n] + refs[n + 2:-1])
    c = pl.program_id(1)
    C = GLA_CHUNK

    @pl.when(c == 0)
    def _():
        s_scr[...] = jnp.zeros(s_scr.shape, F32)

    row = lax.broadcasted_iota(jnp.int32, (C, C), 0)
    col = lax.broadcasted_iota(jnp.int32, (C, C), 1)
    causal = row >= col
    ltri = jnp.where(causal, 1.0, 0.0).astype(BF)
    gn = gn_ref[...]
    for h in range(GLA_HEADS):
        sk = slice(h * GLA_DKP, (h + 1) * GLA_DKP)
        sv = slice(h * GLA_DVP, (h + 1) * GLA_DVP)
        la = la_ref[:, sk]
        hi = la.astype(BF)
        lo = (la - hi.astype(F32)).astype(BF)
        bb = _dot(ltri, jnp.concatenate([hi, lo], axis=1))
        b = bb[:, :GLA_DKP] + bb[:, GLA_DKP:]
        b_ref = b[C // 2 - 1:C // 2, :]
        b_last = b[C - 1:C, :]
        q = q_ref[:, sk]
        k = k_ref[:, sk]
        v = v_ref[:, sv].astype(BF)
        qe = (q * jnp.exp(b - b_ref)).astype(BF)
        ke = (k * jnp.exp(b_ref - b)).astype(BF)
        a = jnp.where(causal, _dot_nt(qe, ke), 0.0).astype(BF)
        st = s_scr[h]
        qb = (q * jnp.exp(b)).astype(BF)
        o = _dot(a, v) + _dot_nt(qb, st.astype(BF))
        kd = (k * jnp.exp(b_last - b)).astype(BF)
        s_scr[h] = st * jnp.exp(b_last) + _dot_tn(v, kd)
        ms = jnp.sum(o * o, axis=1, keepdims=True) * (1.0 / GLA_DV)
        on = o * lax.rsqrt(ms + EPS) * gn
        tok_ref[:, sv] = (on * _silu(r_ref[:, sv])).astype(BF)

    @pl.when(c == pl.num_programs(1) - 1)
    def _():
        st_ref[0] = s_scr[...]


def _gla_prompt(q, k, la, v, r, gn, casts, *, batch, seq):
    m = batch * seq
    nc = seq // GLA_CHUNK
    qk = GLA_HEADS * GLA_DKP
    vr = GLA_HEADS * GLA_DVP

    def tok_map(b, c):
        return (b * nc + c, 0)

    jobs = [_cast_job(cw, layer, rows, (batch, nc)) for cw, layer, rows in casts]
    return pl.pallas_call(
        _gla_prompt_kernel,
        grid=(batch, nc),
        in_specs=[pl.BlockSpec((GLA_CHUNK, qk), tok_map)] * 3
                 + [pl.BlockSpec((GLA_CHUNK, vr), tok_map)] * 2
                 + [_resident((1, GLA_DVP))] + [j[0] for j in jobs],
        out_specs=[pl.BlockSpec((GLA_CHUNK, vr), tok_map),
                   pl.BlockSpec((1, GLA_HEADS, GLA_DVP, GLA_DKP), lambda b, c: (b, 0, 0, 0))]
                  + [j[1] for j in jobs],
        out_shape=[jax.ShapeDtypeStruct((m, vr), BF),
                   jax.ShapeDtypeStruct((batch, GLA_HEADS, GLA_DVP, GLA_DKP), F32)]
                  + [j[2] for j in jobs],
        scratch_shapes=[pltpu.VMEM((GLA_HEADS, GLA_DVP, GLA_DKP), F32)],
        compiler_params=_params("arbitrary", "arbitrary"),
        name="gla_prompt",
    )(q, k, la, v, r, gn, *[c[0] for c in casts])


def _mem_attn_prompt_kernel(q_ref, mk_ref, mv_ref, o_ref, kbd_scr, vbd_scr):
    tq = q_ref.shape[0]

    @pl.when(pl.program_id(1) == 0)
    def _():
        shape = (MEM_Q, MEM_HEADS * MEM_TOKENS)
        rh = lax.broadcasted_iota(jnp.int32, shape, 0) // MEM_HEAD_DIM
        ch = lax.broadcasted_iota(jnp.int32, shape, 1) // MEM_TOKENS
        diag = rh == ch
        kbd_scr[...] = jnp.where(diag, jnp.concatenate([mk_ref[0, 0]] * MEM_HEADS, axis=1),
                                 0.0).astype(BF)
        vbd_scr[...] = jnp.where(diag, jnp.concatenate([mv_ref[0, 0]] * MEM_HEADS, axis=1),
                                 0.0).astype(BF)

    s = _dot(q_ref[...], kbd_scr[...])
    ps, inv = [], []
    for h in range(MEM_HEADS):
        sh = s[:, h * MEM_TOKENS:(h + 1) * MEM_TOKENS]
        e = jnp.exp(sh - jnp.max(sh, axis=1, keepdims=True))
        inv.append(1.0 / jnp.sum(e, axis=1, keepdims=True))
        ps.append(e.astype(BF))
    o = _dot_nt(jnp.concatenate(ps, axis=1), vbd_scr[...])
    lane_h = lax.broadcasted_iota(jnp.int32, (tq, MEM_Q), 1) // MEM_HEAD_DIM
    scale = jnp.where(lane_h == 0, inv[0],
                      jnp.where(lane_h == 1, inv[1], jnp.where(lane_h == 2, inv[2], inv[3])))
    o_ref[...] = (o * scale).astype(BF)


def _mem_attn_prompt(mq, mk_t, mv_t, layer, *, batch, seq):
    tq = TOKEN_TILE
    nq = seq // tq
    kv_spec = pl.BlockSpec((1, 1, MEM_Q, MEM_TOKENS), lambda b, i: (layer, b, 0, 0))
    q_spec = pl.BlockSpec((tq, MEM_Q), lambda b, i: (b * nq + i, 0))
    bd = pltpu.VMEM((MEM_Q, MEM_HEADS * MEM_TOKENS), BF)
    return pl.pallas_call(
        _mem_attn_prompt_kernel,
        grid=(batch, nq),
        in_specs=[q_spec, kv_spec, kv_spec],
        out_specs=q_spec,
        out_shape=jax.ShapeDtypeStruct((batch * seq, MEM_Q), BF),
        scratch_shapes=[bd, bd],
        compiler_params=_params("parallel", "arbitrary"),
        name="mem_attn_prompt",
    )(mq, mk_t, mv_t)


def _swa_prompt_kernel(sink_ref, q_ref, kp_ref, kc_ref, vp_ref, vc_ref, *refs):
    nc = (len(refs) - 1) // 2
    o_ref = refs[nc]
    _run_cast_jobs(refs[:nc] + refs[nc + 1:])
    n = pl.program_id(1)
    blk = WINDOW
    qi = lax.broadcasted_iota(jnp.int32, (blk, 2 * blk), 0)
    kj = lax.broadcasted_iota(jnp.int32, (blk, 2 * blk), 1)
    d = blk + qi - kj
    first_key = jnp.where(n > 0, 0, blk)
    valid = (d >= 0) & (d < WINDOW) & (kj >= first_key)
    bias = jnp.where(valid, 0.0, -jnp.inf)
    for kh in range(SWA_KV_HEADS):
        sl = slice(kh * HEAD_DIMP, (kh + 1) * HEAD_DIMP)
        kb = jnp.concatenate([kp_ref[:, sl], kc_ref[:, sl]], axis=0).astype(BF)
        vb = jnp.concatenate([vp_ref[:, sl], vc_ref[:, sl]], axis=0).astype(BF)
        for g in range(SWA_GROUP):
            hh = kh * SWA_GROUP + g
            s = _dot_nt(q_ref[hh], kb) + bias
            sink = sink_ref[hh]
            m = jnp.maximum(jnp.max(s, axis=1, keepdims=True), sink)
            e = jnp.exp(s - m)
            l = jnp.sum(e, axis=1, keepdims=True) + jnp.exp(sink - m)
            o_ref[hh] = (_dot(e.astype(BF), vb) * (1.0 / l)).astype(BF)


def _swa_prompt(sinks, q, k, v, casts, *, batch, seq):
    nb = seq // WINDOW
    kw = SWA_KV_HEADS * HEAD_DIMP
    q_spec = pl.BlockSpec((SWA_Q_HEADS, WINDOW, HEAD_DIMP), lambda b, n: (0, b * nb + n, 0))
    prev = pl.BlockSpec((WINDOW, kw), lambda b, n: (b * nb + jnp.maximum(n - 1, 0), 0))
    cur = pl.BlockSpec((WINDOW, kw), lambda b, n: (b * nb + n, 0))
    jobs = [_cast_job(cw, layer, rows, (batch, nb)) for cw, layer, rows in casts]
    return pl.pallas_call(
        _swa_prompt_kernel,
        grid=(batch, nb),
        in_specs=[pl.BlockSpec(memory_space=pltpu.SMEM), q_spec, prev, cur, prev, cur]
                 + [j[0] for j in jobs],
        out_specs=[q_spec] + [j[1] for j in jobs],
        out_shape=[jax.ShapeDtypeStruct((SWA_Q_HEADS, batch * seq, HEAD_DIMP), BF)]
                  + [j[2] for j in jobs],
        compiler_params=_params("arbitrary", "arbitrary"),
        name="swa_prompt",
    )(sinks, q, k, k, v, v, *[c[0] for c in casts])


_SAMPLE_BB = 8
_DEC_SEQ = 4
_DEC_BATCH = 128
_GLA_DK_BLK = 32


def _gla_sample_kernel(s_ref, q_ref, k_ref, la_ref, v_ref, r_ref, gn_ref, so_ref, tok_ref, o_scr):
    j = pl.program_id(1)

    @pl.when(j == 0)
    def _():
        o_scr[...] = jnp.zeros(o_scr.shape, F32)

    def body(dk, carry):
        s = s_ref[0, 0, dk]
        for t in range(_DEC_SEQ):
            a = jnp.exp(la_ref[t, 0, pl.ds(dk, 1), :])
            s = a * s + k_ref[t, 0, pl.ds(dk, 1), :] * v_ref[t, 0, :GLA_DV, :]
            o_scr[t] = o_scr[t] + q_ref[t, 0, pl.ds(dk, 1), :] * s
        so_ref[0, 0, dk] = s
        return carry

    lax.fori_loop(0, _GLA_DK_BLK, body, 0)

    @pl.when(j == pl.num_programs(1) - 1)
    def _():
        tok_ref[...] = jnp.zeros(tok_ref.shape, F32)
        for t in range(_DEC_SEQ):
            o = o_scr[t]
            ms = jnp.sum(o * o, axis=0, keepdims=True) * (1.0 / GLA_DV)
            on = o * lax.rsqrt(ms + EPS) * gn_ref[...]
            tok_ref[t, 0, :GLA_DV, :] = on * _silu(r_ref[t, 0, :GLA_DV, :])


def _gla_sample(state, q, k, la, v, r, gn):
    qk_spec = pl.BlockSpec((_DEC_SEQ, 1, _GLA_DK_BLK, _DEC_BATCH), lambda h, j: (0, h, j, 0))
    vr_spec = pl.BlockSpec((_DEC_SEQ, 1, GLA_DVP, _DEC_BATCH), lambda h, j: (0, h, 0, 0))
    s_spec = pl.BlockSpec((1, 1, _GLA_DK_BLK, GLA_DV, _DEC_BATCH), lambda h, j: (0, h, j, 0, 0))
    return pl.pallas_call(
        _gla_sample_kernel,
        grid=(GLA_HEADS, GLA_DK // _GLA_DK_BLK),
        in_specs=[s_spec, qk_spec, qk_spec, qk_spec, vr_spec, vr_spec,
                  _resident((GLA_DV, _DEC_BATCH))],
        out_specs=[s_spec, vr_spec],
        out_shape=[jax.ShapeDtypeStruct(state.shape, F32),
                   jax.ShapeDtypeStruct(v.shape, F32)],
        scratch_shapes=[pltpu.VMEM((_DEC_SEQ, GLA_DV, _DEC_BATCH), F32)],
        compiler_params=_params("parallel", "arbitrary"),
        name="gla_sample",
    )(state, q, k, la, v, r, gn)


def _mem_attn_sample_kernel(q_ref, mk_ref, mv_ref, o_ref):
    for bi in range(_SAMPLE_BB):
        s = _dot(q_ref[bi], mk_ref[0, bi].astype(BF))
        e = jnp.exp(s - jnp.max(s, axis=1, keepdims=True))
        p = (e * (1.0 / jnp.sum(e, axis=1, keepdims=True))).astype(BF)
        o_ref[bi] = _dot_nt(p, mv_ref[0, bi].astype(BF))


def _mem_attn_sample(qbd, mk_t, mv_t, layer):
    nb = qbd.shape[0]
    nr = MEM_HEADS * _DEC_SEQ
    kv_spec = pl.BlockSpec((1, _SAMPLE_BB, MEM_Q, MEM_TOKENS), lambda i: (layer, i, 0, 0))
    q_spec = pl.BlockSpec((_SAMPLE_BB, nr, MEM_Q), lambda i: (i, 0, 0))
    return pl.pallas_call(
        _mem_attn_sample_kernel,
        grid=(nb // _SAMPLE_BB,),
        in_specs=[q_spec, kv_spec, kv_spec],
        out_specs=q_spec,
        out_shape=jax.ShapeDtypeStruct((nb, nr, MEM_Q), F32),
        compiler_params=_params("parallel"),
        name="mem_attn_sample",
    )(qbd, mk_t, mv_t)


def _swa_sample_kernel(q_ref, sink_ref, kc_ref, vc_ref, kn_ref, vn_ref, o_ref, ko_ref, vo_ref):
    nq = SWA_Q_HEADS * _DEC_SEQ
    t = lax.broadcasted_iota(jnp.int32, (nq, WINDOW), 0) % _DEC_SEQ
    pos = lax.broadcasted_iota(jnp.int32, (nq, WINDOW), 1)
    new0 = WINDOW - _DEC_SEQ
    bias_c = jnp.where(pos > t, 0.0, -jnp.inf)
    bias_n = jnp.where((pos >= new0) & (pos - new0 <= t), 0.0, -jnp.inf)
    is_new = lax.broadcasted_iota(jnp.int32, (SWA_KV_HEADS * HEAD_DIM, WINDOW), 1) >= new0
    sink = sink_ref[...]
    for bi in range(_SAMPLE_BB):
        q = q_ref[bi]
        kc, vc, kn, vn = kc_ref[bi], vc_ref[bi], kn_ref[bi], vn_ref[bi]
        sc = _dot(q, kc.astype(BF)) + bias_c
        sn = _dot(q, kn.astype(BF)) + bias_n
        m = jnp.maximum(jnp.maximum(jnp.max(sc, axis=1, keepdims=True),
                                    jnp.max(sn, axis=1, keepdims=True)), sink)
        ec = jnp.exp(sc - m)
        en = jnp.exp(sn - m)
        l = (jnp.sum(ec, axis=1, keepdims=True) + jnp.sum(en, axis=1, keepdims=True)
             + jnp.exp(sink - m))
        inv = 1.0 / l
        o_ref[bi] = (_dot_nt((ec * inv).astype(BF), vc.astype(BF))
                     + _dot_nt((en * inv).astype(BF), vn.astype(BF)))
        ko_ref[bi] = jnp.where(is_new, kn, pltpu.roll(kc, new0, 1))
        vo_ref[bi] = jnp.where(is_new, vn, pltpu.roll(vc, new0, 1))


def _swa_sample(qbd, sink_col, kc, vc, kn, vn):
    nb = qbd.shape[0]
    kw = SWA_KV_HEADS * HEAD_DIM
    nq = SWA_Q_HEADS * _DEC_SEQ
    kv_spec = pl.BlockSpec((_SAMPLE_BB, kw, WINDOW), lambda i: (i, 0, 0))
    q_spec = pl.BlockSpec((_SAMPLE_BB, nq, kw), lambda i: (i, 0, 0))
    return pl.pallas_call(
        _swa_sample_kernel,
        grid=(nb // _SAMPLE_BB,),
        in_specs=[q_spec, _resident((nq, 1)), kv_spec, kv_spec, kv_spec, kv_spec],
        out_specs=[q_spec, kv_spec, kv_spec],
        out_shape=[jax.ShapeDtypeStruct((nb, nq, kw), F32),
                   jax.ShapeDtypeStruct(kc.shape, F32), jax.ShapeDtypeStruct(kc.shape, F32)],
        compiler_params=_params("parallel"),
        name="swa_sample",
    )(qbd, sink_col, kc, vc, kn, vn)


def _pad_heads(w, heads, dim, dim_p, axis):
    shape = w.shape
    w = w.reshape(shape[:axis] + (heads, dim) + shape[axis + 1:])
    pad = [(0, 0)] * w.ndim
    pad[axis + 1] = (0, dim_p - dim)
    w = jnp.pad(w, pad)
    return w.reshape(shape[:axis] + (heads * dim_p,) + shape[axis + 1:])


def _rope_tables(pos):
    half = ROT_DIM // 2
    inv_freq = jnp.exp(-math.log(ROPE_THETA) * jnp.arange(0, ROT_DIM, 2, dtype=F32) / ROT_DIM)
    ang = pos[:, None] * inv_freq[None, :]
    cos, sin = jnp.cos(ang), jnp.sin(ang)
    n = pos.shape[0]
    rest = HEAD_DIMP - ROT_DIM
    c = jnp.concatenate([cos, cos, jnp.ones((n, rest), F32)], axis=1)
    s1 = jnp.concatenate([-sin, jnp.zeros((n, HEAD_DIMP - half), F32)], axis=1)
    s2 = jnp.concatenate([jnp.zeros((n, half), F32), sin, jnp.zeros((n, rest), F32)], axis=1)
    return c, s1, s2


def _prep_weights(p):
    w = {}
    for name in ("ffn1_norm", "ffn2_norm", "mix_norm"):
        w[name] = p[name][:, None, :]
    qk = GLA_HEADS * GLA_DK
    vv = GLA_HEADS * GLA_DV
    a_in = p["a_w_in"][0]
    o = 0
    wq = _pad_heads(a_in[:, o:o + qk], GLA_HEADS, GLA_DK, GLA_DKP, 1); o += qk
    wk = _pad_heads(a_in[:, o:o + qk], GLA_HEADS, GLA_DK, GLA_DKP, 1); o += qk
    wv = _pad_heads(a_in[:, o:o + vv], GLA_HEADS, GLA_DV, GLA_DVP, 1); o += vv
    wr = _pad_heads(a_in[:, o:o + vv], GLA_HEADS, GLA_DV, GLA_DVP, 1); o += vv
    wg = jnp.pad(a_in[:, o:o + GLA_RANK], ((0, 0), (0, GLA_RANKP - GLA_RANK))); o += GLA_RANK
    wm = a_in[:, o:]
    w["a_in"] = jnp.concatenate([wq, wk, wv, wr, wg, wm], axis=1).astype(BF)
    gate = _pad_heads(p["a_w_gate"][0], GLA_HEADS, GLA_DK, GLA_DKP, 1)
    w["a_gate"] = jnp.pad(gate, ((0, GLA_RANKP - GLA_RANK), (0, 0))).astype(BF)
    w["a_bgate"] = _pad_heads(p["a_b_gate"][0][None, :], GLA_HEADS, GLA_DK, GLA_DKP, 1)
    w["a_gn"] = jnp.pad(p["a_out_norm"][0], (0, GLA_DVP - GLA_DV))[None, :]
    a_out = p["a_w_out"][0]
    w["a_out"] = jnp.concatenate(
        [_pad_heads(a_out[:vv], GLA_HEADS, GLA_DV, GLA_DVP, 0), a_out[vv:]], axis=0).astype(BF)
    nq = SWA_Q_HEADS * HEAD_DIM
    b_in = p["b_w_in"][0]
    w["b_in"] = jnp.concatenate(
        [_pad_heads(b_in[:, :nq], SWA_Q_HEADS, HEAD_DIM, HEAD_DIMP, 1), b_in[:, nq:]],
        axis=1).astype(BF)
    b_out = p["b_w_out"][0]
    w["b_out"] = jnp.concatenate(
        [_pad_heads(b_out[:nq], SWA_Q_HEADS, HEAD_DIM, HEAD_DIMP, 0), b_out[nq:]],
        axis=0).astype(BF)
    nkv = SWA_KV_HEADS * HEAD_DIM
    w_kv = p["w_kv"]
    w["kv"] = jnp.concatenate(
        [_pad_heads(w_kv[:, :nkv], SWA_KV_HEADS, HEAD_DIM, HEAD_DIMP, 1),
         _pad_heads(w_kv[:, nkv:], SWA_KV_HEADS, HEAD_DIM, HEAD_DIMP, 1)], axis=1).astype(BF)
    w["mem_t"] = p["mem_w_kv"].transpose(0, 2, 1).astype(BF)
    return w


def _compact_kv(a, batch, seq):
    return a.reshape(batch, seq, SWA_KV_HEADS, HEAD_DIMP)[..., :HEAD_DIM]


def kernel(x_prompt, x_sample, state_gla, cache_swa_k, cache_swa_v, cache_mem_k, cache_mem_v,
           mem_prompt, ffn1_norm, ffn1_w_gu, ffn1_w_down, mix_norm, ffn2_norm, ffn2_w_gu,
           ffn2_w_down, mem_norm, mem_w_kv, a_w_in, a_w_gate, a_b_gate, a_out_norm, a_w_out,
           kv_norm, w_kv, b_w_in, b_sinks, b_w_out, final_norm):
    p = dict(ffn1_norm=ffn1_norm, ffn1_w_gu=ffn1_w_gu, ffn1_w_down=ffn1_w_down,
             mix_norm=mix_norm, ffn2_norm=ffn2_norm, ffn2_w_gu=ffn2_w_gu,
             ffn2_w_down=ffn2_w_down, mem_w_kv=mem_w_kv, a_w_in=a_w_in, a_w_gate=a_w_gate,
             a_b_gate=a_b_gate, a_out_norm=a_out_norm, a_w_out=a_w_out, w_kv=w_kv,
             b_w_in=b_w_in, b_w_out=b_w_out)
    w = _prep_weights(p)
    batch, seq, _ = x_prompt.shape
    nb, t, _ = x_sample.shape
    assert nb == _DEC_BATCH and t == _DEC_SEQ and nb * t == TOKEN_TILE
    assert seq % TOKEN_TILE == 0 and seq % GLA_CHUNK == 0
    mp = batch * seq
    kw = SWA_KV_HEADS * HEAD_DIM
    sinks = b_sinks[0]

    mem_k_t, mem_v_t, wgu_a1, wd_a1 = _mem_kv(
        mem_prompt.reshape(batch * MEM_TOKENS, D_MODEL), mem_norm[:, None, :], w["mem_t"],
        [(ffn1_w_gu, 0, D_MODEL // batch), (ffn1_w_down, 0, FFN_DIM // batch)], batch=batch)
    state_t = state_gla.transpose(0, 2, 3, 4, 1)
    kc_t = cache_swa_k.transpose(0, 2, 3, 1).reshape(nb, kw, WINDOW)
    vc_t = cache_swa_v.transpose(0, 2, 3, 1).reshape(nb, kw, WINDOW)
    cmk_t = cache_mem_k.transpose(0, 1, 3, 4, 2).reshape(2, nb, MEM_Q, MEM_TOKENS)
    cmv_t = cache_mem_v.transpose(0, 1, 3, 4, 2).reshape(2, nb, MEM_Q, MEM_TOKENS)

    tabs = tuple(
        jnp.concatenate([a, jnp.tile(b, (nb, 1))], axis=0)
        for a, b in zip(_rope_tables(jnp.arange(seq, dtype=F32)),
                        _rope_tables(PAST_LEN + jnp.arange(t, dtype=F32))))

    mem_mask = (jnp.arange(MEM_Q) // MEM_HEAD_DIM)[None, :] == jnp.arange(MEM_HEADS)[:, None]
    kv_mask = (jnp.arange(kw) // HEAD_DIM)[None, :] == jnp.arange(SWA_KV_HEADS)[:, None]

    def mem_attn_sample(mq, layer):
        q4 = mq[mp:].reshape(nb, 1, t, MEM_Q)
        qbd = jnp.where(mem_mask[None, :, None, :], q4, 0).reshape(nb, MEM_HEADS * t, MEM_Q)
        o = _mem_attn_sample(qbd, cmk_t, cmv_t, layer).reshape(nb, MEM_HEADS, t, MEM_Q)
        o = jnp.sum(jnp.where(mem_mask[None, :, None, :], o, 0.0), axis=1)
        return o.reshape(nb * t, MEM_Q).astype(BF)

    def lanes(a, width):
        return a[mp:].reshape(nb, t, GLA_HEADS, width).transpose(1, 2, 3, 0)

    x, q, k, la, v, r, mq = _layer_a_in(
        x_prompt.reshape(mp, D_MODEL), x_sample.reshape(nb * t, D_MODEL), w["ffn1_norm"],
        wgu_a1, wd_a1, w["mix_norm"], w["a_in"], w["a_gate"], w["a_bgate"])
    tok_p, st_p, wgu_a2, wd_a2, wgu_b1, wd_b1 = _gla_prompt(
        q, k, la, v, r, w["a_gn"],
        [(ffn2_w_gu, 0, 16), (ffn2_w_down, 0, 64), (ffn1_w_gu, 1, 16), (ffn1_w_down, 1, 64)],
        batch=batch, seq=seq)
    mo_p = _mem_attn_prompt(mq, mem_k_t, mem_v_t, 0, batch=batch, seq=seq)
    gn = jnp.broadcast_to(a_out_norm[0][:, None], (GLA_DV, nb))
    st_s, tok_s = _gla_sample(state_t, lanes(q, GLA_DKP), lanes(k, GLA_DKP), lanes(la, GLA_DKP),
                              lanes(v, GLA_DVP), lanes(r, GLA_DVP), gn)
    tok_s = tok_s.transpose(3, 0, 1, 2).reshape(nb * t, GLA_HEADS * GLA_DVP).astype(BF)
    mo_s = mem_attn_sample(mq, 0)
    x, k_sh, v_sh = _layer_a_out(x, tok_p, tok_s, mo_p, mo_s, w["a_out"], w["ffn2_norm"],
                                 wgu_a2, wd_a2, kv_norm[None, :], w["kv"], tabs, seq=seq)

    x, qs, mq = _layer_b_in(x, w["ffn1_norm"], wgu_b1, wd_b1, w["mix_norm"],
                            w["b_in"], tabs, n_prompt=mp // TOKEN_TILE, seq=seq)
    tok_p, wgu_b2, wd_b2 = _swa_prompt(
        sinks, qs, k_sh, v_sh, [(ffn2_w_gu, 1, 16), (ffn2_w_down, 1, 32)], batch=batch, seq=seq)
    mo_p = _mem_attn_prompt(mq, mem_k_t, mem_v_t, 1, batch=batch, seq=seq)

    def new_rows(a):
        a = _compact_kv(a[mp:], nb, t).reshape(nb, t, kw).transpose(0, 2, 1)
        return jnp.pad(a, ((0, 0), (0, 0), (WINDOW - t, 0)))

    q5 = qs[:, mp:, :HEAD_DIM].reshape(SWA_KV_HEADS, SWA_GROUP, nb, t, HEAD_DIM)
    q5 = q5.transpose(2, 0, 1, 3, 4)
    qbd = jnp.where(kv_mask[None, :, None, None, :], jnp.tile(q5, (1, 1, 1, 1, SWA_KV_HEADS)), 0)
    qbd = qbd.reshape(nb, SWA_Q_HEADS * t, kw)
    o, k_s, v_s = _swa_sample(qbd, jnp.repeat(sinks, t)[:, None], kc_t, vc_t,
                              new_rows(k_sh), new_rows(v_sh))
    o = o.reshape(nb, SWA_KV_HEADS, SWA_GROUP, t, kw)
    o = jnp.where(kv_mask[None, :, None, None, :], o, 0.0)
    o = o.reshape(nb, SWA_KV_HEADS, SWA_GROUP, t, SWA_KV_HEADS, HEAD_DIM).sum(axis=4)
    tok_s = o.transpose(1, 2, 0, 3, 4).reshape(SWA_Q_HEADS, nb * t, HEAD_DIM)
    tok_s = jnp.pad(tok_s, ((0, 0), (0, 0), (0, HEAD_DIMP - HEAD_DIM))).astype(BF)
    mo_s = mem_attn_sample(mq, 1)
    y_p, y_s = _layer_b_out(x, tok_p, tok_s, mo_p, mo_s, w["b_out"], w["ffn2_norm"],
                            wgu_b2, wd_b2, final_norm[None, :])

    gla_prompt = st_p.transpose(0, 1, 3, 2)[None, :, :, :GLA_DK, :GLA_DV]
    gla_sample = st_s.transpose(0, 4, 1, 2, 3)
    def last_window(a):
        tiles_per_seq = seq // TOKEN_TILE
        a = a.reshape(-1, TOKEN_TILE, SWA_KV_HEADS * HEAD_DIMP)
        a = a[tiles_per_seq - 1:batch * tiles_per_seq:tiles_per_seq, TOKEN_TILE - WINDOW:]
        return a.reshape(batch, WINDOW, SWA_KV_HEADS, HEAD_DIMP)[..., :HEAD_DIM]

    swa_k_prompt = last_window(k_sh)
    swa_v_prompt = last_window(v_sh)
    swa_k_sample = k_s.reshape(nb, SWA_KV_HEADS, HEAD_DIM, WINDOW).transpose(0, 3, 1, 2)
    swa_v_sample = v_s.reshape(nb, SWA_KV_HEADS, HEAD_DIM, WINDOW).transpose(0, 3, 1, 2)
    mem_shape = (2, batch, MEM_HEADS, MEM_HEAD_DIM, MEM_TOKENS)
    mem_k_prompt = mem_k_t.reshape(mem_shape).transpose(0, 1, 4, 2, 3)
    mem_v_prompt = mem_v_t.reshape(mem_shape).transpose(0, 1, 4, 2, 3)
    return (y_p.reshape(batch, seq, D_MODEL), y_s.reshape(nb, t, D_MODEL), gla_prompt,
            gla_sample, swa_k_prompt, swa_v_prompt, swa_k_sample, swa_v_sample,
            mem_k_prompt, mem_v_prompt)
```

```python
import functools
import math

import jax
import jax.numpy as jnp
from jax import lax
from jax.experimental import pallas as pl
from jax.experimental.pallas import tpu as pltpu

F32 = jnp.float32
BF = jnp.bfloat16

D_MODEL = 1024
FFN_DIM = 2816
EPS = 1e-6

GLA_HEADS = 4
GLA_DK = 96
GLA_DV = 192
GLA_DKP = 128
GLA_DVP = 256
GLA_RANK = 16
GLA_RANKP = 128
GLA_GATE_NORM = 16.0
GLA_CHUNK = 256
GLA_SEQS = 2

HEAD_DIM = 64
HEAD_DIMP = 128
SWA_Q_HEADS = 12
SWA_KV_HEADS = 3
SWA_GROUP = SWA_Q_HEADS // SWA_KV_HEADS
SWA_Q_PAIRS = SWA_Q_HEADS // 2
SWA_PAIRS_PER_KV = SWA_GROUP // 2
WINDOW = 128
ROT_DIM = 16
ROPE_THETA = 500000.0
PAST_LEN = 8192

MEM_TOKENS = 256
MEM_HEADS = 4
MEM_HEAD_DIM = 64
MEM_Q = MEM_HEADS * MEM_HEAD_DIM

FFN_TF = 256
FFN_CHUNKS = FFN_DIM // FFN_TF
TOKEN_TILE = 512

VMEM_LIMIT = 60 * 1024 * 1024


def _params(*sem):
    return pltpu.CompilerParams(dimension_semantics=sem, vmem_limit_bytes=VMEM_LIMIT)


def _resident(shape):
    nd = len(shape)
    return pl.BlockSpec(shape, lambda *_: (0,) * nd, pipeline_mode=pl.Buffered(1))


def _layer_block(shape, layer):
    nd = len(shape)
    return pl.BlockSpec((1,) + tuple(shape[1:]), lambda *_: (layer,) + (0,) * (nd - 1),
                        pipeline_mode=pl.Buffered(1))


def _rows(width):
    return pl.BlockSpec((TOKEN_TILE, width), lambda i: (i, 0))


def _prompt_rows(width, n_prompt):
    return pl.BlockSpec((TOKEN_TILE, width), lambda i: (jnp.minimum(i, n_prompt - 1), 0))


def _sample_rows(width):
    return pl.BlockSpec((TOKEN_TILE, width), lambda i: (0, 0), pipeline_mode=pl.Buffered(1))


def _rms(x, g):
    ms = jnp.mean(x * x, axis=-1, keepdims=True)
    return x * lax.rsqrt(ms + EPS) * g


def _silu(x):
    return x * (1.0 / (1.0 + jnp.exp(-x)))


def _dot(a, b):
    return jnp.dot(a, b, preferred_element_type=F32)


def _dot_nt(a, b):
    return lax.dot_general(a, b, (((1,), (1,)), ((), ())), preferred_element_type=F32)


def _dot_tn(a, b):
    return lax.dot_general(a, b, (((0,), (0,)), ((), ())), preferred_element_type=F32)


def _pick(is_prompt, p_ref, s_ref):
    return jnp.where(is_prompt, p_ref[...], s_ref[...])


def _cast_job(w, layer, rows, grid):
    _, r, c = w.shape
    assert r % rows == 0 and rows % 16 == 0
    nblk = r // rows
    total = math.prod(grid)
    assert nblk <= total
    steps_per_block = total // nblk

    def block(*idx):
        step = idx[0]
        for dim, i in zip(grid[1:], idx[1:]):
            step = step * dim + i
        return jnp.minimum(step // steps_per_block, nblk - 1)

    in_spec = pl.BlockSpec((1, rows, c), lambda *idx: (layer, block(*idx), 0))
    out_spec = pl.BlockSpec((1, rows, c), lambda *idx: (0, block(*idx), 0))
    return in_spec, out_spec, jax.ShapeDtypeStruct((1, r, c), BF)


def _run_cast_jobs(refs):
    n = len(refs) // 2
    for src, dst in zip(refs[:n], refs[n:]):
        dst[...] = src[...].astype(BF)


def _ffn_half(x, g_ref, wgu_ref, wd_ref):
    h = _rms(x, g_ref[0]).astype(BF)
    acc = jnp.zeros(x.shape, F32)
    for c in range(FFN_CHUNKS):
        lo, hi = c * FFN_TF, (c + 1) * FFN_TF
        gate = _dot(h, wgu_ref[0, :, lo:hi])
        up = _dot(h, wgu_ref[0, :, FFN_DIM + lo:FFN_DIM + hi])
        a = (_silu(gate) * up).astype(BF)
        acc = acc + _dot(a, wd_ref[0, lo:hi, :])
    return x + 0.5 * acc


def _rope(x, c, s1, s2):
    return (x * c + pltpu.roll(x, HEAD_DIMP - ROT_DIM // 2, 1) * s1
            + pltpu.roll(x, ROT_DIM // 2, 1) * s2)


_A_Q = 0
_A_K = _A_Q + GLA_HEADS * GLA_DKP
_A_V = _A_K + GLA_HEADS * GLA_DKP
_A_R = _A_V + GLA_HEADS * GLA_DVP
_A_G = _A_R + GLA_HEADS * GLA_DVP
_A_M = _A_G + GLA_RANKP
_A_END = _A_M + MEM_Q


def _layer_a_in_kernel(xp_ref, xs_ref, g1_ref, wgu_ref, wd_ref, gm_ref, w_ref, wg_ref, bg_ref,
                       gn_ref, x_ref, q_ref, k_ref, la_ref, v_ref, r_ref, mq_ref, *, n_prompt):
    is_prompt = pl.program_id(0) < n_prompt
    x = _ffn_half(_pick(is_prompt, xp_ref, xs_ref), g1_ref, wgu_ref, wd_ref)
    x_ref[...] = x
    h = _rms(x, gm_ref[0]).astype(BF)

    proj = _dot(h, w_ref[...])

    def mm(a, b):
        return proj[:, a:b]

    q_ref[...] = mm(_A_Q, _A_K) * (GLA_DK ** -0.5)
    k_ref[...] = mm(_A_K, _A_V)
    v_ref[...] = mm(_A_V, _A_R)
    r_ref[...] = _silu(mm(_A_R, _A_G)) * gn_ref[...]
    z = _dot(mm(_A_G, _A_M).astype(BF), wg_ref[...]) + bg_ref[...]
    la_ref[...] = (jnp.minimum(z, 0.0) - jnp.log1p(jnp.exp(-jnp.abs(z)))) * (1.0 / GLA_GATE_NORM)
    mq_ref[...] = (mm(_A_M, _A_END) * (MEM_HEAD_DIM ** -0.5)).astype(BF)


def _layer_a_in(xp, xs, g1, wgu, wd, gm, w, wg, bg, gn):
    n_prompt = xp.shape[0] // TOKEN_TILE
    m = xp.shape[0] + xs.shape[0]
    qk = GLA_HEADS * GLA_DKP
    vr = GLA_HEADS * GLA_DVP
    outs = ((D_MODEL, F32), (qk, F32), (qk, F32), (qk, F32), (vr, F32), (vr, F32), (MEM_Q, BF))
    return pl.pallas_call(
        functools.partial(_layer_a_in_kernel, n_prompt=n_prompt),
        grid=(m // TOKEN_TILE,),
        in_specs=[_prompt_rows(D_MODEL, n_prompt), _sample_rows(D_MODEL),
                  _layer_block(g1.shape, 0), _layer_block(wgu.shape, 0), _layer_block(wd.shape, 0),
                  _layer_block(gm.shape, 0), _resident(w.shape), _resident(wg.shape),
                  _resident(bg.shape), _resident(gn.shape)],
        out_specs=[_rows(n) for n, _ in outs],
        out_shape=[jax.ShapeDtypeStruct((m, n), dt) for n, dt in outs],
        compiler_params=_params("arbitrary"),
        name="layer_a_in",
    )(xp, xs, g1, wgu, wd, gm, w, wg, bg, gn)


def _layer_a_out_kernel(x_ref, tokp_ref, toks_ref, mop_ref, mos_ref, wo_ref, g2_ref, wgu_ref,
                        wd_ref, gkv_ref, wkv_ref, c_ref, s1_ref, s2_ref,
                        xo_ref, k_ref, v_ref, *, n_prompt):
    is_prompt = pl.program_id(0) < n_prompt
    nt = GLA_HEADS * GLA_DVP
    x = (x_ref[...] + _dot(_pick(is_prompt, tokp_ref, toks_ref), wo_ref[:nt, :])
         + _dot(_pick(is_prompt, mop_ref, mos_ref), wo_ref[nt:, :]))
    x = _ffn_half(x, g2_ref, wgu_ref, wd_ref)
    xo_ref[...] = x
    h = _rms(x, gkv_ref[...]).astype(BF)
    c, s1, s2 = c_ref[...], s1_ref[...], s2_ref[...]
    kw = SWA_KV_HEADS * HEAD_DIMP
    kv = _dot(h, wkv_ref[...])
    for hh in range(SWA_KV_HEADS):
        sl = slice(hh * HEAD_DIMP, (hh + 1) * HEAD_DIMP)
        k_ref[:, sl] = _rope(kv[:, sl], c, s1, s2)
    v_ref[...] = kv[:, kw:]


def _tab_spec(n_prompt, blocks_per_seq):
    return pl.BlockSpec((TOKEN_TILE, HEAD_DIMP),
                        lambda i: (jnp.where(i < n_prompt, i % blocks_per_seq, blocks_per_seq), 0))


def _layer_a_out(x, tokp, toks, mop, mos, wo, g2, wgu, wd, gkv, wkv, tabs, *, seq):
    m = x.shape[0]
    n_prompt = tokp.shape[0] // TOKEN_TILE
    kw = SWA_KV_HEADS * HEAD_DIMP
    nt = GLA_HEADS * GLA_DVP
    tab = _tab_spec(n_prompt, seq // TOKEN_TILE)
    return pl.pallas_call(
        functools.partial(_layer_a_out_kernel, n_prompt=n_prompt),
        grid=(m // TOKEN_TILE,),
        in_specs=[_rows(D_MODEL), _prompt_rows(nt, n_prompt), _sample_rows(nt),
                  _prompt_rows(MEM_Q, n_prompt), _sample_rows(MEM_Q), _resident(wo.shape),
                  _layer_block(g2.shape, 0), _layer_block(wgu.shape, 0), _layer_block(wd.shape, 0),
                  _resident(gkv.shape), _resident(wkv.shape), tab, tab, tab],
        out_specs=[_rows(D_MODEL), _rows(kw), _rows(kw)],
        out_shape=[jax.ShapeDtypeStruct((m, D_MODEL), F32),
                   jax.ShapeDtypeStruct((m, kw), F32), jax.ShapeDtypeStruct((m, kw), F32)],
        compiler_params=_params("arbitrary"),
        name="layer_a_out",
    )(x, tokp, toks, mop, mos, wo, g2, wgu, wd, gkv, wkv, *tabs)


def _layer_b_in_kernel(x_ref, g1_ref, wgu_ref, wd_ref, gm_ref, w_ref, c_ref, s1_ref, s2_ref,
                       xo_ref, q_ref, mq_ref):
    x = _ffn_half(x_ref[...], g1_ref, wgu_ref, wd_ref)
    xo_ref[...] = x
    h = _rms(x, gm_ref[0]).astype(BF)
    c, s1, s2 = c_ref[...], s1_ref[...], s2_ref[...]
    nq = SWA_Q_HEADS * HEAD_DIM
    qm = _dot(h, w_ref[...])
    for pp in range(SWA_Q_PAIRS):
        q = qm[:, pp * HEAD_DIMP:(pp + 1) * HEAD_DIMP]
        q_ref[pp] = (_rope(q, c, s1, s2) * (HEAD_DIM ** -0.5)).astype(BF)
    mq_ref[...] = (qm[:, nq:] * (MEM_HEAD_DIM ** -0.5)).astype(BF)


def _layer_b_in(x, g1, wgu, wd, gm, w, tabs, *, n_prompt, seq):
    m = x.shape[0]
    tab = _tab_spec(n_prompt, seq // TOKEN_TILE)
    q_spec = pl.BlockSpec((SWA_Q_PAIRS, TOKEN_TILE, HEAD_DIMP), lambda i: (0, i, 0))
    return pl.pallas_call(
        _layer_b_in_kernel,
        grid=(m // TOKEN_TILE,),
        in_specs=[_rows(D_MODEL), _layer_block(g1.shape, 1), _layer_block(wgu.shape, 0),
                  _layer_block(wd.shape, 0), _layer_block(gm.shape, 1), _resident(w.shape),
                  tab, tab, tab],
        out_specs=[_rows(D_MODEL), q_spec, _rows(MEM_Q)],
        out_shape=[jax.ShapeDtypeStruct((m, D_MODEL), F32),
                   jax.ShapeDtypeStruct((SWA_Q_PAIRS, m, HEAD_DIMP), BF),
                   jax.ShapeDtypeStruct((m, MEM_Q), BF)],
        compiler_params=_params("arbitrary"),
        name="layer_b_in",
    )(x, g1, wgu, wd, gm, w, *tabs)


def _layer_b_out_kernel(x_ref, tokp_ref, toks_ref, mop_ref, mos_ref, wo_ref, g2_ref, wgu_ref,
                        wd_ref, gf_ref, yp_ref, ys_ref, *, n_prompt):
    is_prompt = pl.program_id(0) < n_prompt
    nt = SWA_Q_HEADS * HEAD_DIM
    tok = _pick(is_prompt, tokp_ref, toks_ref)
    tok = jnp.concatenate([tok[pp] for pp in range(SWA_Q_PAIRS)], axis=1)
    x = (x_ref[...] + _dot(tok, wo_ref[:nt, :])
         + _dot(_pick(is_prompt, mop_ref, mos_ref), wo_ref[nt:, :]))
    y = _rms(_ffn_half(x, g2_ref, wgu_ref, wd_ref), gf_ref[...])

    @pl.when(is_prompt)
    def _():
        yp_ref[...] = y

    @pl.when(jnp.logical_not(is_prompt))
    def _():
        ys_ref[...] = y


def _layer_b_out(x, tokp, toks, mop, mos, wo, g2, wgu, wd, gf):
    m = x.shape[0]
    n_prompt = tokp.shape[1] // TOKEN_TILE
    tokp_spec = pl.BlockSpec((SWA_Q_PAIRS, TOKEN_TILE, HEAD_DIMP),
                             lambda i: (0, jnp.minimum(i, n_prompt - 1), 0))
    toks_spec = pl.BlockSpec((SWA_Q_PAIRS, TOKEN_TILE, HEAD_DIMP), lambda i: (0, 0, 0),
                             pipeline_mode=pl.Buffered(1))
    return pl.pallas_call(
        functools.partial(_layer_b_out_kernel, n_prompt=n_prompt),
        grid=(m // TOKEN_TILE,),
        in_specs=[_rows(D_MODEL), tokp_spec, toks_spec, _prompt_rows(MEM_Q, n_prompt),
                  _sample_rows(MEM_Q), _resident(wo.shape), _layer_block(g2.shape, 1),
                  _layer_block(wgu.shape, 0), _layer_block(wd.shape, 0), _resident(gf.shape)],
        out_specs=[_prompt_rows(D_MODEL, n_prompt),
                   pl.BlockSpec((TOKEN_TILE, D_MODEL), lambda i: (0, 0))],
        out_shape=[jax.ShapeDtypeStruct((n_prompt * TOKEN_TILE, D_MODEL), F32),
                   jax.ShapeDtypeStruct((TOKEN_TILE, D_MODEL), F32)],
        compiler_params=_params("arbitrary"),
        name="layer_b_out",
    )(x, tokp, toks, mop, mos, wo, g2, wgu, wd, gf)


def _mem_kv_kernel(x_ref, g_ref, wt_ref, *refs):
    n = (len(refs) - 2) // 2
    k_ref, v_ref = refs[n:n + 2]
    _run_cast_jobs(refs[:n] + refs[n + 2:])
    x = x_ref[...]
    xn = x * lax.rsqrt(jnp.mean(x * x, axis=-1, keepdims=True) + EPS)
    for l in range(2):
        h = (xn * g_ref[l]).astype(BF)
        kvt = _dot_nt(wt_ref[l], h)
        k_ref[l, 0] = kvt[:MEM_Q, :]
        v_ref[l, 0] = kvt[MEM_Q:, :]


def _mem_kv(mem, g, wt, casts, *, batch):
    out_spec = pl.BlockSpec((2, 1, MEM_Q, MEM_TOKENS), lambda b: (0, b, 0, 0))
    jobs = [_cast_job(cw, layer, rows, (batch,)) for cw, layer, rows in casts]
    return pl.pallas_call(
        _mem_kv_kernel,
        grid=(batch,),
        in_specs=[pl.BlockSpec((MEM_TOKENS, D_MODEL), lambda b: (b, 0)), _resident(g.shape),
                  _resident(wt.shape)] + [j[0] for j in jobs],
        out_specs=[out_spec, out_spec] + [j[1] for j in jobs],
        out_shape=[jax.ShapeDtypeStruct((2, batch, MEM_Q, MEM_TOKENS), F32)] * 2
                  + [j[2] for j in jobs],
        compiler_params=_params("arbitrary"),
        name="mem_kv",
    )(mem, g, wt, *[c[0] for c in casts])


def _gla_prompt_kernel(*refs):
    ns = GLA_SEQS
    seq_in = [refs[5 * i:5 * i + 5] for i in range(ns)]
    rest = refs[5 * ns:-1]
    s_scr = refs[-1]
    n = (len(rest) - 2) // 2
    tok_ref, st_ref = rest[n:n + 2]
    _run_cast_jobs(rest[:n] + rest[n + 2:])
    c = pl.program_id(1)
    C = GLA_CHUNK

    @pl.when(c == 0)
    def _():
        s_scr[...] = jnp.zeros(s_scr.shape, F32)

    row = lax.broadcasted_iota(jnp.int32, (C, C), 0)
    col = lax.broadcasted_iota(jnp.int32, (C, C), 1)
    causal = row >= col
    ltri = jnp.where(causal, 1.0, 0.0).astype(BF)
    for i in range(ns):
        _gla_chunk(seq_in[i], tok_ref.at[i], s_scr.at[i], causal, ltri)

    @pl.when(c == pl.num_programs(1) - 1)
    def _():
        st_ref[...] = s_scr[...]


def _gla_chunk(in_refs, tok_ref, s_scr, causal, ltri):
    q_ref, k_ref, la_ref, v_ref, r_ref = in_refs
    C = GLA_CHUNK
    for h in range(GLA_HEADS):
        sk = slice(h * GLA_DKP, (h + 1) * GLA_DKP)
        sv = slice(h * GLA_DVP, (h + 1) * GLA_DVP)
        la = la_ref[:, sk]
        hi = la.astype(BF)
        lo = (la - hi.astype(F32)).astype(BF)
        bb = _dot(ltri, jnp.concatenate([hi, lo], axis=1))
        b = bb[:, :GLA_DKP] + bb[:, GLA_DKP:]
        b_ref = b[C // 2 - 1:C // 2, :]
        b_last = b[C - 1:C, :]
        q = q_ref[:, sk]
        k = k_ref[:, sk]
        v = v_ref[:, sv].astype(BF)
        qe = (q * jnp.exp(b - b_ref)).astype(BF)
        ke = (k * jnp.exp(b_ref - b)).astype(BF)
        a = jnp.where(causal, _dot_nt(qe, ke), 0.0).astype(BF)
        st = s_scr[h]
        qb = (q * jnp.exp(b)).astype(BF)
        o = _dot(a, v) + _dot_nt(qb, st.astype(BF))
        kd = (k * jnp.exp(b_last - b)).astype(BF)
        s_scr[h] = st * jnp.exp(b_last) + _dot_tn(v, kd)
        ms = jnp.sum(o * o, axis=1, keepdims=True) * (1.0 / GLA_DV)
        tok_ref[:, sv] = (o * lax.rsqrt(ms + EPS) * r_ref[:, sv]).astype(BF)


def _gla_prompt(q, k, la, v, r, casts, *, batch, seq):
    ns = GLA_SEQS
    m = batch * seq
    nc = seq // GLA_CHUNK
    qk = GLA_HEADS * GLA_DKP
    vr = GLA_HEADS * GLA_DVP
    grid = (batch // ns, nc)

    def tok_map(i):
        return lambda b, c: ((b * ns + i) * nc + c, 0)

    seq_specs = []
    for i in range(ns):
        seq_specs += [pl.BlockSpec((GLA_CHUNK, qk), tok_map(i))] * 3
        seq_specs += [pl.BlockSpec((GLA_CHUNK, vr), tok_map(i))] * 2
    jobs = [_cast_job(cw, layer, rows, grid) for cw, layer, rows in casts]
    res = pl.pallas_call(
        _gla_prompt_kernel,
        grid=grid,
        in_specs=seq_specs + [j[0] for j in jobs],
        out_specs=[pl.BlockSpec((ns, GLA_CHUNK, vr), lambda b, c: (b, c, 0)),
                   pl.BlockSpec((ns, GLA_HEADS, GLA_DVP, GLA_DKP), lambda b, c: (b, 0, 0, 0))]
                  + [j[1] for j in jobs],
        out_shape=[jax.ShapeDtypeStruct((batch, seq, vr), BF),
                   jax.ShapeDtypeStruct((batch, GLA_HEADS, GLA_DVP, GLA_DKP), F32)]
                  + [j[2] for j in jobs],
        scratch_shapes=[pltpu.VMEM((ns, GLA_HEADS, GLA_DVP, GLA_DKP), F32)],
        compiler_params=_params("arbitrary", "arbitrary"),
        name="gla_prompt",
    )(*([q, k, la, v, r] * ns), *[c[0] for c in casts])
    return (res[0].reshape(m, vr),) + tuple(res[1:])


def _mem_attn_prompt_kernel(q_ref, mk_ref, mv_ref, o_ref, kbd_scr, vbd_scr):
    tq = q_ref.shape[0]

    @pl.when(pl.program_id(1) == 0)
    def _():
        shape = (MEM_Q, MEM_HEADS * MEM_TOKENS)
        rh = lax.broadcasted_iota(jnp.int32, shape, 0) // MEM_HEAD_DIM
        ch = lax.broadcasted_iota(jnp.int32, shape, 1) // MEM_TOKENS
        diag = rh == ch
        kbd_scr[...] = jnp.where(diag, jnp.concatenate([mk_ref[0, 0]] * MEM_HEADS, axis=1),
                                 0.0).astype(BF)
        vbd_scr[...] = jnp.where(diag, jnp.concatenate([mv_ref[0, 0]] * MEM_HEADS, axis=1),
                                 0.0).astype(BF)

    s = _dot(q_ref[...], kbd_scr[...])
    ps, inv = [], []
    for h in range(MEM_HEADS):
        sh = s[:, h * MEM_TOKENS:(h + 1) * MEM_TOKENS]
        e = jnp.exp(sh - jnp.max(sh, axis=1, keepdims=True))
        inv.append(1.0 / jnp.sum(e, axis=1, keepdims=True))
        ps.append(e.astype(BF))
    o = _dot_nt(jnp.concatenate(ps, axis=1), vbd_scr[...])
    lane_h = lax.broadcasted_iota(jnp.int32, (tq, MEM_Q), 1) // MEM_HEAD_DIM
    scale = jnp.where(lane_h == 0, inv[0],
                      jnp.where(lane_h == 1, inv[1], jnp.where(lane_h == 2, inv[2], inv[3])))
    o_ref[...] = (o * scale).astype(BF)


def _mem_attn_prompt(mq, mk_t, mv_t, layer, *, batch, seq):
    tq = TOKEN_TILE
    nq = seq // tq
    kv_spec = pl.BlockSpec((1, 1, MEM_Q, MEM_TOKENS), lambda b, i: (layer, b, 0, 0))
    q_spec = pl.BlockSpec((tq, MEM_Q), lambda b, i: (b * nq + i, 0))
    bd = pltpu.VMEM((MEM_Q, MEM_HEADS * MEM_TOKENS), BF)
    return pl.pallas_call(
        _mem_attn_prompt_kernel,
        grid=(batch, nq),
        in_specs=[q_spec, kv_spec, kv_spec],
        out_specs=q_spec,
        out_shape=jax.ShapeDtypeStruct((batch * seq, MEM_Q), BF),
        scratch_shapes=[bd, bd],
        compiler_params=_params("parallel", "arbitrary"),
        name="mem_attn_prompt",
    )(mq, mk_t, mv_t)


def _swa_prompt_kernel(sink_ref, q_ref, kp_ref, kc_ref, vp_ref, vc_ref, *refs):
    nc = (len(refs) - 1) // 2
    o_ref = refs[nc]
    _run_cast_jobs(refs[:nc] + refs[nc + 1:])
    n = pl.program_id(1)
    blk = WINDOW
    qi = lax.broadcasted_iota(jnp.int32, (blk, 2 * blk), 0)
    kj = lax.broadcasted_iota(jnp.int32, (blk, 2 * blk), 1)
    d = blk + qi - kj
    first_key = jnp.where(n > 0, 0, blk)
    valid = (d >= 0) & (d < WINDOW) & (kj >= first_key)
    bias = jnp.where(valid, 0.0, -jnp.inf)
    for kh in range(SWA_KV_HEADS):
        sl = slice(kh * HEAD_DIMP, (kh + 1) * HEAD_DIMP)
        k_lo = jnp.concatenate([kp_ref[:, sl], kc_ref[:, sl]], axis=0)
        v_lo = jnp.concatenate([vp_ref[:, sl], vc_ref[:, sl]], axis=0)
        kb = (k_lo.astype(BF), pltpu.roll(k_lo, HEAD_DIM, 1).astype(BF))
        vb = (v_lo.astype(BF), pltpu.roll(v_lo, HEAD_DIM, 1).astype(BF))
        for pj in range(SWA_PAIRS_PER_KV):
            pp = kh * SWA_PAIRS_PER_KV + pj
            q = q_ref[pp]
            o = None
            for half in range(2):
                s = _dot_nt(q, kb[half]) + bias
                sink = sink_ref[2 * pp + half]
                m = jnp.maximum(jnp.max(s, axis=1, keepdims=True), sink)
                e = jnp.exp(s - m)
                l = jnp.sum(e, axis=1, keepdims=True) + jnp.exp(sink - m)
                oh = _dot(e.astype(BF), vb[half]) * (1.0 / l)
                o = oh if o is None else o + oh
            o_ref[pp] = o.astype(BF)


def _swa_prompt(sinks, q, k, v, casts, *, batch, seq):
    nb = seq // WINDOW
    kw = SWA_KV_HEADS * HEAD_DIMP
    q_spec = pl.BlockSpec((SWA_Q_PAIRS, WINDOW, HEAD_DIMP), lambda b, n: (0, b * nb + n, 0))
    prev = pl.BlockSpec((WINDOW, kw), lambda b, n: (b * nb + jnp.maximum(n - 1, 0), 0))
    cur = pl.BlockSpec((WINDOW, kw), lambda b, n: (b * nb + n, 0))
    jobs = [_cast_job(cw, layer, rows, (batch, nb)) for cw, layer, rows in casts]
    return pl.pallas_call(
        _swa_prompt_kernel,
        grid=(batch, nb),
        in_specs=[pl.BlockSpec(memory_space=pltpu.SMEM), q_spec, prev, cur, prev, cur]
                 + [j[0] for j in jobs],
        out_specs=[q_spec] + [j[1] for j in jobs],
        out_shape=[jax.ShapeDtypeStruct((SWA_Q_PAIRS, batch * seq, HEAD_DIMP), BF)]
                  + [j[2] for j in jobs],
        compiler_params=_params("arbitrary", "arbitrary"),
        name="swa_prompt",
    )(sinks, q, k, k, v, v, *[c[0] for c in casts])


_SAMPLE_BB = 8
_DEC_SEQ = 4
_DEC_BATCH = 128
_GLA_DK_BLK = 32


def _gla_sample_kernel(s_ref, q_ref, k_ref, la_ref, v_ref, r_ref, so_ref, tok_ref, o_scr):
    j = pl.program_id(1)

    @pl.when(j == 0)
    def _():
        o_scr[...] = jnp.zeros(o_scr.shape, F32)

    def body(dk, carry):
        s = s_ref[0, 0, dk]
        for t in range(_DEC_SEQ):
            a = jnp.exp(la_ref[t, 0, pl.ds(dk, 1), :])
            s = a * s + k_ref[t, 0, pl.ds(dk, 1), :] * v_ref[t, 0, :GLA_DV, :]
            o_scr[t] = o_scr[t] + q_ref[t, 0, pl.ds(dk, 1), :] * s
        so_ref[0, 0, dk] = s
        return carry

    lax.fori_loop(0, _GLA_DK_BLK, body, 0)

    @pl.when(j == pl.num_programs(1) - 1)
    def _():
        tok_ref[...] = jnp.zeros(tok_ref.shape, F32)
        for t in range(_DEC_SEQ):
            o = o_scr[t]
            ms = jnp.sum(o * o, axis=0, keepdims=True) * (1.0 / GLA_DV)
            tok_ref[t, 0, :GLA_DV, :] = o * lax.rsqrt(ms + EPS) * r_ref[t, 0, :GLA_DV, :]


def _gla_sample(state, q, k, la, v, r):
    qk_spec = pl.BlockSpec((_DEC_SEQ, 1, _GLA_DK_BLK, _DEC_BATCH), lambda h, j: (0, h, j, 0))
    vr_spec = pl.BlockSpec((_DEC_SEQ, 1, GLA_DVP, _DEC_BATCH), lambda h, j: (0, h, 0, 0))
    s_spec = pl.BlockSpec((1, 1, _GLA_DK_BLK, GLA_DV, _DEC_BATCH), lambda h, j: (0, h, j, 0, 0))
    return pl.pallas_call(
        _gla_sample_kernel,
        grid=(GLA_HEADS, GLA_DK // _GLA_DK_BLK),
        in_specs=[s_spec, qk_spec, qk_spec, qk_spec, vr_spec, vr_spec],
        out_specs=[s_spec, vr_spec],
        out_shape=[jax.ShapeDtypeStruct(state.shape, F32),
                   jax.ShapeDtypeStruct(v.shape, F32)],
        scratch_shapes=[pltpu.VMEM((_DEC_SEQ, GLA_DV, _DEC_BATCH), F32)],
        compiler_params=_params("parallel", "arbitrary"),
        name="gla_sample",
    )(state, q, k, la, v, r)


def _mem_attn_sample_kernel(q_ref, mk_ref, mv_ref, o_ref):
    for bi in range(_SAMPLE_BB):
        s = _dot(q_ref[bi], mk_ref[0, bi].astype(BF))
        e = jnp.exp(s - jnp.max(s, axis=1, keepdims=True))
        p = (e * (1.0 / jnp.sum(e, axis=1, keepdims=True))).astype(BF)
        o_ref[bi] = _dot_nt(p, mv_ref[0, bi].astype(BF))


def _mem_attn_sample(qbd, mk_t, mv_t, layer):
    nb = qbd.shape[0]
    nr = MEM_HEADS * _DEC_SEQ
    kv_spec = pl.BlockSpec((1, _SAMPLE_BB, MEM_Q, MEM_TOKENS), lambda i: (layer, i, 0, 0))
    q_spec = pl.BlockSpec((_SAMPLE_BB, nr, MEM_Q), lambda i: (i, 0, 0))
    return pl.pallas_call(
        _mem_attn_sample_kernel,
        grid=(nb // _SAMPLE_BB,),
        in_specs=[q_spec, kv_spec, kv_spec],
        out_specs=q_spec,
        out_shape=jax.ShapeDtypeStruct((nb, nr, MEM_Q), F32),
        compiler_params=_params("parallel"),
        name="mem_attn_sample",
    )(qbd, mk_t, mv_t)


def _swa_sample_kernel(q_ref, sink_ref, kc_ref, vc_ref, kn_ref, vn_ref, o_ref, ko_ref, vo_ref):
    nq = SWA_Q_HEADS * _DEC_SEQ
    t = lax.broadcasted_iota(jnp.int32, (nq, WINDOW), 0) % _DEC_SEQ
    pos = lax.broadcasted_iota(jnp.int32, (nq, WINDOW), 1)
    new0 = WINDOW - _DEC_SEQ
    bias_c = jnp.where(pos > t, 0.0, -jnp.inf)
    bias_n = jnp.where((pos >= new0) & (pos - new0 <= t), 0.0, -jnp.inf)
    is_new = lax.broadcasted_iota(jnp.int32, (SWA_KV_HEADS * HEAD_DIM, WINDOW), 1) >= new0
    sink = sink_ref[...]
    for bi in range(_SAMPLE_BB):
        q = q_ref[bi]
        kc, vc, kn, vn = kc_ref[bi], vc_ref[bi], kn_ref[bi], vn_ref[bi]
        sc = _dot(q, kc.astype(BF)) + bias_c
        sn = _dot(q, kn.astype(BF)) + bias_n
        m = jnp.maximum(jnp.maximum(jnp.max(sc, axis=1, keepdims=True),
                                    jnp.max(sn, axis=1, keepdims=True)), sink)
        ec = jnp.exp(sc - m)
        en = jnp.exp(sn - m)
        l = (jnp.sum(ec, axis=1, keepdims=True) + jnp.sum(en, axis=1, keepdims=True)
             + jnp.exp(sink - m))
        inv = 1.0 / l
        o_ref[bi] = (_dot_nt((ec * inv).astype(BF), vc.astype(BF))
                     + _dot_nt((en * inv).astype(BF), vn.astype(BF)))
        ko_ref[bi] = jnp.where(is_new, kn, pltpu.roll(kc, new0, 1))
        vo_ref[bi] = jnp.where(is_new, vn, pltpu.roll(vc, new0, 1))


def _swa_sample(qbd, sink_col, kc, vc, kn, vn):
    nb = qbd.shape[0]
    kw = SWA_KV_HEADS * HEAD_DIM
    nq = SWA_Q_HEADS * _DEC_SEQ
    kv_spec = pl.BlockSpec((_SAMPLE_BB, kw, WINDOW), lambda i: (i, 0, 0))
    q_spec = pl.BlockSpec((_SAMPLE_BB, nq, kw), lambda i: (i, 0, 0))
    return pl.pallas_call(
        _swa_sample_kernel,
        grid=(nb // _SAMPLE_BB,),
        in_specs=[q_spec, _resident((nq, 1)), kv_spec, kv_spec, kv_spec, kv_spec],
        out_specs=[q_spec, kv_spec, kv_spec],
        out_shape=[jax.ShapeDtypeStruct((nb, nq, kw), F32),
                   jax.ShapeDtypeStruct(kc.shape, F32), jax.ShapeDtypeStruct(kc.shape, F32)],
        compiler_params=_params("parallel"),
        name="swa_sample",
    )(qbd, sink_col, kc, vc, kn, vn)


def _pad_heads(w, heads, dim, dim_p, axis):
    shape = w.shape
    w = w.reshape(shape[:axis] + (heads, dim) + shape[axis + 1:])
    pad = [(0, 0)] * w.ndim
    pad[axis + 1] = (0, dim_p - dim)
    w = jnp.pad(w, pad)
    return w.reshape(shape[:axis] + (heads * dim_p,) + shape[axis + 1:])


def _rope_tables(pos):
    half = ROT_DIM // 2
    inv_freq = jnp.exp(-math.log(ROPE_THETA) * jnp.arange(0, ROT_DIM, 2, dtype=F32) / ROT_DIM)
    ang = pos[:, None] * inv_freq[None, :]
    cos, sin = jnp.cos(ang), jnp.sin(ang)
    n = pos.shape[0]
    rest = HEAD_DIM - ROT_DIM
    c = jnp.concatenate([cos, cos, jnp.ones((n, rest), F32)], axis=1)
    s1 = jnp.concatenate([-sin, jnp.zeros((n, HEAD_DIM - half), F32)], axis=1)
    s2 = jnp.concatenate([jnp.zeros((n, half), F32), sin, jnp.zeros((n, rest), F32)], axis=1)
    return tuple(jnp.tile(a, (1, HEAD_DIMP // HEAD_DIM)) for a in (c, s1, s2))


def _prep_weights(p):
    w = {}
    for name in ("ffn1_norm", "ffn2_norm", "mix_norm"):
        w[name] = p[name][:, None, :]
    qk = GLA_HEADS * GLA_DK
    vv = GLA_HEADS * GLA_DV
    a_in = p["a_w_in"][0]
    o = 0
    wq = _pad_heads(a_in[:, o:o + qk], GLA_HEADS, GLA_DK, GLA_DKP, 1); o += qk
    wk = _pad_heads(a_in[:, o:o + qk], GLA_HEADS, GLA_DK, GLA_DKP, 1); o += qk
    wv = _pad_heads(a_in[:, o:o + vv], GLA_HEADS, GLA_DV, GLA_DVP, 1); o += vv
    wr = _pad_heads(a_in[:, o:o + vv], GLA_HEADS, GLA_DV, GLA_DVP, 1); o += vv
    wg = jnp.pad(a_in[:, o:o + GLA_RANK], ((0, 0), (0, GLA_RANKP - GLA_RANK))); o += GLA_RANK
    wm = a_in[:, o:]
    w["a_in"] = jnp.concatenate([wq, wk, wv, wr, wg, wm], axis=1).astype(BF)
    gate = _pad_heads(p["a_w_gate"][0], GLA_HEADS, GLA_DK, GLA_DKP, 1)
    w["a_gate"] = jnp.pad(gate, ((0, GLA_RANKP - GLA_RANK), (0, 0))).astype(BF)
    w["a_bgate"] = _pad_heads(p["a_b_gate"][0][None, :], GLA_HEADS, GLA_DK, GLA_DKP, 1)
    w["a_gn"] = jnp.tile(jnp.pad(p["a_out_norm"][0], (0, GLA_DVP - GLA_DV)), GLA_HEADS)[None, :]
    a_out = p["a_w_out"][0]
    w["a_out"] = jnp.concatenate(
        [_pad_heads(a_out[:vv], GLA_HEADS, GLA_DV, GLA_DVP, 0), a_out[vv:]], axis=0).astype(BF)
    w["b_in"] = p["b_w_in"][0].astype(BF)
    w["b_out"] = p["b_w_out"][0].astype(BF)
    nkv = SWA_KV_HEADS * HEAD_DIM
    w_kv = p["w_kv"]
    w["kv"] = jnp.concatenate(
        [_pad_heads(w_kv[:, :nkv], SWA_KV_HEADS, HEAD_DIM, HEAD_DIMP, 1),
         _pad_heads(w_kv[:, nkv:], SWA_KV_HEADS, HEAD_DIM, HEAD_DIMP, 1)], axis=1).astype(BF)
    w["mem_t"] = p["mem_w_kv"].transpose(0, 2, 1).astype(BF)
    return w


def _compact_kv(a, batch, seq):
    return a.reshape(batch, seq, SWA_KV_HEADS, HEAD_DIMP)[..., :HEAD_DIM]


def kernel(x_prompt, x_sample, state_gla, cache_swa_k, cache_swa_v, cache_mem_k, cache_mem_v,
           mem_prompt, ffn1_norm, ffn1_w_gu, ffn1_w_down, mix_norm, ffn2_norm, ffn2_w_gu,
           ffn2_w_down, mem_norm, mem_w_kv, a_w_in, a_w_gate, a_b_gate, a_out_norm, a_w_out,
           kv_norm, w_kv, b_w_in, b_sinks, b_w_out, final_norm):
    p = dict(ffn1_norm=ffn1_norm, ffn1_w_gu=ffn1_w_gu, ffn1_w_down=ffn1_w_down,
             mix_norm=mix_norm, ffn2_norm=ffn2_norm, ffn2_w_gu=ffn2_w_gu,
             ffn2_w_down=ffn2_w_down, mem_w_kv=mem_w_kv, a_w_in=a_w_in, a_w_gate=a_w_gate,
             a_b_gate=a_b_gate, a_out_norm=a_out_norm, a_w_out=a_w_out, w_kv=w_kv,
             b_w_in=b_w_in, b_w_out=b_w_out)
    w = _prep_weights(p)
    batch, seq, _ = x_prompt.shape
    nb, t, _ = x_sample.shape
    assert nb == _DEC_BATCH and t == _DEC_SEQ and nb * t == TOKEN_TILE
    assert seq % TOKEN_TILE == 0 and seq % GLA_CHUNK == 0
    mp = batch * seq
    kw = SWA_KV_HEADS * HEAD_DIM
    sinks = b_sinks[0]

    mem_k_t, mem_v_t, wgu_a1, wd_a1 = _mem_kv(
        mem_prompt.reshape(batch * MEM_TOKENS, D_MODEL), mem_norm[:, None, :], w["mem_t"],
        [(ffn1_w_gu, 0, D_MODEL // batch), (ffn1_w_down, 0, FFN_DIM // batch)], batch=batch)
    state_t = state_gla.transpose(0, 2, 3, 4, 1)
    kc_t = cache_swa_k.transpose(0, 2, 3, 1).reshape(nb, kw, WINDOW)
    vc_t = cache_swa_v.transpose(0, 2, 3, 1).reshape(nb, kw, WINDOW)
    cmk_t = cache_mem_k.transpose(0, 1, 3, 4, 2).reshape(2, nb, MEM_Q, MEM_TOKENS)
    cmv_t = cache_mem_v.transpose(0, 1, 3, 4, 2).reshape(2, nb, MEM_Q, MEM_TOKENS)

    tabs = tuple(
        jnp.concatenate([a, jnp.tile(b, (nb, 1))], axis=0)
        for a, b in zip(_rope_tables(jnp.arange(seq, dtype=F32)),
                        _rope_tables(PAST_LEN + jnp.arange(t, dtype=F32))))

    mem_mask = (jnp.arange(MEM_Q) // MEM_HEAD_DIM)[None, :] == jnp.arange(MEM_HEADS)[:, None]
    kv_mask = (jnp.arange(kw) // HEAD_DIM)[None, :] == jnp.arange(SWA_KV_HEADS)[:, None]

    def mem_attn_sample(mq, layer):
        q4 = mq[mp:].reshape(nb, 1, t, MEM_Q)
        qbd = jnp.where(mem_mask[None, :, None, :], q4, 0).reshape(nb, MEM_HEADS * t, MEM_Q)
        o = _mem_attn_sample(qbd, cmk_t, cmv_t, layer).reshape(nb, MEM_HEADS, t, MEM_Q)
        o = jnp.sum(jnp.where(mem_mask[None, :, None, :], o, 0.0), axis=1)
        return o.reshape(nb * t, MEM_Q).astype(BF)

    def lanes(a, width):
        return a[mp:].reshape(nb, t, GLA_HEADS, width).transpose(1, 2, 3, 0)

    x, q, k, la, v, r, mq = _layer_a_in(
        x_prompt.reshape(mp, D_MODEL), x_sample.reshape(nb * t, D_MODEL), w["ffn1_norm"],
        wgu_a1, wd_a1, w["mix_norm"], w["a_in"], w["a_gate"], w["a_bgate"], w["a_gn"])
    tok_p, st_p, wgu_a2, wd_a2, wgu_b1, wd_b1 = _gla_prompt(
        q, k, la, v, r,
        [(ffn2_w_gu, 0, 32), (ffn2_w_down, 0, 128), (ffn1_w_gu, 1, 32), (ffn1_w_down, 1, 128)],
        batch=batch, seq=seq)
    mo_p = _mem_attn_prompt(mq, mem_k_t, mem_v_t, 0, batch=batch, seq=seq)
    st_s, tok_s = _gla_sample(state_t, lanes(q, GLA_DKP), lanes(k, GLA_DKP), lanes(la, GLA_DKP),
                              lanes(v, GLA_DVP), lanes(r, GLA_DVP))
    tok_s = tok_s.transpose(3, 0, 1, 2).reshape(nb * t, GLA_HEADS * GLA_DVP).astype(BF)
    mo_s = mem_attn_sample(mq, 0)
    x, k_sh, v_sh = _layer_a_out(x, tok_p, tok_s, mo_p, mo_s, w["a_out"], w["ffn2_norm"],
                                 wgu_a2, wd_a2, kv_norm[None, :], w["kv"], tabs, seq=seq)

    x, qs, mq = _layer_b_in(x, w["ffn1_norm"], wgu_b1, wd_b1, w["mix_norm"],
                            w["b_in"], tabs, n_prompt=mp // TOKEN_TILE, seq=seq)
    tok_p, wgu_b2, wd_b2 = _swa_prompt(
        sinks, qs, k_sh, v_sh, [(ffn2_w_gu, 1, 16), (ffn2_w_down, 1, 32)], batch=batch, seq=seq)
    mo_p = _mem_attn_prompt(mq, mem_k_t, mem_v_t, 1, batch=batch, seq=seq)

    def new_rows(a):
        a = _compact_kv(a[mp:], nb, t).reshape(nb, t, kw).transpose(0, 2, 1)
        return jnp.pad(a, ((0, 0), (0, 0), (WINDOW - t, 0)))

    q5 = qs[:, mp:].reshape(SWA_Q_PAIRS, nb, t, 2, HEAD_DIM).transpose(1, 0, 3, 2, 4)
    q5 = q5.reshape(nb, SWA_KV_HEADS, SWA_GROUP, t, HEAD_DIM)
    qbd = jnp.where(kv_mask[None, :, None, None, :], jnp.tile(q5, (1, 1, 1, 1, SWA_KV_HEADS)), 0)
    qbd = qbd.reshape(nb, SWA_Q_HEADS * t, kw)
    o, k_s, v_s = _swa_sample(qbd, jnp.repeat(sinks, t)[:, None], kc_t, vc_t,
                              new_rows(k_sh), new_rows(v_sh))
    o = o.reshape(nb, SWA_KV_HEADS, SWA_GROUP, t, kw)
    o = jnp.where(kv_mask[None, :, None, None, :], o, 0.0)
    o = o.reshape(nb, SWA_KV_HEADS, SWA_GROUP, t, SWA_KV_HEADS, HEAD_DIM).sum(axis=4)
    tok_s = o.reshape(nb, SWA_Q_PAIRS, 2, t, HEAD_DIM).transpose(1, 0, 3, 2, 4)
    tok_s = tok_s.reshape(SWA_Q_PAIRS, nb * t, HEAD_DIMP).astype(BF)
    mo_s = mem_attn_sample(mq, 1)
    y_p, y_s = _layer_b_out(x, tok_p, tok_s, mo_p, mo_s, w["b_out"], w["ffn2_norm"],
                            wgu_b2, wd_b2, final_norm[None, :])

    gla_prompt = st_p.transpose(0, 1, 3, 2)[None, :, :, :GLA_DK, :GLA_DV]
    gla_sample = st_s.transpose(0, 4, 1, 2, 3)
    def last_window(a):
        tiles_per_seq = seq // TOKEN_TILE
        a = a.reshape(-1, TOKEN_TILE, SWA_KV_HEADS * HEAD_DIMP)
        a = a[tiles_per_seq - 1:batch * tiles_per_seq:tiles_per_seq, TOKEN_TILE - WINDOW:]
        return a.reshape(batch, WINDOW, SWA_KV_HEADS, HEAD_DIMP)[..., :HEAD_DIM]

    swa_k_prompt = last_window(k_sh)
    swa_v_prompt = last_window(v_sh)
    swa_k_sample = k_s.reshape(nb, SWA_KV_HEADS, HEAD_DIM, WINDOW).transpose(0, 3, 1, 2)
    swa_v_sample = v_s.reshape(nb, SWA_KV_HEADS, HEAD_DIM, WINDOW).transpose(0, 3, 1, 2)
    mem_shape = (2, batch, MEM_HEADS, MEM_HEAD_DIM, MEM_TOKENS)
    mem_k_prompt = mem_k_t.reshape(mem_shape).transpose(0, 1, 4, 2, 3)
    mem_v_prompt = mem_v_t.reshape(mem_shape).transpose(0, 1, 4, 2, 3)
    return (y_p.reshape(batch, seq, D_MODEL), y_s.reshape(nb, t, D_MODEL), gla_prompt,
            gla_sample, swa_k_prompt, swa_v_prompt, swa_k_sample, swa_v_sample,
            mem_k_prompt, mem_v_prompt)
```

```python
import functools
import math

import jax
import jax.numpy as jnp
from jax import lax
from jax.experimental import pallas as pl
from jax.experimental.pallas import tpu as pltpu

F32 = jnp.float32
BF = jnp.bfloat16

D_MODEL = 1024
FFN_DIM = 2816
EPS = 1e-6

GLA_HEADS = 4
GLA_DK = 96
GLA_DV = 192
GLA_DKP = 128
GLA_DVP = 256
GLA_RANK = 16
GLA_RANKP = 128
GLA_GATE_NORM = 16.0
GLA_CHUNK = 256
GLA_SEQS = 2

HEAD_DIM = 64
HEAD_DIMP = 128
SWA_Q_HEADS = 12
SWA_KV_HEADS = 3
SWA_GROUP = SWA_Q_HEADS // SWA_KV_HEADS
SWA_Q_PAIRS = SWA_Q_HEADS // 2
SWA_PAIRS_PER_KV = SWA_GROUP // 2
WINDOW = 128
ROT_DIM = 16
ROPE_THETA = 500000.0
PAST_LEN = 8192

MEM_TOKENS = 256
MEM_HEADS = 4
MEM_HEAD_DIM = 64
MEM_Q = MEM_HEADS * MEM_HEAD_DIM

FFN_TF = 256
FFN_CHUNKS = FFN_DIM // FFN_TF
TOKEN_TILE = 512

VMEM_LIMIT = 60 * 1024 * 1024


def _params(*sem):
    return pltpu.CompilerParams(dimension_semantics=sem, vmem_limit_bytes=VMEM_LIMIT)


def _resident(shape):
    nd = len(shape)
    return pl.BlockSpec(shape, lambda *_: (0,) * nd, pipeline_mode=pl.Buffered(1))


def _layer_block(shape, layer):
    nd = len(shape)
    return pl.BlockSpec((1,) + tuple(shape[1:]), lambda *_: (layer,) + (0,) * (nd - 1),
                        pipeline_mode=pl.Buffered(1))


def _rows(width):
    return pl.BlockSpec((TOKEN_TILE, width), lambda i: (i, 0))


def _prompt_rows(width, n_prompt):
    return pl.BlockSpec((TOKEN_TILE, width), lambda i: (jnp.minimum(i, n_prompt - 1), 0))


def _sample_rows(width):
    return pl.BlockSpec((TOKEN_TILE, width), lambda i: (0, 0), pipeline_mode=pl.Buffered(1))


def _rms(x, g):
    ms = jnp.mean(x * x, axis=-1, keepdims=True)
    return x * lax.rsqrt(ms + EPS) * g


def _silu(x):
    return x * (1.0 / (1.0 + jnp.exp(-x)))


def _dot(a, b):
    return jnp.dot(a, b, preferred_element_type=F32)


def _dot_nt(a, b):
    return lax.dot_general(a, b, (((1,), (1,)), ((), ())), preferred_element_type=F32)


def _dot_tn(a, b):
    return lax.dot_general(a, b, (((0,), (0,)), ((), ())), preferred_element_type=F32)


def _pick(is_prompt, p_ref, s_ref):
    return jnp.where(is_prompt, p_ref[...], s_ref[...])


def _cast_job(w, layer, rows, grid):
    _, r, c = w.shape
    assert r % rows == 0 and rows % 16 == 0
    nblk = r // rows
    total = math.prod(grid)
    assert nblk <= total
    steps_per_block = total // nblk

    def block(*idx):
        step = idx[0]
        for dim, i in zip(grid[1:], idx[1:]):
            step = step * dim + i
        return jnp.minimum(step // steps_per_block, nblk - 1)

    in_spec = pl.BlockSpec((1, rows, c), lambda *idx: (layer, block(*idx), 0))
    out_spec = pl.BlockSpec((1, rows, c), lambda *idx: (0, block(*idx), 0))
    return in_spec, out_spec, jax.ShapeDtypeStruct((1, r, c), BF)


def _run_cast_jobs(refs):
    n = len(refs) // 2
    for src, dst in zip(refs[:n], refs[n:]):
        dst[...] = src[...].astype(BF)


def _ffn_half(x, g_ref, wgu_ref, wd_ref):
    h = _rms(x, g_ref[0]).astype(BF)
    acc = jnp.zeros(x.shape, F32)
    for c in range(FFN_CHUNKS):
        lo, hi = c * FFN_TF, (c + 1) * FFN_TF
        gate = _dot(h, wgu_ref[0, :, lo:hi])
        up = _dot(h, wgu_ref[0, :, FFN_DIM + lo:FFN_DIM + hi])
        a = (_silu(gate) * up).astype(BF)
        acc = acc + _dot(a, wd_ref[0, lo:hi, :])
    return x + 0.5 * acc


def _rope(x, c, s1, s2):
    return (x * c + pltpu.roll(x, HEAD_DIMP - ROT_DIM // 2, 1) * s1
            + pltpu.roll(x, ROT_DIM // 2, 1) * s2)


_A_Q = 0
_A_K = _A_Q + GLA_HEADS * GLA_DKP
_A_V = _A_K + GLA_HEADS * GLA_DKP
_A_R = _A_V + GLA_HEADS * GLA_DVP
_A_G = _A_R + GLA_HEADS * GLA_DVP
_A_M = _A_G + GLA_RANKP
_A_END = _A_M + MEM_Q


def _layer_a_in_kernel(xp_ref, xs_ref, g1_ref, wgu_ref, wd_ref, gm_ref, w_ref, wg_ref, bg_ref,
                       gn_ref, x_ref, q_ref, k_ref, la_ref, v_ref, r_ref, mq_ref, *, n_prompt):
    is_prompt = pl.program_id(0) < n_prompt
    x = _ffn_half(_pick(is_prompt, xp_ref, xs_ref), g1_ref, wgu_ref, wd_ref)
    x_ref[...] = x
    h = _rms(x, gm_ref[0]).astype(BF)

    proj = _dot(h, w_ref[...])

    def mm(a, b):
        return proj[:, a:b]

    q_ref[...] = mm(_A_Q, _A_K) * (GLA_DK ** -0.5)
    k_ref[...] = mm(_A_K, _A_V)
    v_ref[...] = mm(_A_V, _A_R)
    r_ref[...] = _silu(mm(_A_R, _A_G)) * gn_ref[...]
    z = _dot(mm(_A_G, _A_M).astype(BF), wg_ref[...]) + bg_ref[...]
    la_ref[...] = (jnp.minimum(z, 0.0) - jnp.log1p(jnp.exp(-jnp.abs(z)))) * (1.0 / GLA_GATE_NORM)
    mq_ref[...] = (mm(_A_M, _A_END) * (MEM_HEAD_DIM ** -0.5)).astype(BF)


def _layer_a_in(xp, xs, g1, wgu, wd, gm, w, wg, bg, gn):
    n_prompt = xp.shape[0] // TOKEN_TILE
    m = xp.shape[0] + xs.shape[0]
    qk = GLA_HEADS * GLA_DKP
    vr = GLA_HEADS * GLA_DVP
    outs = ((D_MODEL, F32), (qk, F32), (qk, F32), (qk, F32), (vr, F32), (vr, F32), (MEM_Q, BF))
    return pl.pallas_call(
        functools.partial(_layer_a_in_kernel, n_prompt=n_prompt),
        grid=(m // TOKEN_TILE,),
        in_specs=[_prompt_rows(D_MODEL, n_prompt), _sample_rows(D_MODEL),
                  _layer_block(g1.shape, 0), _layer_block(wgu.shape, 0), _layer_block(wd.shape, 0),
                  _layer_block(gm.shape, 0), _resident(w.shape), _resident(wg.shape),
                  _resident(bg.shape), _resident(gn.shape)],
        out_specs=[_rows(n) for n, _ in outs],
        out_shape=[jax.ShapeDtypeStruct((m, n), dt) for n, dt in outs],
        compiler_params=_params("arbitrary"),
        name="layer_a_in",
    )(xp, xs, g1, wgu, wd, gm, w, wg, bg, gn)


def _build_mem_block_diag(mk_ref, mv_ref, kbd_scr, vbd_scr):
    shape = (MEM_Q, MEM_HEADS * MEM_TOKENS)
    rh = lax.broadcasted_iota(jnp.int32, shape, 0) // MEM_HEAD_DIM
    ch = lax.broadcasted_iota(jnp.int32, shape, 1) // MEM_TOKENS
    diag = rh == ch
    kbd_scr[...] = jnp.where(diag, jnp.concatenate([mk_ref[0, 0]] * MEM_HEADS, axis=1),
                             0.0).astype(BF)
    vbd_scr[...] = jnp.where(diag, jnp.concatenate([mv_ref[0, 0]] * MEM_HEADS, axis=1),
                             0.0).astype(BF)


def _mem_attn_tile(q, kbd, vbd):
    tq = q.shape[0]
    s = _dot(q, kbd)
    ps, inv = [], []
    for h in range(MEM_HEADS):
        sh = s[:, h * MEM_TOKENS:(h + 1) * MEM_TOKENS]
        e = jnp.exp(sh - jnp.max(sh, axis=1, keepdims=True))
        inv.append(1.0 / jnp.sum(e, axis=1, keepdims=True))
        ps.append(e.astype(BF))
    o = _dot_nt(jnp.concatenate(ps, axis=1), vbd)
    lane_h = lax.broadcasted_iota(jnp.int32, (tq, MEM_Q), 1) // MEM_HEAD_DIM
    scale = jnp.where(lane_h == 0, inv[0],
                      jnp.where(lane_h == 1, inv[1], jnp.where(lane_h == 2, inv[2], inv[3])))
    return o * scale


def _swa_tile(q_ref, kprev_ref, kcur_ref, vprev_ref, vcur_ref, sink_ref, first_of_seq):
    blk = WINDOW
    nsub = TOKEN_TILE // blk
    qi = lax.broadcasted_iota(jnp.int32, (blk, 2 * blk), 0)
    kj = lax.broadcasted_iota(jnp.int32, (blk, 2 * blk), 1)
    d = blk + qi - kj
    band = (d >= 0) & (d < WINDOW)
    bias_mid = jnp.where(band, 0.0, -jnp.inf)
    bias_first = jnp.where(band & (kj >= blk), 0.0, -jnp.inf)
    bias0 = jnp.where(first_of_seq, bias_first, bias_mid)
    outs = [[None] * SWA_Q_PAIRS for _ in range(nsub)]
    for kh in range(SWA_KV_HEADS):
        sl = slice(kh * HEAD_DIMP, (kh + 1) * HEAD_DIMP)
        kblk = [kprev_ref[:, sl]] + [kcur_ref[s * blk:(s + 1) * blk, sl] for s in range(nsub)]
        vblk = [vprev_ref[:, sl]] + [vcur_ref[s * blk:(s + 1) * blk, sl] for s in range(nsub)]
        k2 = [(b.astype(BF), pltpu.roll(b, HEAD_DIM, 1).astype(BF)) for b in kblk]
        v2 = [(b.astype(BF), pltpu.roll(b, HEAD_DIM, 1).astype(BF)) for b in vblk]
        for sub in range(nsub):
            bias = bias0 if sub == 0 else bias_mid
            for pj in range(SWA_PAIRS_PER_KV):
                pp = kh * SWA_PAIRS_PER_KV + pj
                q = q_ref[pp, sub * blk:(sub + 1) * blk, :]
                o = None
                for half in range(2):
                    kb = jnp.concatenate([k2[sub][half], k2[sub + 1][half]], axis=0)
                    vb = jnp.concatenate([v2[sub][half], v2[sub + 1][half]], axis=0)
                    s = _dot_nt(q, kb) + bias
                    sink = sink_ref[2 * pp + half]
                    m = jnp.maximum(jnp.max(s, axis=1, keepdims=True), sink)
                    e = jnp.exp(s - m)
                    l = jnp.sum(e, axis=1, keepdims=True) + jnp.exp(sink - m)
                    oh = _dot(e.astype(BF), vb) * (1.0 / l)
                    o = oh if o is None else o + oh
                outs[sub][pp] = o.astype(BF)
    return jnp.concatenate([jnp.concatenate(row, axis=1) for row in outs], axis=0)


def _layer_a_out_kernel(x_ref, tokp_ref, toks_ref, mq_ref, mk_ref, mv_ref, mos_ref, wo_ref,
                        g2_ref, wgu_ref, wd_ref, gkv_ref, wkv_ref, c_ref, s1_ref, s2_ref,
                        xo_ref, k_ref, v_ref, kbd_scr, vbd_scr, *, n_prompt, tiles_per_seq):
    i = pl.program_id(0)
    is_prompt = i < n_prompt

    @pl.when(i % tiles_per_seq == 0)
    def _():
        _build_mem_block_diag(mk_ref, mv_ref, kbd_scr, vbd_scr)

    nt = GLA_HEADS * GLA_DVP
    mo = _mem_attn_tile(mq_ref[...], kbd_scr[...], vbd_scr[...]).astype(BF)
    mo = jnp.where(is_prompt, mo, mos_ref[...])
    x = (x_ref[...] + _dot(_pick(is_prompt, tokp_ref, toks_ref), wo_ref[:nt, :])
         + _dot(mo, wo_ref[nt:, :]))
    x = _ffn_half(x, g2_ref, wgu_ref, wd_ref)
    xo_ref[...] = x
    h = _rms(x, gkv_ref[...]).astype(BF)
    c, s1, s2 = c_ref[...], s1_ref[...], s2_ref[...]
    kw = SWA_KV_HEADS * HEAD_DIMP
    kv = _dot(h, wkv_ref[...])
    for hh in range(SWA_KV_HEADS):
        sl = slice(hh * HEAD_DIMP, (hh + 1) * HEAD_DIMP)
        k_ref[:, sl] = _rope(kv[:, sl], c, s1, s2)
    v_ref[...] = kv[:, kw:]


def _tab_spec(n_prompt, blocks_per_seq):
    return pl.BlockSpec((TOKEN_TILE, HEAD_DIMP),
                        lambda i: (jnp.where(i < n_prompt, i % blocks_per_seq, blocks_per_seq), 0))


def _mem_kv_spec(layer, n_prompt, tiles_per_seq):
    return pl.BlockSpec(
        (1, 1, MEM_Q, MEM_TOKENS),
        lambda i: (layer, jnp.minimum(i, n_prompt - 1) // tiles_per_seq, 0, 0))


def _mem_scratch():
    return pltpu.VMEM((MEM_Q, MEM_HEADS * MEM_TOKENS), BF)


def _layer_a_out(x, tokp, toks, mq, mk_t, mv_t, mos, wo, g2, wgu, wd, gkv, wkv, tabs, *, seq):
    m = x.shape[0]
    n_prompt = tokp.shape[0] // TOKEN_TILE
    tiles_per_seq = seq // TOKEN_TILE
    kw = SWA_KV_HEADS * HEAD_DIMP
    nt = GLA_HEADS * GLA_DVP
    tab = _tab_spec(n_prompt, tiles_per_seq)
    mem_spec = _mem_kv_spec(0, n_prompt, tiles_per_seq)
    return pl.pallas_call(
        functools.partial(_layer_a_out_kernel, n_prompt=n_prompt, tiles_per_seq=tiles_per_seq),
        grid=(m // TOKEN_TILE,),
        in_specs=[_rows(D_MODEL), _prompt_rows(nt, n_prompt), _sample_rows(nt),
                  _rows(MEM_Q), mem_spec, mem_spec, _sample_rows(MEM_Q), _resident(wo.shape),
                  _layer_block(g2.shape, 0), _layer_block(wgu.shape, 0), _layer_block(wd.shape, 0),
                  _resident(gkv.shape), _resident(wkv.shape), tab, tab, tab],
        out_specs=[_rows(D_MODEL), _rows(kw), _rows(kw)],
        out_shape=[jax.ShapeDtypeStruct((m, D_MODEL), F32),
                   jax.ShapeDtypeStruct((m, kw), F32), jax.ShapeDtypeStruct((m, kw), F32)],
        scratch_shapes=[_mem_scratch(), _mem_scratch()],
        compiler_params=_params("arbitrary"),
        name="layer_a_out",
    )(x, tokp, toks, mq, mk_t, mv_t, mos, wo, g2, wgu, wd, gkv, wkv, *tabs)


def _layer_b_in_kernel(x_ref, g1_ref, wgu_ref, wd_ref, gm_ref, w_ref, c_ref, s1_ref, s2_ref,
                       xo_ref, q_ref, mq_ref):
    x = _ffn_half(x_ref[...], g1_ref, wgu_ref, wd_ref)
    xo_ref[...] = x
    h = _rms(x, gm_ref[0]).astype(BF)
    c, s1, s2 = c_ref[...], s1_ref[...], s2_ref[...]
    nq = SWA_Q_HEADS * HEAD_DIM
    qm = _dot(h, w_ref[...])
    for pp in range(SWA_Q_PAIRS):
        q = qm[:, pp * HEAD_DIMP:(pp + 1) * HEAD_DIMP]
        q_ref[pp] = (_rope(q, c, s1, s2) * (HEAD_DIM ** -0.5)).astype(BF)
    mq_ref[...] = (qm[:, nq:] * (MEM_HEAD_DIM ** -0.5)).astype(BF)


def _layer_b_in(x, g1, wgu, wd, gm, w, tabs, *, n_prompt, seq):
    m = x.shape[0]
    tab = _tab_spec(n_prompt, seq // TOKEN_TILE)
    q_spec = pl.BlockSpec((SWA_Q_PAIRS, TOKEN_TILE, HEAD_DIMP), lambda i: (0, i, 0))
    return pl.pallas_call(
        _layer_b_in_kernel,
        grid=(m // TOKEN_TILE,),
        in_specs=[_rows(D_MODEL), _layer_block(g1.shape, 1), _layer_block(wgu.shape, 0),
                  _layer_block(wd.shape, 0), _layer_block(gm.shape, 1), _resident(w.shape),
                  tab, tab, tab],
        out_specs=[_rows(D_MODEL), q_spec, _rows(MEM_Q)],
        out_shape=[jax.ShapeDtypeStruct((m, D_MODEL), F32),
                   jax.ShapeDtypeStruct((SWA_Q_PAIRS, m, HEAD_DIMP), BF),
                   jax.ShapeDtypeStruct((m, MEM_Q), BF)],
        compiler_params=_params("arbitrary"),
        name="layer_b_in",
    )(x, g1, wgu, wd, gm, w, *tabs)


def _layer_b_out_kernel(sink_ref, x_ref, q_ref, kprev_ref, kcur_ref, vprev_ref, vcur_ref,
                        mq_ref, mk_ref, mv_ref, toks_ref, mos_ref, wo_ref, g2_ref, wgu_ref,
                        wd_ref, gf_ref, yp_ref, ys_ref, kbd_scr, vbd_scr,
                        *, n_prompt, tiles_per_seq):
    i = pl.program_id(0)
    is_prompt = i < n_prompt
    first_of_seq = i % tiles_per_seq == 0

    @pl.when(first_of_seq)
    def _():
        _build_mem_block_diag(mk_ref, mv_ref, kbd_scr, vbd_scr)

    nt = SWA_Q_HEADS * HEAD_DIM
    mo = _mem_attn_tile(mq_ref[...], kbd_scr[...], vbd_scr[...]).astype(BF)
    mo = jnp.where(is_prompt, mo, mos_ref[...])
    tok = _swa_tile(q_ref, kprev_ref, kcur_ref, vprev_ref, vcur_ref, sink_ref, first_of_seq)
    toks = jnp.concatenate([toks_ref[pp] for pp in range(SWA_Q_PAIRS)], axis=1)
    tok = jnp.where(is_prompt, tok, toks)
    x = x_ref[...] + _dot(tok, wo_ref[:nt, :]) + _dot(mo, wo_ref[nt:, :])
    y = _rms(_ffn_half(x, g2_ref, wgu_ref, wd_ref), gf_ref[...])

    @pl.when(is_prompt)
    def _():
        yp_ref[...] = y

    @pl.when(jnp.logical_not(is_prompt))
    def _():
        ys_ref[...] = y


def _layer_b_out(sinks, x, qs, k_sh, v_sh, mq, mk_t, mv_t, toks, mos, wo, g2, wgu, wd, gf, *, seq):
    m = x.shape[0]
    n_tiles = m // TOKEN_TILE
    n_prompt = n_tiles - 1
    tiles_per_seq = seq // TOKEN_TILE
    kw = SWA_KV_HEADS * HEAD_DIMP
    blocks_per_tile = TOKEN_TILE // WINDOW
    q_spec = pl.BlockSpec((SWA_Q_PAIRS, TOKEN_TILE, HEAD_DIMP), lambda i: (0, i, 0))
    prev_spec = pl.BlockSpec((WINDOW, kw), lambda i: (jnp.maximum(i * blocks_per_tile - 1, 0), 0))
    toks_spec = pl.BlockSpec((SWA_Q_PAIRS, TOKEN_TILE, HEAD_DIMP), lambda i: (0, 0, 0),
                             pipeline_mode=pl.Buffered(1))
    mem_spec = _mem_kv_spec(1, n_prompt, tiles_per_seq)
    return pl.pallas_call(
        functools.partial(_layer_b_out_kernel, n_prompt=n_prompt, tiles_per_seq=tiles_per_seq),
        grid=(n_tiles,),
        in_specs=[pl.BlockSpec(memory_space=pltpu.SMEM), _rows(D_MODEL), q_spec,
                  prev_spec, _rows(kw), prev_spec, _rows(kw), _rows(MEM_Q), mem_spec, mem_spec,
                  toks_spec, _sample_rows(MEM_Q), _resident(wo.shape), _layer_block(g2.shape, 1),
                  _layer_block(wgu.shape, 0), _layer_block(wd.shape, 0), _resident(gf.shape)],
        out_specs=[_prompt_rows(D_MODEL, n_prompt),
                   pl.BlockSpec((TOKEN_TILE, D_MODEL), lambda i: (0, 0))],
        out_shape=[jax.ShapeDtypeStruct((n_prompt * TOKEN_TILE, D_MODEL), F32),
                   jax.ShapeDtypeStruct((TOKEN_TILE, D_MODEL), F32)],
        scratch_shapes=[_mem_scratch(), _mem_scratch()],
        compiler_params=_params("arbitrary"),
        name="layer_b_out",
    )(sinks, x, qs, k_sh, k_sh, v_sh, v_sh, mq, mk_t, mv_t, toks, mos, wo, g2, wgu, wd, gf)


def _mem_kv_kernel(x_ref, g_ref, wt_ref, *refs):
    n = (len(refs) - 2) // 2
    k_ref, v_ref = refs[n:n + 2]
    _run_cast_jobs(refs[:n] + refs[n + 2:])
    x = x_ref[...]
    xn = x * lax.rsqrt(jnp.mean(x * x, axis=-1, keepdims=True) + EPS)
    for l in range(2):
        h = (xn * g_ref[l]).astype(BF)
        kvt = _dot_nt(wt_ref[l], h)
        k_ref[l, 0] = kvt[:MEM_Q, :]
        v_ref[l, 0] = kvt[MEM_Q:, :]


def _mem_kv(mem, g, wt, casts, *, batch):
    out_spec = pl.BlockSpec((2, 1, MEM_Q, MEM_TOKENS), lambda b: (0, b, 0, 0))
    jobs = [_cast_job(cw, layer, rows, (batch,)) for cw, layer, rows in casts]
    return pl.pallas_call(
        _mem_kv_kernel,
        grid=(batch,),
        in_specs=[pl.BlockSpec((MEM_TOKENS, D_MODEL), lambda b: (b, 0)), _resident(g.shape),
                  _resident(wt.shape)] + [j[0] for j in jobs],
        out_specs=[out_spec, out_spec] + [j[1] for j in jobs],
        out_shape=[jax.ShapeDtypeStruct((2, batch, MEM_Q, MEM_TOKENS), F32)] * 2
                  + [j[2] for j in jobs],
        compiler_params=_params("arbitrary"),
        name="mem_kv",
    )(mem, g, wt, *[c[0] for c in casts])


def _gla_prompt_kernel(*refs):
    ns = GLA_SEQS
    seq_in = [refs[5 * i:5 * i + 5] for i in range(ns)]
    rest = refs[5 * ns:-1]
    s_scr = refs[-1]
    n = (len(rest) - 2) // 2
    tok_ref, st_ref = rest[n:n + 2]
    _run_cast_jobs(rest[:n] + rest[n + 2:])
    c = pl.program_id(1)
    C = GLA_CHUNK

    @pl.when(c == 0)
    def _():
        s_scr[...] = jnp.zeros(s_scr.shape, F32)

    row = lax.broadcasted_iota(jnp.int32, (C, C), 0)
    col = lax.broadcasted_iota(jnp.int32, (C, C), 1)
    causal = row >= col
    ltri = jnp.where(causal, 1.0, 0.0).astype(BF)
    for i in range(ns):
        _gla_chunk(seq_in[i], tok_ref.at[i], s_scr.at[i], causal, ltri)

    @pl.when(c == pl.num_programs(1) - 1)
    def _():
        st_ref[...] = s_scr[...]


def _gla_chunk(in_refs, tok_ref, s_scr, causal, ltri):
    q_ref, k_ref, la_ref, v_ref, r_ref = in_refs
    C = GLA_CHUNK
    for h in range(GLA_HEADS):
        sk = slice(h * GLA_DKP, (h + 1) * GLA_DKP)
        sv = slice(h * GLA_DVP, (h + 1) * GLA_DVP)
        la = la_ref[:, sk]
        hi = la.astype(BF)
        lo = (la - hi.astype(F32)).astype(BF)
        bb = _dot(ltri, jnp.concatenate([hi, lo], axis=1))
        b = bb[:, :GLA_DKP] + bb[:, GLA_DKP:]
        b_ref = b[C // 2 - 1:C // 2, :]
        b_last = b[C - 1:C, :]
        q = q_ref[:, sk]
        k = k_ref[:, sk]
        v = v_ref[:, sv].astype(BF)
        qe = (q * jnp.exp(b - b_ref)).astype(BF)
        ke = (k * jnp.exp(b_ref - b)).astype(BF)
        a = jnp.where(causal, _dot_nt(qe, ke), 0.0).astype(BF)
        st = s_scr[h]
        qb = (q * jnp.exp(b)).astype(BF)
        o = _dot(a, v) + _dot_nt(qb, st.astype(BF))
        kd = (k * jnp.exp(b_last - b)).astype(BF)
        s_scr[h] = st * jnp.exp(b_last) + _dot_tn(v, kd)
        ms = jnp.sum(o * o, axis=1, keepdims=True) * (1.0 / GLA_DV)
        tok_ref[:, sv] = (o * lax.rsqrt(ms + EPS) * r_ref[:, sv]).astype(BF)


def _gla_prompt(q, k, la, v, r, casts, *, batch, seq):
    ns = GLA_SEQS
    m = batch * seq
    nc = seq // GLA_CHUNK
    qk = GLA_HEADS * GLA_DKP
    vr = GLA_HEADS * GLA_DVP
    grid = (batch // ns, nc)

    def tok_map(i):
        return lambda b, c: ((b * ns + i) * nc + c, 0)

    seq_specs = []
    for i in range(ns):
        seq_specs += [pl.BlockSpec((GLA_CHUNK, qk), tok_map(i))] * 3
        seq_specs += [pl.BlockSpec((GLA_CHUNK, vr), tok_map(i))] * 2
    jobs = [_cast_job(cw, layer, rows, grid) for cw, layer, rows in casts]
    res = pl.pallas_call(
        _gla_prompt_kernel,
        grid=grid,
        in_specs=seq_specs + [j[0] for j in jobs],
        out_specs=[pl.BlockSpec((ns, GLA_CHUNK, vr), lambda b, c: (b, c, 0)),
                   pl.BlockSpec((ns, GLA_HEADS, GLA_DVP, GLA_DKP), lambda b, c: (b, 0, 0, 0))]
                  + [j[1] for j in jobs],
        out_shape=[jax.ShapeDtypeStruct((batch, seq, vr), BF),
                   jax.ShapeDtypeStruct((batch, GLA_HEADS, GLA_DVP, GLA_DKP), F32)]
                  + [j[2] for j in jobs],
        scratch_shapes=[pltpu.VMEM((ns, GLA_HEADS, GLA_DVP, GLA_DKP), F32)],
        compiler_params=_params("arbitrary", "arbitrary"),
        name="gla_prompt",
    )(*([q, k, la, v, r] * ns), *[c[0] for c in casts])
    return (res[0].reshape(m, vr),) + tuple(res[1:])


_SAMPLE_BB = 8
_DEC_SEQ = 4
_DEC_BATCH = 128
_GLA_DK_BLK = 32


def _gla_sample_kernel(s_ref, q_ref, k_ref, la_ref, v_ref, r_ref, so_ref, tok_ref, o_scr):
    j = pl.program_id(1)

    @pl.when(j == 0)
    def _():
        o_scr[...] = jnp.zeros(o_scr.shape, F32)

    def body(dk, carry):
        s = s_ref[0, 0, dk]
        for t in range(_DEC_SEQ):
            a = jnp.exp(la_ref[t, 0, pl.ds(dk, 1), :])
            s = a * s + k_ref[t, 0, pl.ds(dk, 1), :] * v_ref[t, 0, :GLA_DV, :]
            o_scr[t] = o_scr[t] + q_ref[t, 0, pl.ds(dk, 1), :] * s
        so_ref[0, 0, dk] = s
        return carry

    lax.fori_loop(0, _GLA_DK_BLK, body, 0)

    @pl.when(j == pl.num_programs(1) - 1)
    def _():
        tok_ref[...] = jnp.zeros(tok_ref.shape, F32)
        for t in range(_DEC_SEQ):
            o = o_scr[t]
            ms = jnp.sum(o * o, axis=0, keepdims=True) * (1.0 / GLA_DV)
            tok_ref[t, 0, :GLA_DV, :] = o * lax.rsqrt(ms + EPS) * r_ref[t, 0, :GLA_DV, :]


def _gla_sample(state, q, k, la, v, r):
    qk_spec = pl.BlockSpec((_DEC_SEQ, 1, _GLA_DK_BLK, _DEC_BATCH), lambda h, j: (0, h, j, 0))
    vr_spec = pl.BlockSpec((_DEC_SEQ, 1, GLA_DVP, _DEC_BATCH), lambda h, j: (0, h, 0, 0))
    s_spec = pl.BlockSpec((1, 1, _GLA_DK_BLK, GLA_DV, _DEC_BATCH), lambda h, j: (0, h, j, 0, 0))
    return pl.pallas_call(
        _gla_sample_kernel,
        grid=(GLA_HEADS, GLA_DK // _GLA_DK_BLK),
        in_specs=[s_spec, qk_spec, qk_spec, qk_spec, vr_spec, vr_spec],
        out_specs=[s_spec, vr_spec],
        out_shape=[jax.ShapeDtypeStruct(state.shape, F32),
                   jax.ShapeDtypeStruct(v.shape, F32)],
        scratch_shapes=[pltpu.VMEM((_DEC_SEQ, GLA_DV, _DEC_BATCH), F32)],
        compiler_params=_params("parallel", "arbitrary"),
        name="gla_sample",
    )(state, q, k, la, v, r)


def _mem_attn_sample_kernel(q_ref, mk_ref, mv_ref, o_ref):
    for bi in range(_SAMPLE_BB):
        s = _dot(q_ref[bi], mk_ref[0, bi].astype(BF))
        e = jnp.exp(s - jnp.max(s, axis=1, keepdims=True))
        p = (e * (1.0 / jnp.sum(e, axis=1, keepdims=True))).astype(BF)
        o_ref[bi] = _dot_nt(p, mv_ref[0, bi].astype(BF))


def _mem_attn_sample(qbd, mk_t, mv_t, layer):
    nb = qbd.shape[0]
    nr = MEM_HEADS * _DEC_SEQ
    kv_spec = pl.BlockSpec((1, _SAMPLE_BB, MEM_Q, MEM_TOKENS), lambda i: (layer, i, 0, 0))
    q_spec = pl.BlockSpec((_SAMPLE_BB, nr, MEM_Q), lambda i: (i, 0, 0))
    return pl.pallas_call(
        _mem_attn_sample_kernel,
        grid=(nb // _SAMPLE_BB,),
        in_specs=[q_spec, kv_spec, kv_spec],
        out_specs=q_spec,
        out_shape=jax.ShapeDtypeStruct((nb, nr, MEM_Q), F32),
        compiler_params=_params("parallel"),
        name="mem_attn_sample",
    )(qbd, mk_t, mv_t)


def _swa_sample_kernel(q_ref, sink_ref, kc_ref, vc_ref, kn_ref, vn_ref, o_ref, ko_ref, vo_ref):
    nq = SWA_Q_HEADS * _DEC_SEQ
    t = lax.broadcasted_iota(jnp.int32, (nq, WINDOW), 0) % _DEC_SEQ
    pos = lax.broadcasted_iota(jnp.int32, (nq, WINDOW), 1)
    new0 = WINDOW - _DEC_SEQ
    bias_c = jnp.where(pos > t, 0.0, -jnp.inf)
    bias_n = jnp.where((pos >= new0) & (pos - new0 <= t), 0.0, -jnp.inf)
    is_new = lax.broadcasted_iota(jnp.int32, (SWA_KV_HEADS * HEAD_DIM, WINDOW), 1) >= new0
    sink = sink_ref[...]
    for bi in range(_SAMPLE_BB):
        q = q_ref[bi]
        kc, vc, kn, vn = kc_ref[bi], vc_ref[bi], kn_ref[bi], vn_ref[bi]
        sc = _dot(q, kc.astype(BF)) + bias_c
        sn = _dot(q, kn.astype(BF)) + bias_n
        m = jnp.maximum(jnp.maximum(jnp.max(sc, axis=1, keepdims=True),
                                    jnp.max(sn, axis=1, keepdims=True)), sink)
        ec = jnp.exp(sc - m)
        en = jnp.exp(sn - m)
        l = (jnp.sum(ec, axis=1, keepdims=True) + jnp.sum(en, axis=1, keepdims=True)
             + jnp.exp(sink - m))
        inv = 1.0 / l
        o_ref[bi] = (_dot_nt((ec * inv).astype(BF), vc.astype(BF))
                     + _dot_nt((en * inv).astype(BF), vn.astype(BF)))
        ko_ref[bi] = jnp.where(is_new, kn, pltpu.roll(kc, new0, 1))
        vo_ref[bi] = jnp.where(is_new, vn, pltpu.roll(vc, new0, 1))


def _swa_sample(qbd, sink_col, kc, vc, kn, vn):
    nb = qbd.shape[0]
    kw = SWA_KV_HEADS * HEAD_DIM
    nq = SWA_Q_HEADS * _DEC_SEQ
    kv_spec = pl.BlockSpec((_SAMPLE_BB, kw, WINDOW), lambda i: (i, 0, 0))
    q_spec = pl.BlockSpec((_SAMPLE_BB, nq, kw), lambda i: (i, 0, 0))
    return pl.pallas_call(
        _swa_sample_kernel,
        grid=(nb // _SAMPLE_BB,),
        in_specs=[q_spec, _resident((nq, 1)), kv_spec, kv_spec, kv_spec, kv_spec],
        out_specs=[q_spec, kv_spec, kv_spec],
        out_shape=[jax.ShapeDtypeStruct((nb, nq, kw), F32),
                   jax.ShapeDtypeStruct(kc.shape, F32), jax.ShapeDtypeStruct(kc.shape, F32)],
        compiler_params=_params("parallel"),
        name="swa_sample",
    )(qbd, sink_col, kc, vc, kn, vn)


def _pad_heads(w, heads, dim, dim_p, axis):
    shape = w.shape
    w = w.reshape(shape[:axis] + (heads, dim) + shape[axis + 1:])
    pad = [(0, 0)] * w.ndim
    pad[axis + 1] = (0, dim_p - dim)
    w = jnp.pad(w, pad)
    return w.reshape(shape[:axis] + (heads * dim_p,) + shape[axis + 1:])


def _rope_tables(pos):
    half = ROT_DIM // 2
    inv_freq = jnp.exp(-math.log(ROPE_THETA) * jnp.arange(0, ROT_DIM, 2, dtype=F32) / ROT_DIM)
    ang = pos[:, None] * inv_freq[None, :]
    cos, sin = jnp.cos(ang), jnp.sin(ang)
    n = pos.shape[0]
    rest = HEAD_DIM - ROT_DIM
    c = jnp.concatenate([cos, cos, jnp.ones((n, rest), F32)], axis=1)
    s1 = jnp.concatenate([-sin, jnp.zeros((n, HEAD_DIM - half), F32)], axis=1)
    s2 = jnp.concatenate([jnp.zeros((n, half), F32), sin, jnp.zeros((n, rest), F32)], axis=1)
    return tuple(jnp.tile(a, (1, HEAD_DIMP // HEAD_DIM)) for a in (c, s1, s2))


def _prep_weights(p):
    w = {}
    for name in ("ffn1_norm", "ffn2_norm", "mix_norm"):
        w[name] = p[name][:, None, :]
    qk = GLA_HEADS * GLA_DK
    vv = GLA_HEADS * GLA_DV
    a_in = p["a_w_in"][0]
    o = 0
    wq = _pad_heads(a_in[:, o:o + qk], GLA_HEADS, GLA_DK, GLA_DKP, 1); o += qk
    wk = _pad_heads(a_in[:, o:o + qk], GLA_HEADS, GLA_DK, GLA_DKP, 1); o += qk
    wv = _pad_heads(a_in[:, o:o + vv], GLA_HEADS, GLA_DV, GLA_DVP, 1); o += vv
    wr = _pad_heads(a_in[:, o:o + vv], GLA_HEADS, GLA_DV, GLA_DVP, 1); o += vv
    wg = jnp.pad(a_in[:, o:o + GLA_RANK], ((0, 0), (0, GLA_RANKP - GLA_RANK))); o += GLA_RANK
    wm = a_in[:, o:]
    w["a_in"] = jnp.concatenate([wq, wk, wv, wr, wg, wm], axis=1).astype(BF)
    gate = _pad_heads(p["a_w_gate"][0], GLA_HEADS, GLA_DK, GLA_DKP, 1)
    w["a_gate"] = jnp.pad(gate, ((0, GLA_RANKP - GLA_RANK), (0, 0))).astype(BF)
    w["a_bgate"] = _pad_heads(p["a_b_gate"][0][None, :], GLA_HEADS, GLA_DK, GLA_DKP, 1)
    w["a_gn"] = jnp.tile(jnp.pad(p["a_out_norm"][0], (0, GLA_DVP - GLA_DV)), GLA_HEADS)[None, :]
    a_out = p["a_w_out"][0]
    w["a_out"] = jnp.concatenate(
        [_pad_heads(a_out[:vv], GLA_HEADS, GLA_DV, GLA_DVP, 0), a_out[vv:]], axis=0).astype(BF)
    w["b_in"] = p["b_w_in"][0].astype(BF)
    w["b_out"] = p["b_w_out"][0].astype(BF)
    nkv = SWA_KV_HEADS * HEAD_DIM
    w_kv = p["w_kv"]
    w["kv"] = jnp.concatenate(
        [_pad_heads(w_kv[:, :nkv], SWA_KV_HEADS, HEAD_DIM, HEAD_DIMP, 1),
         _pad_heads(w_kv[:, nkv:], SWA_KV_HEADS, HEAD_DIM, HEAD_DIMP, 1)], axis=1).astype(BF)
    w["mem_t"] = p["mem_w_kv"].transpose(0, 2, 1).astype(BF)
    return w


def _compact_kv(a, batch, seq):
    return a.reshape(batch, seq, SWA_KV_HEADS, HEAD_DIMP)[..., :HEAD_DIM]


def kernel(x_prompt, x_sample, state_gla, cache_swa_k, cache_swa_v, cache_mem_k, cache_mem_v,
           mem_prompt, ffn1_norm, ffn1_w_gu, ffn1_w_down, mix_norm, ffn2_norm, ffn2_w_gu,
           ffn2_w_down, mem_norm, mem_w_kv, a_w_in, a_w_gate, a_b_gate, a_out_norm, a_w_out,
           kv_norm, w_kv, b_w_in, b_sinks, b_w_out, final_norm):
    p = dict(ffn1_norm=ffn1_norm, ffn1_w_gu=ffn1_w_gu, ffn1_w_down=ffn1_w_down,
             mix_norm=mix_norm, ffn2_norm=ffn2_norm, ffn2_w_gu=ffn2_w_gu,
             ffn2_w_down=ffn2_w_down, mem_w_kv=mem_w_kv, a_w_in=a_w_in, a_w_gate=a_w_gate,
             a_b_gate=a_b_gate, a_out_norm=a_out_norm, a_w_out=a_w_out, w_kv=w_kv,
             b_w_in=b_w_in, b_w_out=b_w_out)
    w = _prep_weights(p)
    batch, seq, _ = x_prompt.shape
    nb, t, _ = x_sample.shape
    assert nb == _DEC_BATCH and t == _DEC_SEQ and nb * t == TOKEN_TILE
    assert seq % TOKEN_TILE == 0 and seq % GLA_CHUNK == 0
    mp = batch * seq
    kw = SWA_KV_HEADS * HEAD_DIM
    sinks = b_sinks[0]

    mem_k_t, mem_v_t, wgu_a1, wd_a1 = _mem_kv(
        mem_prompt.reshape(batch * MEM_TOKENS, D_MODEL), mem_norm[:, None, :], w["mem_t"],
        [(ffn1_w_gu, 0, D_MODEL // batch), (ffn1_w_down, 0, FFN_DIM // batch)], batch=batch)
    state_t = state_gla.transpose(0, 2, 3, 4, 1)
    kc_t = cache_swa_k.transpose(0, 2, 3, 1).reshape(nb, kw, WINDOW)
    vc_t = cache_swa_v.transpose(0, 2, 3, 1).reshape(nb, kw, WINDOW)
    cmk_t = cache_mem_k.transpose(0, 1, 3, 4, 2).reshape(2, nb, MEM_Q, MEM_TOKENS)
    cmv_t = cache_mem_v.transpose(0, 1, 3, 4, 2).reshape(2, nb, MEM_Q, MEM_TOKENS)

    tabs = tuple(
        jnp.concatenate([a, jnp.tile(b, (nb, 1))], axis=0)
        for a, b in zip(_rope_tables(jnp.arange(seq, dtype=F32)),
                        _rope_tables(PAST_LEN + jnp.arange(t, dtype=F32))))

    mem_mask = (jnp.arange(MEM_Q) // MEM_HEAD_DIM)[None, :] == jnp.arange(MEM_HEADS)[:, None]
    kv_mask = (jnp.arange(kw) // HEAD_DIM)[None, :] == jnp.arange(SWA_KV_HEADS)[:, None]

    def mem_attn_sample(mq, layer):
        q4 = mq[mp:].reshape(nb, 1, t, MEM_Q)
        qbd = jnp.where(mem_mask[None, :, None, :], q4, 0).reshape(nb, MEM_HEADS * t, MEM_Q)
        o = _mem_attn_sample(qbd, cmk_t, cmv_t, layer).reshape(nb, MEM_HEADS, t, MEM_Q)
        o = jnp.sum(jnp.where(mem_mask[None, :, None, :], o, 0.0), axis=1)
        return o.reshape(nb * t, MEM_Q).astype(BF)

    def lanes(a, width):
        return a[mp:].reshape(nb, t, GLA_HEADS, width).transpose(1, 2, 3, 0)

    x, q, k, la, v, r, mq = _layer_a_in(
        x_prompt.reshape(mp, D_MODEL), x_sample.reshape(nb * t, D_MODEL), w["ffn1_norm"],
        wgu_a1, wd_a1, w["mix_norm"], w["a_in"], w["a_gate"], w["a_bgate"], w["a_gn"])
    tok_p, st_p, wgu_a2, wd_a2, wgu_b1, wd_b1, wgu_b2, wd_b2 = _gla_prompt(
        q, k, la, v, r,
        [(ffn2_w_gu, 0, 32), (ffn2_w_down, 0, 128), (ffn1_w_gu, 1, 32), (ffn1_w_down, 1, 128),
         (ffn2_w_gu, 1, 32), (ffn2_w_down, 1, 128)],
        batch=batch, seq=seq)
    st_s, tok_s = _gla_sample(state_t, lanes(q, GLA_DKP), lanes(k, GLA_DKP), lanes(la, GLA_DKP),
                              lanes(v, GLA_DVP), lanes(r, GLA_DVP))
    tok_s = tok_s.transpose(3, 0, 1, 2).reshape(nb * t, GLA_HEADS * GLA_DVP).astype(BF)
    mo_s = mem_attn_sample(mq, 0)
    x, k_sh, v_sh = _layer_a_out(x, tok_p, tok_s, mq, mem_k_t, mem_v_t, mo_s, w["a_out"],
                                 w["ffn2_norm"], wgu_a2, wd_a2, kv_norm[None, :], w["kv"], tabs,
                                 seq=seq)

    x, qs, mq = _layer_b_in(x, w["ffn1_norm"], wgu_b1, wd_b1, w["mix_norm"],
                            w["b_in"], tabs, n_prompt=mp // TOKEN_TILE, seq=seq)

    def new_rows(a):
        a = _compact_kv(a[mp:], nb, t).reshape(nb, t, kw).transpose(0, 2, 1)
        return jnp.pad(a, ((0, 0), (0, 0), (WINDOW - t, 0)))

    q5 = qs[:, mp:].reshape(SWA_Q_PAIRS, nb, t, 2, HEAD_DIM).transpose(1, 0, 3, 2, 4)
    q5 = q5.reshape(nb, SWA_KV_HEADS, SWA_GROUP, t, HEAD_DIM)
    qbd = jnp.where(kv_mask[None, :, None, None, :], jnp.tile(q5, (1, 1, 1, 1, SWA_KV_HEADS)), 0)
    qbd = qbd.reshape(nb, SWA_Q_HEADS * t, kw)
    o, k_s, v_s = _swa_sample(qbd, jnp.repeat(sinks, t)[:, None], kc_t, vc_t,
                              new_rows(k_sh), new_rows(v_sh))
    o = o.reshape(nb, SWA_KV_HEADS, SWA_GROUP, t, kw)
    o = jnp.where(kv_mask[None, :, None, None, :], o, 0.0)
    o = o.reshape(nb, SWA_KV_HEADS, SWA_GROUP, t, SWA_KV_HEADS, HEAD_DIM).sum(axis=4)
    tok_s = o.reshape(nb, SWA_Q_PAIRS, 2, t, HEAD_DIM).transpose(1, 0, 3, 2, 4)
    tok_s = tok_s.reshape(SWA_Q_PAIRS, nb * t, HEAD_DIMP).astype(BF)
    mo_s = mem_attn_sample(mq, 1)
    y_p, y_s = _layer_b_out(sinks, x, qs, k_sh, v_sh, mq, mem_k_t, mem_v_t, tok_s, mo_s,
                            w["b_out"], w["ffn2_norm"], wgu_b2, wd_b2, final_norm[None, :],
                            seq=seq)

    gla_prompt = st_p.transpose(0, 1, 3, 2)[None, :, :, :GLA_DK, :GLA_DV]
    gla_sample = st_s.transpose(0, 4, 1, 2, 3)
    def last_window(a):
        tiles_per_seq = seq // TOKEN_TILE
        a = a.reshape(-1, TOKEN_TILE, SWA_KV_HEADS * HEAD_DIMP)
        a = a[tiles_per_seq - 1:batch * tiles_per_seq:tiles_per_seq, TOKEN_TILE - WINDOW:]
        return a.reshape(batch, WINDOW, SWA_KV_HEADS, HEAD_DIMP)[..., :HEAD_DIM]

    swa_k_prompt = last_window(k_sh)
    swa_v_prompt = last_window(v_sh)
    swa_k_sample = k_s.reshape(nb, SWA_KV_HEADS, HEAD_DIM, WINDOW).transpose(0, 3, 1, 2)
    swa_v_sample = v_s.reshape(nb, SWA_KV_HEADS, HEAD_DIM, WINDOW).transpose(0, 3, 1, 2)
    mem_shape = (2, batch, MEM_HEADS, MEM_HEAD_DIM, MEM_TOKENS)
    mem_k_prompt = mem_k_t.reshape(mem_shape).transpose(0, 1, 4, 2, 3)
    mem_v_prompt = mem_v_t.reshape(mem_shape).transpose(0, 1, 4, 2, 3)
    return (y_p.reshape(batch, seq, D_MODEL), y_s.reshape(nb, t, D_MODEL), gla_prompt,
            gla_sample, swa_k_prompt, swa_v_prompt, swa_k_sample, swa_v_sample,
            mem_k_prompt, mem_v_prompt)
```

```python
import functools
import math

import jax
import jax.numpy as jnp
from jax import lax
from jax.experimental import pallas as pl
from jax.experimental.pallas import tpu as pltpu

F32 = jnp.float32
BF = jnp.bfloat16

D_MODEL = 1024
FFN_DIM = 2816
EPS = 1e-6

GLA_HEADS = 4
GLA_DK = 96
GLA_DV = 192
GLA_DKP = 128
GLA_DVP = 256
GLA_RANK = 16
GLA_RANKP = 128
GLA_GATE_NORM = 16.0
GLA_CHUNK = 256
GLA_SEQS = 2

HEAD_DIM = 64
HEAD_DIMP = 128
SWA_Q_HEADS = 12
SWA_KV_HEADS = 3
SWA_GROUP = SWA_Q_HEADS // SWA_KV_HEADS
SWA_Q_PAIRS = SWA_Q_HEADS // 2
SWA_PAIRS_PER_KV = SWA_GROUP // 2
WINDOW = 128
ROT_DIM = 16
ROPE_THETA = 500000.0
PAST_LEN = 8192

MEM_TOKENS = 256
MEM_HEADS = 4
MEM_HEAD_DIM = 64
MEM_Q = MEM_HEADS * MEM_HEAD_DIM

FFN_TF = 256
FFN_CHUNKS = FFN_DIM // FFN_TF
TOKEN_TILE = 512

VMEM_LIMIT = 60 * 1024 * 1024


def _params(*sem):
    return pltpu.CompilerParams(dimension_semantics=sem, vmem_limit_bytes=VMEM_LIMIT)


def _resident(shape):
    nd = len(shape)
    return pl.BlockSpec(shape, lambda *_: (0,) * nd, pipeline_mode=pl.Buffered(1))


def _layer_block(shape, layer):
    nd = len(shape)
    return pl.BlockSpec((1,) + tuple(shape[1:]), lambda *_: (layer,) + (0,) * (nd - 1),
                        pipeline_mode=pl.Buffered(1))


def _rows(width):
    return pl.BlockSpec((TOKEN_TILE, width), lambda i: (i, 0))


def _prompt_rows(width, n_prompt):
    return pl.BlockSpec((TOKEN_TILE, width), lambda i: (jnp.minimum(i, n_prompt - 1), 0))


def _sample_rows(width):
    return pl.BlockSpec((TOKEN_TILE, width), lambda i: (0, 0), pipeline_mode=pl.Buffered(1))


def _rms(x, g):
    ms = jnp.mean(x * x, axis=-1, keepdims=True)
    return x * lax.rsqrt(ms + EPS) * g


def _silu(x):
    return x * (1.0 / (1.0 + jnp.exp(-x)))


def _dot(a, b):
    return jnp.dot(a, b, preferred_element_type=F32)


def _dot_nt(a, b):
    return lax.dot_general(a, b, (((1,), (1,)), ((), ())), preferred_element_type=F32)


def _dot_tn(a, b):
    return lax.dot_general(a, b, (((0,), (0,)), ((), ())), preferred_element_type=F32)


def _pick(is_prompt, p_ref, s_ref):
    return jnp.where(is_prompt, p_ref[...], s_ref[...])


def _cast_job(w, layer, rows, grid):
    _, r, c = w.shape
    assert r % rows == 0 and rows % 16 == 0
    nblk = r // rows
    total = math.prod(grid)
    assert nblk <= total
    steps_per_block = total // nblk

    def block(*idx):
        step = idx[0]
        for dim, i in zip(grid[1:], idx[1:]):
            step = step * dim + i
        return jnp.minimum(step // steps_per_block, nblk - 1)

    in_spec = pl.BlockSpec((1, rows, c), lambda *idx: (layer, block(*idx), 0))
    out_spec = pl.BlockSpec((1, rows, c), lambda *idx: (0, block(*idx), 0))
    return in_spec, out_spec, jax.ShapeDtypeStruct((1, r, c), BF)


def _run_cast_jobs(refs):
    n = len(refs) // 2
    for src, dst in zip(refs[:n], refs[n:]):
        dst[...] = src[...].astype(BF)


def _ffn_half(x, g_ref, wgu_ref, wd_ref):
    h = _rms(x, g_ref[0]).astype(BF)
    acc = jnp.zeros(x.shape, F32)
    for c in range(FFN_CHUNKS):
        lo, hi = c * FFN_TF, (c + 1) * FFN_TF
        gate = _dot(h, wgu_ref[0, :, lo:hi])
        up = _dot(h, wgu_ref[0, :, FFN_DIM + lo:FFN_DIM + hi])
        a = (_silu(gate) * up).astype(BF)
        acc = acc + _dot(a, wd_ref[0, lo:hi, :])
    return x + 0.5 * acc


def _rope(x, c, s1, s2):
    return (x * c + pltpu.roll(x, HEAD_DIMP - ROT_DIM // 2, 1) * s1
            + pltpu.roll(x, ROT_DIM // 2, 1) * s2)


_A_Q = 0
_A_K = _A_Q + GLA_HEADS * GLA_DKP
_A_V = _A_K + GLA_HEADS * GLA_DKP
_A_R = _A_V + GLA_HEADS * GLA_DVP
_A_G = _A_R + GLA_HEADS * GLA_DVP
_A_M = _A_G + GLA_RANKP
_A_END = _A_M + MEM_Q


def _layer_a_in_kernel(xp_ref, xs_ref, g1_ref, wgu_ref, wd_ref, gm_ref, w_ref, wg_ref, bg_ref,
                       gn_ref, x_ref, q_ref, k_ref, la_ref, v_ref, r_ref, mq_ref, *, n_prompt):
    is_prompt = pl.program_id(0) < n_prompt
    x = _ffn_half(_pick(is_prompt, xp_ref, xs_ref), g1_ref, wgu_ref, wd_ref)
    x_ref[...] = x
    h = _rms(x, gm_ref[0]).astype(BF)

    proj = _dot(h, w_ref[...])

    def mm(a, b):
        return proj[:, a:b]

    q_ref[...] = mm(_A_Q, _A_K) * (GLA_DK ** -0.5)
    k_ref[...] = mm(_A_K, _A_V)
    v_ref[...] = mm(_A_V, _A_R)
    r_ref[...] = _silu(mm(_A_R, _A_G)) * gn_ref[...]
    z = _dot(mm(_A_G, _A_M).astype(BF), wg_ref[...]) + bg_ref[...]
    la_ref[...] = (jnp.minimum(z, 0.0) - jnp.log1p(jnp.exp(-jnp.abs(z)))) * (1.0 / GLA_GATE_NORM)
    mq_ref[...] = (mm(_A_M, _A_END) * (MEM_HEAD_DIM ** -0.5)).astype(BF)


def _layer_a_in(xp, xs, g1, wgu, wd, gm, w, wg, bg, gn):
    n_prompt = xp.shape[0] // TOKEN_TILE
    m = xp.shape[0] + xs.shape[0]
    qk = GLA_HEADS * GLA_DKP
    vr = GLA_HEADS * GLA_DVP
    outs = ((D_MODEL, F32), (qk, F32), (qk, F32), (qk, F32), (vr, F32), (vr, F32), (MEM_Q, BF))
    return pl.pallas_call(
        functools.partial(_layer_a_in_kernel, n_prompt=n_prompt),
        grid=(m // TOKEN_TILE,),
        in_specs=[_prompt_rows(D_MODEL, n_prompt), _sample_rows(D_MODEL),
                  _layer_block(g1.shape, 0), _layer_block(wgu.shape, 0), _layer_block(wd.shape, 0),
                  _layer_block(gm.shape, 0), _resident(w.shape), _resident(wg.shape),
                  _resident(bg.shape), _resident(gn.shape)],
        out_specs=[_rows(n) for n, _ in outs],
        out_shape=[jax.ShapeDtypeStruct((m, n), dt) for n, dt in outs],
        compiler_params=_params("arbitrary"),
        name="layer_a_in",
    )(xp, xs, g1, wgu, wd, gm, w, wg, bg, gn)


def _build_mem_block_diag(mk_ref, mv_ref, kbd_scr, vbd_scr):
    shape = (MEM_Q, MEM_HEADS * MEM_TOKENS)
    rh = lax.broadcasted_iota(jnp.int32, shape, 0) // MEM_HEAD_DIM
    ch = lax.broadcasted_iota(jnp.int32, shape, 1) // MEM_TOKENS
    diag = rh == ch
    kbd_scr[...] = jnp.where(diag, jnp.concatenate([mk_ref[0, 0]] * MEM_HEADS, axis=1),
                             0.0).astype(BF)
    vbd_scr[...] = jnp.where(diag, jnp.concatenate([mv_ref[0, 0]] * MEM_HEADS, axis=1),
                             0.0).astype(BF)


def _mem_attn_tile(q, kbd, vbd):
    tq = q.shape[0]
    s = _dot(q, kbd)
    ps, inv = [], []
    for h in range(MEM_HEADS):
        sh = s[:, h * MEM_TOKENS:(h + 1) * MEM_TOKENS]
        e = jnp.exp(sh - jnp.max(sh, axis=1, keepdims=True))
        inv.append(1.0 / jnp.sum(e, axis=1, keepdims=True))
        ps.append(e.astype(BF))
    o = _dot_nt(jnp.concatenate(ps, axis=1), vbd)
    lane_h = lax.broadcasted_iota(jnp.int32, (tq, MEM_Q), 1) // MEM_HEAD_DIM
    scale = jnp.where(lane_h == 0, inv[0],
                      jnp.where(lane_h == 1, inv[1], jnp.where(lane_h == 2, inv[2], inv[3])))
    return o * scale


def _swa_tile(q_ref, kprev_ref, kcur_ref, vprev_ref, vcur_ref, sink_ref, first_of_seq):
    blk = WINDOW
    nsub = TOKEN_TILE // blk
    qi = lax.broadcasted_iota(jnp.int32, (blk, 2 * blk), 0)
    kj = lax.broadcasted_iota(jnp.int32, (blk, 2 * blk), 1)
    d = blk + qi - kj
    band = (d >= 0) & (d < WINDOW)
    bias_mid = jnp.where(band, 0.0, -jnp.inf)
    bias_first = jnp.where(band & (kj >= blk), 0.0, -jnp.inf)
    bias0 = jnp.where(first_of_seq, bias_first, bias_mid)
    outs = [[None] * SWA_Q_PAIRS for _ in range(nsub)]
    for kh in range(SWA_KV_HEADS):
        sl = slice(kh * HEAD_DIMP, (kh + 1) * HEAD_DIMP)
        kblk = [kprev_ref[:, sl]] + [kcur_ref[s * blk:(s + 1) * blk, sl] for s in range(nsub)]
        vblk = [vprev_ref[:, sl]] + [vcur_ref[s * blk:(s + 1) * blk, sl] for s in range(nsub)]
        k2 = [(b.astype(BF), pltpu.roll(b, HEAD_DIM, 1).astype(BF)) for b in kblk]
        v2 = [(b.astype(BF), pltpu.roll(b, HEAD_DIM, 1).astype(BF)) for b in vblk]
        for sub in range(nsub):
            bias = bias0 if sub == 0 else bias_mid
            for pj in range(SWA_PAIRS_PER_KV):
                pp = kh * SWA_PAIRS_PER_KV + pj
                q = q_ref[pp, sub * blk:(sub + 1) * blk, :]
                o = None
                for half in range(2):
                    kb = jnp.concatenate([k2[sub][half], k2[sub + 1][half]], axis=0)
                    vb = jnp.concatenate([v2[sub][half], v2[sub + 1][half]], axis=0)
                    s = _dot_nt(q, kb) + bias
                    sink = sink_ref[2 * pp + half]
                    m = jnp.maximum(jnp.max(s, axis=1, keepdims=True), sink)
                    e = jnp.exp(s - m)
                    l = jnp.sum(e, axis=1, keepdims=True) + jnp.exp(sink - m)
                    oh = _dot(e.astype(BF), vb) * (1.0 / l)
                    o = oh if o is None else o + oh
                outs[sub][pp] = o.astype(BF)
    return jnp.concatenate([jnp.concatenate(row, axis=1) for row in outs], axis=0)


def _layer_a_out_kernel(x_ref, tokp_ref, toks_ref, mq_ref, mk_ref, mv_ref, mos_ref, wo_ref,
                        g2_ref, wgu_ref, wd_ref, gkv_ref, wkv_ref, c_ref, s1_ref, s2_ref,
                        xo_ref, k_ref, v_ref, kbd_scr, vbd_scr, *, n_prompt, tiles_per_seq):
    i = pl.program_id(0)
    is_prompt = i < n_prompt

    @pl.when(i % tiles_per_seq == 0)
    def _():
        _build_mem_block_diag(mk_ref, mv_ref, kbd_scr, vbd_scr)

    nt = GLA_HEADS * GLA_DVP
    mo = _mem_attn_tile(mq_ref[...], kbd_scr[...], vbd_scr[...]).astype(BF)
    mo = jnp.where(is_prompt, mo, mos_ref[...])
    x = (x_ref[...] + _dot(_pick(is_prompt, tokp_ref, toks_ref), wo_ref[:nt, :])
         + _dot(mo, wo_ref[nt:, :]))
    x = _ffn_half(x, g2_ref, wgu_ref, wd_ref)
    xo_ref[...] = x
    h = _rms(x, gkv_ref[...]).astype(BF)
    c, s1, s2 = c_ref[...], s1_ref[...], s2_ref[...]
    kw = SWA_KV_HEADS * HEAD_DIMP
    kv = _dot(h, wkv_ref[...])
    for hh in range(SWA_KV_HEADS):
        sl = slice(hh * HEAD_DIMP, (hh + 1) * HEAD_DIMP)
        k_ref[:, sl] = _rope(kv[:, sl], c, s1, s2)
    v_ref[...] = kv[:, kw:]


def _tab_spec(n_prompt, blocks_per_seq):
    return pl.BlockSpec((TOKEN_TILE, HEAD_DIMP),
                        lambda i: (jnp.where(i < n_prompt, i % blocks_per_seq, blocks_per_seq), 0))


def _mem_kv_spec(layer, n_prompt, tiles_per_seq):
    return pl.BlockSpec(
        (1, 1, MEM_Q, MEM_TOKENS),
        lambda i: (layer, jnp.minimum(i, n_prompt - 1) // tiles_per_seq, 0, 0))


def _mem_scratch():
    return pltpu.VMEM((MEM_Q, MEM_HEADS * MEM_TOKENS), BF)


def _layer_a_out(x, tokp, toks, mq, mk_t, mv_t, mos, wo, g2, wgu, wd, gkv, wkv, tabs, *, seq):
    m = x.shape[0]
    n_prompt = tokp.shape[0] // TOKEN_TILE
    tiles_per_seq = seq // TOKEN_TILE
    kw = SWA_KV_HEADS * HEAD_DIMP
    nt = GLA_HEADS * GLA_DVP
    tab = _tab_spec(n_prompt, tiles_per_seq)
    mem_spec = _mem_kv_spec(0, n_prompt, tiles_per_seq)
    return pl.pallas_call(
        functools.partial(_layer_a_out_kernel, n_prompt=n_prompt, tiles_per_seq=tiles_per_seq),
        grid=(m // TOKEN_TILE,),
        in_specs=[_rows(D_MODEL), _prompt_rows(nt, n_prompt), _sample_rows(nt),
                  _rows(MEM_Q), mem_spec, mem_spec, _sample_rows(MEM_Q), _resident(wo.shape),
                  _layer_block(g2.shape, 0), _layer_block(wgu.shape, 0), _layer_block(wd.shape, 0),
                  _resident(gkv.shape), _resident(wkv.shape), tab, tab, tab],
        out_specs=[_rows(D_MODEL), _rows(kw), _rows(kw)],
        out_shape=[jax.ShapeDtypeStruct((m, D_MODEL), F32),
                   jax.ShapeDtypeStruct((m, kw), F32), jax.ShapeDtypeStruct((m, kw), F32)],
        scratch_shapes=[_mem_scratch(), _mem_scratch()],
        compiler_params=_params("arbitrary"),
        name="layer_a_out",
    )(x, tokp, toks, mq, mk_t, mv_t, mos, wo, g2, wgu, wd, gkv, wkv, *tabs)


def _layer_b_in_kernel(x_ref, g1_ref, wgu_ref, wd_ref, gm_ref, w_ref, c_ref, s1_ref, s2_ref,
                       xo_ref, q_ref, mq_ref):
    x = _ffn_half(x_ref[...], g1_ref, wgu_ref, wd_ref)
    xo_ref[...] = x
    h = _rms(x, gm_ref[0]).astype(BF)
    c, s1, s2 = c_ref[...], s1_ref[...], s2_ref[...]
    nq = SWA_Q_HEADS * HEAD_DIM
    qm = _dot(h, w_ref[...])
    for pp in range(SWA_Q_PAIRS):
        q = qm[:, pp * HEAD_DIMP:(pp + 1) * HEAD_DIMP]
        q_ref[pp] = (_rope(q, c, s1, s2) * (HEAD_DIM ** -0.5)).astype(BF)
    mq_ref[...] = (qm[:, nq:] * (MEM_HEAD_DIM ** -0.5)).astype(BF)


def _layer_b_in(x, g1, wgu, wd, gm, w, tabs, *, n_prompt, seq):
    m = x.shape[0]
    tab = _tab_spec(n_prompt, seq // TOKEN_TILE)
    q_spec = pl.BlockSpec((SWA_Q_PAIRS, TOKEN_TILE, HEAD_DIMP), lambda i: (0, i, 0))
    return pl.pallas_call(
        _layer_b_in_kernel,
        grid=(m // TOKEN_TILE,),
        in_specs=[_rows(D_MODEL), _layer_block(g1.shape, 1), _layer_block(wgu.shape, 0),
                  _layer_block(wd.shape, 0), _layer_block(gm.shape, 1), _resident(w.shape),
                  tab, tab, tab],
        out_specs=[_rows(D_MODEL), q_spec, _rows(MEM_Q)],
        out_shape=[jax.ShapeDtypeStruct((m, D_MODEL), F32),
                   jax.ShapeDtypeStruct((SWA_Q_PAIRS, m, HEAD_DIMP), BF),
                   jax.ShapeDtypeStruct((m, MEM_Q), BF)],
        compiler_params=_params("arbitrary"),
        name="layer_b_in",
    )(x, g1, wgu, wd, gm, w, *tabs)


def _layer_b_out_kernel(sink_ref, x_ref, q_ref, kprev_ref, kcur_ref, vprev_ref, vcur_ref,
                        mq_ref, mk_ref, mv_ref, toks_ref, mos_ref, wo_ref, g2_ref, wgu_ref,
                        wd_ref, gf_ref, yp_ref, ys_ref, kbd_scr, vbd_scr,
                        *, n_prompt, tiles_per_seq):
    i = pl.program_id(0)
    is_prompt = i < n_prompt
    first_of_seq = i % tiles_per_seq == 0

    @pl.when(first_of_seq)
    def _():
        _build_mem_block_diag(mk_ref, mv_ref, kbd_scr, vbd_scr)

    nt = SWA_Q_HEADS * HEAD_DIM
    mo = _mem_attn_tile(mq_ref[...], kbd_scr[...], vbd_scr[...]).astype(BF)
    mo = jnp.where(is_prompt, mo, mos_ref[...])
    tok = _swa_tile(q_ref, kprev_ref, kcur_ref, vprev_ref, vcur_ref, sink_ref, first_of_seq)
    toks = jnp.concatenate([toks_ref[pp] for pp in range(SWA_Q_PAIRS)], axis=1)
    tok = jnp.where(is_prompt, tok, toks)
    x = x_ref[...] + _dot(tok, wo_ref[:nt, :]) + _dot(mo, wo_ref[nt:, :])
    y = _rms(_ffn_half(x, g2_ref, wgu_ref, wd_ref), gf_ref[...])

    @pl.when(is_prompt)
    def _():
        yp_ref[...] = y

    @pl.when(jnp.logical_not(is_prompt))
    def _():
        ys_ref[...] = y


def _layer_b_out(sinks, x, qs, k_sh, v_sh, mq, mk_t, mv_t, toks, mos, wo, g2, wgu, wd, gf, *, seq):
    m = x.shape[0]
    n_tiles = m // TOKEN_TILE
    n_prompt = n_tiles - 1
    tiles_per_seq = seq // TOKEN_TILE
    kw = SWA_KV_HEADS * HEAD_DIMP
    blocks_per_tile = TOKEN_TILE // WINDOW
    q_spec = pl.BlockSpec((SWA_Q_PAIRS, TOKEN_TILE, HEAD_DIMP), lambda i: (0, i, 0))
    prev_spec = pl.BlockSpec((WINDOW, kw), lambda i: (jnp.maximum(i * blocks_per_tile - 1, 0), 0))
    toks_spec = pl.BlockSpec((SWA_Q_PAIRS, TOKEN_TILE, HEAD_DIMP), lambda i: (0, 0, 0),
                             pipeline_mode=pl.Buffered(1))
    mem_spec = _mem_kv_spec(1, n_prompt, tiles_per_seq)
    return pl.pallas_call(
        functools.partial(_layer_b_out_kernel, n_prompt=n_prompt, tiles_per_seq=tiles_per_seq),
        grid=(n_tiles,),
        in_specs=[pl.BlockSpec(memory_space=pltpu.SMEM), _rows(D_MODEL), q_spec,
                  prev_spec, _rows(kw), prev_spec, _rows(kw), _rows(MEM_Q), mem_spec, mem_spec,
                  toks_spec, _sample_rows(MEM_Q), _resident(wo.shape), _layer_block(g2.shape, 1),
                  _layer_block(wgu.shape, 0), _layer_block(wd.shape, 0), _resident(gf.shape)],
        out_specs=[_prompt_rows(D_MODEL, n_prompt),
                   pl.BlockSpec((TOKEN_TILE, D_MODEL), lambda i: (0, 0))],
        out_shape=[jax.ShapeDtypeStruct((n_prompt * TOKEN_TILE, D_MODEL), F32),
                   jax.ShapeDtypeStruct((TOKEN_TILE, D_MODEL), F32)],
        scratch_shapes=[_mem_scratch(), _mem_scratch()],
        compiler_params=_params("arbitrary"),
        name="layer_b_out",
    )(sinks, x, qs, k_sh, k_sh, v_sh, v_sh, mq, mk_t, mv_t, toks, mos, wo, g2, wgu, wd, gf)


def _mem_kv_kernel(x_ref, g_ref, wt_ref, *refs):
    n = (len(refs) - 2) // 2
    k_ref, v_ref = refs[n:n + 2]
    _run_cast_jobs(refs[:n] + refs[n + 2:])
    x = x_ref[...]
    xn = x * lax.rsqrt(jnp.mean(x * x, axis=-1, keepdims=True) + EPS)
    for l in range(2):
        h = (xn * g_ref[l]).astype(BF)
        kvt = _dot_nt(wt_ref[l], h)
        k_ref[l, 0] = kvt[:MEM_Q, :]
        v_ref[l, 0] = kvt[MEM_Q:, :]


def _mem_kv(mem, g, wt, casts, *, batch):
    out_spec = pl.BlockSpec((2, 1, MEM_Q, MEM_TOKENS), lambda b: (0, b, 0, 0))
    jobs = [_cast_job(cw, layer, rows, (batch,)) for cw, layer, rows in casts]
    return pl.pallas_call(
        _mem_kv_kernel,
        grid=(batch,),
        in_specs=[pl.BlockSpec((MEM_TOKENS, D_MODEL), lambda b: (b, 0)), _resident(g.shape),
                  _resident(wt.shape)] + [j[0] for j in jobs],
        out_specs=[out_spec, out_spec] + [j[1] for j in jobs],
        out_shape=[jax.ShapeDtypeStruct((2, batch, MEM_Q, MEM_TOKENS), F32)] * 2
                  + [j[2] for j in jobs],
        compiler_params=_params("arbitrary"),
        name="mem_kv",
    )(mem, g, wt, *[c[0] for c in casts])


def _gla_prompt_kernel(*refs):
    ns = GLA_SEQS
    seq_in = [refs[5 * i:5 * i + 5] for i in range(ns)]
    rest = refs[5 * ns:-1]
    s_scr = refs[-1]
    n = (len(rest) - 2) // 2
    tok_ref, st_ref = rest[n:n + 2]
    _run_cast_jobs(rest[:n] + rest[n + 2:])
    c = pl.program_id(1)
    C = GLA_CHUNK

    @pl.when(c == 0)
    def _():
        s_scr[...] = jnp.zeros(s_scr.shape, F32)

    row = lax.broadcasted_iota(jnp.int32, (C, C), 0)
    col = lax.broadcasted_iota(jnp.int32, (C, C), 1)
    causal = row >= col
    ltri = jnp.where(causal, 1.0, 0.0).astype(BF)
    _gla_chunks([(seq_in[i], tok_ref.at[i], s_scr.at[i], h)
                 for i in range(ns) for h in range(GLA_HEADS)], causal, ltri)

    @pl.when(c == pl.num_programs(1) - 1)
    def _():
        st_ref[...] = s_scr[...]


def _gla_chunks(chains, causal, ltri):
    C = GLA_CHUNK

    def sk(h):
        return slice(h * GLA_DKP, (h + 1) * GLA_DKP)

    def sv(h):
        return slice(h * GLA_DVP, (h + 1) * GLA_DVP)

    bs = []
    for (q_ref, k_ref, la_ref, v_ref, r_ref), _, _, h in chains:
        la = la_ref[:, sk(h)]
        hi = la.astype(BF)
        lo = (la - hi.astype(F32)).astype(BF)
        bb = _dot(ltri, jnp.concatenate([hi, lo], axis=1))
        bs.append(bb[:, :GLA_DKP] + bb[:, GLA_DKP:])
    ops = []
    for ((q_ref, k_ref, la_ref, v_ref, r_ref), _, _, h), b in zip(chains, bs):
        b_mid = b[C // 2 - 1:C // 2, :]
        b_last = b[C - 1:C, :]
        qe = q_ref[:, sk(h)] * jnp.exp(b - b_mid)
        ke = k_ref[:, sk(h)] * jnp.exp(b_mid - b)
        qb = (qe * jnp.exp(b_mid)).astype(BF)
        kd = (ke * jnp.exp(b_last - b_mid)).astype(BF)
        ops.append((qe.astype(BF), ke.astype(BF), qb, kd, jnp.exp(b_last)))
    As = [jnp.where(causal, _dot_nt(qe, ke), 0.0).astype(BF) for qe, ke, _, _, _ in ops]
    outs = []
    for ((q_ref, k_ref, la_ref, v_ref, r_ref), _, s_ref, h), a, (_, _, qb, kd, decay) in zip(
            chains, As, ops):
        v = v_ref[:, sv(h)].astype(BF)
        st = s_ref[h]
        outs.append(_dot(a, v) + _dot_nt(qb, st.astype(BF)))
        s_ref[h] = st * decay + _dot_tn(v, kd)
    for ((q_ref, k_ref, la_ref, v_ref, r_ref), tok_ref, _, h), o in zip(chains, outs):
        ms = jnp.sum(o * o, axis=1, keepdims=True) * (1.0 / GLA_DV)
        tok_ref[:, sv(h)] = (o * lax.rsqrt(ms + EPS) * r_ref[:, sv(h)]).astype(BF)


def _gla_prompt(q, k, la, v, r, casts, *, batch, seq):
    ns = GLA_SEQS
    m = batch * seq
    nc = seq // GLA_CHUNK
    qk = GLA_HEADS * GLA_DKP
    vr = GLA_HEADS * GLA_DVP
    grid = (batch // ns, nc)

    def tok_map(i):
        return lambda b, c: ((b * ns + i) * nc + c, 0)

    seq_specs = []
    for i in range(ns):
        seq_specs += [pl.BlockSpec((GLA_CHUNK, qk), tok_map(i))] * 3
        seq_specs += [pl.BlockSpec((GLA_CHUNK, vr), tok_map(i))] * 2
    jobs = [_cast_job(cw, layer, rows, grid) for cw, layer, rows in casts]
    res = pl.pallas_call(
        _gla_prompt_kernel,
        grid=grid,
        in_specs=seq_specs + [j[0] for j in jobs],
        out_specs=[pl.BlockSpec((ns, GLA_CHUNK, vr), lambda b, c: (b, c, 0)),
                   pl.BlockSpec((ns, GLA_HEADS, GLA_DVP, GLA_DKP), lambda b, c: (b, 0, 0, 0))]
                  + [j[1] for j in jobs],
        out_shape=[jax.ShapeDtypeStruct((batch, seq, vr), BF),
                   jax.ShapeDtypeStruct((batch, GLA_HEADS, GLA_DVP, GLA_DKP), F32)]
                  + [j[2] for j in jobs],
        scratch_shapes=[pltpu.VMEM((ns, GLA_HEADS, GLA_DVP, GLA_DKP), F32)],
        compiler_params=_params("arbitrary", "arbitrary"),
        name="gla_prompt",
    )(*([q, k, la, v, r] * ns), *[c[0] for c in casts])
    return (res[0].reshape(m, vr),) + tuple(res[1:])


_SAMPLE_BB = 8
_DEC_SEQ = 4
_DEC_BATCH = 128
_GLA_DK_BLK = 32


def _gla_sample_kernel(s_ref, q_ref, k_ref, la_ref, v_ref, r_ref, so_ref, tok_ref, o_scr):
    j = pl.program_id(1)

    @pl.when(j == 0)
    def _():
        o_scr[...] = jnp.zeros(o_scr.shape, F32)

    def body(dk, carry):
        s = s_ref[0, 0, dk]
        for t in range(_DEC_SEQ):
            a = jnp.exp(la_ref[t, 0, pl.ds(dk, 1), :])
            s = a * s + k_ref[t, 0, pl.ds(dk, 1), :] * v_ref[t, 0, :GLA_DV, :]
            o_scr[t] = o_scr[t] + q_ref[t, 0, pl.ds(dk, 1), :] * s
        so_ref[0, 0, dk] = s
        return carry

    lax.fori_loop(0, _GLA_DK_BLK, body, 0, unroll=4)

    @pl.when(j == pl.num_programs(1) - 1)
    def _():
        tok_ref[...] = jnp.zeros(tok_ref.shape, F32)
        for t in range(_DEC_SEQ):
            o = o_scr[t]
            ms = jnp.sum(o * o, axis=0, keepdims=True) * (1.0 / GLA_DV)
            tok_ref[t, 0, :GLA_DV, :] = o * lax.rsqrt(ms + EPS) * r_ref[t, 0, :GLA_DV, :]


def _gla_sample(state, q, k, la, v, r):
    qk_spec = pl.BlockSpec((_DEC_SEQ, 1, _GLA_DK_BLK, _DEC_BATCH), lambda h, j: (0, h, j, 0))
    vr_spec = pl.BlockSpec((_DEC_SEQ, 1, GLA_DVP, _DEC_BATCH), lambda h, j: (0, h, 0, 0))
    s_spec = pl.BlockSpec((1, 1, _GLA_DK_BLK, GLA_DV, _DEC_BATCH), lambda h, j: (0, h, j, 0, 0))
    return pl.pallas_call(
        _gla_sample_kernel,
        grid=(GLA_HEADS, GLA_DK // _GLA_DK_BLK),
        in_specs=[s_spec, qk_spec, qk_spec, qk_spec, vr_spec, vr_spec],
        out_specs=[s_spec, vr_spec],
        out_shape=[jax.ShapeDtypeStruct(state.shape, F32),
                   jax.ShapeDtypeStruct(v.shape, F32)],
        scratch_shapes=[pltpu.VMEM((_DEC_SEQ, GLA_DV, _DEC_BATCH), F32)],
        compiler_params=_params("parallel", "arbitrary"),
        name="gla_sample",
    )(state, q, k, la, v, r)


def _mem_attn_sample_kernel(q_ref, mk_ref, mv_ref, o_ref):
    scores = [_dot(q_ref[bi], mk_ref[0, bi].astype(BF)) for bi in range(_SAMPLE_BB)]
    probs = []
    for s in scores:
        e = jnp.exp(s - jnp.max(s, axis=1, keepdims=True))
        probs.append((e * (1.0 / jnp.sum(e, axis=1, keepdims=True))).astype(BF))
    for bi, p in enumerate(probs):
        o_ref[bi] = _dot_nt(p, mv_ref[0, bi].astype(BF))


def _mem_attn_sample(qbd, mk_t, mv_t, layer):
    nb = qbd.shape[0]
    nr = MEM_HEADS * _DEC_SEQ
    kv_spec = pl.BlockSpec((1, _SAMPLE_BB, MEM_Q, MEM_TOKENS), lambda i: (layer, i, 0, 0))
    q_spec = pl.BlockSpec((_SAMPLE_BB, nr, MEM_Q), lambda i: (i, 0, 0))
    return pl.pallas_call(
        _mem_attn_sample_kernel,
        grid=(nb // _SAMPLE_BB,),
        in_specs=[q_spec, kv_spec, kv_spec],
        out_specs=q_spec,
        out_shape=jax.ShapeDtypeStruct((nb, nr, MEM_Q), F32),
        compiler_params=_params("parallel"),
        name="mem_attn_sample",
    )(qbd, mk_t, mv_t)


def _swa_sample_kernel(q_ref, sink_ref, kc_ref, vc_ref, kn_ref, vn_ref, o_ref, ko_ref, vo_ref):
    nq = SWA_Q_HEADS * _DEC_SEQ
    t = lax.broadcasted_iota(jnp.int32, (nq, WINDOW), 0) % _DEC_SEQ
    pos = lax.broadcasted_iota(jnp.int32, (nq, WINDOW), 1)
    new0 = WINDOW - _DEC_SEQ
    bias_c = jnp.where(pos > t, 0.0, -jnp.inf)
    bias_n = jnp.where((pos >= new0) & (pos - new0 <= t), 0.0, -jnp.inf)
    is_new = lax.broadcasted_iota(jnp.int32, (SWA_KV_HEADS * HEAD_DIM, WINDOW), 1) >= new0
    sink = sink_ref[...]
    seqs = range(_SAMPLE_BB)
    scores = [(_dot(q_ref[bi], kc_ref[bi].astype(BF)) + bias_c,
               _dot(q_ref[bi], kn_ref[bi].astype(BF)) + bias_n) for bi in seqs]
    probs = []
    for sc, sn in scores:
        m = jnp.maximum(jnp.maximum(jnp.max(sc, axis=1, keepdims=True),
                                    jnp.max(sn, axis=1, keepdims=True)), sink)
        ec = jnp.exp(sc - m)
        en = jnp.exp(sn - m)
        l = (jnp.sum(ec, axis=1, keepdims=True) + jnp.sum(en, axis=1, keepdims=True)
             + jnp.exp(sink - m))
        inv = 1.0 / l
        probs.append(((ec * inv).astype(BF), (en * inv).astype(BF)))
    for bi, (pc, pn) in zip(seqs, probs):
        o_ref[bi] = (_dot_nt(pc, vc_ref[bi].astype(BF)) + _dot_nt(pn, vn_ref[bi].astype(BF)))
    for bi in seqs:
        ko_ref[bi] = jnp.where(is_new, kn_ref[bi], pltpu.roll(kc_ref[bi], new0, 1))
        vo_ref[bi] = jnp.where(is_new, vn_ref[bi], pltpu.roll(vc_ref[bi], new0, 1))


def _swa_sample(qbd, sink_col, kc, vc, kn, vn):
    nb = qbd.shape[0]
    kw = SWA_KV_HEADS * HEAD_DIM
    nq = SWA_Q_HEADS * _DEC_SEQ
    kv_spec = pl.BlockSpec((_SAMPLE_BB, kw, WINDOW), lambda i: (i, 0, 0))
    q_spec = pl.BlockSpec((_SAMPLE_BB, nq, kw), lambda i: (i, 0, 0))
    return pl.pallas_call(
        _swa_sample_kernel,
        grid=(nb // _SAMPLE_BB,),
        in_specs=[q_spec, _resident((nq, 1)), kv_spec, kv_spec, kv_spec, kv_spec],
        out_specs=[q_spec, kv_spec, kv_spec],
        out_shape=[jax.ShapeDtypeStruct((nb, nq, kw), F32),
                   jax.ShapeDtypeStruct(kc.shape, F32), jax.ShapeDtypeStruct(kc.shape, F32)],
        compiler_params=_params("parallel"),
        name="swa_sample",
    )(qbd, sink_col, kc, vc, kn, vn)


def _pad_heads(w, heads, dim, dim_p, axis):
    shape = w.shape
    w = w.reshape(shape[:axis] + (heads, dim) + shape[axis + 1:])
    pad = [(0, 0)] * w.ndim
    pad[axis + 1] = (0, dim_p - dim)
    w = jnp.pad(w, pad)
    return w.reshape(shape[:axis] + (heads * dim_p,) + shape[axis + 1:])


def _rope_tables(pos):
    half = ROT_DIM // 2
    inv_freq = jnp.exp(-math.log(ROPE_THETA) * jnp.arange(0, ROT_DIM, 2, dtype=F32) / ROT_DIM)
    ang = pos[:, None] * inv_freq[None, :]
    cos, sin = jnp.cos(ang), jnp.sin(ang)
    n = pos.shape[0]
    rest = HEAD_DIM - ROT_DIM
    c = jnp.concatenate([cos, cos, jnp.ones((n, rest), F32)], axis=1)
    s1 = jnp.concatenate([-sin, jnp.zeros((n, HEAD_DIM - half), F32)], axis=1)
    s2 = jnp.concatenate([jnp.zeros((n, half), F32), sin, jnp.zeros((n, rest), F32)], axis=1)
    return tuple(jnp.tile(a, (1, HEAD_DIMP // HEAD_DIM)) for a in (c, s1, s2))


def _prep_weights(p):
    w = {}
    for name in ("ffn1_norm", "ffn2_norm", "mix_norm"):
        w[name] = p[name][:, None, :]
    qk = GLA_HEADS * GLA_DK
    vv = GLA_HEADS * GLA_DV
    a_in = p["a_w_in"][0]
    o = 0
    wq = _pad_heads(a_in[:, o:o + qk], GLA_HEADS, GLA_DK, GLA_DKP, 1); o += qk
    wk = _pad_heads(a_in[:, o:o + qk], GLA_HEADS, GLA_DK, GLA_DKP, 1); o += qk
    wv = _pad_heads(a_in[:, o:o + vv], GLA_HEADS, GLA_DV, GLA_DVP, 1); o += vv
    wr = _pad_heads(a_in[:, o:o + vv], GLA_HEADS, GLA_DV, GLA_DVP, 1); o += vv
    wg = jnp.pad(a_in[:, o:o + GLA_RANK], ((0, 0), (0, GLA_RANKP - GLA_RANK))); o += GLA_RANK
    wm = a_in[:, o:]
    w["a_in"] = jnp.concatenate([wq, wk, wv, wr, wg, wm], axis=1).astype(BF)
    gate = _pad_heads(p["a_w_gate"][0], GLA_HEADS, GLA_DK, GLA_DKP, 1)
    w["a_gate"] = jnp.pad(gate, ((0, GLA_RANKP - GLA_RANK), (0, 0))).astype(BF)
    w["a_bgate"] = _pad_heads(p["a_b_gate"][0][None, :], GLA_HEADS, GLA_DK, GLA_DKP, 1)
    w["a_gn"] = jnp.tile(jnp.pad(p["a_out_norm"][0], (0, GLA_DVP - GLA_DV)), GLA_HEADS)[None, :]
    a_out = p["a_w_out"][0]
    w["a_out"] = jnp.concatenate(
        [_pad_heads(a_out[:vv], GLA_HEADS, GLA_DV, GLA_DVP, 0), a_out[vv:]], axis=0).astype(BF)
    w["b_in"] = p["b_w_in"][0].astype(BF)
    w["b_out"] = p["b_w_out"][0].astype(BF)
    nkv = SWA_KV_HEADS * HEAD_DIM
    w_kv = p["w_kv"]
    w["kv"] = jnp.concatenate(
        [_pad_heads(w_kv[:, :nkv], SWA_KV_HEADS, HEAD_DIM, HEAD_DIMP, 1),
         _pad_heads(w_kv[:, nkv:], SWA_KV_HEADS, HEAD_DIM, HEAD_DIMP, 1)], axis=1).astype(BF)
    w["mem_t"] = p["mem_w_kv"].transpose(0, 2, 1).astype(BF)
    return w


def _compact_kv(a, batch, seq):
    return a.reshape(batch, seq, SWA_KV_HEADS, HEAD_DIMP)[..., :HEAD_DIM]


def kernel(x_prompt, x_sample, state_gla, cache_swa_k, cache_swa_v, cache_mem_k, cache_mem_v,
           mem_prompt, ffn1_norm, ffn1_w_gu, ffn1_w_down, mix_norm, ffn2_norm, ffn2_w_gu,
           ffn2_w_down, mem_norm, mem_w_kv, a_w_in, a_w_gate, a_b_gate, a_out_norm, a_w_out,
           kv_norm, w_kv, b_w_in, b_sinks, b_w_out, final_norm):
    p = dict(ffn1_norm=ffn1_norm, ffn1_w_gu=ffn1_w_gu, ffn1_w_down=ffn1_w_down,
             mix_norm=mix_norm, ffn2_norm=ffn2_norm, ffn2_w_gu=ffn2_w_gu,
             ffn2_w_down=ffn2_w_down, mem_w_kv=mem_w_kv, a_w_in=a_w_in, a_w_gate=a_w_gate,
             a_b_gate=a_b_gate, a_out_norm=a_out_norm, a_w_out=a_w_out, w_kv=w_kv,
             b_w_in=b_w_in, b_w_out=b_w_out)
    w = _prep_weights(p)
    batch, seq, _ = x_prompt.shape
    nb, t, _ = x_sample.shape
    assert nb == _DEC_BATCH and t == _DEC_SEQ and nb * t == TOKEN_TILE
    assert seq % TOKEN_TILE == 0 and seq % GLA_CHUNK == 0
    mp = batch * seq
    kw = SWA_KV_HEADS * HEAD_DIM
    sinks = b_sinks[0]

    mem_k_t, mem_v_t, wgu_a1, wd_a1 = _mem_kv(
        mem_prompt.reshape(batch * MEM_TOKENS, D_MODEL), mem_norm[:, None, :], w["mem_t"],
        [(ffn1_w_gu, 0, D_MODEL // batch), (ffn1_w_down, 0, FFN_DIM // batch)], batch=batch)
    state_t = state_gla.transpose(0, 2, 3, 4, 1)
    kc_t = cache_swa_k.transpose(0, 2, 3, 1).reshape(nb, kw, WINDOW)
    vc_t = cache_swa_v.transpose(0, 2, 3, 1).reshape(nb, kw, WINDOW)
    cmk_t = cache_mem_k.transpose(0, 1, 3, 4, 2).reshape(2, nb, MEM_Q, MEM_TOKENS)
    cmv_t = cache_mem_v.transpose(0, 1, 3, 4, 2).reshape(2, nb, MEM_Q, MEM_TOKENS)

    tabs = tuple(
        jnp.concatenate([a, jnp.tile(b, (nb, 1))], axis=0)
        for a, b in zip(_rope_tables(jnp.arange(seq, dtype=F32)),
                        _rope_tables(PAST_LEN + jnp.arange(t, dtype=F32))))

    mem_mask = (jnp.arange(MEM_Q) // MEM_HEAD_DIM)[None, :] == jnp.arange(MEM_HEADS)[:, None]
    kv_mask = (jnp.arange(kw) // HEAD_DIM)[None, :] == jnp.arange(SWA_KV_HEADS)[:, None]

    def mem_attn_sample(mq, layer):
        q4 = mq[mp:].reshape(nb, 1, t, MEM_Q)
        qbd = jnp.where(mem_mask[None, :, None, :], q4, 0).reshape(nb, MEM_HEADS * t, MEM_Q)
        o = _mem_attn_sample(qbd, cmk_t, cmv_t, layer).reshape(nb, MEM_HEADS, t, MEM_Q)
        o = jnp.sum(jnp.where(mem_mask[None, :, None, :], o, 0.0), axis=1)
        return o.reshape(nb * t, MEM_Q).astype(BF)

    def lanes(a, width):
        return a[mp:].reshape(nb, t, GLA_HEADS, width).transpose(1, 2, 3, 0)

    x, q, k, la, v, r, mq = _layer_a_in(
        x_prompt.reshape(mp, D_MODEL), x_sample.reshape(nb * t, D_MODEL), w["ffn1_norm"],
        wgu_a1, wd_a1, w["mix_norm"], w["a_in"], w["a_gate"], w["a_bgate"], w["a_gn"])
    tok_p, st_p, wgu_a2, wd_a2, wgu_b1, wd_b1, wgu_b2, wd_b2 = _gla_prompt(
        q, k, la, v, r,
        [(ffn2_w_gu, 0, 32), (ffn2_w_down, 0, 128), (ffn1_w_gu, 1, 32), (ffn1_w_down, 1, 128),
         (ffn2_w_gu, 1, 32), (ffn2_w_down, 1, 128)],
        batch=batch, seq=seq)
    st_s, tok_s = _gla_sample(state_t, lanes(q, GLA_DKP), lanes(k, GLA_DKP), lanes(la, GLA_DKP),
                              lanes(v, GLA_DVP), lanes(r, GLA_DVP))
    tok_s = tok_s.transpose(3, 0, 1, 2).reshape(nb * t, GLA_HEADS * GLA_DVP).astype(BF)
    mo_s = mem_attn_sample(mq, 0)
    x, k_sh, v_sh = _layer_a_out(x, tok_p, tok_s, mq, mem_k_t, mem_v_t, mo_s, w["a_out"],
                                 w["ffn2_norm"], wgu_a2, wd_a2, kv_norm[None, :], w["kv"], tabs,
                                 seq=seq)

    x, qs, mq = _layer_b_in(x, w["ffn1_norm"], wgu_b1, wd_b1, w["mix_norm"],
                            w["b_in"], tabs, n_prompt=mp // TOKEN_TILE, seq=seq)

    def new_rows(a):
        a = _compact_kv(a[mp:], nb, t).reshape(nb, t, kw).transpose(0, 2, 1)
        return jnp.pad(a, ((0, 0), (0, 0), (WINDOW - t, 0)))

    q5 = qs[:, mp:].reshape(SWA_Q_PAIRS, nb, t, 2, HEAD_DIM).transpose(1, 0, 3, 2, 4)
    q5 = q5.reshape(nb, SWA_KV_HEADS, SWA_GROUP, t, HEAD_DIM)
    qbd = jnp.where(kv_mask[None, :, None, None, :], jnp.tile(q5, (1, 1, 1, 1, SWA_KV_HEADS)), 0)
    qbd = qbd.reshape(nb, SWA_Q_HEADS * t, kw)
    o, k_s, v_s = _swa_sample(qbd, jnp.repeat(sinks, t)[:, None], kc_t, vc_t,
                              new_rows(k_sh), new_rows(v_sh))
    o = o.reshape(nb, SWA_KV_HEADS, SWA_GROUP, t, kw)
    o = jnp.where(kv_mask[None, :, None, None, :], o, 0.0)
    o = o.reshape(nb, SWA_KV_HEADS, SWA_GROUP, t, SWA_KV_HEADS, HEAD_DIM).sum(axis=4)
    tok_s = o.reshape(nb, SWA_Q_PAIRS, 2, t, HEAD_DIM).transpose(1, 0, 3, 2, 4)
    tok_s = tok_s.reshape(SWA_Q_PAIRS, nb * t, HEAD_DIMP).astype(BF)
    mo_s = mem_attn_sample(mq, 1)
    y_p, y_s = _layer_b_out(sinks, x, qs, k_sh, v_sh, mq, mem_k_t, mem_v_t, tok_s, mo_s,
                            w["b_out"], w["ffn2_norm"], wgu_b2, wd_b2, final_norm[None, :],
                            seq=seq)

    gla_prompt = st_p.transpose(0, 1, 3, 2)[None, :, :, :GLA_DK, :GLA_DV]
    gla_sample = st_s.transpose(0, 4, 1, 2, 3)
    def last_window(a):
        tiles_per_seq = seq // TOKEN_TILE
        a = a.reshape(-1, TOKEN_TILE, SWA_KV_HEADS * HEAD_DIMP)
        a = a[tiles_per_seq - 1:batch * tiles_per_seq:tiles_per_seq, TOKEN_TILE - WINDOW:]
        return a.reshape(batch, WINDOW, SWA_KV_HEADS, HEAD_DIMP)[..., :HEAD_DIM]

    swa_k_prompt = last_window(k_sh)
    swa_v_prompt = last_window(v_sh)
    swa_k_sample = k_s.reshape(nb, SWA_KV_HEADS, HEAD_DIM, WINDOW).transpose(0, 3, 1, 2)
    swa_v_sample = v_s.reshape(nb, SWA_KV_HEADS, HEAD_DIM, WINDOW).transpose(0, 3, 1, 2)
    mem_shape = (2, batch, MEM_HEADS, MEM_HEAD_DIM, MEM_TOKENS)
    mem_k_prompt = mem_k_t.reshape(mem_shape).transpose(0, 1, 4, 2, 3)
    mem_v_prompt = mem_v_t.reshape(mem_shape).transpose(0, 1, 4, 2, 3)
    return (y_p.reshape(batch, seq, D_MODEL), y_s.reshape(nb, t, D_MODEL), gla_prompt,
            gla_sample, swa_k_prompt, swa_v_prompt, swa_k_sample, swa_v_sample,
            mem_k_prompt, mem_v_prompt)
```

```python
import functools
import math

import jax
import jax.numpy as jnp
import numpy as np
from jax import lax
from jax.experimental import pallas as pl
from jax.experimental.pallas import tpu as pltpu

F32 = jnp.float32
BF = jnp.bfloat16

D_MODEL = 1024
FFN_DIM = 2816
EPS = 1e-6

GLA_HEADS = 4
GLA_DK = 96
GLA_DV = 192
GLA_DKP = 128
GLA_DVP = 256
GLA_RANK = 16
GLA_RANKP = 128
GLA_GATE_NORM = 16.0
GLA_CHUNK = 256
GLA_SEQS = 2

HEAD_DIM = 64
HEAD_DIMP = 128
SWA_Q_HEADS = 12
SWA_KV_HEADS = 3
SWA_GROUP = SWA_Q_HEADS // SWA_KV_HEADS
SWA_Q_PAIRS = SWA_Q_HEADS // 2
SWA_PAIRS_PER_KV = SWA_GROUP // 2
WINDOW = 128
ROT_DIM = 16
ROPE_THETA = 500000.0
PAST_LEN = 8192

MEM_TOKENS = 256
MEM_HEADS = 4
MEM_HEAD_DIM = 64
MEM_Q = MEM_HEADS * MEM_HEAD_DIM

FFN_TF = 256
FFN_CHUNKS = FFN_DIM // FFN_TF
TOKEN_TILE = 512

VMEM_LIMIT = 60 * 1024 * 1024


def _params(*sem):
    return pltpu.CompilerParams(dimension_semantics=sem, vmem_limit_bytes=VMEM_LIMIT)


def _resident(shape):
    nd = len(shape)
    return pl.BlockSpec(shape, lambda *_: (0,) * nd, pipeline_mode=pl.Buffered(1))


def _layer_block(shape, layer):
    nd = len(shape)
    return pl.BlockSpec((1,) + tuple(shape[1:]), lambda *_: (layer,) + (0,) * (nd - 1),
                        pipeline_mode=pl.Buffered(1))


def _rows(width):
    return pl.BlockSpec((TOKEN_TILE, width), lambda i: (i, 0))


def _prompt_rows(width, n_prompt):
    return pl.BlockSpec((TOKEN_TILE, width), lambda i: (jnp.minimum(i, n_prompt - 1), 0))


def _sample_rows(width):
    return pl.BlockSpec((TOKEN_TILE, width), lambda i: (0, 0), pipeline_mode=pl.Buffered(1))


def _rms(x, g):
    ms = jnp.mean(x * x, axis=-1, keepdims=True)
    return x * lax.rsqrt(ms + EPS) * g


def _silu(x):
    return x * (1.0 / (1.0 + jnp.exp(-x)))


def _dot(a, b):
    return jnp.dot(a, b, preferred_element_type=F32)


def _dot_nt(a, b):
    return lax.dot_general(a, b, (((1,), (1,)), ((), ())), preferred_element_type=F32)


def _dot_tn(a, b):
    return lax.dot_general(a, b, (((0,), (0,)), ((), ())), preferred_element_type=F32)


def _pick(is_prompt, p_ref, s_ref):
    return jnp.where(is_prompt, p_ref[...], s_ref[...])


def _cast_job(w, layer, rows, grid):
    _, r, c = w.shape
    assert r % rows == 0 and rows % 16 == 0
    nblk = r // rows
    total = math.prod(grid)
    assert nblk <= total
    steps_per_block = total // nblk

    def block(*idx):
        step = idx[0]
        for dim, i in zip(grid[1:], idx[1:]):
            step = step * dim + i
        return jnp.minimum(step // steps_per_block, nblk - 1)

    in_spec = pl.BlockSpec((1, rows, c), lambda *idx: (layer, block(*idx), 0))
    out_spec = pl.BlockSpec((1, rows, c), lambda *idx: (0, block(*idx), 0))
    return in_spec, out_spec, jax.ShapeDtypeStruct((1, r, c), BF)


def _run_cast_jobs(refs):
    n = len(refs) // 2
    for src, dst in zip(refs[:n], refs[n:]):
        dst[...] = src[...].astype(BF)


def _ffn_half(x, g_ref, wgu_ref, wd_ref):
    h = _rms(x, g_ref[0]).astype(BF)
    acc = jnp.zeros(x.shape, F32)
    for c in range(FFN_CHUNKS):
        lo, hi = c * FFN_TF, (c + 1) * FFN_TF
        gate = _dot(h, wgu_ref[0, :, lo:hi])
        up = _dot(h, wgu_ref[0, :, FFN_DIM + lo:FFN_DIM + hi])
        a = (_silu(gate) * up).astype(BF)
        acc = acc + _dot(a, wd_ref[0, lo:hi, :])
    return x + 0.5 * acc


def _rope(x, c, s1, s2):
    return (x * c + pltpu.roll(x, HEAD_DIMP - ROT_DIM // 2, 1) * s1
            + pltpu.roll(x, ROT_DIM // 2, 1) * s2)


_A_Q = 0
_A_K = _A_Q + GLA_HEADS * GLA_DKP
_A_V = _A_K + GLA_HEADS * GLA_DKP
_A_R = _A_V + GLA_HEADS * GLA_DVP
_A_G = _A_R + GLA_HEADS * GLA_DVP
_A_M = _A_G + GLA_RANKP
_A_END = _A_M + MEM_Q


def _layer_a_in_kernel(xp_ref, xs_ref, g1_ref, wgu_ref, wd_ref, gm_ref, w_ref, wg_ref, bg_ref,
                       gn_ref, x_ref, q_ref, k_ref, la_ref, v_ref, r_ref, mq_ref, *, n_prompt):
    is_prompt = pl.program_id(0) < n_prompt
    x = _ffn_half(_pick(is_prompt, xp_ref, xs_ref), g1_ref, wgu_ref, wd_ref)
    x_ref[...] = x
    h = _rms(x, gm_ref[0]).astype(BF)

    proj = _dot(h, w_ref[...])

    def mm(a, b):
        return proj[:, a:b]

    q_ref[...] = mm(_A_Q, _A_K) * (GLA_DK ** -0.5)
    k_ref[...] = mm(_A_K, _A_V)
    v_ref[...] = mm(_A_V, _A_R)
    r_ref[...] = _silu(mm(_A_R, _A_G)) * gn_ref[...]
    z = _dot(mm(_A_G, _A_M).astype(BF), wg_ref[...]) + bg_ref[...]
    la_ref[...] = (jnp.minimum(z, 0.0) - jnp.log1p(jnp.exp(-jnp.abs(z)))) * (1.0 / GLA_GATE_NORM)
    mq_ref[...] = (mm(_A_M, _A_END) * (MEM_HEAD_DIM ** -0.5)).astype(BF)


def _layer_a_in(xp, xs, g1, wgu, wd, gm, w, wg, bg, gn):
    n_prompt = xp.shape[0] // TOKEN_TILE
    m = xp.shape[0] + xs.shape[0]
    qk = GLA_HEADS * GLA_DKP
    vr = GLA_HEADS * GLA_DVP
    outs = ((D_MODEL, F32), (qk, F32), (qk, F32), (qk, F32), (vr, F32), (vr, F32), (MEM_Q, BF))
    return pl.pallas_call(
        functools.partial(_layer_a_in_kernel, n_prompt=n_prompt),
        grid=(m // TOKEN_TILE,),
        in_specs=[_prompt_rows(D_MODEL, n_prompt), _sample_rows(D_MODEL),
                  _layer_block(g1.shape, 0), _layer_block(wgu.shape, 0), _layer_block(wd.shape, 0),
                  _layer_block(gm.shape, 0), _resident(w.shape), _resident(wg.shape),
                  _resident(bg.shape), _resident(gn.shape)],
        out_specs=[_rows(n) for n, _ in outs],
        out_shape=[jax.ShapeDtypeStruct((m, n), dt) for n, dt in outs],
        compiler_params=_params("arbitrary"),
        name="layer_a_in",
    )(xp, xs, g1, wgu, wd, gm, w, wg, bg, gn)


def _build_mem_block_diag(mk_ref, mv_ref, kbd_scr, vbd_scr):
    shape = (MEM_Q, MEM_HEADS * MEM_TOKENS)
    rh = lax.broadcasted_iota(jnp.int32, shape, 0) // MEM_HEAD_DIM
    ch = lax.broadcasted_iota(jnp.int32, shape, 1) // MEM_TOKENS
    diag = rh == ch
    kbd_scr[...] = jnp.where(diag, jnp.concatenate([mk_ref[0, 0]] * MEM_HEADS, axis=1),
                             0.0).astype(BF)
    vbd_scr[...] = jnp.where(diag, jnp.concatenate([mv_ref[0, 0]] * MEM_HEADS, axis=1),
                             0.0).astype(BF)


def _mem_attn_tile(q, kbd, vbd):
    tq = q.shape[0]
    s = _dot(q, kbd)
    ps, inv = [], []
    for h in range(MEM_HEADS):
        sh = s[:, h * MEM_TOKENS:(h + 1) * MEM_TOKENS]
        e = jnp.exp(sh - jnp.max(sh, axis=1, keepdims=True))
        inv.append(1.0 / jnp.sum(e, axis=1, keepdims=True))
        ps.append(e.astype(BF))
    o = _dot_nt(jnp.concatenate(ps, axis=1), vbd)
    lane_h = lax.broadcasted_iota(jnp.int32, (tq, MEM_Q), 1) // MEM_HEAD_DIM
    scale = jnp.where(lane_h == 0, inv[0],
                      jnp.where(lane_h == 1, inv[1], jnp.where(lane_h == 2, inv[2], inv[3])))
    return o * scale


def _swa_tile(q_ref, kprev_ref, kcur_ref, vprev_ref, vcur_ref, sink_ref, first_of_seq):
    blk = WINDOW
    nsub = TOKEN_TILE // blk
    qi = lax.broadcasted_iota(jnp.int32, (blk, 2 * blk), 0)
    kj = lax.broadcasted_iota(jnp.int32, (blk, 2 * blk), 1)
    d = blk + qi - kj
    band = (d >= 0) & (d < WINDOW)
    bias_mid = jnp.where(band, 0.0, -jnp.inf)
    bias_first = jnp.where(band & (kj >= blk), 0.0, -jnp.inf)
    bias0 = jnp.where(first_of_seq, bias_first, bias_mid)
    outs = [[None] * SWA_Q_PAIRS for _ in range(nsub)]
    for kh in range(SWA_KV_HEADS):
        sl = slice(kh * HEAD_DIMP, (kh + 1) * HEAD_DIMP)
        kblk = [kprev_ref[:, sl]] + [kcur_ref[s * blk:(s + 1) * blk, sl] for s in range(nsub)]
        vblk = [vprev_ref[:, sl]] + [vcur_ref[s * blk:(s + 1) * blk, sl] for s in range(nsub)]
        k2 = [(b.astype(BF), pltpu.roll(b, HEAD_DIM, 1).astype(BF)) for b in kblk]
        v2 = [(b.astype(BF), pltpu.roll(b, HEAD_DIM, 1).astype(BF)) for b in vblk]
        for sub in range(nsub):
            bias = bias0 if sub == 0 else bias_mid
            for pj in range(SWA_PAIRS_PER_KV):
                pp = kh * SWA_PAIRS_PER_KV + pj
                q = q_ref[pp, sub * blk:(sub + 1) * blk, :]
                o = None
                for half in range(2):
                    kb = jnp.concatenate([k2[sub][half], k2[sub + 1][half]], axis=0)
                    vb = jnp.concatenate([v2[sub][half], v2[sub + 1][half]], axis=0)
                    s = _dot_nt(q, kb) + bias
                    sink = sink_ref[2 * pp + half]
                    m = jnp.maximum(jnp.max(s, axis=1, keepdims=True), sink)
                    e = jnp.exp(s - m)
                    l = jnp.sum(e, axis=1, keepdims=True) + jnp.exp(sink - m)
                    oh = _dot(e.astype(BF), vb) * (1.0 / l)
                    o = oh if o is None else o + oh
                outs[sub][pp] = o.astype(BF)
    return jnp.concatenate([jnp.concatenate(row, axis=1) for row in outs], axis=0)


def _layer_a_out_kernel(x_ref, tokp_ref, toks_ref, mq_ref, mk_ref, mv_ref, mos_ref, wo_ref,
                        g2_ref, wgu_ref, wd_ref, gkv_ref, wkv_ref, c_ref, s1_ref, s2_ref,
                        *refs, n_prompt, tiles_per_seq):
    n = (len(refs) - 5) // 2
    xo_ref, k_ref, v_ref = refs[n:n + 3]
    kbd_scr, vbd_scr = refs[-2:]
    _run_cast_jobs(refs[:n] + refs[n + 3:-2])
    i = pl.program_id(0)
    is_prompt = i < n_prompt

    @pl.when(i % tiles_per_seq == 0)
    def _():
        _build_mem_block_diag(mk_ref, mv_ref, kbd_scr, vbd_scr)

    nt = GLA_HEADS * GLA_DVP
    mo = _mem_attn_tile(mq_ref[...], kbd_scr[...], vbd_scr[...]).astype(BF)
    mo = jnp.where(is_prompt, mo, mos_ref[...])
    x = (x_ref[...] + _dot(_pick(is_prompt, tokp_ref, toks_ref), wo_ref[:nt, :])
         + _dot(mo, wo_ref[nt:, :]))
    x = _ffn_half(x, g2_ref, wgu_ref, wd_ref)
    xo_ref[...] = x
    h = _rms(x, gkv_ref[...]).astype(BF)
    c, s1, s2 = c_ref[...], s1_ref[...], s2_ref[...]
    kw = SWA_KV_HEADS * HEAD_DIMP
    kv = _dot(h, wkv_ref[...])
    for hh in range(SWA_KV_HEADS):
        sl = slice(hh * HEAD_DIMP, (hh + 1) * HEAD_DIMP)
        k_ref[:, sl] = _rope(kv[:, sl], c, s1, s2)
    v_ref[...] = kv[:, kw:]


def _tab_spec(n_prompt, blocks_per_seq):
    return pl.BlockSpec((TOKEN_TILE, HEAD_DIMP),
                        lambda i: (jnp.where(i < n_prompt, i % blocks_per_seq, blocks_per_seq), 0))


def _mem_kv_spec(layer, n_prompt, tiles_per_seq):
    return pl.BlockSpec(
        (1, 1, MEM_Q, MEM_TOKENS),
        lambda i: (layer, jnp.minimum(i, n_prompt - 1) // tiles_per_seq, 0, 0))


def _mem_scratch():
    return pltpu.VMEM((MEM_Q, MEM_HEADS * MEM_TOKENS), BF)


def _layer_a_out(x, tokp, toks, mq, mk_t, mv_t, mos, wo, g2, wgu, wd, gkv, wkv, tabs, casts,
                 *, seq):
    m = x.shape[0]
    grid = (m // TOKEN_TILE,)
    n_prompt = tokp.shape[0] // TOKEN_TILE
    tiles_per_seq = seq // TOKEN_TILE
    kw = SWA_KV_HEADS * HEAD_DIMP
    nt = GLA_HEADS * GLA_DVP
    tab = _tab_spec(n_prompt, tiles_per_seq)
    mem_spec = _mem_kv_spec(0, n_prompt, tiles_per_seq)
    jobs = [_cast_job(cw, layer, rows, grid) for cw, layer, rows in casts]
    return pl.pallas_call(
        functools.partial(_layer_a_out_kernel, n_prompt=n_prompt, tiles_per_seq=tiles_per_seq),
        grid=grid,
        in_specs=[_rows(D_MODEL), _prompt_rows(nt, n_prompt), _sample_rows(nt),
                  _rows(MEM_Q), mem_spec, mem_spec, _sample_rows(MEM_Q), _resident(wo.shape),
                  _layer_block(g2.shape, 0), _layer_block(wgu.shape, 0), _layer_block(wd.shape, 0),
                  _resident(gkv.shape), _resident(wkv.shape), tab, tab, tab]
                 + [j[0] for j in jobs],
        out_specs=[_rows(D_MODEL), _rows(kw), _rows(kw)] + [j[1] for j in jobs],
        out_shape=[jax.ShapeDtypeStruct((m, D_MODEL), F32),
                   jax.ShapeDtypeStruct((m, kw), F32), jax.ShapeDtypeStruct((m, kw), F32)]
                  + [j[2] for j in jobs],
        scratch_shapes=[_mem_scratch(), _mem_scratch()],
        compiler_params=_params("arbitrary"),
        name="layer_a_out",
    )(x, tokp, toks, mq, mk_t, mv_t, mos, wo, g2, wgu, wd, gkv, wkv, *tabs,
      *[c[0] for c in casts])


def _layer_b_in_kernel(x_ref, g1_ref, wgu_ref, wd_ref, gm_ref, w_ref, c_ref, s1_ref, s2_ref,
                       *refs):
    n = (len(refs) - 3) // 2
    xo_ref, q_ref, mq_ref = refs[n:n + 3]
    _run_cast_jobs(refs[:n] + refs[n + 3:])
    x = _ffn_half(x_ref[...], g1_ref, wgu_ref, wd_ref)
    xo_ref[...] = x
    h = _rms(x, gm_ref[0]).astype(BF)
    c, s1, s2 = c_ref[...], s1_ref[...], s2_ref[...]
    nq = SWA_Q_HEADS * HEAD_DIM
    qm = _dot(h, w_ref[...])
    for pp in range(SWA_Q_PAIRS):
        q = qm[:, pp * HEAD_DIMP:(pp + 1) * HEAD_DIMP]
        q_ref[pp] = (_rope(q, c, s1, s2) * (HEAD_DIM ** -0.5)).astype(BF)
    mq_ref[...] = (qm[:, nq:] * (MEM_HEAD_DIM ** -0.5)).astype(BF)


def _layer_b_in(x, g1, wgu, wd, gm, w, tabs, casts, *, n_prompt, seq):
    m = x.shape[0]
    grid = (m // TOKEN_TILE,)
    tab = _tab_spec(n_prompt, seq // TOKEN_TILE)
    q_spec = pl.BlockSpec((SWA_Q_PAIRS, TOKEN_TILE, HEAD_DIMP), lambda i: (0, i, 0))
    jobs = [_cast_job(cw, layer, rows, grid) for cw, layer, rows in casts]
    return pl.pallas_call(
        _layer_b_in_kernel,
        grid=grid,
        in_specs=[_rows(D_MODEL), _layer_block(g1.shape, 1), _layer_block(wgu.shape, 0),
                  _layer_block(wd.shape, 0), _layer_block(gm.shape, 1), _resident(w.shape),
                  tab, tab, tab] + [j[0] for j in jobs],
        out_specs=[_rows(D_MODEL), q_spec, _rows(MEM_Q)] + [j[1] for j in jobs],
        out_shape=[jax.ShapeDtypeStruct((m, D_MODEL), F32),
                   jax.ShapeDtypeStruct((SWA_Q_PAIRS, m, HEAD_DIMP), BF),
                   jax.ShapeDtypeStruct((m, MEM_Q), BF)] + [j[2] for j in jobs],
        compiler_params=_params("arbitrary"),
        name="layer_b_in",
    )(x, g1, wgu, wd, gm, w, *tabs, *[c[0] for c in casts])


def _layer_b_out_kernel(sink_ref, x_ref, q_ref, kprev_ref, kcur_ref, vprev_ref, vcur_ref,
                        mq_ref, mk_ref, mv_ref, toks_ref, mos_ref, wo_ref, g2_ref, wgu_ref,
                        wd_ref, gf_ref, yp_ref, ys_ref, kbd_scr, vbd_scr,
                        *, n_prompt, tiles_per_seq):
    i = pl.program_id(0)
    is_prompt = i < n_prompt
    first_of_seq = i % tiles_per_seq == 0

    @pl.when(first_of_seq)
    def _():
        _build_mem_block_diag(mk_ref, mv_ref, kbd_scr, vbd_scr)

    nt = SWA_Q_HEADS * HEAD_DIM
    mo = _mem_attn_tile(mq_ref[...], kbd_scr[...], vbd_scr[...]).astype(BF)
    mo = jnp.where(is_prompt, mo, mos_ref[...])
    tok = _swa_tile(q_ref, kprev_ref, kcur_ref, vprev_ref, vcur_ref, sink_ref, first_of_seq)
    toks = jnp.concatenate([toks_ref[pp] for pp in range(SWA_Q_PAIRS)], axis=1)
    tok = jnp.where(is_prompt, tok, toks)
    x = x_ref[...] + _dot(tok, wo_ref[:nt, :]) + _dot(mo, wo_ref[nt:, :])
    y = _rms(_ffn_half(x, g2_ref, wgu_ref, wd_ref), gf_ref[...])

    @pl.when(is_prompt)
    def _():
        yp_ref[...] = y

    @pl.when(jnp.logical_not(is_prompt))
    def _():
        ys_ref[...] = y


def _layer_b_out(sinks, x, qs, k_sh, v_sh, mq, mk_t, mv_t, toks, mos, wo, g2, wgu, wd, gf, *, seq):
    m = x.shape[0]
    n_tiles = m // TOKEN_TILE
    n_prompt = n_tiles - 1
    tiles_per_seq = seq // TOKEN_TILE
    kw = SWA_KV_HEADS * HEAD_DIMP
    blocks_per_tile = TOKEN_TILE // WINDOW
    q_spec = pl.BlockSpec((SWA_Q_PAIRS, TOKEN_TILE, HEAD_DIMP), lambda i: (0, i, 0))
    prev_spec = pl.BlockSpec((WINDOW, kw), lambda i: (jnp.maximum(i * blocks_per_tile - 1, 0), 0))
    toks_spec = pl.BlockSpec((SWA_Q_PAIRS, TOKEN_TILE, HEAD_DIMP), lambda i: (0, 0, 0),
                             pipeline_mode=pl.Buffered(1))
    mem_spec = _mem_kv_spec(1, n_prompt, tiles_per_seq)
    return pl.pallas_call(
        functools.partial(_layer_b_out_kernel, n_prompt=n_prompt, tiles_per_seq=tiles_per_seq),
        grid=(n_tiles,),
        in_specs=[pl.BlockSpec(memory_space=pltpu.SMEM), _rows(D_MODEL), q_spec,
                  prev_spec, _rows(kw), prev_spec, _rows(kw), _rows(MEM_Q), mem_spec, mem_spec,
                  toks_spec, _sample_rows(MEM_Q), _resident(wo.shape), _layer_block(g2.shape, 1),
                  _layer_block(wgu.shape, 0), _layer_block(wd.shape, 0), _resident(gf.shape)],
        out_specs=[_prompt_rows(D_MODEL, n_prompt),
                   pl.BlockSpec((TOKEN_TILE, D_MODEL), lambda i: (0, 0))],
        out_shape=[jax.ShapeDtypeStruct((n_prompt * TOKEN_TILE, D_MODEL), F32),
                   jax.ShapeDtypeStruct((TOKEN_TILE, D_MODEL), F32)],
        scratch_shapes=[_mem_scratch(), _mem_scratch()],
        compiler_params=_params("arbitrary"),
        name="layer_b_out",
    )(sinks, x, qs, k_sh, k_sh, v_sh, v_sh, mq, mk_t, mv_t, toks, mos, wo, g2, wgu, wd, gf)


def _mem_kv_kernel(x_ref, g_ref, wt_ref, *refs):
    n = (len(refs) - 2) // 2
    k_ref, v_ref = refs[n:n + 2]
    _run_cast_jobs(refs[:n] + refs[n + 2:])
    x = x_ref[...]
    xn = x * lax.rsqrt(jnp.mean(x * x, axis=-1, keepdims=True) + EPS)
    for l in range(2):
        h = (xn * g_ref[l]).astype(BF)
        kvt = _dot_nt(wt_ref[l], h)
        k_ref[l, 0] = kvt[:MEM_Q, :]
        v_ref[l, 0] = kvt[MEM_Q:, :]


def _mem_kv(mem, g, wt, casts, *, batch):
    out_spec = pl.BlockSpec((2, 1, MEM_Q, MEM_TOKENS), lambda b: (0, b, 0, 0))
    jobs = [_cast_job(cw, layer, rows, (batch,)) for cw, layer, rows in casts]
    return pl.pallas_call(
        _mem_kv_kernel,
        grid=(batch,),
        in_specs=[pl.BlockSpec((MEM_TOKENS, D_MODEL), lambda b: (b, 0)), _resident(g.shape),
                  _resident(wt.shape)] + [j[0] for j in jobs],
        out_specs=[out_spec, out_spec] + [j[1] for j in jobs],
        out_shape=[jax.ShapeDtypeStruct((2, batch, MEM_Q, MEM_TOKENS), F32)] * 2
                  + [j[2] for j in jobs],
        compiler_params=_params("arbitrary"),
        name="mem_kv",
    )(mem, g, wt, *[c[0] for c in casts])


def _gla_prompt_kernel(*refs):
    ns = GLA_SEQS
    seq_in = [refs[5 * i:5 * i + 5] for i in range(ns)]
    rest = refs[5 * ns:-1]
    s_scr = refs[-1]
    n = (len(rest) - 2) // 2
    tok_ref, st_ref = rest[n:n + 2]
    _run_cast_jobs(rest[:n] + rest[n + 2:])
    c = pl.program_id(1)
    C = GLA_CHUNK

    @pl.when(c == 0)
    def _():
        s_scr[...] = jnp.zeros(s_scr.shape, F32)

    row = lax.broadcasted_iota(jnp.int32, (C, C), 0)
    col = lax.broadcasted_iota(jnp.int32, (C, C), 1)
    causal = row >= col
    ltri = jnp.where(causal, 1.0, 0.0).astype(BF)
    _gla_chunks([(seq_in[i], tok_ref.at[i], s_scr.at[i], h)
                 for i in range(ns) for h in range(GLA_HEADS)], causal, ltri)

    @pl.when(c == pl.num_programs(1) - 1)
    def _():
        st_ref[...] = s_scr[...]


def _gla_chunks(chains, causal, ltri):
    C = GLA_CHUNK

    def sk(h):
        return slice(h * GLA_DKP, (h + 1) * GLA_DKP)

    def sv(h):
        return slice(h * GLA_DVP, (h + 1) * GLA_DVP)

    bs = []
    for (q_ref, k_ref, la_ref, v_ref, r_ref), _, _, h in chains:
        la = la_ref[:, sk(h)]
        hi = la.astype(BF)
        lo = (la - hi.astype(F32)).astype(BF)
        bb = _dot(ltri, jnp.concatenate([hi, lo], axis=1))
        bs.append(bb[:, :GLA_DKP] + bb[:, GLA_DKP:])
    ops = []
    for ((q_ref, k_ref, la_ref, v_ref, r_ref), _, _, h), b in zip(chains, bs):
        b_mid = b[C // 2 - 1:C // 2, :]
        b_last = b[C - 1:C, :]
        qe = q_ref[:, sk(h)] * jnp.exp(b - b_mid)
        ke = k_ref[:, sk(h)] * jnp.exp(b_mid - b)
        qb = (qe * jnp.exp(b_mid)).astype(BF)
        kd = (ke * jnp.exp(b_last - b_mid)).astype(BF)
        ops.append((qe.astype(BF), ke.astype(BF), qb, kd, jnp.exp(b_last)))
    As = [jnp.where(causal, _dot_nt(qe, ke), 0.0).astype(BF) for qe, ke, _, _, _ in ops]
    outs = []
    for ((q_ref, k_ref, la_ref, v_ref, r_ref), _, s_ref, h), a, (_, _, qb, kd, decay) in zip(
            chains, As, ops):
        v = v_ref[:, sv(h)].astype(BF)
        st = s_ref[h]
        outs.append(_dot(a, v) + _dot_nt(qb, st.astype(BF)))
        s_ref[h] = st * decay + _dot_tn(v, kd)
    for ((q_ref, k_ref, la_ref, v_ref, r_ref), tok_ref, _, h), o in zip(chains, outs):
        ms = jnp.sum(o * o, axis=1, keepdims=True) * (1.0 / GLA_DV)
        tok_ref[:, sv(h)] = (o * lax.rsqrt(ms + EPS) * r_ref[:, sv(h)]).astype(BF)


def _gla_prompt(q, k, la, v, r, casts, *, batch, seq):
    ns = GLA_SEQS
    m = batch * seq
    nc = seq // GLA_CHUNK
    qk = GLA_HEADS * GLA_DKP
    vr = GLA_HEADS * GLA_DVP
    grid = (batch // ns, nc)

    def tok_map(i):
        return lambda b, c: ((b * ns + i) * nc + c, 0)

    seq_specs = []
    for i in range(ns):
        seq_specs += [pl.BlockSpec((GLA_CHUNK, qk), tok_map(i))] * 3
        seq_specs += [pl.BlockSpec((GLA_CHUNK, vr), tok_map(i))] * 2
    jobs = [_cast_job(cw, layer, rows, grid) for cw, layer, rows in casts]
    res = pl.pallas_call(
        _gla_prompt_kernel,
        grid=grid,
        in_specs=seq_specs + [j[0] for j in jobs],
        out_specs=[pl.BlockSpec((ns, GLA_CHUNK, vr), lambda b, c: (b, c, 0)),
                   pl.BlockSpec((ns, GLA_HEADS, GLA_DVP, GLA_DKP), lambda b, c: (b, 0, 0, 0))]
                  + [j[1] for j in jobs],
        out_shape=[jax.ShapeDtypeStruct((batch, seq, vr), BF),
                   jax.ShapeDtypeStruct((batch, GLA_HEADS, GLA_DVP, GLA_DKP), F32)]
                  + [j[2] for j in jobs],
        scratch_shapes=[pltpu.VMEM((ns, GLA_HEADS, GLA_DVP, GLA_DKP), F32)],
        compiler_params=_params("arbitrary", "arbitrary"),
        name="gla_prompt",
    )(*([q, k, la, v, r] * ns), *[c[0] for c in casts])
    return (res[0].reshape(m, vr),) + tuple(res[1:])


_SAMPLE_BB = 8
_DEC_SEQ = 4
_DEC_BATCH = 128
_GLA_DK_BLK = 32


def _gla_sample_kernel(s_ref, q_ref, k_ref, la_ref, v_ref, r_ref, so_ref, tok_ref, o_scr):
    j = pl.program_id(1)

    @pl.when(j == 0)
    def _():
        o_scr[...] = jnp.zeros(o_scr.shape, F32)

    def body(dk, carry):
        s = s_ref[0, 0, dk]
        for t in range(_DEC_SEQ):
            a = jnp.exp(la_ref[t, 0, pl.ds(dk, 1), :])
            s = a * s + k_ref[t, 0, pl.ds(dk, 1), :] * v_ref[t, 0, :GLA_DV, :]
            o_scr[t] = o_scr[t] + q_ref[t, 0, pl.ds(dk, 1), :] * s
        so_ref[0, 0, dk] = s
        return carry

    lax.fori_loop(0, _GLA_DK_BLK, body, 0, unroll=4)

    @pl.when(j == pl.num_programs(1) - 1)
    def _():
        tok_ref[...] = jnp.zeros(tok_ref.shape, F32)
        for t in range(_DEC_SEQ):
            o = o_scr[t]
            ms = jnp.sum(o * o, axis=0, keepdims=True) * (1.0 / GLA_DV)
            tok_ref[t, 0, :GLA_DV, :] = o * lax.rsqrt(ms + EPS) * r_ref[t, 0, :GLA_DV, :]


def _gla_sample(state, q, k, la, v, r):
    qk_spec = pl.BlockSpec((_DEC_SEQ, 1, _GLA_DK_BLK, _DEC_BATCH), lambda h, j: (0, h, j, 0))
    vr_spec = pl.BlockSpec((_DEC_SEQ, 1, GLA_DVP, _DEC_BATCH), lambda h, j: (0, h, 0, 0))
    s_spec = pl.BlockSpec((1, 1, _GLA_DK_BLK, GLA_DV, _DEC_BATCH), lambda h, j: (0, h, j, 0, 0))
    return pl.pallas_call(
        _gla_sample_kernel,
        grid=(GLA_HEADS, GLA_DK // _GLA_DK_BLK),
        in_specs=[s_spec, qk_spec, qk_spec, qk_spec, vr_spec, vr_spec],
        out_specs=[s_spec, vr_spec],
        out_shape=[jax.ShapeDtypeStruct(state.shape, F32),
                   jax.ShapeDtypeStruct(v.shape, F32)],
        scratch_shapes=[pltpu.VMEM((_DEC_SEQ, GLA_DV, _DEC_BATCH), F32)],
        compiler_params=_params("parallel", "arbitrary"),
        name="gla_sample",
    )(state, q, k, la, v, r)


def _mem_attn_sample_kernel(q_ref, mk_ref, mv_ref, o_ref):
    scores = [_dot(q_ref[bi], mk_ref[0, bi].astype(BF)) for bi in range(_SAMPLE_BB)]
    probs = []
    for s in scores:
        e = jnp.exp(s - jnp.max(s, axis=1, keepdims=True))
        probs.append((e * (1.0 / jnp.sum(e, axis=1, keepdims=True))).astype(BF))
    for bi, p in enumerate(probs):
        o_ref[bi] = _dot_nt(p, mv_ref[0, bi].astype(BF))


def _mem_attn_sample(qbd, mk_t, mv_t, layer):
    nb = qbd.shape[0]
    nr = MEM_HEADS * _DEC_SEQ
    kv_spec = pl.BlockSpec((1, _SAMPLE_BB, MEM_Q, MEM_TOKENS), lambda i: (layer, i, 0, 0))
    q_spec = pl.BlockSpec((_SAMPLE_BB, nr, MEM_Q), lambda i: (i, 0, 0))
    return pl.pallas_call(
        _mem_attn_sample_kernel,
        grid=(nb // _SAMPLE_BB,),
        in_specs=[q_spec, kv_spec, kv_spec],
        out_specs=q_spec,
        out_shape=jax.ShapeDtypeStruct((nb, nr, MEM_Q), F32),
        compiler_params=_params("parallel"),
        name="mem_attn_sample",
    )(qbd, mk_t, mv_t)


def _swa_sample_kernel(q_ref, sink_ref, kc_ref, vc_ref, kn_ref, vn_ref, o_ref, ko_ref, vo_ref):
    nq = SWA_Q_HEADS * _DEC_SEQ
    t = lax.broadcasted_iota(jnp.int32, (nq, WINDOW), 0) % _DEC_SEQ
    pos = lax.broadcasted_iota(jnp.int32, (nq, WINDOW), 1)
    new0 = WINDOW - _DEC_SEQ
    bias_c = jnp.where(pos > t, 0.0, -jnp.inf)
    bias_n = jnp.where((pos >= new0) & (pos - new0 <= t), 0.0, -jnp.inf)
    is_new = lax.broadcasted_iota(jnp.int32, (SWA_KV_HEADS * HEAD_DIM, WINDOW), 1) >= new0
    sink = sink_ref[...]
    seqs = range(_SAMPLE_BB)
    scores = [(_dot(q_ref[bi], kc_ref[bi].astype(BF)) + bias_c,
               _dot(q_ref[bi], kn_ref[bi].astype(BF)) + bias_n) for bi in seqs]
    probs = []
    for sc, sn in scores:
        m = jnp.maximum(jnp.maximum(jnp.max(sc, axis=1, keepdims=True),
                                    jnp.max(sn, axis=1, keepdims=True)), sink)
        ec = jnp.exp(sc - m)
        en = jnp.exp(sn - m)
        l = (jnp.sum(ec, axis=1, keepdims=True) + jnp.sum(en, axis=1, keepdims=True)
             + jnp.exp(sink - m))
        inv = 1.0 / l
        probs.append(((ec * inv).astype(BF), (en * inv).astype(BF)))
    for bi, (pc, pn) in zip(seqs, probs):
        o_ref[bi] = (_dot_nt(pc, vc_ref[bi].astype(BF)) + _dot_nt(pn, vn_ref[bi].astype(BF)))
    for bi in seqs:
        ko_ref[bi] = jnp.where(is_new, kn_ref[bi], pltpu.roll(kc_ref[bi], new0, 1))
        vo_ref[bi] = jnp.where(is_new, vn_ref[bi], pltpu.roll(vc_ref[bi], new0, 1))


def _swa_sample(qbd, sink_col, kc, vc, kn, vn):
    nb = qbd.shape[0]
    kw = SWA_KV_HEADS * HEAD_DIM
    nq = SWA_Q_HEADS * _DEC_SEQ
    kv_spec = pl.BlockSpec((_SAMPLE_BB, kw, WINDOW), lambda i: (i, 0, 0))
    q_spec = pl.BlockSpec((_SAMPLE_BB, nq, kw), lambda i: (i, 0, 0))
    return pl.pallas_call(
        _swa_sample_kernel,
        grid=(nb // _SAMPLE_BB,),
        in_specs=[q_spec, _resident((nq, 1)), kv_spec, kv_spec, kv_spec, kv_spec],
        out_specs=[q_spec, kv_spec, kv_spec],
        out_shape=[jax.ShapeDtypeStruct((nb, nq, kw), F32),
                   jax.ShapeDtypeStruct(kc.shape, F32), jax.ShapeDtypeStruct(kc.shape, F32)],
        compiler_params=_params("parallel"),
        name="swa_sample",
    )(qbd, sink_col, kc, vc, kn, vn)


def _pad_heads(w, heads, dim, dim_p, axis):
    shape = w.shape
    w = w.reshape(shape[:axis] + (heads, dim) + shape[axis + 1:])
    pad = [(0, 0)] * w.ndim
    pad[axis + 1] = (0, dim_p - dim)
    w = jnp.pad(w, pad)
    return w.reshape(shape[:axis] + (heads * dim_p,) + shape[axis + 1:])


def _rope_tables(pos):
    half = ROT_DIM // 2
    inv_freq = np.exp(-math.log(ROPE_THETA) * np.arange(0, ROT_DIM, 2, dtype=np.float64) / ROT_DIM)
    ang = pos.astype(np.float64)[:, None] * inv_freq[None, :]
    cos, sin = np.cos(ang), np.sin(ang)
    n = pos.shape[0]
    rest = HEAD_DIM - ROT_DIM
    c = np.concatenate([cos, cos, np.ones((n, rest))], axis=1)
    s1 = np.concatenate([-sin, np.zeros((n, HEAD_DIM - half))], axis=1)
    s2 = np.concatenate([np.zeros((n, half)), sin, np.zeros((n, rest))], axis=1)
    return tuple(np.tile(a, (1, HEAD_DIMP // HEAD_DIM)).astype(np.float32) for a in (c, s1, s2))


def _prep_weights(p):
    w = {}
    for name in ("ffn1_norm", "ffn2_norm", "mix_norm"):
        w[name] = p[name][:, None, :]
    qk = GLA_HEADS * GLA_DK
    vv = GLA_HEADS * GLA_DV
    a_in = p["a_w_in"][0]
    o = 0
    wq = _pad_heads(a_in[:, o:o + qk], GLA_HEADS, GLA_DK, GLA_DKP, 1); o += qk
    wk = _pad_heads(a_in[:, o:o + qk], GLA_HEADS, GLA_DK, GLA_DKP, 1); o += qk
    wv = _pad_heads(a_in[:, o:o + vv], GLA_HEADS, GLA_DV, GLA_DVP, 1); o += vv
    wr = _pad_heads(a_in[:, o:o + vv], GLA_HEADS, GLA_DV, GLA_DVP, 1); o += vv
    wg = jnp.pad(a_in[:, o:o + GLA_RANK], ((0, 0), (0, GLA_RANKP - GLA_RANK))); o += GLA_RANK
    wm = a_in[:, o:]
    w["a_in"] = jnp.concatenate([wq, wk, wv, wr, wg, wm], axis=1).astype(BF)
    gate = _pad_heads(p["a_w_gate"][0], GLA_HEADS, GLA_DK, GLA_DKP, 1)
    w["a_gate"] = jnp.pad(gate, ((0, GLA_RANKP - GLA_RANK), (0, 0))).astype(BF)
    w["a_bgate"] = _pad_heads(p["a_b_gate"][0][None, :], GLA_HEADS, GLA_DK, GLA_DKP, 1)
    w["a_gn"] = jnp.tile(jnp.pad(p["a_out_norm"][0], (0, GLA_DVP - GLA_DV)), GLA_HEADS)[None, :]
    a_out = p["a_w_out"][0]
    w["a_out"] = jnp.concatenate(
        [_pad_heads(a_out[:vv], GLA_HEADS, GLA_DV, GLA_DVP, 0), a_out[vv:]], axis=0).astype(BF)
    w["b_in"] = p["b_w_in"][0].astype(BF)
    w["b_out"] = p["b_w_out"][0].astype(BF)
    nkv = SWA_KV_HEADS * HEAD_DIM
    w_kv = p["w_kv"]
    w["kv"] = jnp.concatenate(
        [_pad_heads(w_kv[:, :nkv], SWA_KV_HEADS, HEAD_DIM, HEAD_DIMP, 1),
         _pad_heads(w_kv[:, nkv:], SWA_KV_HEADS, HEAD_DIM, HEAD_DIMP, 1)], axis=1).astype(BF)
    w["mem_t"] = p["mem_w_kv"].transpose(0, 2, 1).astype(BF)
    return w


def _compact_kv(a, batch, seq):
    return a.reshape(batch, seq, SWA_KV_HEADS, HEAD_DIMP)[..., :HEAD_DIM]


def kernel(x_prompt, x_sample, state_gla, cache_swa_k, cache_swa_v, cache_mem_k, cache_mem_v,
           mem_prompt, ffn1_norm, ffn1_w_gu, ffn1_w_down, mix_norm, ffn2_norm, ffn2_w_gu,
           ffn2_w_down, mem_norm, mem_w_kv, a_w_in, a_w_gate, a_b_gate, a_out_norm, a_w_out,
           kv_norm, w_kv, b_w_in, b_sinks, b_w_out, final_norm):
    p = dict(ffn1_norm=ffn1_norm, ffn1_w_gu=ffn1_w_gu, ffn1_w_down=ffn1_w_down,
             mix_norm=mix_norm, ffn2_norm=ffn2_norm, ffn2_w_gu=ffn2_w_gu,
             ffn2_w_down=ffn2_w_down, mem_w_kv=mem_w_kv, a_w_in=a_w_in, a_w_gate=a_w_gate,
             a_b_gate=a_b_gate, a_out_norm=a_out_norm, a_w_out=a_w_out, w_kv=w_kv,
             b_w_in=b_w_in, b_w_out=b_w_out)
    w = _prep_weights(p)
    batch, seq, _ = x_prompt.shape
    nb, t, _ = x_sample.shape
    assert nb == _DEC_BATCH and t == _DEC_SEQ and nb * t == TOKEN_TILE
    assert seq % TOKEN_TILE == 0 and seq % GLA_CHUNK == 0
    mp = batch * seq
    kw = SWA_KV_HEADS * HEAD_DIM
    sinks = b_sinks[0]

    mem_k_t, mem_v_t, wgu_a1, wd_a1 = _mem_kv(
        mem_prompt.reshape(batch * MEM_TOKENS, D_MODEL), mem_norm[:, None, :], w["mem_t"],
        [(ffn1_w_gu, 0, D_MODEL // batch), (ffn1_w_down, 0, FFN_DIM // batch)], batch=batch)
    state_t = state_gla.transpose(0, 2, 3, 4, 1)
    kc_t = cache_swa_k.transpose(0, 2, 3, 1).reshape(nb, kw, WINDOW)
    vc_t = cache_swa_v.transpose(0, 2, 3, 1).reshape(nb, kw, WINDOW)
    cmk_t = cache_mem_k.transpose(0, 1, 3, 4, 2).reshape(2, nb, MEM_Q, MEM_TOKENS)
    cmv_t = cache_mem_v.transpose(0, 1, 3, 4, 2).reshape(2, nb, MEM_Q, MEM_TOKENS)

    tabs = tuple(
        jnp.asarray(np.concatenate([a, np.tile(b, (nb, 1))], axis=0))
        for a, b in zip(_rope_tables(np.arange(seq)), _rope_tables(PAST_LEN + np.arange(t))))

    mem_mask = (np.arange(MEM_Q) // MEM_HEAD_DIM)[None, :] == np.arange(MEM_HEADS)[:, None]
    kv_mask = (np.arange(kw) // HEAD_DIM)[None, :] == np.arange(SWA_KV_HEADS)[:, None]

    def mem_attn_sample(mq, layer):
        q4 = mq[mp:].reshape(nb, 1, t, MEM_Q)
        qbd = jnp.where(mem_mask[None, :, None, :], q4, 0).reshape(nb, MEM_HEADS * t, MEM_Q)
        o = _mem_attn_sample(qbd, cmk_t, cmv_t, layer).reshape(nb, MEM_HEADS, t, MEM_Q)
        o = jnp.sum(jnp.where(mem_mask[None, :, None, :], o, 0.0), axis=1)
        return o.reshape(nb * t, MEM_Q).astype(BF)

    def lanes(a, width):
        return a[mp:].reshape(nb, t, GLA_HEADS, width).transpose(1, 2, 3, 0)

    x, q, k, la, v, r, mq = _layer_a_in(
        x_prompt.reshape(mp, D_MODEL), x_sample.reshape(nb * t, D_MODEL), w["ffn1_norm"],
        wgu_a1, wd_a1, w["mix_norm"], w["a_in"], w["a_gate"], w["a_bgate"], w["a_gn"])
    tok_p, st_p, wgu_a2, wd_a2 = _gla_prompt(
        q, k, la, v, r, [(ffn2_w_gu, 0, 32), (ffn2_w_down, 0, 128)], batch=batch, seq=seq)
    st_s, tok_s = _gla_sample(state_t, lanes(q, GLA_DKP), lanes(k, GLA_DKP), lanes(la, GLA_DKP),
                              lanes(v, GLA_DVP), lanes(r, GLA_DVP))
    tok_s = tok_s.transpose(3, 0, 1, 2).reshape(nb * t, GLA_HEADS * GLA_DVP).astype(BF)
    mo_s = mem_attn_sample(mq, 0)
    x, k_sh, v_sh, wgu_b1, wd_b1 = _layer_a_out(
        x, tok_p, tok_s, mq, mem_k_t, mem_v_t, mo_s, w["a_out"], w["ffn2_norm"], wgu_a2, wd_a2,
        kv_norm[None, :], w["kv"], tabs, [(ffn1_w_gu, 1, 32), (ffn1_w_down, 1, 128)], seq=seq)

    x, qs, mq, wgu_b2, wd_b2 = _layer_b_in(
        x, w["ffn1_norm"], wgu_b1, wd_b1, w["mix_norm"], w["b_in"], tabs,
        [(ffn2_w_gu, 1, 32), (ffn2_w_down, 1, 128)], n_prompt=mp // TOKEN_TILE, seq=seq)

    def new_rows(a):
        a = _compact_kv(a[mp:], nb, t).reshape(nb, t, kw).transpose(0, 2, 1)
        return jnp.pad(a, ((0, 0), (0, 0), (WINDOW - t, 0)))

    q5 = qs[:, mp:].reshape(SWA_Q_PAIRS, nb, t, 2, HEAD_DIM).transpose(1, 0, 3, 2, 4)
    q5 = q5.reshape(nb, SWA_KV_HEADS, SWA_GROUP, t, HEAD_DIM)
    qbd = jnp.where(kv_mask[None, :, None, None, :], jnp.tile(q5, (1, 1, 1, 1, SWA_KV_HEADS)), 0)
    qbd = qbd.reshape(nb, SWA_Q_HEADS * t, kw)
    o, k_s, v_s = _swa_sample(qbd, jnp.repeat(sinks, t)[:, None], kc_t, vc_t,
                              new_rows(k_sh), new_rows(v_sh))
    o = o.reshape(nb, SWA_KV_HEADS, SWA_GROUP, t, kw)
    o = jnp.where(kv_mask[None, :, None, None, :], o, 0.0)
    o = o.reshape(nb, SWA_KV_HEADS, SWA_GROUP, t, SWA_KV_HEADS, HEAD_DIM).sum(axis=4)
    tok_s = o.reshape(nb, SWA_Q_PAIRS, 2, t, HEAD_DIM).transpose(1, 0, 3, 2, 4)
    tok_s = tok_s.reshape(SWA_Q_PAIRS, nb * t, HEAD_DIMP).astype(BF)
    mo_s = mem_attn_sample(mq, 1)
    y_p, y_s = _layer_b_out(sinks, x, qs, k_sh, v_sh, mq, mem_k_t, mem_v_t, tok_s, mo_s,
                            w["b_out"], w["ffn2_norm"], wgu_b2, wd_b2, final_norm[None, :],
                            seq=seq)

    gla_prompt = st_p.transpose(0, 1, 3, 2)[None, :, :, :GLA_DK, :GLA_DV]
    gla_sample = st_s.transpose(0, 4, 1, 2, 3)
    def last_window(a):
        tiles_per_seq = seq // TOKEN_TILE
        a = a.reshape(-1, TOKEN_TILE, SWA_KV_HEADS * HEAD_DIMP)
        a = a[tiles_per_seq - 1:batch * tiles_per_seq:tiles_per_seq, TOKEN_TILE - WINDOW:]
        return a.reshape(batch, WINDOW, SWA_KV_HEADS, HEAD_DIMP)[..., :HEAD_DIM]

    swa_k_prompt = last_window(k_sh)
    swa_v_prompt = last_window(v_sh)
    swa_k_sample = k_s.reshape(nb, SWA_KV_HEADS, HEAD_DIM, WINDOW).transpose(0, 3, 1, 2)
    swa_v_sample = v_s.reshape(nb, SWA_KV_HEADS, HEAD_DIM, WINDOW).transpose(0, 3, 1, 2)
    mem_shape = (2, batch, MEM_HEADS, MEM_HEAD_DIM, MEM_TOKENS)
    mem_k_prompt = mem_k_t.reshape(mem_shape).transpose(0, 1, 4, 2, 3)
    mem_v_prompt = mem_v_t.reshape(mem_shape).transpose(0, 1, 4, 2, 3)
    return (y_p.reshape(batch, seq, D_MODEL), y_s.reshape(nb, t, D_MODEL), gla_prompt,
            gla_sample, swa_k_prompt, swa_v_prompt, swa_k_sample, swa_v_sample,
            mem_k_prompt, mem_v_prompt)
```

```python
import functools
import math

import jax
import jax.numpy as jnp
import numpy as np
from jax import lax
from jax.experimental import pallas as pl
from jax.experimental.pallas import tpu as pltpu

F32 = jnp.float32
BF = jnp.bfloat16

D_MODEL = 1024
FFN_DIM = 2816
EPS = 1e-6

GLA_HEADS = 4
GLA_DK = 96
GLA_DV = 192
GLA_DKP = 128
GLA_DVP = 256
GLA_RANK = 16
GLA_RANKP = 128
GLA_GATE_NORM = 16.0
GLA_CHUNK = 256
GLA_SEQS = 2

HEAD_DIM = 64
HEAD_DIMP = 128
SWA_Q_HEADS = 12
SWA_KV_HEADS = 3
SWA_GROUP = SWA_Q_HEADS // SWA_KV_HEADS
SWA_Q_PAIRS = SWA_Q_HEADS // 2
SWA_PAIRS_PER_KV = SWA_GROUP // 2
WINDOW = 128
ROT_DIM = 16
ROPE_THETA = 500000.0
PAST_LEN = 8192

MEM_TOKENS = 256
MEM_HEADS = 4
MEM_HEAD_DIM = 64
MEM_Q = MEM_HEADS * MEM_HEAD_DIM

FFN_TF = 256
FFN_CHUNKS = FFN_DIM // FFN_TF
TOKEN_TILE = 512

VMEM_LIMIT = 60 * 1024 * 1024


def _params(*sem):
    return pltpu.CompilerParams(dimension_semantics=sem, vmem_limit_bytes=VMEM_LIMIT)


def _resident(shape):
    nd = len(shape)
    return pl.BlockSpec(shape, lambda *_: (0,) * nd, pipeline_mode=pl.Buffered(1))


def _layer_block(shape, layer):
    nd = len(shape)
    return pl.BlockSpec((1,) + tuple(shape[1:]), lambda *_: (layer,) + (0,) * (nd - 1),
                        pipeline_mode=pl.Buffered(1))


def _rows(width):
    return pl.BlockSpec((TOKEN_TILE, width), lambda i: (i, 0))


def _prompt_rows(width, n_prompt):
    return pl.BlockSpec((TOKEN_TILE, width), lambda i: (jnp.minimum(i, n_prompt - 1), 0))


def _sample_rows(width):
    return pl.BlockSpec((TOKEN_TILE, width), lambda i: (0, 0), pipeline_mode=pl.Buffered(1))


def _rms(x, g):
    ms = jnp.mean(x * x, axis=-1, keepdims=True)
    return x * lax.rsqrt(ms + EPS) * g


def _silu(x):
    return x * (1.0 / (1.0 + jnp.exp(-x)))


def _dot(a, b):
    return jnp.dot(a, b, preferred_element_type=F32)


def _dot_nt(a, b):
    return lax.dot_general(a, b, (((1,), (1,)), ((), ())), preferred_element_type=F32)


def _dot_tn(a, b):
    return lax.dot_general(a, b, (((0,), (0,)), ((), ())), preferred_element_type=F32)


def _pick(is_prompt, p_ref, s_ref):
    return jnp.where(is_prompt, p_ref[...], s_ref[...])


def _cast_job(w, layer, rows, grid):
    _, r, c = w.shape
    assert r % rows == 0 and rows % 16 == 0
    nblk = r // rows
    total = math.prod(grid)
    assert nblk <= total
    steps_per_block = total // nblk

    def block(*idx):
        step = idx[0]
        for dim, i in zip(grid[1:], idx[1:]):
            step = step * dim + i
        return jnp.minimum(step // steps_per_block, nblk - 1)

    in_spec = pl.BlockSpec((1, rows, c), lambda *idx: (layer, block(*idx), 0))
    out_spec = pl.BlockSpec((1, rows, c), lambda *idx: (0, block(*idx), 0))
    return in_spec, out_spec, jax.ShapeDtypeStruct((1, r, c), BF)


def _run_cast_jobs(refs):
    n = len(refs) // 2
    for src, dst in zip(refs[:n], refs[n:]):
        dst[...] = src[...].astype(BF)


def _ffn_half(x, g_ref, wgu_ref, wd_ref):
    h = _rms(x, g_ref[0]).astype(BF)
    acc = jnp.zeros(x.shape, F32)
    for c in range(FFN_CHUNKS):
        lo, hi = c * FFN_TF, (c + 1) * FFN_TF
        gate = _dot(h, wgu_ref[0, :, lo:hi])
        up = _dot(h, wgu_ref[0, :, FFN_DIM + lo:FFN_DIM + hi])
        a = (_silu(gate) * up).astype(BF)
        acc = acc + _dot(a, wd_ref[0, lo:hi, :])
    return x + 0.5 * acc


def _rope(x, c, s1, s2):
    return (x * c + pltpu.roll(x, HEAD_DIMP - ROT_DIM // 2, 1) * s1
            + pltpu.roll(x, ROT_DIM // 2, 1) * s2)


_A_Q = 0
_A_K = _A_Q + GLA_HEADS * GLA_DKP
_A_V = _A_K + GLA_HEADS * GLA_DKP
_A_R = _A_V + GLA_HEADS * GLA_DVP
_A_G = _A_R + GLA_HEADS * GLA_DVP
_A_M = _A_G + GLA_RANKP
_A_END = _A_M + MEM_Q


def _layer_a_in_kernel(xp_ref, xs_ref, g1_ref, wgu_ref, wd_ref, gm_ref, w_ref, wg_ref, bg_ref,
                       gn_ref, x_ref, q_ref, k_ref, la_ref, v_ref, r_ref, mq_ref, *, n_prompt):
    is_prompt = pl.program_id(0) < n_prompt
    x = _ffn_half(_pick(is_prompt, xp_ref, xs_ref), g1_ref, wgu_ref, wd_ref)
    x_ref[...] = x
    h = _rms(x, gm_ref[0]).astype(BF)

    proj = _dot(h, w_ref[...])

    def mm(a, b):
        return proj[:, a:b]

    q_ref[...] = mm(_A_Q, _A_K) * (GLA_DK ** -0.5)
    k_ref[...] = mm(_A_K, _A_V)
    v_ref[...] = mm(_A_V, _A_R)
    r_ref[...] = _silu(mm(_A_R, _A_G)) * gn_ref[...]
    z = _dot(mm(_A_G, _A_M).astype(BF), wg_ref[...]) + bg_ref[...]
    la_ref[...] = (jnp.minimum(z, 0.0) - jnp.log1p(jnp.exp(-jnp.abs(z)))) * (1.0 / GLA_GATE_NORM)
    mq_ref[...] = (mm(_A_M, _A_END) * (MEM_HEAD_DIM ** -0.5)).astype(BF)


def _layer_a_in(xp, xs, g1, wgu, wd, gm, w, wg, bg, gn):
    n_prompt = xp.shape[0] // TOKEN_TILE
    m = xp.shape[0] + xs.shape[0]
    qk = GLA_HEADS * GLA_DKP
    vr = GLA_HEADS * GLA_DVP
    outs = ((D_MODEL, F32), (qk, F32), (qk, F32), (qk, F32), (vr, F32), (vr, F32), (MEM_Q, BF))
    return pl.pallas_call(
        functools.partial(_layer_a_in_kernel, n_prompt=n_prompt),
        grid=(m // TOKEN_TILE,),
        in_specs=[_prompt_rows(D_MODEL, n_prompt), _sample_rows(D_MODEL),
                  _layer_block(g1.shape, 0), _layer_block(wgu.shape, 0), _layer_block(wd.shape, 0),
                  _layer_block(gm.shape, 0), _resident(w.shape), _resident(wg.shape),
                  _resident(bg.shape), _resident(gn.shape)],
        out_specs=[_rows(n) for n, _ in outs],
        out_shape=[jax.ShapeDtypeStruct((m, n), dt) for n, dt in outs],
        compiler_params=_params("arbitrary"),
        name="layer_a_in",
    )(xp, xs, g1, wgu, wd, gm, w, wg, bg, gn)


def _build_mem_block_diag(mk_ref, mv_ref, kbd_scr, vbd_scr):
    shape = (MEM_Q, MEM_HEADS * MEM_TOKENS)
    rh = lax.broadcasted_iota(jnp.int32, shape, 0) // MEM_HEAD_DIM
    ch = lax.broadcasted_iota(jnp.int32, shape, 1) // MEM_TOKENS
    diag = rh == ch
    kbd_scr[...] = jnp.where(diag, jnp.concatenate([mk_ref[0, 0]] * MEM_HEADS, axis=1),
                             0.0).astype(BF)
    vbd_scr[...] = jnp.where(diag, jnp.concatenate([mv_ref[0, 0]] * MEM_HEADS, axis=1),
                             0.0).astype(BF)


def _mem_attn_tile(q, kbd, vbd):
    tq = q.shape[0]
    s = _dot(q, kbd)
    ps, inv = [], []
    for h in range(MEM_HEADS):
        sh = s[:, h * MEM_TOKENS:(h + 1) * MEM_TOKENS]
        e = jnp.exp(sh - jnp.max(sh, axis=1, keepdims=True))
        inv.append(1.0 / jnp.sum(e, axis=1, keepdims=True))
        ps.append(e.astype(BF))
    o = _dot_nt(jnp.concatenate(ps, axis=1), vbd)
    lane_h = lax.broadcasted_iota(jnp.int32, (tq, MEM_Q), 1) // MEM_HEAD_DIM
    scale = jnp.where(lane_h == 0, inv[0],
                      jnp.where(lane_h == 1, inv[1], jnp.where(lane_h == 2, inv[2], inv[3])))
    return o * scale


def _swa_tile(q_ref, kprev_ref, kcur_ref, vprev_ref, vcur_ref, sink_ref, first_of_seq):
    blk = WINDOW
    nsub = TOKEN_TILE // blk
    qi = lax.broadcasted_iota(jnp.int32, (blk, 2 * blk), 0)
    kj = lax.broadcasted_iota(jnp.int32, (blk, 2 * blk), 1)
    d = blk + qi - kj
    band = (d >= 0) & (d < WINDOW)
    bias_mid = jnp.where(band, 0.0, -jnp.inf)
    bias_first = jnp.where(band & (kj >= blk), 0.0, -jnp.inf)
    bias0 = jnp.where(first_of_seq, bias_first, bias_mid)
    outs = [[None] * SWA_Q_PAIRS for _ in range(nsub)]
    for kh in range(SWA_KV_HEADS):
        sl = slice(kh * HEAD_DIMP, (kh + 1) * HEAD_DIMP)
        kblk = [kprev_ref[:, sl]] + [kcur_ref[s * blk:(s + 1) * blk, sl] for s in range(nsub)]
        vblk = [vprev_ref[:, sl]] + [vcur_ref[s * blk:(s + 1) * blk, sl] for s in range(nsub)]
        k2 = [(b.astype(BF), pltpu.roll(b, HEAD_DIM, 1).astype(BF)) for b in kblk]
        v2 = [(b.astype(BF), pltpu.roll(b, HEAD_DIM, 1).astype(BF)) for b in vblk]
        for sub in range(nsub):
            bias = bias0 if sub == 0 else bias_mid
            for pj in range(SWA_PAIRS_PER_KV):
                pp = kh * SWA_PAIRS_PER_KV + pj
                q = q_ref[pp, sub * blk:(sub + 1) * blk, :]
                o = None
                for half in range(2):
                    kb = jnp.concatenate([k2[sub][half], k2[sub + 1][half]], axis=0)
                    vb = jnp.concatenate([v2[sub][half], v2[sub + 1][half]], axis=0)
                    s = _dot_nt(q, kb) + bias
                    sink = sink_ref[2 * pp + half]
                    m = jnp.maximum(jnp.max(s, axis=1, keepdims=True), sink)
                    e = jnp.exp(s - m)
                    l = jnp.sum(e, axis=1, keepdims=True) + jnp.exp(sink - m)
                    oh = _dot(e.astype(BF), vb) * (1.0 / l)
                    o = oh if o is None else o + oh
                outs[sub][pp] = o.astype(BF)
    return jnp.concatenate([jnp.concatenate(row, axis=1) for row in outs], axis=0)


def _layer_a_out_kernel(x_ref, tokp_ref, toks_ref, mq_ref, mk_ref, mv_ref, mos_ref, wo_ref,
                        g2_ref, wgu_ref, wd_ref, gkv_ref, wkv_ref, c_ref, s1_ref, s2_ref,
                        *refs, n_prompt, tiles_per_seq):
    n = (len(refs) - 5) // 2
    xo_ref, k_ref, v_ref = refs[n:n + 3]
    kbd_scr, vbd_scr = refs[-2:]
    _run_cast_jobs(refs[:n] + refs[n + 3:-2])
    i = pl.program_id(0)
    is_prompt = i < n_prompt

    @pl.when(i % tiles_per_seq == 0)
    def _():
        _build_mem_block_diag(mk_ref, mv_ref, kbd_scr, vbd_scr)

    nt = GLA_HEADS * GLA_DVP
    mo = _mem_attn_tile(mq_ref[...], kbd_scr[...], vbd_scr[...]).astype(BF)
    mo = jnp.where(is_prompt, mo, mos_ref[...])
    x = (x_ref[...] + _dot(_pick(is_prompt, tokp_ref, toks_ref), wo_ref[:nt, :])
         + _dot(mo, wo_ref[nt:, :]))
    x = _ffn_half(x, g2_ref, wgu_ref, wd_ref)
    xo_ref[...] = x
    h = _rms(x, gkv_ref[...]).astype(BF)
    c, s1, s2 = c_ref[...], s1_ref[...], s2_ref[...]
    kw = SWA_KV_HEADS * HEAD_DIMP
    kv = _dot(h, wkv_ref[...])
    for hh in range(SWA_KV_HEADS):
        sl = slice(hh * HEAD_DIMP, (hh + 1) * HEAD_DIMP)
        k_ref[:, sl] = _rope(kv[:, sl], c, s1, s2)
    v_ref[...] = kv[:, kw:]


def _tab_spec(n_prompt, blocks_per_seq):
    return pl.BlockSpec((TOKEN_TILE, HEAD_DIMP),
                        lambda i: (jnp.where(i < n_prompt, i % blocks_per_seq, blocks_per_seq), 0))


def _mem_kv_spec(layer, n_prompt, tiles_per_seq):
    return pl.BlockSpec(
        (1, 1, MEM_Q, MEM_TOKENS),
        lambda i: (layer, jnp.minimum(i, n_prompt - 1) // tiles_per_seq, 0, 0))


def _mem_scratch():
    return pltpu.VMEM((MEM_Q, MEM_HEADS * MEM_TOKENS), BF)


def _layer_a_out(x, tokp, toks, mq, mk_t, mv_t, mos, wo, g2, wgu, wd, gkv, wkv, tabs, casts,
                 *, seq):
    m = x.shape[0]
    grid = (m // TOKEN_TILE,)
    n_prompt = tokp.shape[0] // TOKEN_TILE
    tiles_per_seq = seq // TOKEN_TILE
    kw = SWA_KV_HEADS * HEAD_DIMP
    nt = GLA_HEADS * GLA_DVP
    tab = _tab_spec(n_prompt, tiles_per_seq)
    mem_spec = _mem_kv_spec(0, n_prompt, tiles_per_seq)
    jobs = [_cast_job(cw, layer, rows, grid) for cw, layer, rows in casts]
    return pl.pallas_call(
        functools.partial(_layer_a_out_kernel, n_prompt=n_prompt, tiles_per_seq=tiles_per_seq),
        grid=grid,
        in_specs=[_rows(D_MODEL), _prompt_rows(nt, n_prompt), _sample_rows(nt),
                  _rows(MEM_Q), mem_spec, mem_spec, _sample_rows(MEM_Q), _resident(wo.shape),
                  _layer_block(g2.shape, 0), _layer_block(wgu.shape, 0), _layer_block(wd.shape, 0),
                  _resident(gkv.shape), _resident(wkv.shape), tab, tab, tab]
                 + [j[0] for j in jobs],
        out_specs=[_rows(D_MODEL), _rows(kw), _rows(kw)] + [j[1] for j in jobs],
        out_shape=[jax.ShapeDtypeStruct((m, D_MODEL), F32),
                   jax.ShapeDtypeStruct((m, kw), F32), jax.ShapeDtypeStruct((m, kw), F32)]
                  + [j[2] for j in jobs],
        scratch_shapes=[_mem_scratch(), _mem_scratch()],
        compiler_params=_params("arbitrary"),
        name="layer_a_out",
    )(x, tokp, toks, mq, mk_t, mv_t, mos, wo, g2, wgu, wd, gkv, wkv, *tabs,
      *[c[0] for c in casts])


def _layer_b_in_kernel(x_ref, g1_ref, wgu_ref, wd_ref, gm_ref, w_ref, c_ref, s1_ref, s2_ref,
                       *refs):
    n = (len(refs) - 3) // 2
    xo_ref, q_ref, mq_ref = refs[n:n + 3]
    _run_cast_jobs(refs[:n] + refs[n + 3:])
    x = _ffn_half(x_ref[...], g1_ref, wgu_ref, wd_ref)
    xo_ref[...] = x
    h = _rms(x, gm_ref[0]).astype(BF)
    c, s1, s2 = c_ref[...], s1_ref[...], s2_ref[...]
    nq = SWA_Q_HEADS * HEAD_DIM
    qm = _dot(h, w_ref[...])
    for pp in range(SWA_Q_PAIRS):
        q = qm[:, pp * HEAD_DIMP:(pp + 1) * HEAD_DIMP]
        q_ref[pp] = (_rope(q, c, s1, s2) * (HEAD_DIM ** -0.5)).astype(BF)
    mq_ref[...] = (qm[:, nq:] * (MEM_HEAD_DIM ** -0.5)).astype(BF)


def _layer_b_in(x, g1, wgu, wd, gm, w, tabs, casts, *, n_prompt, seq):
    m = x.shape[0]
    grid = (m // TOKEN_TILE,)
    tab = _tab_spec(n_prompt, seq // TOKEN_TILE)
    q_spec = pl.BlockSpec((SWA_Q_PAIRS, TOKEN_TILE, HEAD_DIMP), lambda i: (0, i, 0))
    jobs = [_cast_job(cw, layer, rows, grid) for cw, layer, rows in casts]
    return pl.pallas_call(
        _layer_b_in_kernel,
        grid=grid,
        in_specs=[_rows(D_MODEL), _layer_block(g1.shape, 1), _layer_block(wgu.shape, 0),
                  _layer_block(wd.shape, 0), _layer_block(gm.shape, 1), _resident(w.shape),
                  tab, tab, tab] + [j[0] for j in jobs],
        out_specs=[_rows(D_MODEL), q_spec, _rows(MEM_Q)] + [j[1] for j in jobs],
        out_shape=[jax.ShapeDtypeStruct((m, D_MODEL), F32),
                   jax.ShapeDtypeStruct((SWA_Q_PAIRS, m, HEAD_DIMP), BF),
                   jax.ShapeDtypeStruct((m, MEM_Q), BF)] + [j[2] for j in jobs],
        compiler_params=_params("arbitrary"),
        name="layer_b_in",
    )(x, g1, wgu, wd, gm, w, *tabs, *[c[0] for c in casts])


def _layer_b_out_kernel(sink_ref, x_ref, q_ref, kprev_ref, kcur_ref, vprev_ref, vcur_ref,
                        mq_ref, mk_ref, mv_ref, toks_ref, mos_ref, wo_ref, g2_ref, wgu_ref,
                        wd_ref, gf_ref, yp_ref, ys_ref, kbd_scr, vbd_scr,
                        *, n_prompt, tiles_per_seq):
    i = pl.program_id(0)
    is_prompt = i < n_prompt
    first_of_seq = i % tiles_per_seq == 0

    @pl.when(first_of_seq)
    def _():
        _build_mem_block_diag(mk_ref, mv_ref, kbd_scr, vbd_scr)

    nt = SWA_Q_HEADS * HEAD_DIM
    mo = _mem_attn_tile(mq_ref[...], kbd_scr[...], vbd_scr[...]).astype(BF)
    mo = jnp.where(is_prompt, mo, mos_ref[...])
    tok = _swa_tile(q_ref, kprev_ref, kcur_ref, vprev_ref, vcur_ref, sink_ref, first_of_seq)
    toks = jnp.concatenate([toks_ref[pp] for pp in range(SWA_Q_PAIRS)], axis=1)
    tok = jnp.where(is_prompt, tok, toks)
    x = x_ref[...] + _dot(tok, wo_ref[:nt, :]) + _dot(mo, wo_ref[nt:, :])
    y = _rms(_ffn_half(x, g2_ref, wgu_ref, wd_ref), gf_ref[...])

    @pl.when(is_prompt)
    def _():
        yp_ref[...] = y

    @pl.when(jnp.logical_not(is_prompt))
    def _():
        ys_ref[...] = y


def _layer_b_out(sinks, x, qs, k_sh, v_sh, mq, mk_t, mv_t, toks, mos, wo, g2, wgu, wd, gf, *, seq):
    m = x.shape[0]
    n_tiles = m // TOKEN_TILE
    n_prompt = n_tiles - 1
    tiles_per_seq = seq // TOKEN_TILE
    kw = SWA_KV_HEADS * HEAD_DIMP
    blocks_per_tile = TOKEN_TILE // WINDOW
    q_spec = pl.BlockSpec((SWA_Q_PAIRS, TOKEN_TILE, HEAD_DIMP), lambda i: (0, i, 0))
    prev_spec = pl.BlockSpec((WINDOW, kw), lambda i: (jnp.maximum(i * blocks_per_tile - 1, 0), 0))
    toks_spec = pl.BlockSpec((SWA_Q_PAIRS, TOKEN_TILE, HEAD_DIMP), lambda i: (0, 0, 0),
                             pipeline_mode=pl.Buffered(1))
    mem_spec = _mem_kv_spec(1, n_prompt, tiles_per_seq)
    return pl.pallas_call(
        functools.partial(_layer_b_out_kernel, n_prompt=n_prompt, tiles_per_seq=tiles_per_seq),
        grid=(n_tiles,),
        in_specs=[pl.BlockSpec(memory_space=pltpu.SMEM), _rows(D_MODEL), q_spec,
                  prev_spec, _rows(kw), prev_spec, _rows(kw), _rows(MEM_Q), mem_spec, mem_spec,
                  toks_spec, _sample_rows(MEM_Q), _resident(wo.shape), _layer_block(g2.shape, 1),
                  _layer_block(wgu.shape, 0), _layer_block(wd.shape, 0), _resident(gf.shape)],
        out_specs=[_prompt_rows(D_MODEL, n_prompt),
                   pl.BlockSpec((TOKEN_TILE, D_MODEL), lambda i: (0, 0))],
        out_shape=[jax.ShapeDtypeStruct((n_prompt * TOKEN_TILE, D_MODEL), F32),
                   jax.ShapeDtypeStruct((TOKEN_TILE, D_MODEL), F32)],
        scratch_shapes=[_mem_scratch(), _mem_scratch()],
        compiler_params=_params("arbitrary"),
        name="layer_b_out",
    )(sinks, x, qs, k_sh, k_sh, v_sh, v_sh, mq, mk_t, mv_t, toks, mos, wo, g2, wgu, wd, gf)


def _mem_kv_kernel(x_ref, g_ref, wt_ref, *refs):
    n = (len(refs) - 2) // 2
    k_ref, v_ref = refs[n:n + 2]
    _run_cast_jobs(refs[:n] + refs[n + 2:])
    x = x_ref[...]
    xn = x * lax.rsqrt(jnp.mean(x * x, axis=-1, keepdims=True) + EPS)
    for l in range(2):
        h = (xn * g_ref[l]).astype(BF)
        kvt = _dot_nt(wt_ref[l], h)
        k_ref[l, 0] = kvt[:MEM_Q, :]
        v_ref[l, 0] = kvt[MEM_Q:, :]


def _mem_kv(mem, g, wt, casts, *, batch):
    out_spec = pl.BlockSpec((2, 1, MEM_Q, MEM_TOKENS), lambda b: (0, b, 0, 0))
    jobs = [_cast_job(cw, layer, rows, (batch,)) for cw, layer, rows in casts]
    return pl.pallas_call(
        _mem_kv_kernel,
        grid=(batch,),
        in_specs=[pl.BlockSpec((MEM_TOKENS, D_MODEL), lambda b: (b, 0)), _resident(g.shape),
                  _resident(wt.shape)] + [j[0] for j in jobs],
        out_specs=[out_spec, out_spec] + [j[1] for j in jobs],
        out_shape=[jax.ShapeDtypeStruct((2, batch, MEM_Q, MEM_TOKENS), F32)] * 2
                  + [j[2] for j in jobs],
        compiler_params=_params("arbitrary"),
        name="mem_kv",
    )(mem, g, wt, *[c[0] for c in casts])


def _gla_prompt_kernel(*refs):
    ns = GLA_SEQS
    seq_in = [refs[5 * i:5 * i + 5] for i in range(ns)]
    rest = refs[5 * ns:-1]
    s_scr = refs[-1]
    n = (len(rest) - 2) // 2
    tok_ref, st_ref = rest[n:n + 2]
    _run_cast_jobs(rest[:n] + rest[n + 2:])
    c = pl.program_id(1)
    C = GLA_CHUNK

    @pl.when(c == 0)
    def _():
        s_scr[...] = jnp.zeros(s_scr.shape, F32)

    row = lax.broadcasted_iota(jnp.int32, (C, C), 0)
    col = lax.broadcasted_iota(jnp.int32, (C, C), 1)
    causal = row >= col
    ltri = jnp.where(causal, 1.0, 0.0).astype(BF)
    _gla_chunks([(seq_in[i], tok_ref.at[i], s_scr.at[i], h)
                 for i in range(ns) for h in range(GLA_HEADS)], causal, ltri)

    @pl.when(c == pl.num_programs(1) - 1)
    def _():
        st_ref[...] = s_scr[...]


def _gla_chunks(chains, causal, ltri):
    C = GLA_CHUNK

    def sk(h):
        return slice(h * GLA_DKP, (h + 1) * GLA_DKP)

    def sv(h):
        return slice(h * GLA_DVP, (h + 1) * GLA_DVP)

    bs = []
    for (q_ref, k_ref, la_ref, v_ref, r_ref), _, _, h in chains:
        la = la_ref[:, sk(h)]
        hi = la.astype(BF)
        lo = (la - hi.astype(F32)).astype(BF)
        bb = _dot(ltri, jnp.concatenate([hi, lo], axis=1))
        bs.append(bb[:, :GLA_DKP] + bb[:, GLA_DKP:])
    ops = []
    for ((q_ref, k_ref, la_ref, v_ref, r_ref), _, _, h), b in zip(chains, bs):
        b_mid = b[C // 2 - 1:C // 2, :]
        b_last = b[C - 1:C, :]
        qe = q_ref[:, sk(h)] * jnp.exp(b - b_mid)
        ke = k_ref[:, sk(h)] * jnp.exp(b_mid - b)
        qb = (qe * jnp.exp(b_mid)).astype(BF)
        kd = (ke * jnp.exp(b_last - b_mid)).astype(BF)
        ops.append((qe.astype(BF), ke.astype(BF), qb, kd, jnp.exp(b_last)))
    As = [jnp.where(causal, _dot_nt(qe, ke), 0.0).astype(BF) for qe, ke, _, _, _ in ops]
    outs = []
    for ((q_ref, k_ref, la_ref, v_ref, r_ref), _, s_ref, h), a, (_, _, qb, kd, decay) in zip(
            chains, As, ops):
        v = v_ref[:, sv(h)].astype(BF)
        st = s_ref[h]
        outs.append(_dot(a, v) + _dot_nt(qb, st.astype(BF)))
        s_ref[h] = st * decay + _dot_tn(v, kd)
    for ((q_ref, k_ref, la_ref, v_ref, r_ref), tok_ref, _, h), o in zip(chains, outs):
        ms = jnp.sum(o * o, axis=1, keepdims=True) * (1.0 / GLA_DV)
        tok_ref[:, sv(h)] = (o * lax.rsqrt(ms + EPS) * r_ref[:, sv(h)]).astype(BF)


def _gla_prompt(q, k, la, v, r, casts, *, batch, seq):
    ns = GLA_SEQS
    m = batch * seq
    nc = seq // GLA_CHUNK
    qk = GLA_HEADS * GLA_DKP
    vr = GLA_HEADS * GLA_DVP
    grid = (batch // ns, nc)

    def tok_map(i):
        return lambda b, c: ((b * ns + i) * nc + c, 0)

    seq_specs = []
    for i in range(ns):
        seq_specs += [pl.BlockSpec((GLA_CHUNK, qk), tok_map(i))] * 3
        seq_specs += [pl.BlockSpec((GLA_CHUNK, vr), tok_map(i))] * 2
    jobs = [_cast_job(cw, layer, rows, grid) for cw, layer, rows in casts]
    res = pl.pallas_call(
        _gla_prompt_kernel,
        grid=grid,
        in_specs=seq_specs + [j[0] for j in jobs],
        out_specs=[pl.BlockSpec((ns, GLA_CHUNK, vr), lambda b, c: (b, c, 0)),
                   pl.BlockSpec((ns, GLA_HEADS, GLA_DVP, GLA_DKP), lambda b, c: (b, 0, 0, 0))]
                  + [j[1] for j in jobs],
        out_shape=[jax.ShapeDtypeStruct((batch, seq, vr), BF),
                   jax.ShapeDtypeStruct((batch, GLA_HEADS, GLA_DVP, GLA_DKP), F32)]
                  + [j[2] for j in jobs],
        scratch_shapes=[pltpu.VMEM((ns, GLA_HEADS, GLA_DVP, GLA_DKP), F32)],
        compiler_params=_params("arbitrary", "arbitrary"),
        name="gla_prompt",
    )(*([q, k, la, v, r] * ns), *[c[0] for c in casts])
    return (res[0].reshape(m, vr),) + tuple(res[1:])


_SAMPLE_BB = 8
_DEC_SEQ = 4
_DEC_BATCH = 128
_GLA_DK_BLK = 32
_NEW_ROWS = 16


def _gla_sample_kernel(s_ref, q_ref, k_ref, la_ref, v_ref, r_ref, so_ref, tok_ref, o_scr):
    j = pl.program_id(1)

    @pl.when(j == 0)
    def _():
        o_scr[...] = jnp.zeros(o_scr.shape, F32)

    def body(dk, carry):
        s = s_ref[0, 0, dk]
        for t in range(_DEC_SEQ):
            a = jnp.exp(la_ref[t, 0, pl.ds(dk, 1), :])
            s = a * s + k_ref[t, 0, pl.ds(dk, 1), :] * v_ref[t, 0, :GLA_DV, :]
            o_scr[t] = o_scr[t] + q_ref[t, 0, pl.ds(dk, 1), :] * s
        so_ref[0, 0, dk] = s
        return carry

    lax.fori_loop(0, _GLA_DK_BLK, body, 0, unroll=4)

    @pl.when(j == pl.num_programs(1) - 1)
    def _():
        tok_ref[...] = jnp.zeros(tok_ref.shape, F32)
        for t in range(_DEC_SEQ):
            o = o_scr[t]
            ms = jnp.sum(o * o, axis=0, keepdims=True) * (1.0 / GLA_DV)
            tok_ref[t, 0, :GLA_DV, :] = o * lax.rsqrt(ms + EPS) * r_ref[t, 0, :GLA_DV, :]


def _gla_sample(state, q, k, la, v, r):
    qk_spec = pl.BlockSpec((_DEC_SEQ, 1, _GLA_DK_BLK, _DEC_BATCH), lambda h, j: (0, h, j, 0))
    vr_spec = pl.BlockSpec((_DEC_SEQ, 1, GLA_DVP, _DEC_BATCH), lambda h, j: (0, h, 0, 0))
    s_spec = pl.BlockSpec((1, 1, _GLA_DK_BLK, GLA_DV, _DEC_BATCH), lambda h, j: (0, h, j, 0, 0))
    return pl.pallas_call(
        _gla_sample_kernel,
        grid=(GLA_HEADS, GLA_DK // _GLA_DK_BLK),
        in_specs=[s_spec, qk_spec, qk_spec, qk_spec, vr_spec, vr_spec],
        out_specs=[s_spec, vr_spec],
        out_shape=[jax.ShapeDtypeStruct(state.shape, F32),
                   jax.ShapeDtypeStruct(v.shape, F32)],
        scratch_shapes=[pltpu.VMEM((_DEC_SEQ, GLA_DV, _DEC_BATCH), F32)],
        compiler_params=_params("parallel", "arbitrary"),
        name="gla_sample",
    )(state, q, k, la, v, r)


def _mem_attn_sample_kernel(q_ref, mk_ref, mv_ref, o_ref):
    scores = [_dot(q_ref[bi], mk_ref[0, bi].astype(BF)) for bi in range(_SAMPLE_BB)]
    probs = []
    for s in scores:
        e = jnp.exp(s - jnp.max(s, axis=1, keepdims=True))
        probs.append((e * (1.0 / jnp.sum(e, axis=1, keepdims=True))).astype(BF))
    for bi, p in enumerate(probs):
        o_ref[bi] = _dot_nt(p, mv_ref[0, bi].astype(BF))


def _mem_attn_sample(qbd, mk_t, mv_t, layer):
    nb = qbd.shape[0]
    nr = MEM_HEADS * _DEC_SEQ
    kv_spec = pl.BlockSpec((1, _SAMPLE_BB, MEM_Q, MEM_TOKENS), lambda i: (layer, i, 0, 0))
    q_spec = pl.BlockSpec((_SAMPLE_BB, nr, MEM_Q), lambda i: (i, 0, 0))
    return pl.pallas_call(
        _mem_attn_sample_kernel,
        grid=(nb // _SAMPLE_BB,),
        in_specs=[q_spec, kv_spec, kv_spec],
        out_specs=q_spec,
        out_shape=jax.ShapeDtypeStruct((nb, nr, MEM_Q), F32),
        compiler_params=_params("parallel"),
        name="mem_attn_sample",
    )(qbd, mk_t, mv_t)


def _swa_sample_kernel(q_ref, sink_ref, kc_ref, vc_ref, kn_ref, vn_ref, o_ref, ko_ref, vo_ref):
    nq = SWA_Q_HEADS * _DEC_SEQ
    t = lax.broadcasted_iota(jnp.int32, (nq, WINDOW), 0) % _DEC_SEQ
    pos = lax.broadcasted_iota(jnp.int32, (nq, WINDOW), 1)
    bias_c = jnp.where(pos > t, 0.0, -jnp.inf)
    tn = lax.broadcasted_iota(jnp.int32, (nq, _NEW_ROWS), 0) % _DEC_SEQ
    new = lax.broadcasted_iota(jnp.int32, (nq, _NEW_ROWS), 1)
    bias_n = jnp.where(new <= tn, 0.0, -jnp.inf)
    sink = sink_ref[...]
    seqs = range(_SAMPLE_BB)
    scores = [(_dot(q_ref[bi], kc_ref[bi].astype(BF)) + bias_c,
               _dot_nt(q_ref[bi], kn_ref[bi].astype(BF)) + bias_n) for bi in seqs]
    probs = []
    for sc, sn in scores:
        m = jnp.maximum(jnp.maximum(jnp.max(sc, axis=1, keepdims=True),
                                    jnp.max(sn, axis=1, keepdims=True)), sink)
        ec = jnp.exp(sc - m)
        en = jnp.exp(sn - m)
        l = (jnp.sum(ec, axis=1, keepdims=True) + jnp.sum(en, axis=1, keepdims=True)
             + jnp.exp(sink - m))
        inv = 1.0 / l
        probs.append(((ec * inv).astype(BF), (en * inv).astype(BF)))
    for bi, (pc, pn) in zip(seqs, probs):
        o_ref[bi] = (_dot_nt(pc, vc_ref[bi].astype(BF)) + _dot(pn, vn_ref[bi].astype(BF)))
    for bi in seqs:
        ko_ref[bi] = pltpu.roll(kc_ref[bi], WINDOW - _DEC_SEQ, 1)
        vo_ref[bi] = pltpu.roll(vc_ref[bi], WINDOW - _DEC_SEQ, 1)


def _swa_sample(qbd, sink_col, kc, vc, kn, vn):
    nb = qbd.shape[0]
    kw = SWA_KV_HEADS * HEAD_DIM
    nq = SWA_Q_HEADS * _DEC_SEQ
    kv_spec = pl.BlockSpec((_SAMPLE_BB, kw, WINDOW), lambda i: (i, 0, 0))
    new_spec = pl.BlockSpec((_SAMPLE_BB, _NEW_ROWS, kw), lambda i: (i, 0, 0))
    q_spec = pl.BlockSpec((_SAMPLE_BB, nq, kw), lambda i: (i, 0, 0))
    return pl.pallas_call(
        _swa_sample_kernel,
        grid=(nb // _SAMPLE_BB,),
        in_specs=[q_spec, _resident((nq, 1)), kv_spec, kv_spec, new_spec, new_spec],
        out_specs=[q_spec, kv_spec, kv_spec],
        out_shape=[jax.ShapeDtypeStruct((nb, nq, kw), F32),
                   jax.ShapeDtypeStruct(kc.shape, F32), jax.ShapeDtypeStruct(kc.shape, F32)],
        compiler_params=_params("parallel"),
        name="swa_sample",
    )(qbd, sink_col, kc, vc, kn, vn)


def _pad_heads(w, heads, dim, dim_p, axis):
    shape = w.shape
    w = w.reshape(shape[:axis] + (heads, dim) + shape[axis + 1:])
    pad = [(0, 0)] * w.ndim
    pad[axis + 1] = (0, dim_p - dim)
    w = jnp.pad(w, pad)
    return w.reshape(shape[:axis] + (heads * dim_p,) + shape[axis + 1:])


def _rope_tables(pos):
    half = ROT_DIM // 2
    inv_freq = np.exp(-math.log(ROPE_THETA) * np.arange(0, ROT_DIM, 2, dtype=np.float64) / ROT_DIM)
    ang = pos.astype(np.float64)[:, None] * inv_freq[None, :]
    cos, sin = np.cos(ang), np.sin(ang)
    n = pos.shape[0]
    rest = HEAD_DIM - ROT_DIM
    c = np.concatenate([cos, cos, np.ones((n, rest))], axis=1)
    s1 = np.concatenate([-sin, np.zeros((n, HEAD_DIM - half))], axis=1)
    s2 = np.concatenate([np.zeros((n, half)), sin, np.zeros((n, rest))], axis=1)
    return tuple(np.tile(a, (1, HEAD_DIMP // HEAD_DIM)).astype(np.float32) for a in (c, s1, s2))


def _prep_weights(p):
    w = {}
    for name in ("ffn1_norm", "ffn2_norm", "mix_norm"):
        w[name] = p[name][:, None, :]
    qk = GLA_HEADS * GLA_DK
    vv = GLA_HEADS * GLA_DV
    a_in = p["a_w_in"][0]
    o = 0
    wq = _pad_heads(a_in[:, o:o + qk], GLA_HEADS, GLA_DK, GLA_DKP, 1); o += qk
    wk = _pad_heads(a_in[:, o:o + qk], GLA_HEADS, GLA_DK, GLA_DKP, 1); o += qk
    wv = _pad_heads(a_in[:, o:o + vv], GLA_HEADS, GLA_DV, GLA_DVP, 1); o += vv
    wr = _pad_heads(a_in[:, o:o + vv], GLA_HEADS, GLA_DV, GLA_DVP, 1); o += vv
    wg = jnp.pad(a_in[:, o:o + GLA_RANK], ((0, 0), (0, GLA_RANKP - GLA_RANK))); o += GLA_RANK
    wm = a_in[:, o:]
    w["a_in"] = jnp.concatenate([wq, wk, wv, wr, wg, wm], axis=1).astype(BF)
    gate = _pad_heads(p["a_w_gate"][0], GLA_HEADS, GLA_DK, GLA_DKP, 1)
    w["a_gate"] = jnp.pad(gate, ((0, GLA_RANKP - GLA_RANK), (0, 0))).astype(BF)
    w["a_bgate"] = _pad_heads(p["a_b_gate"][0][None, :], GLA_HEADS, GLA_DK, GLA_DKP, 1)
    w["a_gn"] = jnp.tile(jnp.pad(p["a_out_norm"][0], (0, GLA_DVP - GLA_DV)), GLA_HEADS)[None, :]
    a_out = p["a_w_out"][0]
    w["a_out"] = jnp.concatenate(
        [_pad_heads(a_out[:vv], GLA_HEADS, GLA_DV, GLA_DVP, 0), a_out[vv:]], axis=0).astype(BF)
    w["b_in"] = p["b_w_in"][0].astype(BF)
    w["b_out"] = p["b_w_out"][0].astype(BF)
    nkv = SWA_KV_HEADS * HEAD_DIM
    w_kv = p["w_kv"]
    w["kv"] = jnp.concatenate(
        [_pad_heads(w_kv[:, :nkv], SWA_KV_HEADS, HEAD_DIM, HEAD_DIMP, 1),
         _pad_heads(w_kv[:, nkv:], SWA_KV_HEADS, HEAD_DIM, HEAD_DIMP, 1)], axis=1).astype(BF)
    w["mem_t"] = p["mem_w_kv"].transpose(0, 2, 1).astype(BF)
    return w


def _compact_kv(a, batch, seq):
    return a.reshape(batch, seq, SWA_KV_HEADS, HEAD_DIMP)[..., :HEAD_DIM]


def kernel(x_prompt, x_sample, state_gla, cache_swa_k, cache_swa_v, cache_mem_k, cache_mem_v,
           mem_prompt, ffn1_norm, ffn1_w_gu, ffn1_w_down, mix_norm, ffn2_norm, ffn2_w_gu,
           ffn2_w_down, mem_norm, mem_w_kv, a_w_in, a_w_gate, a_b_gate, a_out_norm, a_w_out,
           kv_norm, w_kv, b_w_in, b_sinks, b_w_out, final_norm):
    p = dict(ffn1_norm=ffn1_norm, ffn1_w_gu=ffn1_w_gu, ffn1_w_down=ffn1_w_down,
             mix_norm=mix_norm, ffn2_norm=ffn2_norm, ffn2_w_gu=ffn2_w_gu,
             ffn2_w_down=ffn2_w_down, mem_w_kv=mem_w_kv, a_w_in=a_w_in, a_w_gate=a_w_gate,
             a_b_gate=a_b_gate, a_out_norm=a_out_norm, a_w_out=a_w_out, w_kv=w_kv,
             b_w_in=b_w_in, b_w_out=b_w_out)
    w = _prep_weights(p)
    batch, seq, _ = x_prompt.shape
    nb, t, _ = x_sample.shape
    assert nb == _DEC_BATCH and t == _DEC_SEQ and nb * t == TOKEN_TILE
    assert seq % TOKEN_TILE == 0 and seq % GLA_CHUNK == 0
    mp = batch * seq
    kw = SWA_KV_HEADS * HEAD_DIM
    sinks = b_sinks[0]

    mem_k_t, mem_v_t, wgu_a1, wd_a1 = _mem_kv(
        mem_prompt.reshape(batch * MEM_TOKENS, D_MODEL), mem_norm[:, None, :], w["mem_t"],
        [(ffn1_w_gu, 0, D_MODEL // batch), (ffn1_w_down, 0, FFN_DIM // batch)], batch=batch)
    state_t = state_gla.transpose(0, 2, 3, 4, 1)
    kc_t = cache_swa_k.transpose(0, 2, 3, 1).reshape(nb, kw, WINDOW)
    vc_t = cache_swa_v.transpose(0, 2, 3, 1).reshape(nb, kw, WINDOW)
    cmk_t = cache_mem_k.transpose(0, 1, 3, 4, 2).reshape(2, nb, MEM_Q, MEM_TOKENS)
    cmv_t = cache_mem_v.transpose(0, 1, 3, 4, 2).reshape(2, nb, MEM_Q, MEM_TOKENS)

    tabs = tuple(
        jnp.asarray(np.concatenate([a, np.tile(b, (nb, 1))], axis=0))
        for a, b in zip(_rope_tables(np.arange(seq)), _rope_tables(PAST_LEN + np.arange(t))))

    mem_mask = (np.arange(MEM_Q) // MEM_HEAD_DIM)[None, :] == np.arange(MEM_HEADS)[:, None]
    kv_mask = (np.arange(kw) // HEAD_DIM)[None, :] == np.arange(SWA_KV_HEADS)[:, None]

    def mem_attn_sample(mq, layer):
        q4 = mq[mp:].reshape(nb, 1, t, MEM_Q)
        qbd = jnp.where(mem_mask[None, :, None, :], q4, 0).reshape(nb, MEM_HEADS * t, MEM_Q)
        o = _mem_attn_sample(qbd, cmk_t, cmv_t, layer).reshape(nb, MEM_HEADS, t, MEM_Q)
        o = jnp.sum(jnp.where(mem_mask[None, :, None, :], o, 0.0), axis=1)
        return o.reshape(nb * t, MEM_Q).astype(BF)

    def lanes(a, width):
        return a[mp:].reshape(nb, t, GLA_HEADS, width).transpose(1, 2, 3, 0)

    x, q, k, la, v, r, mq = _layer_a_in(
        x_prompt.reshape(mp, D_MODEL), x_sample.reshape(nb * t, D_MODEL), w["ffn1_norm"],
        wgu_a1, wd_a1, w["mix_norm"], w["a_in"], w["a_gate"], w["a_bgate"], w["a_gn"])
    tok_p, st_p, wgu_a2, wd_a2 = _gla_prompt(
        q, k, la, v, r, [(ffn2_w_gu, 0, 32), (ffn2_w_down, 0, 128)], batch=batch, seq=seq)
    st_s, tok_s = _gla_sample(state_t, lanes(q, GLA_DKP), lanes(k, GLA_DKP), lanes(la, GLA_DKP),
                              lanes(v, GLA_DVP), lanes(r, GLA_DVP))
    tok_s = tok_s.transpose(3, 0, 1, 2).reshape(nb * t, GLA_HEADS * GLA_DVP).astype(BF)
    mo_s = mem_attn_sample(mq, 0)
    x, k_sh, v_sh, wgu_b1, wd_b1 = _layer_a_out(
        x, tok_p, tok_s, mq, mem_k_t, mem_v_t, mo_s, w["a_out"], w["ffn2_norm"], wgu_a2, wd_a2,
        kv_norm[None, :], w["kv"], tabs, [(ffn1_w_gu, 1, 32), (ffn1_w_down, 1, 128)], seq=seq)

    x, qs, mq, wgu_b2, wd_b2 = _layer_b_in(
        x, w["ffn1_norm"], wgu_b1, wd_b1, w["mix_norm"], w["b_in"], tabs,
        [(ffn2_w_gu, 1, 32), (ffn2_w_down, 1, 128)], n_prompt=mp // TOKEN_TILE, seq=seq)

    k_new = _compact_kv(k_sh[mp:], nb, t)
    v_new = _compact_kv(v_sh[mp:], nb, t)

    def new_rows(a):
        return jnp.pad(a.reshape(nb, t, kw), ((0, 0), (0, _NEW_ROWS - t), (0, 0)))

    q5 = qs[:, mp:].reshape(SWA_Q_PAIRS, nb, t, 2, HEAD_DIM).transpose(1, 0, 3, 2, 4)
    q5 = q5.reshape(nb, SWA_KV_HEADS, SWA_GROUP, t, HEAD_DIM)
    qbd = jnp.where(kv_mask[None, :, None, None, :], jnp.tile(q5, (1, 1, 1, 1, SWA_KV_HEADS)), 0)
    qbd = qbd.reshape(nb, SWA_Q_HEADS * t, kw)
    o, k_s, v_s = _swa_sample(qbd, jnp.repeat(sinks, t)[:, None], kc_t, vc_t,
                              new_rows(k_new), new_rows(v_new))
    o = o.reshape(nb, SWA_KV_HEADS, SWA_GROUP, t, kw)
    o = jnp.where(kv_mask[None, :, None, None, :], o, 0.0)
    o = o.reshape(nb, SWA_KV_HEADS, SWA_GROUP, t, SWA_KV_HEADS, HEAD_DIM).sum(axis=4)
    tok_s = o.reshape(nb, SWA_Q_PAIRS, 2, t, HEAD_DIM).transpose(1, 0, 3, 2, 4)
    tok_s = tok_s.reshape(SWA_Q_PAIRS, nb * t, HEAD_DIMP).astype(BF)
    mo_s = mem_attn_sample(mq, 1)
    y_p, y_s = _layer_b_out(sinks, x, qs, k_sh, v_sh, mq, mem_k_t, mem_v_t, tok_s, mo_s,
                            w["b_out"], w["ffn2_norm"], wgu_b2, wd_b2, final_norm[None, :],
                            seq=seq)

    gla_prompt = st_p.transpose(0, 1, 3, 2)[None, :, :, :GLA_DK, :GLA_DV]
    gla_sample = st_s.transpose(0, 4, 1, 2, 3)
    def last_window(a):
        tiles_per_seq = seq // TOKEN_TILE
        a = a.reshape(-1, TOKEN_TILE, SWA_KV_HEADS * HEAD_DIMP)
        a = a[tiles_per_seq - 1:batch * tiles_per_seq:tiles_per_seq, TOKEN_TILE - WINDOW:]
        return a.reshape(batch, WINDOW, SWA_KV_HEADS, HEAD_DIMP)[..., :HEAD_DIM]

    swa_k_prompt = last_window(k_sh)
    swa_v_prompt = last_window(v_sh)
    swa_k_sample = k_s.reshape(nb, SWA_KV_HEADS, HEAD_DIM, WINDOW).transpose(0, 3, 1, 2)
    swa_v_sample = v_s.reshape(nb, SWA_KV_HEADS, HEAD_DIM, WINDOW).transpose(0, 3, 1, 2)
    swa_k_sample = swa_k_sample.at[:, WINDOW - t:].set(k_new)
    swa_v_sample = swa_v_sample.at[:, WINDOW - t:].set(v_new)
    mem_shape = (2, batch, MEM_HEADS, MEM_HEAD_DIM, MEM_TOKENS)
    mem_k_prompt = mem_k_t.reshape(mem_shape).transpose(0, 1, 4, 2, 3)
    mem_v_prompt = mem_v_t.reshape(mem_shape).transpose(0, 1, 4, 2, 3)
    return (y_p.reshape(batch, seq, D_MODEL), y_s.reshape(nb, t, D_MODEL), gla_prompt,
            gla_sample, swa_k_prompt, swa_v_prompt, swa_k_sample, swa_v_sample,
            mem_k_prompt, mem_v_prompt)
```

```python
import functools
import math

import jax
import jax.numpy as jnp
import numpy as np
from jax import lax
from jax.experimental import pallas as pl
from jax.experimental.pallas import tpu as pltpu

F32 = jnp.float32
BF = jnp.bfloat16

D_MODEL = 1024
FFN_DIM = 2816
EPS = 1e-6

GLA_HEADS = 4
GLA_DK = 96
GLA_DV = 192
GLA_DKP = 128
GLA_DVP = 256
GLA_RANK = 16
GLA_RANKP = 128
GLA_GATE_NORM = 16.0
GLA_CHUNK = 256
GLA_SEQS = 2

HEAD_DIM = 64
HEAD_DIMP = 128
SWA_Q_HEADS = 12
SWA_KV_HEADS = 3
SWA_GROUP = SWA_Q_HEADS // SWA_KV_HEADS
SWA_Q_PAIRS = SWA_Q_HEADS // 2
SWA_PAIRS_PER_KV = SWA_GROUP // 2
WINDOW = 128
ROT_DIM = 16
ROPE_THETA = 500000.0
PAST_LEN = 8192

MEM_TOKENS = 256
MEM_HEADS = 4
MEM_HEAD_DIM = 64
MEM_Q = MEM_HEADS * MEM_HEAD_DIM

FFN_TF = 256
FFN_CHUNKS = FFN_DIM // FFN_TF
TOKEN_TILE = 512

VMEM_LIMIT = 60 * 1024 * 1024


def _params(*sem):
    return pltpu.CompilerParams(dimension_semantics=sem, vmem_limit_bytes=VMEM_LIMIT)


def _resident(shape):
    nd = len(shape)
    return pl.BlockSpec(shape, lambda *_: (0,) * nd, pipeline_mode=pl.Buffered(1))


def _layer_block(shape, layer):
    nd = len(shape)
    return pl.BlockSpec((1,) + tuple(shape[1:]), lambda *_: (layer,) + (0,) * (nd - 1),
                        pipeline_mode=pl.Buffered(1))


def _rows(width):
    return pl.BlockSpec((TOKEN_TILE, width), lambda i: (i, 0))


def _prompt_rows(width, n_prompt):
    return pl.BlockSpec((TOKEN_TILE, width), lambda i: (jnp.minimum(i, n_prompt - 1), 0))


def _sample_rows(width):
    return pl.BlockSpec((TOKEN_TILE, width), lambda i: (0, 0), pipeline_mode=pl.Buffered(1))


def _rms(x, g):
    ms = jnp.mean(x * x, axis=-1, keepdims=True)
    return x * lax.rsqrt(ms + EPS) * g


def _silu(x):
    return x * (1.0 / (1.0 + jnp.exp(-x)))


def _dot(a, b):
    return jnp.dot(a, b, preferred_element_type=F32)


def _dot_nt(a, b):
    return lax.dot_general(a, b, (((1,), (1,)), ((), ())), preferred_element_type=F32)


def _dot_tn(a, b):
    return lax.dot_general(a, b, (((0,), (0,)), ((), ())), preferred_element_type=F32)


def _pick(is_prompt, p_ref, s_ref):
    return jnp.where(is_prompt, p_ref[...], s_ref[...])


def _cast_job(w, layer, rows, grid):
    _, r, c = w.shape
    assert r % rows == 0 and rows % 16 == 0
    nblk = r // rows
    total = math.prod(grid)
    assert nblk <= total
    steps_per_block = total // nblk

    def block(*idx):
        step = idx[0]
        for dim, i in zip(grid[1:], idx[1:]):
            step = step * dim + i
        return jnp.minimum(step // steps_per_block, nblk - 1)

    in_spec = pl.BlockSpec((1, rows, c), lambda *idx: (layer, block(*idx), 0))
    out_spec = pl.BlockSpec((1, rows, c), lambda *idx: (0, block(*idx), 0))
    return in_spec, out_spec, jax.ShapeDtypeStruct((1, r, c), BF)


def _run_cast_jobs(refs):
    n = len(refs) // 2
    for src, dst in zip(refs[:n], refs[n:]):
        dst[...] = src[...].astype(BF)


def _ffn_half(x, g_ref, wgu_ref, wd_ref):
    h = _rms(x, g_ref[0]).astype(BF)
    acc = jnp.zeros(x.shape, F32)
    for c in range(FFN_CHUNKS):
        lo, hi = c * FFN_TF, (c + 1) * FFN_TF
        gate = _dot(h, wgu_ref[0, :, lo:hi])
        up = _dot(h, wgu_ref[0, :, FFN_DIM + lo:FFN_DIM + hi])
        a = (_silu(gate) * up).astype(BF)
        acc = acc + _dot(a, wd_ref[0, lo:hi, :])
    return x + 0.5 * acc


def _rope(x, c, s1, s2):
    return (x * c + pltpu.roll(x, HEAD_DIMP - ROT_DIM // 2, 1) * s1
            + pltpu.roll(x, ROT_DIM // 2, 1) * s2)


_A_Q = 0
_A_K = _A_Q + GLA_HEADS * GLA_DKP
_A_V = _A_K + GLA_HEADS * GLA_DKP
_A_R = _A_V + GLA_HEADS * GLA_DVP
_A_G = _A_R + GLA_HEADS * GLA_DVP
_A_M = _A_G + GLA_RANKP
_A_END = _A_M + MEM_Q


def _layer_a_in_kernel(xp_ref, xs_ref, g1_ref, wgu_ref, wd_ref, gm_ref, w_ref, wg_ref, bg_ref,
                       gn_ref, x_ref, q_ref, k_ref, la_ref, v_ref, r_ref, mq_ref, *, n_prompt):
    is_prompt = pl.program_id(0) < n_prompt
    x = _ffn_half(_pick(is_prompt, xp_ref, xs_ref), g1_ref, wgu_ref, wd_ref)
    x_ref[...] = x
    h = _rms(x, gm_ref[0]).astype(BF)

    proj = _dot(h, w_ref[...])

    def mm(a, b):
        return proj[:, a:b]

    q_ref[...] = mm(_A_Q, _A_K) * (GLA_DK ** -0.5)
    k_ref[...] = mm(_A_K, _A_V)
    v_ref[...] = mm(_A_V, _A_R)
    r_ref[...] = _silu(mm(_A_R, _A_G)) * gn_ref[...]
    z = _dot(mm(_A_G, _A_M).astype(BF), wg_ref[...]) + bg_ref[...]
    la_ref[...] = (jnp.minimum(z, 0.0) - jnp.log1p(jnp.exp(-jnp.abs(z)))) * (1.0 / GLA_GATE_NORM)
    mq_ref[...] = (mm(_A_M, _A_END) * (MEM_HEAD_DIM ** -0.5)).astype(BF)


def _layer_a_in(xp, xs, g1, wgu, wd, gm, w, wg, bg, gn):
    n_prompt = xp.shape[0] // TOKEN_TILE
    m = xp.shape[0] + xs.shape[0]
    qk = GLA_HEADS * GLA_DKP
    vr = GLA_HEADS * GLA_DVP
    outs = ((D_MODEL, F32), (qk, F32), (qk, F32), (qk, F32), (vr, F32), (vr, F32), (MEM_Q, BF))
    return pl.pallas_call(
        functools.partial(_layer_a_in_kernel, n_prompt=n_prompt),
        grid=(m // TOKEN_TILE,),
        in_specs=[_prompt_rows(D_MODEL, n_prompt), _sample_rows(D_MODEL),
                  _layer_block(g1.shape, 0), _layer_block(wgu.shape, 0), _layer_block(wd.shape, 0),
                  _layer_block(gm.shape, 0), _resident(w.shape), _resident(wg.shape),
                  _resident(bg.shape), _resident(gn.shape)],
        out_specs=[_rows(n) for n, _ in outs],
        out_shape=[jax.ShapeDtypeStruct((m, n), dt) for n, dt in outs],
        compiler_params=_params("arbitrary"),
        name="layer_a_in",
    )(xp, xs, g1, wgu, wd, gm, w, wg, bg, gn)


def _build_mem_block_diag(mk_ref, mv_ref, kbd_scr, vbd_scr):
    shape = (MEM_Q, MEM_HEADS * MEM_TOKENS)
    rh = lax.broadcasted_iota(jnp.int32, shape, 0) // MEM_HEAD_DIM
    ch = lax.broadcasted_iota(jnp.int32, shape, 1) // MEM_TOKENS
    diag = rh == ch
    kbd_scr[...] = jnp.where(diag, jnp.concatenate([mk_ref[0, 0]] * MEM_HEADS, axis=1),
                             0.0).astype(BF)
    vbd_scr[...] = jnp.where(diag, jnp.concatenate([mv_ref[0, 0]] * MEM_HEADS, axis=1),
                             0.0).astype(BF)


def _mem_attn_tile(q, kbd, vbd):
    tq = q.shape[0]
    s = _dot(q, kbd)
    ps, inv = [], []
    for h in range(MEM_HEADS):
        sh = s[:, h * MEM_TOKENS:(h + 1) * MEM_TOKENS]
        e = jnp.exp(sh - jnp.max(sh, axis=1, keepdims=True))
        inv.append(1.0 / jnp.sum(e, axis=1, keepdims=True))
        ps.append(e.astype(BF))
    o = _dot_nt(jnp.concatenate(ps, axis=1), vbd)
    lane_h = lax.broadcasted_iota(jnp.int32, (tq, MEM_Q), 1) // MEM_HEAD_DIM
    scale = jnp.where(lane_h == 0, inv[0],
                      jnp.where(lane_h == 1, inv[1], jnp.where(lane_h == 2, inv[2], inv[3])))
    return o * scale


def _swa_tile(q_ref, kprev_ref, kcur_ref, vprev_ref, vcur_ref, sink_ref, first_of_seq):
    blk = WINDOW
    nsub = TOKEN_TILE // blk
    qi = lax.broadcasted_iota(jnp.int32, (blk, 2 * blk), 0)
    kj = lax.broadcasted_iota(jnp.int32, (blk, 2 * blk), 1)
    d = blk + qi - kj
    band = (d >= 0) & (d < WINDOW)
    bias_mid = jnp.where(band, 0.0, -jnp.inf)
    bias_first = jnp.where(band & (kj >= blk), 0.0, -jnp.inf)
    bias0 = jnp.where(first_of_seq, bias_first, bias_mid)
    outs = [[None] * SWA_Q_PAIRS for _ in range(nsub)]
    for kh in range(SWA_KV_HEADS):
        sl = slice(kh * HEAD_DIMP, (kh + 1) * HEAD_DIMP)
        kblk = [kprev_ref[:, sl]] + [kcur_ref[s * blk:(s + 1) * blk, sl] for s in range(nsub)]
        vblk = [vprev_ref[:, sl]] + [vcur_ref[s * blk:(s + 1) * blk, sl] for s in range(nsub)]
        k2 = [(b.astype(BF), pltpu.roll(b, HEAD_DIM, 1).astype(BF)) for b in kblk]
        v2 = [(b.astype(BF), pltpu.roll(b, HEAD_DIM, 1).astype(BF)) for b in vblk]
        for sub in range(nsub):
            bias = bias0 if sub == 0 else bias_mid
            for pj in range(SWA_PAIRS_PER_KV):
                pp = kh * SWA_PAIRS_PER_KV + pj
                q = q_ref[pp, sub * blk:(sub + 1) * blk, :]
                o = None
                for half in range(2):
                    kb = jnp.concatenate([k2[sub][half], k2[sub + 1][half]], axis=0)
                    vb = jnp.concatenate([v2[sub][half], v2[sub + 1][half]], axis=0)
                    s = _dot_nt(q, kb) + bias
                    sink = sink_ref[2 * pp + half]
                    m = jnp.maximum(jnp.max(s, axis=1, keepdims=True), sink)
                    e = jnp.exp(s - m)
                    l = jnp.sum(e, axis=1, keepdims=True) + jnp.exp(sink - m)
                    oh = _dot(e.astype(BF), vb) * (1.0 / l)
                    o = oh if o is None else o + oh
                outs[sub][pp] = o.astype(BF)
    return jnp.concatenate([jnp.concatenate(row, axis=1) for row in outs], axis=0)


def _layer_a_out_kernel(x_ref, tokp_ref, toks_ref, mq_ref, mk_ref, mv_ref, mos_ref, wo_ref,
                        g2_ref, wgu_ref, wd_ref, gkv_ref, wkv_ref, c_ref, s1_ref, s2_ref,
                        *refs, n_prompt, tiles_per_seq):
    n = (len(refs) - 5) // 2
    xo_ref, k_ref, v_ref = refs[n:n + 3]
    kbd_scr, vbd_scr = refs[-2:]
    _run_cast_jobs(refs[:n] + refs[n + 3:-2])
    i = pl.program_id(0)
    is_prompt = i < n_prompt

    @pl.when(i % tiles_per_seq == 0)
    def _():
        _build_mem_block_diag(mk_ref, mv_ref, kbd_scr, vbd_scr)

    nt = GLA_HEADS * GLA_DVP
    mo = _mem_attn_tile(mq_ref[...], kbd_scr[...], vbd_scr[...]).astype(BF)
    mo = jnp.where(is_prompt, mo, mos_ref[...])
    x = (x_ref[...] + _dot(_pick(is_prompt, tokp_ref, toks_ref), wo_ref[:nt, :])
         + _dot(mo, wo_ref[nt:, :]))
    x = _ffn_half(x, g2_ref, wgu_ref, wd_ref)
    xo_ref[...] = x
    h = _rms(x, gkv_ref[...]).astype(BF)
    c, s1, s2 = c_ref[...], s1_ref[...], s2_ref[...]
    kw = SWA_KV_HEADS * HEAD_DIMP
    kv = _dot(h, wkv_ref[...])
    for hh in range(SWA_KV_HEADS):
        sl = slice(hh * HEAD_DIMP, (hh + 1) * HEAD_DIMP)
        k_ref[:, sl] = _rope(kv[:, sl], c, s1, s2)
    v_ref[...] = kv[:, kw:]


def _tab_spec(n_prompt, blocks_per_seq):
    return pl.BlockSpec((TOKEN_TILE, HEAD_DIMP),
                        lambda i: (jnp.where(i < n_prompt, i % blocks_per_seq, blocks_per_seq), 0))


def _mem_kv_spec(layer, n_prompt, tiles_per_seq):
    return pl.BlockSpec(
        (1, 1, MEM_Q, MEM_TOKENS),
        lambda i: (layer, jnp.minimum(i, n_prompt - 1) // tiles_per_seq, 0, 0))


def _mem_scratch():
    return pltpu.VMEM((MEM_Q, MEM_HEADS * MEM_TOKENS), BF)


def _layer_a_out(x, tokp, toks, mq, mk_t, mv_t, mos, wo, g2, wgu, wd, gkv, wkv, tabs, casts,
                 *, seq):
    m = x.shape[0]
    grid = (m // TOKEN_TILE,)
    n_prompt = tokp.shape[0] // TOKEN_TILE
    tiles_per_seq = seq // TOKEN_TILE
    kw = SWA_KV_HEADS * HEAD_DIMP
    nt = GLA_HEADS * GLA_DVP
    tab = _tab_spec(n_prompt, tiles_per_seq)
    mem_spec = _mem_kv_spec(0, n_prompt, tiles_per_seq)
    jobs = [_cast_job(cw, layer, rows, grid) for cw, layer, rows in casts]
    return pl.pallas_call(
        functools.partial(_layer_a_out_kernel, n_prompt=n_prompt, tiles_per_seq=tiles_per_seq),
        grid=grid,
        in_specs=[_rows(D_MODEL), _prompt_rows(nt, n_prompt), _sample_rows(nt),
                  _rows(MEM_Q), mem_spec, mem_spec, _sample_rows(MEM_Q), _resident(wo.shape),
                  _layer_block(g2.shape, 0), _layer_block(wgu.shape, 0), _layer_block(wd.shape, 0),
                  _resident(gkv.shape), _resident(wkv.shape), tab, tab, tab]
                 + [j[0] for j in jobs],
        out_specs=[_rows(D_MODEL), _rows(kw), _rows(kw)] + [j[1] for j in jobs],
        out_shape=[jax.ShapeDtypeStruct((m, D_MODEL), F32),
                   jax.ShapeDtypeStruct((m, kw), F32), jax.ShapeDtypeStruct((m, kw), F32)]
                  + [j[2] for j in jobs],
        scratch_shapes=[_mem_scratch(), _mem_scratch()],
        compiler_params=_params("arbitrary"),
        name="layer_a_out",
    )(x, tokp, toks, mq, mk_t, mv_t, mos, wo, g2, wgu, wd, gkv, wkv, *tabs,
      *[c[0] for c in casts])


def _layer_b_in_kernel(x_ref, g1_ref, wgu_ref, wd_ref, gm_ref, w_ref, c_ref, s1_ref, s2_ref,
                       *refs):
    n = (len(refs) - 3) // 2
    xo_ref, q_ref, mq_ref = refs[n:n + 3]
    _run_cast_jobs(refs[:n] + refs[n + 3:])
    x = _ffn_half(x_ref[...], g1_ref, wgu_ref, wd_ref)
    xo_ref[...] = x
    h = _rms(x, gm_ref[0]).astype(BF)
    c, s1, s2 = c_ref[...], s1_ref[...], s2_ref[...]
    nq = SWA_Q_HEADS * HEAD_DIM
    qm = _dot(h, w_ref[...])
    for pp in range(SWA_Q_PAIRS):
        q = qm[:, pp * HEAD_DIMP:(pp + 1) * HEAD_DIMP]
        q_ref[pp] = (_rope(q, c, s1, s2) * (HEAD_DIM ** -0.5)).astype(BF)
    mq_ref[...] = (qm[:, nq:] * (MEM_HEAD_DIM ** -0.5)).astype(BF)


def _layer_b_in(x, g1, wgu, wd, gm, w, tabs, casts, *, n_prompt, seq):
    m = x.shape[0]
    grid = (m // TOKEN_TILE,)
    tab = _tab_spec(n_prompt, seq // TOKEN_TILE)
    q_spec = pl.BlockSpec((SWA_Q_PAIRS, TOKEN_TILE, HEAD_DIMP), lambda i: (0, i, 0))
    jobs = [_cast_job(cw, layer, rows, grid) for cw, layer, rows in casts]
    return pl.pallas_call(
        _layer_b_in_kernel,
        grid=grid,
        in_specs=[_rows(D_MODEL), _layer_block(g1.shape, 1), _layer_block(wgu.shape, 0),
                  _layer_block(wd.shape, 0), _layer_block(gm.shape, 1), _resident(w.shape),
                  tab, tab, tab] + [j[0] for j in jobs],
        out_specs=[_rows(D_MODEL), q_spec, _rows(MEM_Q)] + [j[1] for j in jobs],
        out_shape=[jax.ShapeDtypeStruct((m, D_MODEL), F32),
                   jax.ShapeDtypeStruct((SWA_Q_PAIRS, m, HEAD_DIMP), BF),
                   jax.ShapeDtypeStruct((m, MEM_Q), BF)] + [j[2] for j in jobs],
        compiler_params=_params("arbitrary"),
        name="layer_b_in",
    )(x, g1, wgu, wd, gm, w, *tabs, *[c[0] for c in casts])


def _layer_b_out_kernel(sink_ref, x_ref, q_ref, kprev_ref, kcur_ref, vprev_ref, vcur_ref,
                        mq_ref, mk_ref, mv_ref, toks_ref, mos_ref, wo_ref, g2_ref, wgu_ref,
                        wd_ref, gf_ref, yp_ref, ys_ref, kbd_scr, vbd_scr,
                        *, n_prompt, tiles_per_seq):
    i = pl.program_id(0)
    is_prompt = i < n_prompt
    first_of_seq = i % tiles_per_seq == 0

    @pl.when(first_of_seq)
    def _():
        _build_mem_block_diag(mk_ref, mv_ref, kbd_scr, vbd_scr)

    nt = SWA_Q_HEADS * HEAD_DIM
    mo = _mem_attn_tile(mq_ref[...], kbd_scr[...], vbd_scr[...]).astype(BF)
    mo = jnp.where(is_prompt, mo, mos_ref[...])
    tok = _swa_tile(q_ref, kprev_ref, kcur_ref, vprev_ref, vcur_ref, sink_ref, first_of_seq)
    toks = jnp.concatenate([toks_ref[pp] for pp in range(SWA_Q_PAIRS)], axis=1)
    tok = jnp.where(is_prompt, tok, toks)
    x = x_ref[...] + _dot(tok, wo_ref[:nt, :]) + _dot(mo, wo_ref[nt:, :])
    y = _rms(_ffn_half(x, g2_ref, wgu_ref, wd_ref), gf_ref[...])

    @pl.when(is_prompt)
    def _():
        yp_ref[...] = y

    @pl.when(jnp.logical_not(is_prompt))
    def _():
        ys_ref[...] = y


def _layer_b_out(sinks, x, qs, k_sh, v_sh, mq, mk_t, mv_t, toks, mos, wo, g2, wgu, wd, gf, *, seq):
    m = x.shape[0]
    n_tiles = m // TOKEN_TILE
    n_prompt = n_tiles - 1
    tiles_per_seq = seq // TOKEN_TILE
    kw = SWA_KV_HEADS * HEAD_DIMP
    blocks_per_tile = TOKEN_TILE // WINDOW
    q_spec = pl.BlockSpec((SWA_Q_PAIRS, TOKEN_TILE, HEAD_DIMP), lambda i: (0, i, 0))
    prev_spec = pl.BlockSpec((WINDOW, kw), lambda i: (jnp.maximum(i * blocks_per_tile - 1, 0), 0))
    toks_spec = pl.BlockSpec((SWA_Q_PAIRS, TOKEN_TILE, HEAD_DIMP), lambda i: (0, 0, 0),
                             pipeline_mode=pl.Buffered(1))
    mem_spec = _mem_kv_spec(1, n_prompt, tiles_per_seq)
    return pl.pallas_call(
        functools.partial(_layer_b_out_kernel, n_prompt=n_prompt, tiles_per_seq=tiles_per_seq),
        grid=(n_tiles,),
        in_specs=[pl.BlockSpec(memory_space=pltpu.SMEM), _rows(D_MODEL), q_spec,
                  prev_spec, _rows(kw), prev_spec, _rows(kw), _rows(MEM_Q), mem_spec, mem_spec,
                  toks_spec, _sample_rows(MEM_Q), _resident(wo.shape), _layer_block(g2.shape, 1),
                  _layer_block(wgu.shape, 0), _layer_block(wd.shape, 0), _resident(gf.shape)],
        out_specs=[_prompt_rows(D_MODEL, n_prompt),
                   pl.BlockSpec((TOKEN_TILE, D_MODEL), lambda i: (0, 0))],
        out_shape=[jax.ShapeDtypeStruct((n_prompt * TOKEN_TILE, D_MODEL), F32),
                   jax.ShapeDtypeStruct((TOKEN_TILE, D_MODEL), F32)],
        scratch_shapes=[_mem_scratch(), _mem_scratch()],
        compiler_params=_params("arbitrary"),
        name="layer_b_out",
    )(sinks, x, qs, k_sh, k_sh, v_sh, v_sh, mq, mk_t, mv_t, toks, mos, wo, g2, wgu, wd, gf)


def _mem_kv_kernel(x_ref, g_ref, wt_ref, *refs):
    n = (len(refs) - 2) // 2
    k_ref, v_ref = refs[n:n + 2]
    _run_cast_jobs(refs[:n] + refs[n + 2:])
    x = x_ref[...]
    xn = x * lax.rsqrt(jnp.mean(x * x, axis=-1, keepdims=True) + EPS)
    for l in range(2):
        h = (xn * g_ref[l]).astype(BF)
        kvt = _dot_nt(wt_ref[l], h)
        k_ref[l, 0] = kvt[:MEM_Q, :]
        v_ref[l, 0] = kvt[MEM_Q:, :]


def _mem_kv(mem, g, wt, casts, *, batch):
    out_spec = pl.BlockSpec((2, 1, MEM_Q, MEM_TOKENS), lambda b: (0, b, 0, 0))
    jobs = [_cast_job(cw, layer, rows, (batch,)) for cw, layer, rows in casts]
    return pl.pallas_call(
        _mem_kv_kernel,
        grid=(batch,),
        in_specs=[pl.BlockSpec((MEM_TOKENS, D_MODEL), lambda b: (b, 0)), _resident(g.shape),
                  _resident(wt.shape)] + [j[0] for j in jobs],
        out_specs=[out_spec, out_spec] + [j[1] for j in jobs],
        out_shape=[jax.ShapeDtypeStruct((2, batch, MEM_Q, MEM_TOKENS), F32)] * 2
                  + [j[2] for j in jobs],
        compiler_params=_params("arbitrary"),
        name="mem_kv",
    )(mem, g, wt, *[c[0] for c in casts])


def _gla_prompt_kernel(*refs):
    ns = GLA_SEQS
    seq_in = [refs[5 * i:5 * i + 5] for i in range(ns)]
    rest = refs[5 * ns:-1]
    s_scr = refs[-1]
    n = (len(rest) - 2) // 2
    tok_ref, st_ref = rest[n:n + 2]
    _run_cast_jobs(rest[:n] + rest[n + 2:])
    c = pl.program_id(1)
    C = GLA_CHUNK

    @pl.when(c == 0)
    def _():
        s_scr[...] = jnp.zeros(s_scr.shape, F32)

    row = lax.broadcasted_iota(jnp.int32, (C, C), 0)
    col = lax.broadcasted_iota(jnp.int32, (C, C), 1)
    causal = row >= col
    ltri = jnp.where(causal, 1.0, 0.0).astype(BF)
    _gla_chunks([(seq_in[i], tok_ref.at[i], s_scr.at[i], h)
                 for i in range(ns) for h in range(GLA_HEADS)], causal, ltri)

    @pl.when(c == pl.num_programs(1) - 1)
    def _():
        st_ref[...] = s_scr[...]


def _gla_chunks(chains, causal, ltri):
    C = GLA_CHUNK

    def sk(h):
        return slice(h * GLA_DKP, (h + 1) * GLA_DKP)

    def sv(h):
        return slice(h * GLA_DVP, (h + 1) * GLA_DVP)

    bs = []
    for (q_ref, k_ref, la_ref, v_ref, r_ref), _, _, h in chains:
        la = la_ref[:, sk(h)]
        hi = la.astype(BF)
        lo = (la - hi.astype(F32)).astype(BF)
        bb = _dot(ltri, jnp.concatenate([hi, lo], axis=1))
        bs.append(bb[:, :GLA_DKP] + bb[:, GLA_DKP:])
    ops = []
    for ((q_ref, k_ref, la_ref, v_ref, r_ref), _, _, h), b in zip(chains, bs):
        b_mid = b[C // 2 - 1:C // 2, :]
        b_last = b[C - 1:C, :]
        qe = q_ref[:, sk(h)] * jnp.exp(b - b_mid)
        ke = k_ref[:, sk(h)] * jnp.exp(b_mid - b)
        qb = (qe * jnp.exp(b_mid)).astype(BF)
        kd = (ke * jnp.exp(b_last - b_mid)).astype(BF)
        ops.append((qe.astype(BF), ke.astype(BF), qb, kd, jnp.exp(b_last)))
    As = [jnp.where(causal, _dot_nt(qe, ke), 0.0).astype(BF) for qe, ke, _, _, _ in ops]
    outs = []
    for ((q_ref, k_ref, la_ref, v_ref, r_ref), _, s_ref, h), a, (_, _, qb, kd, decay) in zip(
            chains, As, ops):
        v = v_ref[:, sv(h)].astype(BF)
        st = s_ref[h]
        outs.append(_dot(a, v) + _dot_nt(qb, st.astype(BF)))
        s_ref[h] = st * decay + _dot_tn(v, kd)
    for ((q_ref, k_ref, la_ref, v_ref, r_ref), tok_ref, _, h), o in zip(chains, outs):
        ms = jnp.sum(o * o, axis=1, keepdims=True) * (1.0 / GLA_DV)
        tok_ref[:, sv(h)] = (o * lax.rsqrt(ms + EPS) * r_ref[:, sv(h)]).astype(BF)


def _gla_prompt(q, k, la, v, r, casts, *, batch, seq):
    ns = GLA_SEQS
    m = batch * seq
    nc = seq // GLA_CHUNK
    qk = GLA_HEADS * GLA_DKP
    vr = GLA_HEADS * GLA_DVP
    grid = (batch // ns, nc)

    def tok_map(i):
        return lambda b, c: ((b * ns + i) * nc + c, 0)

    seq_specs = []
    for i in range(ns):
        seq_specs += [pl.BlockSpec((GLA_CHUNK, qk), tok_map(i))] * 3
        seq_specs += [pl.BlockSpec((GLA_CHUNK, vr), tok_map(i))] * 2
    jobs = [_cast_job(cw, layer, rows, grid) for cw, layer, rows in casts]
    res = pl.pallas_call(
        _gla_prompt_kernel,
        grid=grid,
        in_specs=seq_specs + [j[0] for j in jobs],
        out_specs=[pl.BlockSpec((ns, GLA_CHUNK, vr), lambda b, c: (b, c, 0)),
                   pl.BlockSpec((ns, GLA_HEADS, GLA_DVP, GLA_DKP), lambda b, c: (b, 0, 0, 0))]
                  + [j[1] for j in jobs],
        out_shape=[jax.ShapeDtypeStruct((batch, seq, vr), BF),
                   jax.ShapeDtypeStruct((batch, GLA_HEADS, GLA_DVP, GLA_DKP), F32)]
                  + [j[2] for j in jobs],
        scratch_shapes=[pltpu.VMEM((ns, GLA_HEADS, GLA_DVP, GLA_DKP), F32)],
        compiler_params=_params("arbitrary", "arbitrary"),
        name="gla_prompt",
    )(*([q, k, la, v, r] * ns), *[c[0] for c in casts])
    return (res[0].reshape(m, vr),) + tuple(res[1:])


_SAMPLE_BB = 8
_DEC_SEQ = 4
_DEC_BATCH = 128
_GLA_DK_BLK = 32
_NEW_ROWS = 16


def _gla_sample_kernel(s_ref, q_ref, k_ref, la_ref, v_ref, r_ref, so_ref, tok_ref, o_scr):
    j = pl.program_id(1)

    @pl.when(j == 0)
    def _():
        o_scr[...] = jnp.zeros(o_scr.shape, F32)

    def body(dk, carry):
        s = s_ref[0, 0, dk]
        for t in range(_DEC_SEQ):
            a = jnp.exp(la_ref[t, 0, pl.ds(dk, 1), :])
            s = a * s + k_ref[t, 0, pl.ds(dk, 1), :] * v_ref[t, 0, :GLA_DV, :]
            o_scr[t] = o_scr[t] + q_ref[t, 0, pl.ds(dk, 1), :] * s
        so_ref[0, 0, dk] = s
        return carry

    lax.fori_loop(0, _GLA_DK_BLK, body, 0, unroll=4)

    @pl.when(j == pl.num_programs(1) - 1)
    def _():
        tok_ref[...] = jnp.zeros(tok_ref.shape, F32)
        for t in range(_DEC_SEQ):
            o = o_scr[t]
            ms = jnp.sum(o * o, axis=0, keepdims=True) * (1.0 / GLA_DV)
            tok_ref[t, 0, :GLA_DV, :] = o * lax.rsqrt(ms + EPS) * r_ref[t, 0, :GLA_DV, :]


def _gla_sample(state, q, k, la, v, r):
    qk_spec = pl.BlockSpec((_DEC_SEQ, 1, _GLA_DK_BLK, _DEC_BATCH), lambda h, j: (0, h, j, 0))
    vr_spec = pl.BlockSpec((_DEC_SEQ, 1, GLA_DVP, _DEC_BATCH), lambda h, j: (0, h, 0, 0))
    s_spec = pl.BlockSpec((1, 1, _GLA_DK_BLK, GLA_DV, _DEC_BATCH), lambda h, j: (0, h, j, 0, 0))
    return pl.pallas_call(
        _gla_sample_kernel,
        grid=(GLA_HEADS, GLA_DK // _GLA_DK_BLK),
        in_specs=[s_spec, qk_spec, qk_spec, qk_spec, vr_spec, vr_spec],
        out_specs=[s_spec, vr_spec],
        out_shape=[jax.ShapeDtypeStruct(state.shape, F32),
                   jax.ShapeDtypeStruct(v.shape, F32)],
        scratch_shapes=[pltpu.VMEM((_DEC_SEQ, GLA_DV, _DEC_BATCH), F32)],
        compiler_params=_params("parallel", "arbitrary"),
        name="gla_sample",
    )(state, q, k, la, v, r)


def _mem_attn_sample_kernel(q_ref, mk_ref, mv_ref, o_ref):
    scores = [_dot(q_ref[bi], mk_ref[0, bi].astype(BF)) for bi in range(_SAMPLE_BB)]
    probs = []
    for s in scores:
        e = jnp.exp(s - jnp.max(s, axis=1, keepdims=True))
        probs.append((e * (1.0 / jnp.sum(e, axis=1, keepdims=True))).astype(BF))
    for bi, p in enumerate(probs):
        o_ref[bi] = _dot_nt(p, mv_ref[0, bi].astype(BF))


def _mem_attn_sample(qbd, mk_t, mv_t, layer):
    nb = qbd.shape[0]
    nr = MEM_HEADS * _DEC_SEQ
    kv_spec = pl.BlockSpec((1, _SAMPLE_BB, MEM_Q, MEM_TOKENS), lambda i: (layer, i, 0, 0))
    q_spec = pl.BlockSpec((_SAMPLE_BB, nr, MEM_Q), lambda i: (i, 0, 0))
    return pl.pallas_call(
        _mem_attn_sample_kernel,
        grid=(nb // _SAMPLE_BB,),
        in_specs=[q_spec, kv_spec, kv_spec],
        out_specs=q_spec,
        out_shape=jax.ShapeDtypeStruct((nb, nr, MEM_Q), F32),
        compiler_params=_params("parallel"),
        name="mem_attn_sample",
    )(qbd, mk_t, mv_t)


def _swa_sample_kernel(q_ref, sink_ref, kc_ref, vc_ref, kn_ref, vn_ref, o_ref, ko_ref, vo_ref):
    nq = SWA_Q_HEADS * _DEC_SEQ
    t = lax.broadcasted_iota(jnp.int32, (nq, WINDOW), 0) % _DEC_SEQ
    pos = lax.broadcasted_iota(jnp.int32, (nq, WINDOW), 1)
    bias_c = jnp.where(pos > t, 0.0, -jnp.inf)
    tn = lax.broadcasted_iota(jnp.int32, (nq, _NEW_ROWS), 0) % _DEC_SEQ
    new = lax.broadcasted_iota(jnp.int32, (nq, _NEW_ROWS), 1)
    bias_n = jnp.where(new <= tn, 0.0, -jnp.inf)
    sink = sink_ref[...]
    seqs = range(_SAMPLE_BB)
    scores = [(_dot(q_ref[bi], kc_ref[bi].astype(BF)) + bias_c,
               _dot_nt(q_ref[bi], kn_ref[bi].astype(BF)) + bias_n) for bi in seqs]
    probs = []
    for sc, sn in scores:
        m = jnp.maximum(jnp.maximum(jnp.max(sc, axis=1, keepdims=True),
                                    jnp.max(sn, axis=1, keepdims=True)), sink)
        ec = jnp.exp(sc - m)
        en = jnp.exp(sn - m)
        l = (jnp.sum(ec, axis=1, keepdims=True) + jnp.sum(en, axis=1, keepdims=True)
             + jnp.exp(sink - m))
        inv = 1.0 / l
        probs.append(((ec * inv).astype(BF), (en * inv).astype(BF)))
    for bi, (pc, pn) in zip(seqs, probs):
        o_ref[bi] = (_dot_nt(pc, vc_ref[bi].astype(BF)) + _dot(pn, vn_ref[bi].astype(BF)))
    new0 = WINDOW - _DEC_SEQ
    row = lax.broadcasted_iota(jnp.int32, (_NEW_ROWS, WINDOW), 0)
    col = lax.broadcasted_iota(jnp.int32, (_NEW_ROWS, WINDOW), 1)
    place = jnp.where((col == new0 + row) & (row < _DEC_SEQ), 1.0, 0.0)
    is_new = lax.broadcasted_iota(jnp.int32, (SWA_KV_HEADS * HEAD_DIM, WINDOW), 1) >= new0

    def placed(new_rows):
        return lax.dot_general(new_rows, place, (((0,), (0,)), ((), ())),
                               precision=lax.Precision.HIGHEST, preferred_element_type=F32)

    for bi in seqs:
        ko_ref[bi] = jnp.where(is_new, placed(kn_ref[bi]), pltpu.roll(kc_ref[bi], new0, 1))
        vo_ref[bi] = jnp.where(is_new, placed(vn_ref[bi]), pltpu.roll(vc_ref[bi], new0, 1))


def _swa_sample(qbd, sink_col, kc, vc, kn, vn):
    nb = qbd.shape[0]
    kw = SWA_KV_HEADS * HEAD_DIM
    nq = SWA_Q_HEADS * _DEC_SEQ
    kv_spec = pl.BlockSpec((_SAMPLE_BB, kw, WINDOW), lambda i: (i, 0, 0))
    new_spec = pl.BlockSpec((_SAMPLE_BB, _NEW_ROWS, kw), lambda i: (i, 0, 0))
    q_spec = pl.BlockSpec((_SAMPLE_BB, nq, kw), lambda i: (i, 0, 0))
    return pl.pallas_call(
        _swa_sample_kernel,
        grid=(nb // _SAMPLE_BB,),
        in_specs=[q_spec, _resident((nq, 1)), kv_spec, kv_spec, new_spec, new_spec],
        out_specs=[q_spec, kv_spec, kv_spec],
        out_shape=[jax.ShapeDtypeStruct((nb, nq, kw), F32),
                   jax.ShapeDtypeStruct(kc.shape, F32), jax.ShapeDtypeStruct(kc.shape, F32)],
        compiler_params=_params("parallel"),
        name="swa_sample",
    )(qbd, sink_col, kc, vc, kn, vn)


def _pad_heads(w, heads, dim, dim_p, axis):
    shape = w.shape
    w = w.reshape(shape[:axis] + (heads, dim) + shape[axis + 1:])
    pad = [(0, 0)] * w.ndim
    pad[axis + 1] = (0, dim_p - dim)
    w = jnp.pad(w, pad)
    return w.reshape(shape[:axis] + (heads * dim_p,) + shape[axis + 1:])


def _rope_tables(pos):
    half = ROT_DIM // 2
    inv_freq = np.exp(-math.log(ROPE_THETA) * np.arange(0, ROT_DIM, 2, dtype=np.float64) / ROT_DIM)
    ang = pos.astype(np.float64)[:, None] * inv_freq[None, :]
    cos, sin = np.cos(ang), np.sin(ang)
    n = pos.shape[0]
    rest = HEAD_DIM - ROT_DIM
    c = np.concatenate([cos, cos, np.ones((n, rest))], axis=1)
    s1 = np.concatenate([-sin, np.zeros((n, HEAD_DIM - half))], axis=1)
    s2 = np.concatenate([np.zeros((n, half)), sin, np.zeros((n, rest))], axis=1)
    return tuple(np.tile(a, (1, HEAD_DIMP // HEAD_DIM)).astype(np.float32) for a in (c, s1, s2))


def _prep_weights(p):
    w = {}
    for name in ("ffn1_norm", "ffn2_norm", "mix_norm"):
        w[name] = p[name][:, None, :]
    qk = GLA_HEADS * GLA_DK
    vv = GLA_HEADS * GLA_DV
    a_in = p["a_w_in"][0]
    o = 0
    wq = _pad_heads(a_in[:, o:o + qk], GLA_HEADS, GLA_DK, GLA_DKP, 1); o += qk
    wk = _pad_heads(a_in[:, o:o + qk], GLA_HEADS, GLA_DK, GLA_DKP, 1); o += qk
    wv = _pad_heads(a_in[:, o:o + vv], GLA_HEADS, GLA_DV, GLA_DVP, 1); o += vv
    wr = _pad_heads(a_in[:, o:o + vv], GLA_HEADS, GLA_DV, GLA_DVP, 1); o += vv
    wg = jnp.pad(a_in[:, o:o + GLA_RANK], ((0, 0), (0, GLA_RANKP - GLA_RANK))); o += GLA_RANK
    wm = a_in[:, o:]
    w["a_in"] = jnp.concatenate([wq, wk, wv, wr, wg, wm], axis=1).astype(BF)
    gate = _pad_heads(p["a_w_gate"][0], GLA_HEADS, GLA_DK, GLA_DKP, 1)
    w["a_gate"] = jnp.pad(gate, ((0, GLA_RANKP - GLA_RANK), (0, 0))).astype(BF)
    w["a_bgate"] = _pad_heads(p["a_b_gate"][0][None, :], GLA_HEADS, GLA_DK, GLA_DKP, 1)
    w["a_gn"] = jnp.tile(jnp.pad(p["a_out_norm"][0], (0, GLA_DVP - GLA_DV)), GLA_HEADS)[None, :]
    a_out = p["a_w_out"][0]
    w["a_out"] = jnp.concatenate(
        [_pad_heads(a_out[:vv], GLA_HEADS, GLA_DV, GLA_DVP, 0), a_out[vv:]], axis=0).astype(BF)
    w["b_in"] = p["b_w_in"][0].astype(BF)
    w["b_out"] = p["b_w_out"][0].astype(BF)
    nkv = SWA_KV_HEADS * HEAD_DIM
    w_kv = p["w_kv"]
    w["kv"] = jnp.concatenate(
        [_pad_heads(w_kv[:, :nkv], SWA_KV_HEADS, HEAD_DIM, HEAD_DIMP, 1),
         _pad_heads(w_kv[:, nkv:], SWA_KV_HEADS, HEAD_DIM, HEAD_DIMP, 1)], axis=1).astype(BF)
    w["mem_t"] = p["mem_w_kv"].transpose(0, 2, 1).astype(BF)
    return w


def _compact_kv(a, batch, seq):
    return a.reshape(batch, seq, SWA_KV_HEADS, HEAD_DIMP)[..., :HEAD_DIM]


def kernel(x_prompt, x_sample, state_gla, cache_swa_k, cache_swa_v, cache_mem_k, cache_mem_v,
           mem_prompt, ffn1_norm, ffn1_w_gu, ffn1_w_down, mix_norm, ffn2_norm, ffn2_w_gu,
           ffn2_w_down, mem_norm, mem_w_kv, a_w_in, a_w_gate, a_b_gate, a_out_norm, a_w_out,
           kv_norm, w_kv, b_w_in, b_sinks, b_w_out, final_norm):
    p = dict(ffn1_norm=ffn1_norm, ffn1_w_gu=ffn1_w_gu, ffn1_w_down=ffn1_w_down,
             mix_norm=mix_norm, ffn2_norm=ffn2_norm, ffn2_w_gu=ffn2_w_gu,
             ffn2_w_down=ffn2_w_down, mem_w_kv=mem_w_kv, a_w_in=a_w_in, a_w_gate=a_w_gate,
             a_b_gate=a_b_gate, a_out_norm=a_out_norm, a_w_out=a_w_out, w_kv=w_kv,
             b_w_in=b_w_in, b_w_out=b_w_out)
    w = _prep_weights(p)
    batch, seq, _ = x_prompt.shape
    nb, t, _ = x_sample.shape
    assert nb == _DEC_BATCH and t == _DEC_SEQ and nb * t == TOKEN_TILE
    assert seq % TOKEN_TILE == 0 and seq % GLA_CHUNK == 0
    mp = batch * seq
    kw = SWA_KV_HEADS * HEAD_DIM
    sinks = b_sinks[0]

    mem_k_t, mem_v_t, wgu_a1, wd_a1 = _mem_kv(
        mem_prompt.reshape(batch * MEM_TOKENS, D_MODEL), mem_norm[:, None, :], w["mem_t"],
        [(ffn1_w_gu, 0, D_MODEL // batch), (ffn1_w_down, 0, FFN_DIM // batch)], batch=batch)
    state_t = state_gla.transpose(0, 2, 3, 4, 1)
    kc_t = cache_swa_k.transpose(0, 2, 3, 1).reshape(nb, kw, WINDOW)
    vc_t = cache_swa_v.transpose(0, 2, 3, 1).reshape(nb, kw, WINDOW)
    cmk_t = cache_mem_k.transpose(0, 1, 3, 4, 2).reshape(2, nb, MEM_Q, MEM_TOKENS)
    cmv_t = cache_mem_v.transpose(0, 1, 3, 4, 2).reshape(2, nb, MEM_Q, MEM_TOKENS)

    tabs = tuple(
        jnp.asarray(np.concatenate([a, np.tile(b, (nb, 1))], axis=0))
        for a, b in zip(_rope_tables(np.arange(seq)), _rope_tables(PAST_LEN + np.arange(t))))

    mem_mask = (np.arange(MEM_Q) // MEM_HEAD_DIM)[None, :] == np.arange(MEM_HEADS)[:, None]
    kv_mask = (np.arange(kw) // HEAD_DIM)[None, :] == np.arange(SWA_KV_HEADS)[:, None]

    def mem_attn_sample(mq, layer):
        q4 = mq[mp:].reshape(nb, 1, t, MEM_Q)
        qbd = jnp.where(mem_mask[None, :, None, :], q4, 0).reshape(nb, MEM_HEADS * t, MEM_Q)
        o = _mem_attn_sample(qbd, cmk_t, cmv_t, layer).reshape(nb, MEM_HEADS, t, MEM_Q)
        o = jnp.sum(jnp.where(mem_mask[None, :, None, :], o, 0.0), axis=1)
        return o.reshape(nb * t, MEM_Q).astype(BF)

    def lanes(a, width):
        return a[mp:].reshape(nb, t, GLA_HEADS, width).transpose(1, 2, 3, 0)

    x, q, k, la, v, r, mq = _layer_a_in(
        x_prompt.reshape(mp, D_MODEL), x_sample.reshape(nb * t, D_MODEL), w["ffn1_norm"],
        wgu_a1, wd_a1, w["mix_norm"], w["a_in"], w["a_gate"], w["a_bgate"], w["a_gn"])
    tok_p, st_p, wgu_a2, wd_a2 = _gla_prompt(
        q, k, la, v, r, [(ffn2_w_gu, 0, 32), (ffn2_w_down, 0, 128)], batch=batch, seq=seq)
    st_s, tok_s = _gla_sample(state_t, lanes(q, GLA_DKP), lanes(k, GLA_DKP), lanes(la, GLA_DKP),
                              lanes(v, GLA_DVP), lanes(r, GLA_DVP))
    tok_s = tok_s.transpose(3, 0, 1, 2).reshape(nb * t, GLA_HEADS * GLA_DVP).astype(BF)
    mo_s = mem_attn_sample(mq, 0)
    x, k_sh, v_sh, wgu_b1, wd_b1 = _layer_a_out(
        x, tok_p, tok_s, mq, mem_k_t, mem_v_t, mo_s, w["a_out"], w["ffn2_norm"], wgu_a2, wd_a2,
        kv_norm[None, :], w["kv"], tabs, [(ffn1_w_gu, 1, 32), (ffn1_w_down, 1, 128)], seq=seq)

    x, qs, mq, wgu_b2, wd_b2 = _layer_b_in(
        x, w["ffn1_norm"], wgu_b1, wd_b1, w["mix_norm"], w["b_in"], tabs,
        [(ffn2_w_gu, 1, 32), (ffn2_w_down, 1, 128)], n_prompt=mp // TOKEN_TILE, seq=seq)

    k_new = _compact_kv(k_sh[mp:], nb, t)
    v_new = _compact_kv(v_sh[mp:], nb, t)

    def new_rows(a):
        return jnp.pad(a.reshape(nb, t, kw), ((0, 0), (0, _NEW_ROWS - t), (0, 0)))

    q5 = qs[:, mp:].reshape(SWA_Q_PAIRS, nb, t, 2, HEAD_DIM).transpose(1, 0, 3, 2, 4)
    q5 = q5.reshape(nb, SWA_KV_HEADS, SWA_GROUP, t, HEAD_DIM)
    qbd = jnp.where(kv_mask[None, :, None, None, :], jnp.tile(q5, (1, 1, 1, 1, SWA_KV_HEADS)), 0)
    qbd = qbd.reshape(nb, SWA_Q_HEADS * t, kw)
    o, k_s, v_s = _swa_sample(qbd, jnp.repeat(sinks, t)[:, None], kc_t, vc_t,
                              new_rows(k_new), new_rows(v_new))
    o = o.reshape(nb, SWA_KV_HEADS, SWA_GROUP, t, kw)
    o = jnp.where(kv_mask[None, :, None, None, :], o, 0.0)
    o = o.reshape(nb, SWA_KV_HEADS, SWA_GROUP, t, SWA_KV_HEADS, HEAD_DIM).sum(axis=4)
    tok_s = o.reshape(nb, SWA_Q_PAIRS, 2, t, HEAD_DIM).transpose(1, 0, 3, 2, 4)
    tok_s = tok_s.reshape(SWA_Q_PAIRS, nb * t, HEAD_DIMP).astype(BF)
    mo_s = mem_attn_sample(mq, 1)
    y_p, y_s = _layer_b_out(sinks, x, qs, k_sh, v_sh, mq, mem_k_t, mem_v_t, tok_s, mo_s,
                            w["b_out"], w["ffn2_norm"], wgu_b2, wd_b2, final_norm[None, :],
                            seq=seq)

    gla_prompt = st_p.transpose(0, 1, 3, 2)[None, :, :, :GLA_DK, :GLA_DV]
    gla_sample = st_s.transpose(0, 4, 1, 2, 3)
    def last_window(a):
        tiles_per_seq = seq // TOKEN_TILE
        a = a.reshape(-1, TOKEN_TILE, SWA_KV_HEADS * HEAD_DIMP)
        a = a[tiles_per_seq - 1:batch * tiles_per_seq:tiles_per_seq, TOKEN_TILE - WINDOW:]
        return a.reshape(batch, WINDOW, SWA_KV_HEADS, HEAD_DIMP)[..., :HEAD_DIM]

    swa_k_prompt = last_window(k_sh)
    swa_v_prompt = last_window(v_sh)
    swa_k_sample = k_s.reshape(nb, SWA_KV_HEADS, HEAD_DIM, WINDOW).transpose(0, 3, 1, 2)
    swa_v_sample = v_s.reshape(nb, SWA_KV_HEADS, HEAD_DIM, WINDOW).transpose(0, 3, 1, 2)
    mem_shape = (2, batch, MEM_HEADS, MEM_HEAD_DIM, MEM_TOKENS)
    mem_k_prompt = mem_k_t.reshape(mem_shape).transpose(0, 1, 4, 2, 3)
    mem_v_prompt = mem_v_t.reshape(mem_shape).transpose(0, 1, 4, 2, 3)
    return (y_p.reshape(batch, seq, D_MODEL), y_s.reshape(nb, t, D_MODEL), gla_prompt,
            gla_sample, swa_k_prompt, swa_v_prompt, swa_k_sample, swa_v_sample,
            mem_k_prompt, mem_v_prompt)
```

```python
import functools
import math

import jax
import jax.numpy as jnp
import numpy as np
from jax import lax
from jax.experimental import pallas as pl
from jax.experimental.pallas import tpu as pltpu

F32 = jnp.float32
BF = jnp.bfloat16

D_MODEL = 1024
FFN_DIM = 2816
EPS = 1e-6

GLA_HEADS = 4
GLA_DK = 96
GLA_DV = 192
GLA_DKP = 128
GLA_DVP = 256
GLA_RANK = 16
GLA_RANKP = 128
GLA_GATE_NORM = 16.0
GLA_CHUNK = 256
GLA_SEQS = 2

HEAD_DIM = 64
HEAD_DIMP = 128
SWA_Q_HEADS = 12
SWA_KV_HEADS = 3
SWA_GROUP = SWA_Q_HEADS // SWA_KV_HEADS
SWA_Q_PAIRS = SWA_Q_HEADS // 2
SWA_PAIRS_PER_KV = SWA_GROUP // 2
WINDOW = 128
ROT_DIM = 16
ROPE_THETA = 500000.0
PAST_LEN = 8192

MEM_TOKENS = 256
MEM_HEADS = 4
MEM_HEAD_DIM = 64
MEM_Q = MEM_HEADS * MEM_HEAD_DIM

FFN_TF = 256
FFN_CHUNKS = FFN_DIM // FFN_TF
TOKEN_TILE = 512

VMEM_LIMIT = 60 * 1024 * 1024


def _params(*sem):
    return pltpu.CompilerParams(dimension_semantics=sem, vmem_limit_bytes=VMEM_LIMIT)


def _resident(shape):
    nd = len(shape)
    return pl.BlockSpec(shape, lambda *_: (0,) * nd, pipeline_mode=pl.Buffered(1))


def _layer_block(shape, layer):
    nd = len(shape)
    return pl.BlockSpec((1,) + tuple(shape[1:]), lambda *_: (layer,) + (0,) * (nd - 1),
                        pipeline_mode=pl.Buffered(1))


def _rows(width):
    return pl.BlockSpec((TOKEN_TILE, width), lambda i: (i, 0))


def _prompt_rows(width, n_prompt):
    return pl.BlockSpec((TOKEN_TILE, width), lambda i: (jnp.minimum(i, n_prompt - 1), 0))


def _sample_rows(width):
    return pl.BlockSpec((TOKEN_TILE, width), lambda i: (0, 0), pipeline_mode=pl.Buffered(1))


def _rms(x, g):
    ms = jnp.mean(x * x, axis=-1, keepdims=True)
    return x * lax.rsqrt(ms + EPS) * g


def _silu(x):
    return x * (1.0 / (1.0 + jnp.exp(-x)))


def _dot(a, b):
    return jnp.dot(a, b, preferred_element_type=F32)


def _dot_nt(a, b):
    return lax.dot_general(a, b, (((1,), (1,)), ((), ())), preferred_element_type=F32)


def _dot_tn(a, b):
    return lax.dot_general(a, b, (((0,), (0,)), ((), ())), preferred_element_type=F32)


def _pick(is_prompt, p_ref, s_ref):
    return jnp.where(is_prompt, p_ref[...], s_ref[...])


def _cast_job(w, layer, rows, grid):
    _, r, c = w.shape
    assert r % rows == 0 and rows % 16 == 0
    nblk = r // rows
    total = math.prod(grid)
    assert nblk <= total
    steps_per_block = total // nblk

    def block(*idx):
        step = idx[0]
        for dim, i in zip(grid[1:], idx[1:]):
            step = step * dim + i
        return jnp.minimum(step // steps_per_block, nblk - 1)

    in_spec = pl.BlockSpec((1, rows, c), lambda *idx: (layer, block(*idx), 0))
    out_spec = pl.BlockSpec((1, rows, c), lambda *idx: (0, block(*idx), 0))
    return in_spec, out_spec, jax.ShapeDtypeStruct((1, r, c), BF)


def _run_cast_jobs(refs):
    n = len(refs) // 2
    for src, dst in zip(refs[:n], refs[n:]):
        dst[...] = src[...].astype(BF)


def _ffn_half(x, g_ref, wgu_ref, wd_ref):
    h = _rms(x, g_ref[0]).astype(BF)
    acc = jnp.zeros(x.shape, F32)
    for c in range(FFN_CHUNKS):
        lo, hi = c * FFN_TF, (c + 1) * FFN_TF
        gate = _dot(h, wgu_ref[0, :, lo:hi])
        up = _dot(h, wgu_ref[0, :, FFN_DIM + lo:FFN_DIM + hi])
        a = (_silu(gate) * up).astype(BF)
        acc = acc + _dot(a, wd_ref[0, lo:hi, :])
    return x + 0.5 * acc


def _rope(x, c, s1, s2):
    return (x * c + pltpu.roll(x, HEAD_DIMP - ROT_DIM // 2, 1) * s1
            + pltpu.roll(x, ROT_DIM // 2, 1) * s2)


_A_SECTIONS = (("g", GLA_RANKP), ("r", GLA_HEADS * GLA_DVP), ("q", GLA_HEADS * GLA_DKP),
               ("k", GLA_HEADS * GLA_DKP), ("v", GLA_HEADS * GLA_DVP), ("m", MEM_Q))
_A_COLS = {}
for _name, _width in _A_SECTIONS:
    _lo = max((hi for _, hi in _A_COLS.values()), default=0)
    _A_COLS[_name] = (_lo, _lo + _width)


def _layer_a_in_kernel(xp_ref, xs_ref, g1_ref, wgu_ref, wd_ref, gm_ref, w_ref, wg_ref, bg_ref,
                       gn_ref, *refs, n_prompt):
    n = (len(refs) - 7) // 2
    x_ref, q_ref, k_ref, la_ref, v_ref, r_ref, mq_ref = refs[n:n + 7]
    _run_cast_jobs(refs[:n] + refs[n + 7:])
    is_prompt = pl.program_id(0) < n_prompt
    x = _ffn_half(_pick(is_prompt, xp_ref, xs_ref), g1_ref, wgu_ref, wd_ref)
    x_ref[...] = x
    h = _rms(x, gm_ref[0]).astype(BF)

    proj = _dot(h, w_ref[...])

    def mm(name):
        lo, hi = _A_COLS[name]
        return proj[:, lo:hi]

    z = _dot(mm("g").astype(BF), wg_ref[...]) + bg_ref[...]
    la_ref[...] = (jnp.minimum(z, 0.0) - jnp.log1p(jnp.exp(-jnp.abs(z)))) * (1.0 / GLA_GATE_NORM)
    r_ref[...] = _silu(mm("r")) * gn_ref[...]
    q_ref[...] = mm("q") * (GLA_DK ** -0.5)
    k_ref[...] = mm("k")
    v_ref[...] = mm("v")
    mq_ref[...] = (mm("m") * (MEM_HEAD_DIM ** -0.5)).astype(BF)


def _layer_a_in(xp, xs, g1, wgu, wd, gm, w, wg, bg, gn, casts):
    n_prompt = xp.shape[0] // TOKEN_TILE
    m = xp.shape[0] + xs.shape[0]
    grid = (m // TOKEN_TILE,)
    qk = GLA_HEADS * GLA_DKP
    vr = GLA_HEADS * GLA_DVP
    outs = ((D_MODEL, F32), (qk, F32), (qk, F32), (qk, F32), (vr, F32), (vr, F32), (MEM_Q, BF))
    jobs = [_cast_job(cw, layer, rows, grid) for cw, layer, rows in casts]
    return pl.pallas_call(
        functools.partial(_layer_a_in_kernel, n_prompt=n_prompt),
        grid=grid,
        in_specs=[_prompt_rows(D_MODEL, n_prompt), _sample_rows(D_MODEL),
                  _layer_block(g1.shape, 0), _layer_block(wgu.shape, 0), _layer_block(wd.shape, 0),
                  _layer_block(gm.shape, 0), _resident(w.shape), _resident(wg.shape),
                  _resident(bg.shape), _resident(gn.shape)] + [j[0] for j in jobs],
        out_specs=[_rows(n) for n, _ in outs] + [j[1] for j in jobs],
        out_shape=[jax.ShapeDtypeStruct((m, n), dt) for n, dt in outs] + [j[2] for j in jobs],
        compiler_params=_params("arbitrary"),
        name="layer_a_in",
    )(xp, xs, g1, wgu, wd, gm, w, wg, bg, gn, *[c[0] for c in casts])


def _build_mem_block_diag(mk_ref, mv_ref, kbd_scr, vbd_scr):
    shape = (MEM_Q, MEM_HEADS * MEM_TOKENS)
    rh = lax.broadcasted_iota(jnp.int32, shape, 0) // MEM_HEAD_DIM
    ch = lax.broadcasted_iota(jnp.int32, shape, 1) // MEM_TOKENS
    diag = rh == ch
    kbd_scr[...] = jnp.where(diag, jnp.concatenate([mk_ref[0, 0]] * MEM_HEADS, axis=1),
                             0.0).astype(BF)
    vbd_scr[...] = jnp.where(diag, jnp.concatenate([mv_ref[0, 0]] * MEM_HEADS, axis=1),
                             0.0).astype(BF)


def _mem_attn_tile(q, kbd, vbd):
    tq = q.shape[0]
    s = _dot(q, kbd)
    sh = [s[:, h * MEM_TOKENS:(h + 1) * MEM_TOKENS] for h in range(MEM_HEADS)]
    ms = [jnp.max(x, axis=1, keepdims=True) for x in sh]
    es = [jnp.exp(x - m) for x, m in zip(sh, ms)]
    inv = [1.0 / jnp.sum(e, axis=1, keepdims=True) for e in es]
    o = _dot_nt(jnp.concatenate([e.astype(BF) for e in es], axis=1), vbd)
    lane_h = lax.broadcasted_iota(jnp.int32, (tq, MEM_Q), 1) // MEM_HEAD_DIM
    scale = jnp.where(lane_h == 0, inv[0],
                      jnp.where(lane_h == 1, inv[1], jnp.where(lane_h == 2, inv[2], inv[3])))
    return o * scale


def _swa_tile(q_ref, kprev_ref, kcur_ref, vprev_ref, vcur_ref, sink_ref, first_of_seq):
    blk = WINDOW
    nsub = TOKEN_TILE // blk
    qi = lax.broadcasted_iota(jnp.int32, (blk, 2 * blk), 0)
    kj = lax.broadcasted_iota(jnp.int32, (blk, 2 * blk), 1)
    d = blk + qi - kj
    band = (d >= 0) & (d < WINDOW)
    bias_mid = jnp.where(band, 0.0, -jnp.inf)
    bias_first = jnp.where(band & (kj >= blk), 0.0, -jnp.inf)
    bias0 = jnp.where(first_of_seq, bias_first, bias_mid)
    outs = [[None] * SWA_Q_PAIRS for _ in range(nsub)]
    for kh in range(SWA_KV_HEADS):
        sl = slice(kh * HEAD_DIMP, (kh + 1) * HEAD_DIMP)
        kblk = [kprev_ref[:, sl]] + [kcur_ref[s * blk:(s + 1) * blk, sl] for s in range(nsub)]
        vblk = [vprev_ref[:, sl]] + [vcur_ref[s * blk:(s + 1) * blk, sl] for s in range(nsub)]
        k2 = [(b.astype(BF), pltpu.roll(b, HEAD_DIM, 1).astype(BF)) for b in kblk]
        v2 = [(b.astype(BF), pltpu.roll(b, HEAD_DIM, 1).astype(BF)) for b in vblk]
        for sub in range(nsub):
            bias = bias0 if sub == 0 else bias_mid
            for pj in range(SWA_PAIRS_PER_KV):
                pp = kh * SWA_PAIRS_PER_KV + pj
                q = q_ref[pp, sub * blk:(sub + 1) * blk, :]
                o = None
                for half in range(2):
                    kb = jnp.concatenate([k2[sub][half], k2[sub + 1][half]], axis=0)
                    vb = jnp.concatenate([v2[sub][half], v2[sub + 1][half]], axis=0)
                    s = _dot_nt(q, kb) + bias
                    sink = sink_ref[2 * pp + half]
                    m = jnp.maximum(jnp.max(s, axis=1, keepdims=True), sink)
                    e = jnp.exp(s - m)
                    l = jnp.sum(e, axis=1, keepdims=True) + jnp.exp(sink - m)
                    oh = _dot(e.astype(BF), vb) * (1.0 / l)
                    o = oh if o is None else o + oh
                outs[sub][pp] = o.astype(BF)
    return jnp.concatenate([jnp.concatenate(row, axis=1) for row in outs], axis=0)


def _layer_a_out_kernel(x_ref, tokp_ref, toks_ref, mq_ref, mk_ref, mv_ref, mos_ref, wo_ref,
                        g2_ref, wgu_ref, wd_ref, gkv_ref, wkv_ref, c_ref, s1_ref, s2_ref,
                        *refs, n_prompt, tiles_per_seq):
    n = (len(refs) - 5) // 2
    xo_ref, k_ref, v_ref = refs[n:n + 3]
    kbd_scr, vbd_scr = refs[-2:]
    _run_cast_jobs(refs[:n] + refs[n + 3:-2])
    i = pl.program_id(0)
    is_prompt = i < n_prompt

    @pl.when(i % tiles_per_seq == 0)
    def _():
        _build_mem_block_diag(mk_ref, mv_ref, kbd_scr, vbd_scr)

    nt = GLA_HEADS * GLA_DVP
    mo = _mem_attn_tile(mq_ref[...], kbd_scr[...], vbd_scr[...]).astype(BF)
    mo = jnp.where(is_prompt, mo, mos_ref[...])
    x = (x_ref[...] + _dot(_pick(is_prompt, tokp_ref, toks_ref), wo_ref[:nt, :])
         + _dot(mo, wo_ref[nt:, :]))
    x = _ffn_half(x, g2_ref, wgu_ref, wd_ref)
    xo_ref[...] = x
    h = _rms(x, gkv_ref[...]).astype(BF)
    c, s1, s2 = c_ref[...], s1_ref[...], s2_ref[...]
    kw = SWA_KV_HEADS * HEAD_DIMP
    kv = _dot(h, wkv_ref[...])
    for hh in range(SWA_KV_HEADS):
        sl = slice(hh * HEAD_DIMP, (hh + 1) * HEAD_DIMP)
        k_ref[:, sl] = _rope(kv[:, sl], c, s1, s2)
    v_ref[...] = kv[:, kw:]


def _tab_spec(n_prompt, blocks_per_seq):
    return pl.BlockSpec((TOKEN_TILE, HEAD_DIMP),
                        lambda i: (jnp.where(i < n_prompt, i % blocks_per_seq, blocks_per_seq), 0))


def _mem_kv_spec(layer, n_prompt, tiles_per_seq):
    return pl.BlockSpec(
        (1, 1, MEM_Q, MEM_TOKENS),
        lambda i: (layer, jnp.minimum(i, n_prompt - 1) // tiles_per_seq, 0, 0))


def _mem_scratch():
    return pltpu.VMEM((MEM_Q, MEM_HEADS * MEM_TOKENS), BF)


def _layer_a_out(x, tokp, toks, mq, mk_t, mv_t, mos, wo, g2, wgu, wd, gkv, wkv, tabs, casts,
                 *, seq):
    m = x.shape[0]
    grid = (m // TOKEN_TILE,)
    n_prompt = tokp.shape[0] // TOKEN_TILE
    tiles_per_seq = seq // TOKEN_TILE
    kw = SWA_KV_HEADS * HEAD_DIMP
    nt = GLA_HEADS * GLA_DVP
    tab = _tab_spec(n_prompt, tiles_per_seq)
    mem_spec = _mem_kv_spec(0, n_prompt, tiles_per_seq)
    jobs = [_cast_job(cw, layer, rows, grid) for cw, layer, rows in casts]
    return pl.pallas_call(
        functools.partial(_layer_a_out_kernel, n_prompt=n_prompt, tiles_per_seq=tiles_per_seq),
        grid=grid,
        in_specs=[_rows(D_MODEL), _prompt_rows(nt, n_prompt), _sample_rows(nt),
                  _rows(MEM_Q), mem_spec, mem_spec, _sample_rows(MEM_Q), _resident(wo.shape),
                  _layer_block(g2.shape, 0), _layer_block(wgu.shape, 0), _layer_block(wd.shape, 0),
                  _resident(gkv.shape), _resident(wkv.shape), tab, tab, tab]
                 + [j[0] for j in jobs],
        out_specs=[_rows(D_MODEL), _rows(kw), _rows(kw)] + [j[1] for j in jobs],
        out_shape=[jax.ShapeDtypeStruct((m, D_MODEL), F32),
                   jax.ShapeDtypeStruct((m, kw), F32), jax.ShapeDtypeStruct((m, kw), F32)]
                  + [j[2] for j in jobs],
        scratch_shapes=[_mem_scratch(), _mem_scratch()],
        compiler_params=_params("arbitrary"),
        name="layer_a_out",
    )(x, tokp, toks, mq, mk_t, mv_t, mos, wo, g2, wgu, wd, gkv, wkv, *tabs,
      *[c[0] for c in casts])


def _layer_b_in_kernel(x_ref, g1_ref, wgu_ref, wd_ref, gm_ref, w_ref, c_ref, s1_ref, s2_ref,
                       *refs):
    n = (len(refs) - 3) // 2
    xo_ref, q_ref, mq_ref = refs[n:n + 3]
    _run_cast_jobs(refs[:n] + refs[n + 3:])
    x = _ffn_half(x_ref[...], g1_ref, wgu_ref, wd_ref)
    xo_ref[...] = x
    h = _rms(x, gm_ref[0]).astype(BF)
    c, s1, s2 = c_ref[...], s1_ref[...], s2_ref[...]
    nq = SWA_Q_HEADS * HEAD_DIM
    qm = _dot(h, w_ref[...])
    for pp in range(SWA_Q_PAIRS):
        q = qm[:, pp * HEAD_DIMP:(pp + 1) * HEAD_DIMP]
        q_ref[pp] = (_rope(q, c, s1, s2) * (HEAD_DIM ** -0.5)).astype(BF)
    mq_ref[...] = (qm[:, nq:] * (MEM_HEAD_DIM ** -0.5)).astype(BF)


def _layer_b_in(x, g1, wgu, wd, gm, w, tabs, casts, *, n_prompt, seq):
    m = x.shape[0]
    grid = (m // TOKEN_TILE,)
    tab = _tab_spec(n_prompt, seq // TOKEN_TILE)
    q_spec = pl.BlockSpec((SWA_Q_PAIRS, TOKEN_TILE, HEAD_DIMP), lambda i: (0, i, 0))
    jobs = [_cast_job(cw, layer, rows, grid) for cw, layer, rows in casts]
    return pl.pallas_call(
        _layer_b_in_kernel,
        grid=grid,
        in_specs=[_rows(D_MODEL), _layer_block(g1.shape, 1), _layer_block(wgu.shape, 0),
                  _layer_block(wd.shape, 0), _layer_block(gm.shape, 1), _resident(w.shape),
                  tab, tab, tab] + [j[0] for j in jobs],
        out_specs=[_rows(D_MODEL), q_spec, _rows(MEM_Q)] + [j[1] for j in jobs],
        out_shape=[jax.ShapeDtypeStruct((m, D_MODEL), F32),
                   jax.ShapeDtypeStruct((SWA_Q_PAIRS, m, HEAD_DIMP), BF),
                   jax.ShapeDtypeStruct((m, MEM_Q), BF)] + [j[2] for j in jobs],
        compiler_params=_params("arbitrary"),
        name="layer_b_in",
    )(x, g1, wgu, wd, gm, w, *tabs, *[c[0] for c in casts])


def _layer_b_out_kernel(sink_ref, x_ref, q_ref, kprev_ref, kcur_ref, vprev_ref, vcur_ref,
                        mq_ref, mk_ref, mv_ref, toks_ref, mos_ref, wo_ref, g2_ref, wgu_ref,
                        wd_ref, gf_ref, yp_ref, ys_ref, kbd_scr, vbd_scr,
                        *, n_prompt, tiles_per_seq):
    i = pl.program_id(0)
    is_prompt = i < n_prompt
    first_of_seq = i % tiles_per_seq == 0

    @pl.when(first_of_seq)
    def _():
        _build_mem_block_diag(mk_ref, mv_ref, kbd_scr, vbd_scr)

    nt = SWA_Q_HEADS * HEAD_DIM
    mo = _mem_attn_tile(mq_ref[...], kbd_scr[...], vbd_scr[...]).astype(BF)
    mo = jnp.where(is_prompt, mo, mos_ref[...])
    tok = _swa_tile(q_ref, kprev_ref, kcur_ref, vprev_ref, vcur_ref, sink_ref, first_of_seq)
    toks = jnp.concatenate([toks_ref[pp] for pp in range(SWA_Q_PAIRS)], axis=1)
    tok = jnp.where(is_prompt, tok, toks)
    x = x_ref[...] + _dot(tok, wo_ref[:nt, :]) + _dot(mo, wo_ref[nt:, :])
    y = _rms(_ffn_half(x, g2_ref, wgu_ref, wd_ref), gf_ref[...])

    @pl.when(is_prompt)
    def _():
        yp_ref[...] = y

    @pl.when(jnp.logical_not(is_prompt))
    def _():
        ys_ref[...] = y


def _layer_b_out(sinks, x, qs, k_sh, v_sh, mq, mk_t, mv_t, toks, mos, wo, g2, wgu, wd, gf, *, seq):
    m = x.shape[0]
    n_tiles = m // TOKEN_TILE
    n_prompt = n_tiles - 1
    tiles_per_seq = seq // TOKEN_TILE
    kw = SWA_KV_HEADS * HEAD_DIMP
    blocks_per_tile = TOKEN_TILE // WINDOW
    q_spec = pl.BlockSpec((SWA_Q_PAIRS, TOKEN_TILE, HEAD_DIMP), lambda i: (0, i, 0))
    prev_spec = pl.BlockSpec((WINDOW, kw), lambda i: (jnp.maximum(i * blocks_per_tile - 1, 0), 0))
    toks_spec = pl.BlockSpec((SWA_Q_PAIRS, TOKEN_TILE, HEAD_DIMP), lambda i: (0, 0, 0),
                             pipeline_mode=pl.Buffered(1))
    mem_spec = _mem_kv_spec(1, n_prompt, tiles_per_seq)
    return pl.pallas_call(
        functools.partial(_layer_b_out_kernel, n_prompt=n_prompt, tiles_per_seq=tiles_per_seq),
        grid=(n_tiles,),
        in_specs=[pl.BlockSpec(memory_space=pltpu.SMEM), _rows(D_MODEL), q_spec,
                  prev_spec, _rows(kw), prev_spec, _rows(kw), _rows(MEM_Q), mem_spec, mem_spec,
                  toks_spec, _sample_rows(MEM_Q), _resident(wo.shape), _layer_block(g2.shape, 1),
                  _layer_block(wgu.shape, 0), _layer_block(wd.shape, 0), _resident(gf.shape)],
        out_specs=[_prompt_rows(D_MODEL, n_prompt),
                   pl.BlockSpec((TOKEN_TILE, D_MODEL), lambda i: (0, 0))],
        out_shape=[jax.ShapeDtypeStruct((n_prompt * TOKEN_TILE, D_MODEL), F32),
                   jax.ShapeDtypeStruct((TOKEN_TILE, D_MODEL), F32)],
        scratch_shapes=[_mem_scratch(), _mem_scratch()],
        compiler_params=_params("arbitrary"),
        name="layer_b_out",
    )(sinks, x, qs, k_sh, k_sh, v_sh, v_sh, mq, mk_t, mv_t, toks, mos, wo, g2, wgu, wd, gf)


def _mem_kv_kernel(x_ref, g_ref, wt_ref, *refs):
    n = (len(refs) - 2) // 2
    k_ref, v_ref = refs[n:n + 2]
    _run_cast_jobs(refs[:n] + refs[n + 2:])
    x = x_ref[...]
    xn = x * lax.rsqrt(jnp.mean(x * x, axis=-1, keepdims=True) + EPS)
    for l in range(2):
        h = (xn * g_ref[l]).astype(BF)
        kvt = _dot_nt(wt_ref[l], h)
        k_ref[l, 0] = kvt[:MEM_Q, :]
        v_ref[l, 0] = kvt[MEM_Q:, :]


def _mem_kv(mem, g, wt, casts, *, batch):
    out_spec = pl.BlockSpec((2, 1, MEM_Q, MEM_TOKENS), lambda b: (0, b, 0, 0))
    jobs = [_cast_job(cw, layer, rows, (batch,)) for cw, layer, rows in casts]
    return pl.pallas_call(
        _mem_kv_kernel,
        grid=(batch,),
        in_specs=[pl.BlockSpec((MEM_TOKENS, D_MODEL), lambda b: (b, 0)), _resident(g.shape),
                  _resident(wt.shape)] + [j[0] for j in jobs],
        out_specs=[out_spec, out_spec] + [j[1] for j in jobs],
        out_shape=[jax.ShapeDtypeStruct((2, batch, MEM_Q, MEM_TOKENS), F32)] * 2
                  + [j[2] for j in jobs],
        compiler_params=_params("arbitrary"),
        name="mem_kv",
    )(mem, g, wt, *[c[0] for c in casts])


def _gla_prompt_kernel(*refs):
    ns = GLA_SEQS
    seq_in = [refs[5 * i:5 * i + 5] for i in range(ns)]
    rest = refs[5 * ns:-1]
    s_scr = refs[-1]
    n = (len(rest) - 2) // 2
    tok_ref, st_ref = rest[n:n + 2]
    _run_cast_jobs(rest[:n] + rest[n + 2:])
    c = pl.program_id(1)
    C = GLA_CHUNK

    @pl.when(c == 0)
    def _():
        s_scr[...] = jnp.zeros(s_scr.shape, F32)

    row = lax.broadcasted_iota(jnp.int32, (C, C), 0)
    col = lax.broadcasted_iota(jnp.int32, (C, C), 1)
    causal = row >= col
    ltri = jnp.where(causal, 1.0, 0.0).astype(BF)
    _gla_chunks([(seq_in[i], tok_ref.at[i], s_scr.at[i], h)
                 for i in range(ns) for h in range(GLA_HEADS)], causal, ltri)

    @pl.when(c == pl.num_programs(1) - 1)
    def _():
        st_ref[...] = s_scr[...]


def _gla_chunks(chains, causal, ltri):
    C = GLA_CHUNK

    def sk(h):
        return slice(h * GLA_DKP, (h + 1) * GLA_DKP)

    def sv(h):
        return slice(h * GLA_DVP, (h + 1) * GLA_DVP)

    bs = []
    for (q_ref, k_ref, la_ref, v_ref, r_ref), _, _, h in chains:
        la = la_ref[:, sk(h)]
        hi = la.astype(BF)
        lo = (la - hi.astype(F32)).astype(BF)
        bb = _dot(ltri, jnp.concatenate([hi, lo], axis=1))
        bs.append(bb[:, :GLA_DKP] + bb[:, GLA_DKP:])
    ops = []
    for ((q_ref, k_ref, la_ref, v_ref, r_ref), _, _, h), b in zip(chains, bs):
        b_mid = b[C // 2 - 1:C // 2, :]
        b_last = b[C - 1:C, :]
        qe = q_ref[:, sk(h)] * jnp.exp(b - b_mid)
        ke = k_ref[:, sk(h)] * jnp.exp(b_mid - b)
        qb = (qe * jnp.exp(b_mid)).astype(BF)
        kd = (ke * jnp.exp(b_last - b_mid)).astype(BF)
        ops.append((qe.astype(BF), ke.astype(BF), qb, kd, jnp.exp(b_last)))
    As = [jnp.where(causal, _dot_nt(qe, ke), 0.0).astype(BF) for qe, ke, _, _, _ in ops]
    outs = []
    for ((q_ref, k_ref, la_ref, v_ref, r_ref), _, s_ref, h), a, (_, _, qb, kd, decay) in zip(
            chains, As, ops):
        v = v_ref[:, sv(h)].astype(BF)
        st = s_ref[h]
        outs.append(_dot(a, v) + _dot_nt(qb, st.astype(BF)))
        s_ref[h] = st * decay + _dot_tn(v, kd)
    for ((q_ref, k_ref, la_ref, v_ref, r_ref), tok_ref, _, h), o in zip(chains, outs):
        ms = jnp.sum(o * o, axis=1, keepdims=True) * (1.0 / GLA_DV)
        tok_ref[:, sv(h)] = (o * lax.rsqrt(ms + EPS) * r_ref[:, sv(h)]).astype(BF)


def _gla_prompt(q, k, la, v, r, casts, *, batch, seq):
    ns = GLA_SEQS
    m = batch * seq
    nc = seq // GLA_CHUNK
    qk = GLA_HEADS * GLA_DKP
    vr = GLA_HEADS * GLA_DVP
    grid = (batch // ns, nc)

    def tok_map(i):
        return lambda b, c: ((b * ns + i) * nc + c, 0)

    seq_specs = []
    for i in range(ns):
        seq_specs += [pl.BlockSpec((GLA_CHUNK, qk), tok_map(i))] * 3
        seq_specs += [pl.BlockSpec((GLA_CHUNK, vr), tok_map(i))] * 2
    jobs = [_cast_job(cw, layer, rows, grid) for cw, layer, rows in casts]
    res = pl.pallas_call(
        _gla_prompt_kernel,
        grid=grid,
        in_specs=seq_specs + [j[0] for j in jobs],
        out_specs=[pl.BlockSpec((ns, GLA_CHUNK, vr), lambda b, c: (b, c, 0)),
                   pl.BlockSpec((ns, GLA_HEADS, GLA_DVP, GLA_DKP), lambda b, c: (b, 0, 0, 0))]
                  + [j[1] for j in jobs],
        out_shape=[jax.ShapeDtypeStruct((batch, seq, vr), BF),
                   jax.ShapeDtypeStruct((batch, GLA_HEADS, GLA_DVP, GLA_DKP), F32)]
                  + [j[2] for j in jobs],
        scratch_shapes=[pltpu.VMEM((ns, GLA_HEADS, GLA_DVP, GLA_DKP), F32)],
        compiler_params=_params("arbitrary", "arbitrary"),
        name="gla_prompt",
    )(*([q, k, la, v, r] * ns), *[c[0] for c in casts])
    return (res[0].reshape(m, vr),) + tuple(res[1:])


_SAMPLE_BB = 8
_DEC_SEQ = 4
_DEC_BATCH = 128
_GLA_DK_BLK = 32
_NEW_ROWS = 16


def _gla_sample_kernel(s_ref, q_ref, k_ref, la_ref, v_ref, r_ref, so_ref, tok_ref, o_scr):
    j = pl.program_id(1)

    @pl.when(j == 0)
    def _():
        o_scr[...] = jnp.zeros(o_scr.shape, F32)

    def body(dk, carry):
        s = s_ref[0, 0, dk]
        for t in range(_DEC_SEQ):
            a = jnp.exp(la_ref[t, 0, pl.ds(dk, 1), :])
            s = a * s + k_ref[t, 0, pl.ds(dk, 1), :] * v_ref[t, 0, :GLA_DV, :]
            o_scr[t] = o_scr[t] + q_ref[t, 0, pl.ds(dk, 1), :] * s
        so_ref[0, 0, dk] = s
        return carry

    lax.fori_loop(0, _GLA_DK_BLK, body, 0, unroll=4)

    @pl.when(j == pl.num_programs(1) - 1)
    def _():
        tok_ref[...] = jnp.zeros(tok_ref.shape, F32)
        for t in range(_DEC_SEQ):
            o = o_scr[t]
            ms = jnp.sum(o * o, axis=0, keepdims=True) * (1.0 / GLA_DV)
            tok_ref[t, 0, :GLA_DV, :] = o * lax.rsqrt(ms + EPS) * r_ref[t, 0, :GLA_DV, :]


def _gla_sample(state, q, k, la, v, r):
    qk_spec = pl.BlockSpec((_DEC_SEQ, 1, _GLA_DK_BLK, _DEC_BATCH), lambda h, j: (0, h, j, 0))
    vr_spec = pl.BlockSpec((_DEC_SEQ, 1, GLA_DVP, _DEC_BATCH), lambda h, j: (0, h, 0, 0))
    s_spec = pl.BlockSpec((1, 1, _GLA_DK_BLK, GLA_DV, _DEC_BATCH), lambda h, j: (0, h, j, 0, 0))
    return pl.pallas_call(
        _gla_sample_kernel,
        grid=(GLA_HEADS, GLA_DK // _GLA_DK_BLK),
        in_specs=[s_spec, qk_spec, qk_spec, qk_spec, vr_spec, vr_spec],
        out_specs=[s_spec, vr_spec],
        out_shape=[jax.ShapeDtypeStruct(state.shape, F32),
                   jax.ShapeDtypeStruct(v.shape, F32)],
        scratch_shapes=[pltpu.VMEM((_DEC_SEQ, GLA_DV, _DEC_BATCH), F32)],
        compiler_params=_params("parallel", "arbitrary"),
        name="gla_sample",
    )(state, q, k, la, v, r)


def _mem_attn_sample_kernel(q_ref, mk_ref, mv_ref, o_ref):
    scores = [_dot(q_ref[bi], mk_ref[0, bi].astype(BF)) for bi in range(_SAMPLE_BB)]
    probs = []
    for s in scores:
        e = jnp.exp(s - jnp.max(s, axis=1, keepdims=True))
        probs.append((e * (1.0 / jnp.sum(e, axis=1, keepdims=True))).astype(BF))
    for bi, p in enumerate(probs):
        o_ref[bi] = _dot_nt(p, mv_ref[0, bi].astype(BF))


def _mem_attn_sample(qbd, mk_t, mv_t, layer):
    nb = qbd.shape[0]
    nr = MEM_HEADS * _DEC_SEQ
    kv_spec = pl.BlockSpec((1, _SAMPLE_BB, MEM_Q, MEM_TOKENS), lambda i: (layer, i, 0, 0))
    q_spec = pl.BlockSpec((_SAMPLE_BB, nr, MEM_Q), lambda i: (i, 0, 0))
    return pl.pallas_call(
        _mem_attn_sample_kernel,
        grid=(nb // _SAMPLE_BB,),
        in_specs=[q_spec, kv_spec, kv_spec],
        out_specs=q_spec,
        out_shape=jax.ShapeDtypeStruct((nb, nr, MEM_Q), F32),
        compiler_params=_params("parallel"),
        name="mem_attn_sample",
    )(qbd, mk_t, mv_t)


def _swa_sample_kernel(q_ref, sink_ref, kc_ref, vc_ref, kn_ref, vn_ref, o_ref, ko_ref, vo_ref):
    nq = SWA_Q_HEADS * _DEC_SEQ
    t = lax.broadcasted_iota(jnp.int32, (nq, WINDOW), 0) % _DEC_SEQ
    pos = lax.broadcasted_iota(jnp.int32, (nq, WINDOW), 1)
    bias_c = jnp.where(pos > t, 0.0, -jnp.inf)
    tn = lax.broadcasted_iota(jnp.int32, (nq, _NEW_ROWS), 0) % _DEC_SEQ
    new = lax.broadcasted_iota(jnp.int32, (nq, _NEW_ROWS), 1)
    bias_n = jnp.where(new <= tn, 0.0, -jnp.inf)
    sink = sink_ref[...]
    seqs = range(_SAMPLE_BB)
    scores = [(_dot(q_ref[bi], kc_ref[bi].astype(BF)) + bias_c,
               _dot_nt(q_ref[bi], kn_ref[bi].astype(BF)) + bias_n) for bi in seqs]
    probs = []
    for sc, sn in scores:
        m = jnp.maximum(jnp.maximum(jnp.max(sc, axis=1, keepdims=True),
                                    jnp.max(sn, axis=1, keepdims=True)), sink)
        ec = jnp.exp(sc - m)
        en = jnp.exp(sn - m)
        l = (jnp.sum(ec, axis=1, keepdims=True) + jnp.sum(en, axis=1, keepdims=True)
             + jnp.exp(sink - m))
        inv = 1.0 / l
        probs.append(((ec * inv).astype(BF), (en * inv).astype(BF)))
    for bi, (pc, pn) in zip(seqs, probs):
        o_ref[bi] = (_dot_nt(pc, vc_ref[bi].astype(BF)) + _dot(pn, vn_ref[bi].astype(BF)))
    new0 = WINDOW - _DEC_SEQ
    row = lax.broadcasted_iota(jnp.int32, (_NEW_ROWS, WINDOW), 0)
    col = lax.broadcasted_iota(jnp.int32, (_NEW_ROWS, WINDOW), 1)
    place = jnp.where((col == new0 + row) & (row < _DEC_SEQ), 1.0, 0.0)
    is_new = lax.broadcasted_iota(jnp.int32, (SWA_KV_HEADS * HEAD_DIM, WINDOW), 1) >= new0

    def placed(new_rows):
        return lax.dot_general(new_rows, place, (((0,), (0,)), ((), ())),
                               precision=lax.Precision.HIGHEST, preferred_element_type=F32)

    for bi in seqs:
        ko_ref[bi] = jnp.where(is_new, placed(kn_ref[bi]), pltpu.roll(kc_ref[bi], new0, 1))
        vo_ref[bi] = jnp.where(is_new, placed(vn_ref[bi]), pltpu.roll(vc_ref[bi], new0, 1))


def _swa_sample(qbd, sink_col, kc, vc, kn, vn):
    nb = qbd.shape[0]
    kw = SWA_KV_HEADS * HEAD_DIM
    nq = SWA_Q_HEADS * _DEC_SEQ
    kv_spec = pl.BlockSpec((_SAMPLE_BB, kw, WINDOW), lambda i: (i, 0, 0))
    new_spec = pl.BlockSpec((_SAMPLE_BB, _NEW_ROWS, kw), lambda i: (i, 0, 0))
    q_spec = pl.BlockSpec((_SAMPLE_BB, nq, kw), lambda i: (i, 0, 0))
    return pl.pallas_call(
        _swa_sample_kernel,
        grid=(nb // _SAMPLE_BB,),
        in_specs=[q_spec, _resident((nq, 1)), kv_spec, kv_spec, new_spec, new_spec],
        out_specs=[q_spec, kv_spec, kv_spec],
        out_shape=[jax.ShapeDtypeStruct((nb, nq, kw), F32),
                   jax.ShapeDtypeStruct(kc.shape, F32), jax.ShapeDtypeStruct(kc.shape, F32)],
        compiler_params=_params("parallel"),
        name="swa_sample",
    )(qbd, sink_col, kc, vc, kn, vn)


def _pad_heads(w, heads, dim, dim_p, axis):
    shape = w.shape
    w = w.reshape(shape[:axis] + (heads, dim) + shape[axis + 1:])
    pad = [(0, 0)] * w.ndim
    pad[axis + 1] = (0, dim_p - dim)
    w = jnp.pad(w, pad)
    return w.reshape(shape[:axis] + (heads * dim_p,) + shape[axis + 1:])


def _rope_tables(pos):
    half = ROT_DIM // 2
    inv_freq = np.exp(-math.log(ROPE_THETA) * np.arange(0, ROT_DIM, 2, dtype=np.float64) / ROT_DIM)
    ang = pos.astype(np.float64)[:, None] * inv_freq[None, :]
    cos, sin = np.cos(ang), np.sin(ang)
    n = pos.shape[0]
    rest = HEAD_DIM - ROT_DIM
    c = np.concatenate([cos, cos, np.ones((n, rest))], axis=1)
    s1 = np.concatenate([-sin, np.zeros((n, HEAD_DIM - half))], axis=1)
    s2 = np.concatenate([np.zeros((n, half)), sin, np.zeros((n, rest))], axis=1)
    return tuple(np.tile(a, (1, HEAD_DIMP // HEAD_DIM)).astype(np.float32) for a in (c, s1, s2))


def _prep_weights(p):
    w = {}
    for name in ("ffn1_norm", "ffn2_norm", "mix_norm"):
        w[name] = p[name][:, None, :]
    qk = GLA_HEADS * GLA_DK
    vv = GLA_HEADS * GLA_DV
    a_in = p["a_w_in"][0]
    o = 0
    wq = _pad_heads(a_in[:, o:o + qk], GLA_HEADS, GLA_DK, GLA_DKP, 1); o += qk
    wk = _pad_heads(a_in[:, o:o + qk], GLA_HEADS, GLA_DK, GLA_DKP, 1); o += qk
    wv = _pad_heads(a_in[:, o:o + vv], GLA_HEADS, GLA_DV, GLA_DVP, 1); o += vv
    wr = _pad_heads(a_in[:, o:o + vv], GLA_HEADS, GLA_DV, GLA_DVP, 1); o += vv
    wg = jnp.pad(a_in[:, o:o + GLA_RANK], ((0, 0), (0, GLA_RANKP - GLA_RANK))); o += GLA_RANK
    wm = a_in[:, o:]
    parts = dict(q=wq, k=wk, v=wv, r=wr, g=wg, m=wm)
    w["a_in"] = jnp.concatenate([parts[name] for name, _ in _A_SECTIONS], axis=1).astype(BF)
    gate = _pad_heads(p["a_w_gate"][0], GLA_HEADS, GLA_DK, GLA_DKP, 1)
    w["a_gate"] = jnp.pad(gate, ((0, GLA_RANKP - GLA_RANK), (0, 0))).astype(BF)
    w["a_bgate"] = _pad_heads(p["a_b_gate"][0][None, :], GLA_HEADS, GLA_DK, GLA_DKP, 1)
    w["a_gn"] = jnp.tile(jnp.pad(p["a_out_norm"][0], (0, GLA_DVP - GLA_DV)), GLA_HEADS)[None, :]
    a_out = p["a_w_out"][0]
    w["a_out"] = jnp.concatenate(
        [_pad_heads(a_out[:vv], GLA_HEADS, GLA_DV, GLA_DVP, 0), a_out[vv:]], axis=0).astype(BF)
    w["b_in"] = p["b_w_in"][0].astype(BF)
    w["b_out"] = p["b_w_out"][0].astype(BF)
    nkv = SWA_KV_HEADS * HEAD_DIM
    w_kv = p["w_kv"]
    w["kv"] = jnp.concatenate(
        [_pad_heads(w_kv[:, :nkv], SWA_KV_HEADS, HEAD_DIM, HEAD_DIMP, 1),
         _pad_heads(w_kv[:, nkv:], SWA_KV_HEADS, HEAD_DIM, HEAD_DIMP, 1)], axis=1).astype(BF)
    w["mem_t"] = p["mem_w_kv"].transpose(0, 2, 1).astype(BF)
    return w


def _compact_kv(a, batch, seq):
    return a.reshape(batch, seq, SWA_KV_HEADS, HEAD_DIMP)[..., :HEAD_DIM]


def kernel(x_prompt, x_sample, state_gla, cache_swa_k, cache_swa_v, cache_mem_k, cache_mem_v,
           mem_prompt, ffn1_norm, ffn1_w_gu, ffn1_w_down, mix_norm, ffn2_norm, ffn2_w_gu,
           ffn2_w_down, mem_norm, mem_w_kv, a_w_in, a_w_gate, a_b_gate, a_out_norm, a_w_out,
           kv_norm, w_kv, b_w_in, b_sinks, b_w_out, final_norm):
    p = dict(ffn1_norm=ffn1_norm, ffn1_w_gu=ffn1_w_gu, ffn1_w_down=ffn1_w_down,
             mix_norm=mix_norm, ffn2_norm=ffn2_norm, ffn2_w_gu=ffn2_w_gu,
             ffn2_w_down=ffn2_w_down, mem_w_kv=mem_w_kv, a_w_in=a_w_in, a_w_gate=a_w_gate,
             a_b_gate=a_b_gate, a_out_norm=a_out_norm, a_w_out=a_w_out, w_kv=w_kv,
             b_w_in=b_w_in, b_w_out=b_w_out)
    w = _prep_weights(p)
    batch, seq, _ = x_prompt.shape
    nb, t, _ = x_sample.shape
    assert nb == _DEC_BATCH and t == _DEC_SEQ and nb * t == TOKEN_TILE
    assert seq % TOKEN_TILE == 0 and seq % GLA_CHUNK == 0
    mp = batch * seq
    kw = SWA_KV_HEADS * HEAD_DIM
    sinks = b_sinks[0]

    mem_k_t, mem_v_t, wgu_a1, wd_a1 = _mem_kv(
        mem_prompt.reshape(batch * MEM_TOKENS, D_MODEL), mem_norm[:, None, :], w["mem_t"],
        [(ffn1_w_gu, 0, D_MODEL // batch), (ffn1_w_down, 0, FFN_DIM // batch)], batch=batch)
    state_t = state_gla.transpose(0, 2, 3, 4, 1)
    kc_t = cache_swa_k.transpose(0, 2, 3, 1).reshape(nb, kw, WINDOW)
    vc_t = cache_swa_v.transpose(0, 2, 3, 1).reshape(nb, kw, WINDOW)
    cmk_t = cache_mem_k.transpose(0, 1, 3, 4, 2).reshape(2, nb, MEM_Q, MEM_TOKENS)
    cmv_t = cache_mem_v.transpose(0, 1, 3, 4, 2).reshape(2, nb, MEM_Q, MEM_TOKENS)

    tabs = tuple(
        jnp.asarray(np.concatenate([a, np.tile(b, (nb, 1))], axis=0))
        for a, b in zip(_rope_tables(np.arange(seq)), _rope_tables(PAST_LEN + np.arange(t))))

    mem_mask = (np.arange(MEM_Q) // MEM_HEAD_DIM)[None, :] == np.arange(MEM_HEADS)[:, None]
    kv_mask = (np.arange(kw) // HEAD_DIM)[None, :] == np.arange(SWA_KV_HEADS)[:, None]

    def mem_attn_sample(mq, layer):
        q4 = mq[mp:].reshape(nb, 1, t, MEM_Q)
        qbd = jnp.where(mem_mask[None, :, None, :], q4, 0).reshape(nb, MEM_HEADS * t, MEM_Q)
        o = _mem_attn_sample(qbd, cmk_t, cmv_t, layer).reshape(nb, MEM_HEADS, t, MEM_Q)
        o = jnp.sum(jnp.where(mem_mask[None, :, None, :], o, 0.0), axis=1)
        return o.reshape(nb * t, MEM_Q).astype(BF)

    def lanes(a, width):
        return a[mp:].reshape(nb, t, GLA_HEADS, width).transpose(1, 2, 3, 0)

    x, q, k, la, v, r, mq, wgu_a2, wd_a2 = _layer_a_in(
        x_prompt.reshape(mp, D_MODEL), x_sample.reshape(nb * t, D_MODEL), w["ffn1_norm"],
        wgu_a1, wd_a1, w["mix_norm"], w["a_in"], w["a_gate"], w["a_bgate"], w["a_gn"],
        [(ffn2_w_gu, 0, 32), (ffn2_w_down, 0, 128)])
    tok_p, st_p = _gla_prompt(q, k, la, v, r, [], batch=batch, seq=seq)
    st_s, tok_s = _gla_sample(state_t, lanes(q, GLA_DKP), lanes(k, GLA_DKP), lanes(la, GLA_DKP),
                              lanes(v, GLA_DVP), lanes(r, GLA_DVP))
    tok_s = tok_s.transpose(3, 0, 1, 2).reshape(nb * t, GLA_HEADS * GLA_DVP).astype(BF)
    mo_s = mem_attn_sample(mq, 0)
    x, k_sh, v_sh, wgu_b1, wd_b1 = _layer_a_out(
        x, tok_p, tok_s, mq, mem_k_t, mem_v_t, mo_s, w["a_out"], w["ffn2_norm"], wgu_a2, wd_a2,
        kv_norm[None, :], w["kv"], tabs, [(ffn1_w_gu, 1, 32), (ffn1_w_down, 1, 128)], seq=seq)

    x, qs, mq, wgu_b2, wd_b2 = _layer_b_in(
        x, w["ffn1_norm"], wgu_b1, wd_b1, w["mix_norm"], w["b_in"], tabs,
        [(ffn2_w_gu, 1, 32), (ffn2_w_down, 1, 128)], n_prompt=mp // TOKEN_TILE, seq=seq)

    k_new = _compact_kv(k_sh[mp:], nb, t)
    v_new = _compact_kv(v_sh[mp:], nb, t)

    def new_rows(a):
        return jnp.pad(a.reshape(nb, t, kw), ((0, 0), (0, _NEW_ROWS - t), (0, 0)))

    q5 = qs[:, mp:].reshape(SWA_Q_PAIRS, nb, t, 2, HEAD_DIM).transpose(1, 0, 3, 2, 4)
    q5 = q5.reshape(nb, SWA_KV_HEADS, SWA_GROUP, t, HEAD_DIM)
    qbd = jnp.where(kv_mask[None, :, None, None, :], jnp.tile(q5, (1, 1, 1, 1, SWA_KV_HEADS)), 0)
    qbd = qbd.reshape(nb, SWA_Q_HEADS * t, kw)
    o, k_s, v_s = _swa_sample(qbd, jnp.repeat(sinks, t)[:, None], kc_t, vc_t,
                              new_rows(k_new), new_rows(v_new))
    o = o.reshape(nb, SWA_KV_HEADS, SWA_GROUP, t, kw)
    o = jnp.where(kv_mask[None, :, None, None, :], o, 0.0)
    o = o.reshape(nb, SWA_KV_HEADS, SWA_GROUP, t, SWA_KV_HEADS, HEAD_DIM).sum(axis=4)
    tok_s = o.reshape(nb, SWA_Q_PAIRS, 2, t, HEAD_DIM).transpose(1, 0, 3, 2, 4)
    tok_s = tok_s.reshape(SWA_Q_PAIRS, nb * t, HEAD_DIMP).astype(BF)
    mo_s = mem_attn_sample(mq, 1)
    y_p, y_s = _layer_b_out(sinks, x, qs, k_sh, v_sh, mq, mem_k_t, mem_v_t, tok_s, mo_s,
                            w["b_out"], w["ffn2_norm"], wgu_b2, wd_b2, final_norm[None, :],
                            seq=seq)

    gla_prompt = st_p.transpose(0, 1, 3, 2)[None, :, :, :GLA_DK, :GLA_DV]
    gla_sample = st_s.transpose(0, 4, 1, 2, 3)
    def last_window(a):
        tiles_per_seq = seq // TOKEN_TILE
        a = a.reshape(-1, TOKEN_TILE, SWA_KV_HEADS * HEAD_DIMP)
        a = a[tiles_per_seq - 1:batch * tiles_per_seq:tiles_per_seq, TOKEN_TILE - WINDOW:]
        return a.reshape(batch, WINDOW, SWA_KV_HEADS, HEAD_DIMP)[..., :HEAD_DIM]

    swa_k_prompt = last_window(k_sh)
    swa_v_prompt = last_window(v_sh)
    swa_k_sample = k_s.reshape(nb, SWA_KV_HEADS, HEAD_DIM, WINDOW).transpose(0, 3, 1, 2)
    swa_v_sample = v_s.reshape(nb, SWA_KV_HEADS, HEAD_DIM, WINDOW).transpose(0, 3, 1, 2)
    mem_shape = (2, batch, MEM_HEADS, MEM_HEAD_DIM, MEM_TOKENS)
    mem_k_prompt = mem_k_t.reshape(mem_shape).transpose(0, 1, 4, 2, 3)
    mem_v_prompt = mem_v_t.reshape(mem_shape).transpose(0, 1, 4, 2, 3)
    return (y_p.reshape(batch, seq, D_MODEL), y_s.reshape(nb, t, D_MODEL), gla_prompt,
            gla_sample, swa_k_prompt, swa_v_prompt, swa_k_sample, swa_v_sample,
            mem_k_prompt, mem_v_prompt)
```

```python
import functools
import math

import jax
import jax.numpy as jnp
import numpy as np
from jax import lax
from jax.experimental import pallas as pl
from jax.experimental.pallas import tpu as pltpu

F32 = jnp.float32
BF = jnp.bfloat16

D_MODEL = 1024
FFN_DIM = 2816
EPS = 1e-6

GLA_HEADS = 4
GLA_DK = 96
GLA_DV = 192
GLA_DKP = 128
GLA_DVP = 256
GLA_RANK = 16
GLA_RANKP = 128
GLA_GATE_NORM = 16.0
GLA_CHUNK = 256
GLA_SEQS = 2

HEAD_DIM = 64
HEAD_DIMP = 128
SWA_Q_HEADS = 12
SWA_KV_HEADS = 3
SWA_GROUP = SWA_Q_HEADS // SWA_KV_HEADS
SWA_Q_PAIRS = SWA_Q_HEADS // 2
SWA_PAIRS_PER_KV = SWA_GROUP // 2
WINDOW = 128
ROT_DIM = 16
ROPE_THETA = 500000.0
PAST_LEN = 8192

MEM_TOKENS = 256
MEM_HEADS = 4
MEM_HEAD_DIM = 64
MEM_Q = MEM_HEADS * MEM_HEAD_DIM

FFN_TF = 256
FFN_CHUNKS = FFN_DIM // FFN_TF
TOKEN_TILE = 512

VMEM_LIMIT = 60 * 1024 * 1024


def _params(*sem):
    return pltpu.CompilerParams(dimension_semantics=sem, vmem_limit_bytes=VMEM_LIMIT)


def _resident(shape):
    nd = len(shape)
    return pl.BlockSpec(shape, lambda *_: (0,) * nd, pipeline_mode=pl.Buffered(1))


def _layer_block(shape, layer):
    nd = len(shape)
    return pl.BlockSpec((1,) + tuple(shape[1:]), lambda *_: (layer,) + (0,) * (nd - 1),
                        pipeline_mode=pl.Buffered(1))


def _rows(width):
    return pl.BlockSpec((TOKEN_TILE, width), lambda i: (i, 0))


def _prompt_rows(width, n_prompt):
    return pl.BlockSpec((TOKEN_TILE, width), lambda i: (jnp.minimum(i, n_prompt - 1), 0))


def _sample_rows(width):
    return pl.BlockSpec((TOKEN_TILE, width), lambda i: (0, 0), pipeline_mode=pl.Buffered(1))


def _rms(x, g):
    ms = jnp.mean(x * x, axis=-1, keepdims=True)
    return x * lax.rsqrt(ms + EPS) * g


def _silu(x):
    return x * (1.0 / (1.0 + jnp.exp(-x)))


def _dot(a, b):
    return jnp.dot(a, b, preferred_element_type=F32)


def _dot_nt(a, b):
    return lax.dot_general(a, b, (((1,), (1,)), ((), ())), preferred_element_type=F32)


def _dot_tn(a, b):
    return lax.dot_general(a, b, (((0,), (0,)), ((), ())), preferred_element_type=F32)


def _pick(is_prompt, p_ref, s_ref):
    return jnp.where(is_prompt, p_ref[...], s_ref[...])


def _cast_job(w, layer, rows, grid):
    _, r, c = w.shape
    assert r % rows == 0 and rows % 16 == 0
    nblk = r // rows
    total = math.prod(grid)
    assert nblk <= total
    steps_per_block = total // nblk

    def block(*idx):
        step = idx[0]
        for dim, i in zip(grid[1:], idx[1:]):
            step = step * dim + i
        return jnp.minimum(step // steps_per_block, nblk - 1)

    in_spec = pl.BlockSpec((1, rows, c), lambda *idx: (layer, block(*idx), 0))
    out_spec = pl.BlockSpec((1, rows, c), lambda *idx: (0, block(*idx), 0))
    return in_spec, out_spec, jax.ShapeDtypeStruct((1, r, c), BF)


def _run_cast_jobs(refs):
    n = len(refs) // 2
    for src, dst in zip(refs[:n], refs[n:]):
        dst[...] = src[...].astype(BF)


def _ffn_half(x, g_ref, wgu_ref, wd_ref):
    h = _rms(x, g_ref[0]).astype(BF)
    acc = jnp.zeros(x.shape, F32)
    for c in range(FFN_CHUNKS):
        lo, hi = c * FFN_TF, (c + 1) * FFN_TF
        gate = _dot(h, wgu_ref[0, :, lo:hi])
        up = _dot(h, wgu_ref[0, :, FFN_DIM + lo:FFN_DIM + hi])
        a = (_silu(gate) * up).astype(BF)
        acc = acc + _dot(a, wd_ref[0, lo:hi, :])
    return x + 0.5 * acc


def _rope(x, c, s1, s2):
    return (x * c + pltpu.roll(x, HEAD_DIMP - ROT_DIM // 2, 1) * s1
            + pltpu.roll(x, ROT_DIM // 2, 1) * s2)


_A_SECTIONS = (("q", GLA_HEADS * GLA_DKP), ("k", GLA_HEADS * GLA_DKP),
               ("v", GLA_HEADS * GLA_DVP), ("r", GLA_HEADS * GLA_DVP),
               ("g", GLA_RANKP), ("m", MEM_Q))
_A_COLS = {}
for _name, _width in _A_SECTIONS:
    _lo = max((hi for _, hi in _A_COLS.values()), default=0)
    _A_COLS[_name] = (_lo, _lo + _width)


def _layer_a_in_kernel(xp_ref, xs_ref, g1_ref, wgu_ref, wd_ref, gm_ref, w_ref, wg_ref, bg_ref,
                       gn_ref, *refs, n_prompt):
    n = (len(refs) - 7) // 2
    x_ref, q_ref, k_ref, la_ref, v_ref, r_ref, mq_ref = refs[n:n + 7]
    _run_cast_jobs(refs[:n] + refs[n + 7:])
    is_prompt = pl.program_id(0) < n_prompt
    x = _ffn_half(_pick(is_prompt, xp_ref, xs_ref), g1_ref, wgu_ref, wd_ref)
    x_ref[...] = x
    h = _rms(x, gm_ref[0]).astype(BF)

    proj = _dot(h, w_ref[...])

    def mm(name):
        lo, hi = _A_COLS[name]
        return proj[:, lo:hi]

    z = _dot(mm("g").astype(BF), wg_ref[...]) + bg_ref[...]
    la_ref[...] = (jnp.minimum(z, 0.0) - jnp.log1p(jnp.exp(-jnp.abs(z)))) * (1.0 / GLA_GATE_NORM)
    r_ref[...] = _silu(mm("r")) * gn_ref[...]
    q_ref[...] = mm("q") * (GLA_DK ** -0.5)
    k_ref[...] = mm("k")
    v_ref[...] = mm("v")
    mq_ref[...] = (mm("m") * (MEM_HEAD_DIM ** -0.5)).astype(BF)


def _layer_a_in(xp, xs, g1, wgu, wd, gm, w, wg, bg, gn, casts):
    n_prompt = xp.shape[0] // TOKEN_TILE
    m = xp.shape[0] + xs.shape[0]
    grid = (m // TOKEN_TILE,)
    qk = GLA_HEADS * GLA_DKP
    vr = GLA_HEADS * GLA_DVP
    outs = ((D_MODEL, F32), (qk, F32), (qk, F32), (qk, F32), (vr, F32), (vr, F32), (MEM_Q, BF))
    jobs = [_cast_job(cw, layer, rows, grid) for cw, layer, rows in casts]
    return pl.pallas_call(
        functools.partial(_layer_a_in_kernel, n_prompt=n_prompt),
        grid=grid,
        in_specs=[_prompt_rows(D_MODEL, n_prompt), _sample_rows(D_MODEL),
                  _layer_block(g1.shape, 0), _layer_block(wgu.shape, 0), _layer_block(wd.shape, 0),
                  _layer_block(gm.shape, 0), _resident(w.shape), _resident(wg.shape),
                  _resident(bg.shape), _resident(gn.shape)] + [j[0] for j in jobs],
        out_specs=[_rows(n) for n, _ in outs] + [j[1] for j in jobs],
        out_shape=[jax.ShapeDtypeStruct((m, n), dt) for n, dt in outs] + [j[2] for j in jobs],
        compiler_params=_params("arbitrary"),
        name="layer_a_in",
    )(xp, xs, g1, wgu, wd, gm, w, wg, bg, gn, *[c[0] for c in casts])


def _build_mem_block_diag(mk_ref, mv_ref, kbd_scr, vbd_scr):
    shape = (MEM_Q, MEM_HEADS * MEM_TOKENS)
    rh = lax.broadcasted_iota(jnp.int32, shape, 0) // MEM_HEAD_DIM
    ch = lax.broadcasted_iota(jnp.int32, shape, 1) // MEM_TOKENS
    diag = rh == ch
    kbd_scr[...] = jnp.where(diag, jnp.concatenate([mk_ref[0, 0]] * MEM_HEADS, axis=1),
                             0.0).astype(BF)
    vbd_scr[...] = jnp.where(diag, jnp.concatenate([mv_ref[0, 0]] * MEM_HEADS, axis=1),
                             0.0).astype(BF)


def _mem_attn_tile(q, kbd, vbd):
    tq = q.shape[0]
    s = _dot(q, kbd)
    sh = [s[:, h * MEM_TOKENS:(h + 1) * MEM_TOKENS] for h in range(MEM_HEADS)]
    ms = [jnp.max(x, axis=1, keepdims=True) for x in sh]
    es = [jnp.exp(x - m) for x, m in zip(sh, ms)]
    inv = [1.0 / jnp.sum(e, axis=1, keepdims=True) for e in es]
    o = _dot_nt(jnp.concatenate([e.astype(BF) for e in es], axis=1), vbd)
    lane_h = lax.broadcasted_iota(jnp.int32, (tq, MEM_Q), 1) // MEM_HEAD_DIM
    scale = jnp.where(lane_h == 0, inv[0],
                      jnp.where(lane_h == 1, inv[1], jnp.where(lane_h == 2, inv[2], inv[3])))
    return o * scale


def _swa_tile(q_ref, kprev_ref, kcur_ref, vprev_ref, vcur_ref, sink_ref, first_of_seq):
    blk = WINDOW
    nsub = TOKEN_TILE // blk
    qi = lax.broadcasted_iota(jnp.int32, (blk, 2 * blk), 0)
    kj = lax.broadcasted_iota(jnp.int32, (blk, 2 * blk), 1)
    d = blk + qi - kj
    band = (d >= 0) & (d < WINDOW)
    bias_mid = jnp.where(band, 0.0, -jnp.inf)
    bias_first = jnp.where(band & (kj >= blk), 0.0, -jnp.inf)
    bias0 = jnp.where(first_of_seq, bias_first, bias_mid)
    outs = [[None] * SWA_Q_PAIRS for _ in range(nsub)]
    for kh in range(SWA_KV_HEADS):
        sl = slice(kh * HEAD_DIMP, (kh + 1) * HEAD_DIMP)
        kblk = [kprev_ref[:, sl]] + [kcur_ref[s * blk:(s + 1) * blk, sl] for s in range(nsub)]
        vblk = [vprev_ref[:, sl]] + [vcur_ref[s * blk:(s + 1) * blk, sl] for s in range(nsub)]
        k2 = [(b.astype(BF), pltpu.roll(b, HEAD_DIM, 1).astype(BF)) for b in kblk]
        v2 = [(b.astype(BF), pltpu.roll(b, HEAD_DIM, 1).astype(BF)) for b in vblk]
        for sub in range(nsub):
            bias = bias0 if sub == 0 else bias_mid
            for pj in range(SWA_PAIRS_PER_KV):
                pp = kh * SWA_PAIRS_PER_KV + pj
                q = q_ref[pp, sub * blk:(sub + 1) * blk, :]
                o = None
                for half in range(2):
                    kb = jnp.concatenate([k2[sub][half], k2[sub + 1][half]], axis=0)
                    vb = jnp.concatenate([v2[sub][half], v2[sub + 1][half]], axis=0)
                    s = _dot_nt(q, kb) + bias
                    sink = sink_ref[2 * pp + half]
                    m = jnp.maximum(jnp.max(s, axis=1, keepdims=True), sink)
                    e = jnp.exp(s - m)
                    l = jnp.sum(e, axis=1, keepdims=True) + jnp.exp(sink - m)
                    oh = _dot(e.astype(BF), vb) * (1.0 / l)
                    o = oh if o is None else o + oh
                outs[sub][pp] = o.astype(BF)
    return jnp.concatenate([jnp.concatenate(row, axis=1) for row in outs], axis=0)


def _layer_a_out_kernel(x_ref, tokp_ref, toks_ref, mq_ref, mk_ref, mv_ref, mos_ref, wo_ref,
                        g2_ref, wgu_ref, wd_ref, gkv_ref, wkv_ref, c_ref, s1_ref, s2_ref,
                        *refs, n_prompt, tiles_per_seq):
    n = (len(refs) - 5) // 2
    xo_ref, k_ref, v_ref = refs[n:n + 3]
    kbd_scr, vbd_scr = refs[-2:]
    _run_cast_jobs(refs[:n] + refs[n + 3:-2])
    i = pl.program_id(0)
    is_prompt = i < n_prompt

    @pl.when(i % tiles_per_seq == 0)
    def _():
        _build_mem_block_diag(mk_ref, mv_ref, kbd_scr, vbd_scr)

    nt = GLA_HEADS * GLA_DVP
    mo = _mem_attn_tile(mq_ref[...], kbd_scr[...], vbd_scr[...]).astype(BF)
    mo = jnp.where(is_prompt, mo, mos_ref[...])
    x = (x_ref[...] + _dot(_pick(is_prompt, tokp_ref, toks_ref), wo_ref[:nt, :])
         + _dot(mo, wo_ref[nt:, :]))
    x = _ffn_half(x, g2_ref, wgu_ref, wd_ref)
    xo_ref[...] = x
    h = _rms(x, gkv_ref[...]).astype(BF)
    c, s1, s2 = c_ref[...], s1_ref[...], s2_ref[...]
    kw = SWA_KV_HEADS * HEAD_DIMP
    kv = _dot(h, wkv_ref[...])
    for hh in range(SWA_KV_HEADS):
        sl = slice(hh * HEAD_DIMP, (hh + 1) * HEAD_DIMP)
        k_ref[:, sl] = _rope(kv[:, sl], c, s1, s2)
    v_ref[...] = kv[:, kw:]


def _tab_spec(n_prompt, blocks_per_seq):
    return pl.BlockSpec((TOKEN_TILE, HEAD_DIMP),
                        lambda i: (jnp.where(i < n_prompt, i % blocks_per_seq, blocks_per_seq), 0))


def _mem_kv_spec(layer, n_prompt, tiles_per_seq):
    return pl.BlockSpec(
        (1, 1, MEM_Q, MEM_TOKENS),
        lambda i: (layer, jnp.minimum(i, n_prompt - 1) // tiles_per_seq, 0, 0))


def _mem_scratch():
    return pltpu.VMEM((MEM_Q, MEM_HEADS * MEM_TOKENS), BF)


def _layer_a_out(x, tokp, toks, mq, mk_t, mv_t, mos, wo, g2, wgu, wd, gkv, wkv, tabs, casts,
                 *, seq):
    m = x.shape[0]
    grid = (m // TOKEN_TILE,)
    n_prompt = tokp.shape[0] // TOKEN_TILE
    tiles_per_seq = seq // TOKEN_TILE
    kw = SWA_KV_HEADS * HEAD_DIMP
    nt = GLA_HEADS * GLA_DVP
    tab = _tab_spec(n_prompt, tiles_per_seq)
    mem_spec = _mem_kv_spec(0, n_prompt, tiles_per_seq)
    jobs = [_cast_job(cw, layer, rows, grid) for cw, layer, rows in casts]
    return pl.pallas_call(
        functools.partial(_layer_a_out_kernel, n_prompt=n_prompt, tiles_per_seq=tiles_per_seq),
        grid=grid,
        in_specs=[_rows(D_MODEL), _prompt_rows(nt, n_prompt), _sample_rows(nt),
                  _rows(MEM_Q), mem_spec, mem_spec, _sample_rows(MEM_Q), _resident(wo.shape),
                  _layer_block(g2.shape, 0), _layer_block(wgu.shape, 0), _layer_block(wd.shape, 0),
                  _resident(gkv.shape), _resident(wkv.shape), tab, tab, tab]
                 + [j[0] for j in jobs],
        out_specs=[_rows(D_MODEL), _rows(kw), _rows(kw)] + [j[1] for j in jobs],
        out_shape=[jax.ShapeDtypeStruct((m, D_MODEL), F32),
                   jax.ShapeDtypeStruct((m, kw), F32), jax.ShapeDtypeStruct((m, kw), F32)]
                  + [j[2] for j in jobs],
        scratch_shapes=[_mem_scratch(), _mem_scratch()],
        compiler_params=_params("arbitrary"),
        name="layer_a_out",
    )(x, tokp, toks, mq, mk_t, mv_t, mos, wo, g2, wgu, wd, gkv, wkv, *tabs,
      *[c[0] for c in casts])


def _layer_b_in_kernel(x_ref, g1_ref, wgu_ref, wd_ref, gm_ref, w_ref, c_ref, s1_ref, s2_ref,
                       *refs):
    n = (len(refs) - 3) // 2
    xo_ref, q_ref, mq_ref = refs[n:n + 3]
    _run_cast_jobs(refs[:n] + refs[n + 3:])
    x = _ffn_half(x_ref[...], g1_ref, wgu_ref, wd_ref)
    xo_ref[...] = x
    h = _rms(x, gm_ref[0]).astype(BF)
    c, s1, s2 = c_ref[...], s1_ref[...], s2_ref[...]
    nq = SWA_Q_HEADS * HEAD_DIM
    qm = _dot(h, w_ref[...])
    for pp in range(SWA_Q_PAIRS):
        q = qm[:, pp * HEAD_DIMP:(pp + 1) * HEAD_DIMP]
        q_ref[pp] = (_rope(q, c, s1, s2) * (HEAD_DIM ** -0.5)).astype(BF)
    mq_ref[...] = (qm[:, nq:] * (MEM_HEAD_DIM ** -0.5)).astype(BF)


def _layer_b_in(x, g1, wgu, wd, gm, w, tabs, casts, *, n_prompt, seq):
    m = x.shape[0]
    grid = (m // TOKEN_TILE,)
    tab = _tab_spec(n_prompt, seq // TOKEN_TILE)
    q_spec = pl.BlockSpec((SWA_Q_PAIRS, TOKEN_TILE, HEAD_DIMP), lambda i: (0, i, 0))
    jobs = [_cast_job(cw, layer, rows, grid) for cw, layer, rows in casts]
    return pl.pallas_call(
        _layer_b_in_kernel,
        grid=grid,
        in_specs=[_rows(D_MODEL), _layer_block(g1.shape, 1), _layer_block(wgu.shape, 0),
                  _layer_block(wd.shape, 0), _layer_block(gm.shape, 1), _resident(w.shape),
                  tab, tab, tab] + [j[0] for j in jobs],
        out_specs=[_rows(D_MODEL), q_spec, _rows(MEM_Q)] + [j[1] for j in jobs],
        out_shape=[jax.ShapeDtypeStruct((m, D_MODEL), F32),
                   jax.ShapeDtypeStruct((SWA_Q_PAIRS, m, HEAD_DIMP), BF),
                   jax.ShapeDtypeStruct((m, MEM_Q), BF)] + [j[2] for j in jobs],
        compiler_params=_params("arbitrary"),
        name="layer_b_in",
    )(x, g1, wgu, wd, gm, w, *tabs, *[c[0] for c in casts])


def _layer_b_out_kernel(sink_ref, x_ref, q_ref, kprev_ref, kcur_ref, vprev_ref, vcur_ref,
                        mq_ref, mk_ref, mv_ref, toks_ref, mos_ref, wo_ref, g2_ref, wgu_ref,
                        wd_ref, gf_ref, yp_ref, ys_ref, kbd_scr, vbd_scr,
                        *, n_prompt, tiles_per_seq):
    i = pl.program_id(0)
    is_prompt = i < n_prompt
    first_of_seq = i % tiles_per_seq == 0

    @pl.when(first_of_seq)
    def _():
        _build_mem_block_diag(mk_ref, mv_ref, kbd_scr, vbd_scr)

    nt = SWA_Q_HEADS * HEAD_DIM
    mo = _mem_attn_tile(mq_ref[...], kbd_scr[...], vbd_scr[...]).astype(BF)
    mo = jnp.where(is_prompt, mo, mos_ref[...])
    tok = _swa_tile(q_ref, kprev_ref, kcur_ref, vprev_ref, vcur_ref, sink_ref, first_of_seq)
    toks = jnp.concatenate([toks_ref[pp] for pp in range(SWA_Q_PAIRS)], axis=1)
    tok = jnp.where(is_prompt, tok, toks)
    x = x_ref[...] + _dot(tok, wo_ref[:nt, :]) + _dot(mo, wo_ref[nt:, :])
    y = _rms(_ffn_half(x, g2_ref, wgu_ref, wd_ref), gf_ref[...])

    @pl.when(is_prompt)
    def _():
        yp_ref[...] = y

    @pl.when(jnp.logical_not(is_prompt))
    def _():
        ys_ref[...] = y


def _layer_b_out(sinks, x, qs, k_sh, v_sh, mq, mk_t, mv_t, toks, mos, wo, g2, wgu, wd, gf, *, seq):
    m = x.shape[0]
    n_tiles = m // TOKEN_TILE
    n_prompt = n_tiles - 1
    tiles_per_seq = seq // TOKEN_TILE
    kw = SWA_KV_HEADS * HEAD_DIMP
    blocks_per_tile = TOKEN_TILE // WINDOW
    q_spec = pl.BlockSpec((SWA_Q_PAIRS, TOKEN_TILE, HEAD_DIMP), lambda i: (0, i, 0))
    prev_spec = pl.BlockSpec((WINDOW, kw), lambda i: (jnp.maximum(i * blocks_per_tile - 1, 0), 0))
    toks_spec = pl.BlockSpec((SWA_Q_PAIRS, TOKEN_TILE, HEAD_DIMP), lambda i: (0, 0, 0),
                             pipeline_mode=pl.Buffered(1))
    mem_spec = _mem_kv_spec(1, n_prompt, tiles_per_seq)
    return pl.pallas_call(
        functools.partial(_layer_b_out_kernel, n_prompt=n_prompt, tiles_per_seq=tiles_per_seq),
        grid=(n_tiles,),
        in_specs=[pl.BlockSpec(memory_space=pltpu.SMEM), _rows(D_MODEL), q_spec,
                  prev_spec, _rows(kw), prev_spec, _rows(kw), _rows(MEM_Q), mem_spec, mem_spec,
                  toks_spec, _sample_rows(MEM_Q), _resident(wo.shape), _layer_block(g2.shape, 1),
                  _layer_block(wgu.shape, 0), _layer_block(wd.shape, 0), _resident(gf.shape)],
        out_specs=[_prompt_rows(D_MODEL, n_prompt),
                   pl.BlockSpec((TOKEN_TILE, D_MODEL), lambda i: (0, 0))],
        out_shape=[jax.ShapeDtypeStruct((n_prompt * TOKEN_TILE, D_MODEL), F32),
                   jax.ShapeDtypeStruct((TOKEN_TILE, D_MODEL), F32)],
        scratch_shapes=[_mem_scratch(), _mem_scratch()],
        compiler_params=_params("arbitrary"),
        name="layer_b_out",
    )(sinks, x, qs, k_sh, k_sh, v_sh, v_sh, mq, mk_t, mv_t, toks, mos, wo, g2, wgu, wd, gf)


def _mem_kv_kernel(x_ref, g_ref, wt_ref, *refs):
    n = (len(refs) - 2) // 2
    k_ref, v_ref = refs[n:n + 2]
    _run_cast_jobs(refs[:n] + refs[n + 2:])
    x = x_ref[...]
    xn = x * lax.rsqrt(jnp.mean(x * x, axis=-1, keepdims=True) + EPS)
    for l in range(2):
        h = (xn * g_ref[l]).astype(BF)
        kvt = _dot_nt(wt_ref[l], h)
        k_ref[l, 0] = kvt[:MEM_Q, :]
        v_ref[l, 0] = kvt[MEM_Q:, :]


def _mem_kv(mem, g, wt, casts, *, batch):
    out_spec = pl.BlockSpec((2, 1, MEM_Q, MEM_TOKENS), lambda b: (0, b, 0, 0))
    jobs = [_cast_job(cw, layer, rows, (batch,)) for cw, layer, rows in casts]
    return pl.pallas_call(
        _mem_kv_kernel,
        grid=(batch,),
        in_specs=[pl.BlockSpec((MEM_TOKENS, D_MODEL), lambda b: (b, 0)), _resident(g.shape),
                  _resident(wt.shape)] + [j[0] for j in jobs],
        out_specs=[out_spec, out_spec] + [j[1] for j in jobs],
        out_shape=[jax.ShapeDtypeStruct((2, batch, MEM_Q, MEM_TOKENS), F32)] * 2
                  + [j[2] for j in jobs],
        compiler_params=_params("arbitrary"),
        name="mem_kv",
    )(mem, g, wt, *[c[0] for c in casts])


def _gla_prompt_kernel(*refs):
    ns = GLA_SEQS
    seq_in = [refs[5 * i:5 * i + 5] for i in range(ns)]
    rest = refs[5 * ns:-1]
    s_scr = refs[-1]
    n = (len(rest) - 2) // 2
    tok_ref, st_ref = rest[n:n + 2]
    _run_cast_jobs(rest[:n] + rest[n + 2:])
    c = pl.program_id(1)
    C = GLA_CHUNK

    @pl.when(c == 0)
    def _():
        s_scr[...] = jnp.zeros(s_scr.shape, F32)

    row = lax.broadcasted_iota(jnp.int32, (C, C), 0)
    col = lax.broadcasted_iota(jnp.int32, (C, C), 1)
    causal = row >= col
    ltri = jnp.where(causal, 1.0, 0.0).astype(BF)
    _gla_chunks([(seq_in[i], tok_ref.at[i], s_scr.at[i], h)
                 for i in range(ns) for h in range(GLA_HEADS)], causal, ltri)

    @pl.when(c == pl.num_programs(1) - 1)
    def _():
        st_ref[...] = s_scr[...]


def _gla_chunks(chains, causal, ltri):
    C = GLA_CHUNK

    def sk(h):
        return slice(h * GLA_DKP, (h + 1) * GLA_DKP)

    def sv(h):
        return slice(h * GLA_DVP, (h + 1) * GLA_DVP)

    bs = []
    for (q_ref, k_ref, la_ref, v_ref, r_ref), _, _, h in chains:
        la = la_ref[:, sk(h)]
        hi = la.astype(BF)
        lo = (la - hi.astype(F32)).astype(BF)
        bb = _dot(ltri, jnp.concatenate([hi, lo], axis=1))
        bs.append(bb[:, :GLA_DKP] + bb[:, GLA_DKP:])
    ops = []
    for ((q_ref, k_ref, la_ref, v_ref, r_ref), _, _, h), b in zip(chains, bs):
        b_mid = b[C // 2 - 1:C // 2, :]
        b_last = b[C - 1:C, :]
        qe = q_ref[:, sk(h)] * jnp.exp(b - b_mid)
        ke = k_ref[:, sk(h)] * jnp.exp(b_mid - b)
        qb = (qe * jnp.exp(b_mid)).astype(BF)
        kd = (ke * jnp.exp(b_last - b_mid)).astype(BF)
        ops.append((qe.astype(BF), ke.astype(BF), qb, kd, jnp.exp(b_last)))
    As = [jnp.where(causal, _dot_nt(qe, ke), 0.0).astype(BF) for qe, ke, _, _, _ in ops]
    outs = []
    for ((q_ref, k_ref, la_ref, v_ref, r_ref), _, s_ref, h), a, (_, _, qb, kd, decay) in zip(
            chains, As, ops):
        v = v_ref[:, sv(h)].astype(BF)
        st = s_ref[h]
        outs.append(_dot(a, v) + _dot_nt(qb, st.astype(BF)))
        s_ref[h] = st * decay + _dot_tn(v, kd)
    for ((q_ref, k_ref, la_ref, v_ref, r_ref), tok_ref, _, h), o in zip(chains, outs):
        ms = jnp.sum(o * o, axis=1, keepdims=True) * (1.0 / GLA_DV)
        tok_ref[:, sv(h)] = (o * lax.rsqrt(ms + EPS) * r_ref[:, sv(h)]).astype(BF)


def _gla_prompt(q, k, la, v, r, casts, *, batch, seq):
    ns = GLA_SEQS
    m = batch * seq
    nc = seq // GLA_CHUNK
    qk = GLA_HEADS * GLA_DKP
    vr = GLA_HEADS * GLA_DVP
    grid = (batch // ns, nc)

    def tok_map(i):
        return lambda b, c: ((b * ns + i) * nc + c, 0)

    seq_specs = []
    for i in range(ns):
        seq_specs += [pl.BlockSpec((GLA_CHUNK, qk), tok_map(i))] * 3
        seq_specs += [pl.BlockSpec((GLA_CHUNK, vr), tok_map(i))] * 2
    jobs = [_cast_job(cw, layer, rows, grid) for cw, layer, rows in casts]
    res = pl.pallas_call(
        _gla_prompt_kernel,
        grid=grid,
        in_specs=seq_specs + [j[0] for j in jobs],
        out_specs=[pl.BlockSpec((ns, GLA_CHUNK, vr), lambda b, c: (b, c, 0)),
                   pl.BlockSpec((ns, GLA_HEADS, GLA_DVP, GLA_DKP), lambda b, c: (b, 0, 0, 0))]
                  + [j[1] for j in jobs],
        out_shape=[jax.ShapeDtypeStruct((batch, seq, vr), BF),
                   jax.ShapeDtypeStruct((batch, GLA_HEADS, GLA_DVP, GLA_DKP), F32)]
                  + [j[2] for j in jobs],
        scratch_shapes=[pltpu.VMEM((ns, GLA_HEADS, GLA_DVP, GLA_DKP), F32)],
        compiler_params=_params("arbitrary", "arbitrary"),
        name="gla_prompt",
    )(*([q, k, la, v, r] * ns), *[c[0] for c in casts])
    return (res[0].reshape(m, vr),) + tuple(res[1:])


_SAMPLE_BB = 8
_DEC_SEQ = 4
_DEC_BATCH = 128
_GLA_DK_BLK = 32
_NEW_ROWS = 16


def _gla_sample_kernel(s_ref, q_ref, k_ref, la_ref, v_ref, r_ref, so_ref, tok_ref, o_scr):
    j = pl.program_id(1)

    @pl.when(j == 0)
    def _():
        o_scr[...] = jnp.zeros(o_scr.shape, F32)

    def body(dk, carry):
        s = s_ref[0, 0, dk]
        for t in range(_DEC_SEQ):
            a = jnp.exp(la_ref[t, 0, pl.ds(dk, 1), :])
            s = a * s + k_ref[t, 0, pl.ds(dk, 1), :] * v_ref[t, 0, :GLA_DV, :]
            o_scr[t] = o_scr[t] + q_ref[t, 0, pl.ds(dk, 1), :] * s
        so_ref[0, 0, dk] = s
        return carry

    lax.fori_loop(0, _GLA_DK_BLK, body, 0, unroll=4)

    @pl.when(j == pl.num_programs(1) - 1)
    def _():
        tok_ref[...] = jnp.zeros(tok_ref.shape, F32)
        for t in range(_DEC_SEQ):
            o = o_scr[t]
            ms = jnp.sum(o * o, axis=0, keepdims=True) * (1.0 / GLA_DV)
            tok_ref[t, 0, :GLA_DV, :] = o * lax.rsqrt(ms + EPS) * r_ref[t, 0, :GLA_DV, :]


def _gla_sample(state, q, k, la, v, r):
    qk_spec = pl.BlockSpec((_DEC_SEQ, 1, _GLA_DK_BLK, _DEC_BATCH), lambda h, j: (0, h, j, 0))
    vr_spec = pl.BlockSpec((_DEC_SEQ, 1, GLA_DVP, _DEC_BATCH), lambda h, j: (0, h, 0, 0))
    s_spec = pl.BlockSpec((1, 1, _GLA_DK_BLK, GLA_DV, _DEC_BATCH), lambda h, j: (0, h, j, 0, 0))
    return pl.pallas_call(
        _gla_sample_kernel,
        grid=(GLA_HEADS, GLA_DK // _GLA_DK_BLK),
        in_specs=[s_spec, qk_spec, qk_spec, qk_spec, vr_spec, vr_spec],
        out_specs=[s_spec, vr_spec],
        out_shape=[jax.ShapeDtypeStruct(state.shape, F32),
                   jax.ShapeDtypeStruct(v.shape, F32)],
        scratch_shapes=[pltpu.VMEM((_DEC_SEQ, GLA_DV, _DEC_BATCH), F32)],
        compiler_params=_params("parallel", "arbitrary"),
        name="gla_sample",
    )(state, q, k, la, v, r)


def _mem_attn_sample_kernel(q_ref, mk_ref, mv_ref, o_ref):
    scores = [_dot(q_ref[bi], mk_ref[0, bi].astype(BF)) for bi in range(_SAMPLE_BB)]
    probs = []
    for s in scores:
        e = jnp.exp(s - jnp.max(s, axis=1, keepdims=True))
        probs.append((e * (1.0 / jnp.sum(e, axis=1, keepdims=True))).astype(BF))
    for bi, p in enumerate(probs):
        o_ref[bi] = _dot_nt(p, mv_ref[0, bi].astype(BF))


def _mem_attn_sample(qbd, mk_t, mv_t, layer):
    nb = qbd.shape[0]
    nr = MEM_HEADS * _DEC_SEQ
    kv_spec = pl.BlockSpec((1, _SAMPLE_BB, MEM_Q, MEM_TOKENS), lambda i: (layer, i, 0, 0))
    q_spec = pl.BlockSpec((_SAMPLE_BB, nr, MEM_Q), lambda i: (i, 0, 0))
    return pl.pallas_call(
        _mem_attn_sample_kernel,
        grid=(nb // _SAMPLE_BB,),
        in_specs=[q_spec, kv_spec, kv_spec],
        out_specs=q_spec,
        out_shape=jax.ShapeDtypeStruct((nb, nr, MEM_Q), F32),
        compiler_params=_params("parallel"),
        name="mem_attn_sample",
    )(qbd, mk_t, mv_t)


def _swa_sample_kernel(q_ref, sink_ref, kc_ref, vc_ref, kn_ref, vn_ref, o_ref, ko_ref, vo_ref):
    nq = SWA_Q_HEADS * _DEC_SEQ
    t = lax.broadcasted_iota(jnp.int32, (nq, WINDOW), 0) % _DEC_SEQ
    pos = lax.broadcasted_iota(jnp.int32, (nq, WINDOW), 1)
    bias_c = jnp.where(pos > t, 0.0, -jnp.inf)
    tn = lax.broadcasted_iota(jnp.int32, (nq, _NEW_ROWS), 0) % _DEC_SEQ
    new = lax.broadcasted_iota(jnp.int32, (nq, _NEW_ROWS), 1)
    bias_n = jnp.where(new <= tn, 0.0, -jnp.inf)
    sink = sink_ref[...]
    seqs = range(_SAMPLE_BB)
    scores = [(_dot(q_ref[bi], kc_ref[bi].astype(BF)) + bias_c,
               _dot_nt(q_ref[bi], kn_ref[bi].astype(BF)) + bias_n) for bi in seqs]
    probs = []
    for sc, sn in scores:
        m = jnp.maximum(jnp.maximum(jnp.max(sc, axis=1, keepdims=True),
                                    jnp.max(sn, axis=1, keepdims=True)), sink)
        ec = jnp.exp(sc - m)
        en = jnp.exp(sn - m)
        l = (jnp.sum(ec, axis=1, keepdims=True) + jnp.sum(en, axis=1, keepdims=True)
             + jnp.exp(sink - m))
        inv = 1.0 / l
        probs.append(((ec * inv).astype(BF), (en * inv).astype(BF)))
    for bi, (pc, pn) in zip(seqs, probs):
        o_ref[bi] = (_dot_nt(pc, vc_ref[bi].astype(BF)) + _dot(pn, vn_ref[bi].astype(BF)))
    new0 = WINDOW - _DEC_SEQ
    row = lax.broadcasted_iota(jnp.int32, (_NEW_ROWS, WINDOW), 0)
    col = lax.broadcasted_iota(jnp.int32, (_NEW_ROWS, WINDOW), 1)
    place = jnp.where((col == new0 + row) & (row < _DEC_SEQ), 1.0, 0.0)
    is_new = lax.broadcasted_iota(jnp.int32, (SWA_KV_HEADS * HEAD_DIM, WINDOW), 1) >= new0

    def placed(new_rows):
        return lax.dot_general(new_rows, place, (((0,), (0,)), ((), ())),
                               precision=lax.Precision.HIGHEST, preferred_element_type=F32)

    for bi in seqs:
        ko_ref[bi] = jnp.where(is_new, placed(kn_ref[bi]), pltpu.roll(kc_ref[bi], new0, 1))
        vo_ref[bi] = jnp.where(is_new, placed(vn_ref[bi]), pltpu.roll(vc_ref[bi], new0, 1))


def _swa_sample(qbd, sink_col, kc, vc, kn, vn):
    nb = qbd.shape[0]
    kw = SWA_KV_HEADS * HEAD_DIM
    nq = SWA_Q_HEADS * _DEC_SEQ
    kv_spec = pl.BlockSpec((_SAMPLE_BB, kw, WINDOW), lambda i: (i, 0, 0))
    new_spec = pl.BlockSpec((_SAMPLE_BB, _NEW_ROWS, kw), lambda i: (i, 0, 0))
    q_spec = pl.BlockSpec((_SAMPLE_BB, nq, kw), lambda i: (i, 0, 0))
    return pl.pallas_call(
        _swa_sample_kernel,
        grid=(nb // _SAMPLE_BB,),
        in_specs=[q_spec, _resident((nq, 1)), kv_spec, kv_spec, new_spec, new_spec],
        out_specs=[q_spec, kv_spec, kv_spec],
        out_shape=[jax.ShapeDtypeStruct((nb, nq, kw), F32),
                   jax.ShapeDtypeStruct(kc.shape, F32), jax.ShapeDtypeStruct(kc.shape, F32)],
        compiler_params=_params("parallel"),
        name="swa_sample",
    )(qbd, sink_col, kc, vc, kn, vn)


def _pad_heads(w, heads, dim, dim_p, axis):
    shape = w.shape
    w = w.reshape(shape[:axis] + (heads, dim) + shape[axis + 1:])
    pad = [(0, 0)] * w.ndim
    pad[axis + 1] = (0, dim_p - dim)
    w = jnp.pad(w, pad)
    return w.reshape(shape[:axis] + (heads * dim_p,) + shape[axis + 1:])


def _rope_tables(pos):
    half = ROT_DIM // 2
    inv_freq = np.exp(-math.log(ROPE_THETA) * np.arange(0, ROT_DIM, 2, dtype=np.float64) / ROT_DIM)
    ang = pos.astype(np.float64)[:, None] * inv_freq[None, :]
    cos, sin = np.cos(ang), np.sin(ang)
    n = pos.shape[0]
    rest = HEAD_DIM - ROT_DIM
    c = np.concatenate([cos, cos, np.ones((n, rest))], axis=1)
    s1 = np.concatenate([-sin, np.zeros((n, HEAD_DIM - half))], axis=1)
    s2 = np.concatenate([np.zeros((n, half)), sin, np.zeros((n, rest))], axis=1)
    return tuple(np.tile(a, (1, HEAD_DIMP // HEAD_DIM)).astype(np.float32) for a in (c, s1, s2))


def _prep_weights(p):
    w = {}
    for name in ("ffn1_norm", "ffn2_norm", "mix_norm"):
        w[name] = p[name][:, None, :]
    qk = GLA_HEADS * GLA_DK
    vv = GLA_HEADS * GLA_DV
    a_in = p["a_w_in"][0].astype(BF)
    w_qk = _pad_heads(a_in[:, :2 * qk], 2 * GLA_HEADS, GLA_DK, GLA_DKP, 1)
    w_vr = _pad_heads(a_in[:, 2 * qk:2 * (qk + vv)], 2 * GLA_HEADS, GLA_DV, GLA_DVP, 1)
    o = 2 * (qk + vv)
    w_g = jnp.pad(a_in[:, o:o + GLA_RANK], ((0, 0), (0, GLA_RANKP - GLA_RANK)))
    w["a_in"] = jnp.concatenate([w_qk, w_vr, w_g, a_in[:, o + GLA_RANK:]], axis=1)
    gate = _pad_heads(p["a_w_gate"][0], GLA_HEADS, GLA_DK, GLA_DKP, 1)
    w["a_gate"] = jnp.pad(gate, ((0, GLA_RANKP - GLA_RANK), (0, 0))).astype(BF)
    w["a_bgate"] = _pad_heads(p["a_b_gate"][0][None, :], GLA_HEADS, GLA_DK, GLA_DKP, 1)
    w["a_gn"] = jnp.tile(jnp.pad(p["a_out_norm"][0], (0, GLA_DVP - GLA_DV)), GLA_HEADS)[None, :]
    a_out = p["a_w_out"][0]
    w["a_out"] = jnp.concatenate(
        [_pad_heads(a_out[:vv], GLA_HEADS, GLA_DV, GLA_DVP, 0), a_out[vv:]], axis=0).astype(BF)
    w["b_in"] = p["b_w_in"][0].astype(BF)
    w["b_out"] = p["b_w_out"][0].astype(BF)
    w["kv"] = _pad_heads(p["w_kv"].astype(BF), 2 * SWA_KV_HEADS, HEAD_DIM, HEAD_DIMP, 1)
    w["mem_t"] = p["mem_w_kv"].transpose(0, 2, 1).astype(BF)
    return w


def _compact_kv(a, batch, seq):
    return a.reshape(batch, seq, SWA_KV_HEADS, HEAD_DIMP)[..., :HEAD_DIM]


def kernel(x_prompt, x_sample, state_gla, cache_swa_k, cache_swa_v, cache_mem_k, cache_mem_v,
           mem_prompt, ffn1_norm, ffn1_w_gu, ffn1_w_down, mix_norm, ffn2_norm, ffn2_w_gu,
           ffn2_w_down, mem_norm, mem_w_kv, a_w_in, a_w_gate, a_b_gate, a_out_norm, a_w_out,
           kv_norm, w_kv, b_w_in, b_sinks, b_w_out, final_norm):
    p = dict(ffn1_norm=ffn1_norm, ffn1_w_gu=ffn1_w_gu, ffn1_w_down=ffn1_w_down,
             mix_norm=mix_norm, ffn2_norm=ffn2_norm, ffn2_w_gu=ffn2_w_gu,
             ffn2_w_down=ffn2_w_down, mem_w_kv=mem_w_kv, a_w_in=a_w_in, a_w_gate=a_w_gate,
             a_b_gate=a_b_gate, a_out_norm=a_out_norm, a_w_out=a_w_out, w_kv=w_kv,
             b_w_in=b_w_in, b_w_out=b_w_out)
    w = _prep_weights(p)
    batch, seq, _ = x_prompt.shape
    nb, t, _ = x_sample.shape
    assert nb == _DEC_BATCH and t == _DEC_SEQ and nb * t == TOKEN_TILE
    assert seq % TOKEN_TILE == 0 and seq % GLA_CHUNK == 0
    mp = batch * seq
    kw = SWA_KV_HEADS * HEAD_DIM
    sinks = b_sinks[0]

    mem_k_t, mem_v_t, wgu_a1, wd_a1 = _mem_kv(
        mem_prompt.reshape(batch * MEM_TOKENS, D_MODEL), mem_norm[:, None, :], w["mem_t"],
        [(ffn1_w_gu, 0, D_MODEL // batch), (ffn1_w_down, 0, FFN_DIM // batch)], batch=batch)
    state_t = state_gla.transpose(0, 2, 3, 4, 1)
    kc_t = cache_swa_k.transpose(0, 2, 3, 1).reshape(nb, kw, WINDOW)
    vc_t = cache_swa_v.transpose(0, 2, 3, 1).reshape(nb, kw, WINDOW)
    cmk_t = cache_mem_k.transpose(0, 1, 3, 4, 2).reshape(2, nb, MEM_Q, MEM_TOKENS)
    cmv_t = cache_mem_v.transpose(0, 1, 3, 4, 2).reshape(2, nb, MEM_Q, MEM_TOKENS)

    tabs = tuple(
        jnp.asarray(np.concatenate([a, np.tile(b, (nb, 1))], axis=0))
        for a, b in zip(_rope_tables(np.arange(seq)), _rope_tables(PAST_LEN + np.arange(t))))

    mem_mask = (np.arange(MEM_Q) // MEM_HEAD_DIM)[None, :] == np.arange(MEM_HEADS)[:, None]
    kv_mask = (np.arange(kw) // HEAD_DIM)[None, :] == np.arange(SWA_KV_HEADS)[:, None]

    def mem_attn_sample(mq, layer):
        q4 = mq[mp:].reshape(nb, 1, t, MEM_Q)
        qbd = jnp.where(mem_mask[None, :, None, :], q4, 0).reshape(nb, MEM_HEADS * t, MEM_Q)
        o = _mem_attn_sample(qbd, cmk_t, cmv_t, layer).reshape(nb, MEM_HEADS, t, MEM_Q)
        o = jnp.sum(jnp.where(mem_mask[None, :, None, :], o, 0.0), axis=1)
        return o.reshape(nb * t, MEM_Q).astype(BF)

    def lanes(a, width):
        return a[mp:].reshape(nb, t, GLA_HEADS, width).transpose(1, 2, 3, 0)

    x, q, k, la, v, r, mq, wgu_a2, wd_a2 = _layer_a_in(
        x_prompt.reshape(mp, D_MODEL), x_sample.reshape(nb * t, D_MODEL), w["ffn1_norm"],
        wgu_a1, wd_a1, w["mix_norm"], w["a_in"], w["a_gate"], w["a_bgate"], w["a_gn"],
        [(ffn2_w_gu, 0, 32), (ffn2_w_down, 0, 128)])
    tok_p, st_p = _gla_prompt(q, k, la, v, r, [], batch=batch, seq=seq)
    st_s, tok_s = _gla_sample(state_t, lanes(q, GLA_DKP), lanes(k, GLA_DKP), lanes(la, GLA_DKP),
                              lanes(v, GLA_DVP), lanes(r, GLA_DVP))
    tok_s = tok_s.transpose(3, 0, 1, 2).reshape(nb * t, GLA_HEADS * GLA_DVP).astype(BF)
    mo_s = mem_attn_sample(mq, 0)
    x, k_sh, v_sh, wgu_b1, wd_b1 = _layer_a_out(
        x, tok_p, tok_s, mq, mem_k_t, mem_v_t, mo_s, w["a_out"], w["ffn2_norm"], wgu_a2, wd_a2,
        kv_norm[None, :], w["kv"], tabs, [(ffn1_w_gu, 1, 32), (ffn1_w_down, 1, 128)], seq=seq)

    x, qs, mq, wgu_b2, wd_b2 = _layer_b_in(
        x, w["ffn1_norm"], wgu_b1, wd_b1, w["mix_norm"], w["b_in"], tabs,
        [(ffn2_w_gu, 1, 32), (ffn2_w_down, 1, 128)], n_prompt=mp // TOKEN_TILE, seq=seq)

    k_new = _compact_kv(k_sh[mp:], nb, t)
    v_new = _compact_kv(v_sh[mp:], nb, t)

    def new_rows(a):
        return jnp.pad(a.reshape(nb, t, kw), ((0, 0), (0, _NEW_ROWS - t), (0, 0)))

    q5 = qs[:, mp:].reshape(SWA_Q_PAIRS, nb, t, 2, HEAD_DIM).transpose(1, 0, 3, 2, 4)
    q5 = q5.reshape(nb, SWA_KV_HEADS, SWA_GROUP, t, HEAD_DIM)
    qbd = jnp.where(kv_mask[None, :, None, None, :], jnp.tile(q5, (1, 1, 1, 1, SWA_KV_HEADS)), 0)
    qbd = qbd.reshape(nb, SWA_Q_HEADS * t, kw)
    o, k_s, v_s = _swa_sample(qbd, jnp.repeat(sinks, t)[:, None], kc_t, vc_t,
                              new_rows(k_new), new_rows(v_new))
    o = o.reshape(nb, SWA_KV_HEADS, SWA_GROUP, t, kw)
    o = jnp.where(kv_mask[None, :, None, None, :], o, 0.0)
    o = o.reshape(nb, SWA_KV_HEADS, SWA_GROUP, t, SWA_KV_HEADS, HEAD_DIM).sum(axis=4)
    tok_s = o.reshape(nb, SWA_Q_PAIRS, 2, t, HEAD_DIM).transpose(1, 0, 3, 2, 4)
    tok_s = tok_s.reshape(SWA_Q_PAIRS, nb * t, HEAD_DIMP).astype(BF)
    mo_s = mem_attn_sample(mq, 1)
    y_p, y_s = _layer_b_out(sinks, x, qs, k_sh, v_sh, mq, mem_k_t, mem_v_t, tok_s, mo_s,
                            w["b_out"], w["ffn2_norm"], wgu_b2, wd_b2, final_norm[None, :],
                            seq=seq)

    gla_prompt = st_p.transpose(0, 1, 3, 2)[None, :, :, :GLA_DK, :GLA_DV]
    gla_sample = st_s.transpose(0, 4, 1, 2, 3)
    def last_window(a):
        tiles_per_seq = seq // TOKEN_TILE
        a = a.reshape(-1, TOKEN_TILE, SWA_KV_HEADS * HEAD_DIMP)
        a = a[tiles_per_seq - 1:batch * tiles_per_seq:tiles_per_seq, TOKEN_TILE - WINDOW:]
        return a.reshape(batch, WINDOW, SWA_KV_HEADS, HEAD_DIMP)[..., :HEAD_DIM]

    swa_k_prompt = last_window(k_sh)
    swa_v_prompt = last_window(v_sh)
    swa_k_sample = k_s.reshape(nb, SWA_KV_HEADS, HEAD_DIM, WINDOW).transpose(0, 3, 1, 2)
    swa_v_sample = v_s.reshape(nb, SWA_KV_HEADS, HEAD_DIM, WINDOW).transpose(0, 3, 1, 2)
    mem_shape = (2, batch, MEM_HEADS, MEM_HEAD_DIM, MEM_TOKENS)
    mem_k_prompt = mem_k_t.reshape(mem_shape).transpose(0, 1, 4, 2, 3)
    mem_v_prompt = mem_v_t.reshape(mem_shape).transpose(0, 1, 4, 2, 3)
    return (y_p.reshape(batch, seq, D_MODEL), y_s.reshape(nb, t, D_MODEL), gla_prompt,
            gla_sample, swa_k_prompt, swa_v_prompt, swa_k_sample, swa_v_sample,
            mem_k_prompt, mem_v_prompt)
```

```python
import functools
import math

import jax
import jax.numpy as jnp
import numpy as np
from jax import lax
from jax.experimental import pallas as pl
from jax.experimental.pallas import tpu as pltpu

F32 = jnp.float32
BF = jnp.bfloat16

D_MODEL = 1024
FFN_DIM = 2816
EPS = 1e-6

GLA_HEADS = 4
GLA_DK = 96
GLA_DV = 192
GLA_DKP = 128
GLA_DVP = 256
GLA_RANK = 16
GLA_RANKP = 128
GLA_GATE_NORM = 16.0
GLA_CHUNK = 256
GLA_SEQS = 2

HEAD_DIM = 64
HEAD_DIMP = 128
SWA_Q_HEADS = 12
SWA_KV_HEADS = 3
SWA_GROUP = SWA_Q_HEADS // SWA_KV_HEADS
SWA_Q_PAIRS = SWA_Q_HEADS // 2
SWA_PAIRS_PER_KV = SWA_GROUP // 2
WINDOW = 128
ROT_DIM = 16
ROPE_THETA = 500000.0
PAST_LEN = 8192

MEM_TOKENS = 256
MEM_HEADS = 4
MEM_HEAD_DIM = 64
MEM_Q = MEM_HEADS * MEM_HEAD_DIM

FFN_TF = 256
FFN_CHUNKS = FFN_DIM // FFN_TF
TOKEN_TILE = 512

VMEM_LIMIT = 60 * 1024 * 1024


def _params(*sem):
    return pltpu.CompilerParams(dimension_semantics=sem, vmem_limit_bytes=VMEM_LIMIT)


def _resident(shape):
    nd = len(shape)
    return pl.BlockSpec(shape, lambda *_: (0,) * nd, pipeline_mode=pl.Buffered(1))


def _layer_block(shape, layer):
    nd = len(shape)
    return pl.BlockSpec((1,) + tuple(shape[1:]), lambda *_: (layer,) + (0,) * (nd - 1),
                        pipeline_mode=pl.Buffered(1))


def _rows(width):
    return pl.BlockSpec((TOKEN_TILE, width), lambda i: (i, 0))


def _prompt_rows(width, n_prompt):
    return pl.BlockSpec((TOKEN_TILE, width), lambda i: (jnp.minimum(i, n_prompt - 1), 0))


def _sample_rows(width):
    return pl.BlockSpec((TOKEN_TILE, width), lambda i: (0, 0), pipeline_mode=pl.Buffered(1))


def _rms(x, g):
    ms = jnp.mean(x * x, axis=-1, keepdims=True)
    return x * lax.rsqrt(ms + EPS) * g


def _silu(x):
    return x * (1.0 / (1.0 + jnp.exp(-x)))


def _dot(a, b):
    return jnp.dot(a, b, preferred_element_type=F32)


def _dot_nt(a, b):
    return lax.dot_general(a, b, (((1,), (1,)), ((), ())), preferred_element_type=F32)


def _dot_tn(a, b):
    return lax.dot_general(a, b, (((0,), (0,)), ((), ())), preferred_element_type=F32)


def _pick(is_prompt, p_ref, s_ref):
    return jnp.where(is_prompt, p_ref[...], s_ref[...])


def _cast_job(w, layer, rows, grid):
    _, r, c = w.shape
    assert r % rows == 0 and rows % 16 == 0
    nblk = r // rows
    total = math.prod(grid)
    assert nblk <= total
    steps_per_block = total // nblk

    def block(*idx):
        step = idx[0]
        for dim, i in zip(grid[1:], idx[1:]):
            step = step * dim + i
        return jnp.minimum(step // steps_per_block, nblk - 1)

    in_spec = pl.BlockSpec((1, rows, c), lambda *idx: (layer, block(*idx), 0))
    out_spec = pl.BlockSpec((1, rows, c), lambda *idx: (0, block(*idx), 0))
    return in_spec, out_spec, jax.ShapeDtypeStruct((1, r, c), BF)


def _run_cast_jobs(refs):
    n = len(refs) // 2
    for src, dst in zip(refs[:n], refs[n:]):
        dst[...] = src[...].astype(BF)


def _ffn_half(x, g_ref, wgu_ref, wd_ref):
    h = _rms(x, g_ref[0]).astype(BF)
    acc = jnp.zeros(x.shape, F32)
    for c in range(FFN_CHUNKS):
        lo, hi = c * FFN_TF, (c + 1) * FFN_TF
        gate = _dot(h, wgu_ref[0, :, lo:hi])
        up = _dot(h, wgu_ref[0, :, FFN_DIM + lo:FFN_DIM + hi])
        a = (_silu(gate) * up).astype(BF)
        acc = acc + _dot(a, wd_ref[0, lo:hi, :])
    return x + 0.5 * acc


def _rope(x, c, s1, s2):
    return (x * c + pltpu.roll(x, HEAD_DIMP - ROT_DIM // 2, 1) * s1
            + pltpu.roll(x, ROT_DIM // 2, 1) * s2)


_A_SECTIONS = (("q", GLA_HEADS * GLA_DKP), ("k", GLA_HEADS * GLA_DKP),
               ("v", GLA_HEADS * GLA_DVP), ("r", GLA_HEADS * GLA_DVP),
               ("g", GLA_RANKP), ("m", MEM_Q))
_A_COLS = {}
for _name, _width in _A_SECTIONS:
    _lo = max((hi for _, hi in _A_COLS.values()), default=0)
    _A_COLS[_name] = (_lo, _lo + _width)


def _layer_a_in_kernel(xp_ref, xs_ref, g1_ref, wgu_ref, wd_ref, gm_ref, w_ref, wg_ref, bg_ref,
                       gn_ref, *refs, n_prompt):
    n = (len(refs) - 7) // 2
    x_ref, q_ref, k_ref, la_ref, v_ref, r_ref, mq_ref = refs[n:n + 7]
    _run_cast_jobs(refs[:n] + refs[n + 7:])
    is_prompt = pl.program_id(0) < n_prompt
    x = _ffn_half(_pick(is_prompt, xp_ref, xs_ref), g1_ref, wgu_ref, wd_ref)
    x_ref[...] = x
    h = _rms(x, gm_ref[0]).astype(BF)

    proj = _dot(h, w_ref[...])

    def mm(name):
        lo, hi = _A_COLS[name]
        return proj[:, lo:hi]

    z = _dot(mm("g").astype(BF), wg_ref[...]) + bg_ref[...]
    la_ref[...] = (jnp.minimum(z, 0.0) - jnp.log1p(jnp.exp(-jnp.abs(z)))) * (1.0 / GLA_GATE_NORM)
    r_ref[...] = _silu(mm("r")) * gn_ref[...]
    q_ref[...] = mm("q") * (GLA_DK ** -0.5)
    k_ref[...] = mm("k")
    v_ref[...] = mm("v")
    mq_ref[...] = (mm("m") * (MEM_HEAD_DIM ** -0.5)).astype(BF)


def _layer_a_in(xp, xs, g1, wgu, wd, gm, w, wg, bg, gn, casts):
    n_prompt = xp.shape[0] // TOKEN_TILE
    m = xp.shape[0] + xs.shape[0]
    grid = (m // TOKEN_TILE,)
    qk = GLA_HEADS * GLA_DKP
    vr = GLA_HEADS * GLA_DVP
    outs = ((D_MODEL, F32), (qk, F32), (qk, F32), (qk, F32), (vr, F32), (vr, F32), (MEM_Q, BF))
    jobs = [_cast_job(cw, layer, rows, grid) for cw, layer, rows in casts]
    return pl.pallas_call(
        functools.partial(_layer_a_in_kernel, n_prompt=n_prompt),
        grid=grid,
        in_specs=[_prompt_rows(D_MODEL, n_prompt), _sample_rows(D_MODEL),
                  _layer_block(g1.shape, 0), _layer_block(wgu.shape, 0), _layer_block(wd.shape, 0),
                  _layer_block(gm.shape, 0), _resident(w.shape), _resident(wg.shape),
                  _resident(bg.shape), _resident(gn.shape)] + [j[0] for j in jobs],
        out_specs=[_rows(n) for n, _ in outs] + [j[1] for j in jobs],
        out_shape=[jax.ShapeDtypeStruct((m, n), dt) for n, dt in outs] + [j[2] for j in jobs],
        compiler_params=_params("arbitrary"),
        name="layer_a_in",
    )(xp, xs, g1, wgu, wd, gm, w, wg, bg, gn, *[c[0] for c in casts])


def _build_mem_block_diag(mk_ref, mv_ref, kbd_scr, vbd_scr):
    shape = (MEM_Q, MEM_HEADS * MEM_TOKENS)
    rh = lax.broadcasted_iota(jnp.int32, shape, 0) // MEM_HEAD_DIM
    ch = lax.broadcasted_iota(jnp.int32, shape, 1) // MEM_TOKENS
    diag = rh == ch
    kbd_scr[...] = jnp.where(diag, jnp.concatenate([mk_ref[0, 0]] * MEM_HEADS, axis=1),
                             0.0).astype(BF)
    vbd_scr[...] = jnp.where(diag, jnp.concatenate([mv_ref[0, 0]] * MEM_HEADS, axis=1),
                             0.0).astype(BF)


def _mem_attn_tile(q, kbd, vbd):
    tq = q.shape[0]
    s = _dot(q, kbd)
    sh = [s[:, h * MEM_TOKENS:(h + 1) * MEM_TOKENS] for h in range(MEM_HEADS)]
    ms = [jnp.max(x, axis=1, keepdims=True) for x in sh]
    es = [jnp.exp(x - m) for x, m in zip(sh, ms)]
    inv = [1.0 / jnp.sum(e, axis=1, keepdims=True) for e in es]
    o = _dot_nt(jnp.concatenate([e.astype(BF) for e in es], axis=1), vbd)
    lane_h = lax.broadcasted_iota(jnp.int32, (tq, MEM_Q), 1) // MEM_HEAD_DIM
    scale = jnp.where(lane_h == 0, inv[0],
                      jnp.where(lane_h == 1, inv[1], jnp.where(lane_h == 2, inv[2], inv[3])))
    return o * scale


def _swa_tile(q_ref, kprev_ref, kcur_ref, vprev_ref, vcur_ref, sink_ref, first_of_seq):
    blk = WINDOW
    nsub = TOKEN_TILE // blk
    qi = lax.broadcasted_iota(jnp.int32, (blk, 2 * blk), 0)
    kj = lax.broadcasted_iota(jnp.int32, (blk, 2 * blk), 1)
    d = blk + qi - kj
    band = (d >= 0) & (d < WINDOW)
    bias_mid = jnp.where(band, 0.0, -jnp.inf)
    bias_first = jnp.where(band & (kj >= blk), 0.0, -jnp.inf)
    bias0 = jnp.where(first_of_seq, bias_first, bias_mid)
    outs = [[None] * SWA_Q_PAIRS for _ in range(nsub)]
    for kh in range(SWA_KV_HEADS):
        sl = slice(kh * HEAD_DIMP, (kh + 1) * HEAD_DIMP)
        kblk = [kprev_ref[:, sl]] + [kcur_ref[s * blk:(s + 1) * blk, sl] for s in range(nsub)]
        vblk = [vprev_ref[:, sl]] + [vcur_ref[s * blk:(s + 1) * blk, sl] for s in range(nsub)]
        k2 = [(b.astype(BF), pltpu.roll(b, HEAD_DIM, 1).astype(BF)) for b in kblk]
        v2 = [(b.astype(BF), pltpu.roll(b, HEAD_DIM, 1).astype(BF)) for b in vblk]
        for sub in range(nsub):
            bias = bias0 if sub == 0 else bias_mid
            for pj in range(SWA_PAIRS_PER_KV):
                pp = kh * SWA_PAIRS_PER_KV + pj
                q = q_ref[pp, sub * blk:(sub + 1) * blk, :]
                o = None
                for half in range(2):
                    kb = jnp.concatenate([k2[sub][half], k2[sub + 1][half]], axis=0)
                    vb = jnp.concatenate([v2[sub][half], v2[sub + 1][half]], axis=0)
                    s = _dot_nt(q, kb) + bias
                    sink = sink_ref[2 * pp + half]
                    m = jnp.maximum(jnp.max(s, axis=1, keepdims=True), sink)
                    e = jnp.exp(s - m)
                    l = jnp.sum(e, axis=1, keepdims=True) + jnp.exp(sink - m)
                    oh = _dot(e.astype(BF), vb) * (1.0 / l)
                    o = oh if o is None else o + oh
                outs[sub][pp] = o.astype(BF)
    return jnp.concatenate([jnp.concatenate(row, axis=1) for row in outs], axis=0)


def _layer_a_out_kernel(x_ref, tokp_ref, toks_ref, mq_ref, mk_ref, mv_ref, mos_ref, wo_ref,
                        g2_ref, wgu_ref, wd_ref, gkv_ref, wkv_ref, c_ref, s1_ref, s2_ref,
                        *refs, n_prompt, tiles_per_seq):
    n = (len(refs) - 5) // 2
    xo_ref, k_ref, v_ref = refs[n:n + 3]
    kbd_scr, vbd_scr = refs[-2:]
    _run_cast_jobs(refs[:n] + refs[n + 3:-2])
    i = pl.program_id(0)
    is_prompt = i < n_prompt

    @pl.when(i % tiles_per_seq == 0)
    def _():
        _build_mem_block_diag(mk_ref, mv_ref, kbd_scr, vbd_scr)

    nt = GLA_HEADS * GLA_DVP
    mo = _mem_attn_tile(mq_ref[...], kbd_scr[...], vbd_scr[...]).astype(BF)
    mo = jnp.where(is_prompt, mo, mos_ref[...])
    x = (x_ref[...] + _dot(_pick(is_prompt, tokp_ref, toks_ref), wo_ref[:nt, :])
         + _dot(mo, wo_ref[nt:, :]))
    x = _ffn_half(x, g2_ref, wgu_ref, wd_ref)
    xo_ref[...] = x
    h = _rms(x, gkv_ref[...]).astype(BF)
    c, s1, s2 = c_ref[...], s1_ref[...], s2_ref[...]
    kw = SWA_KV_HEADS * HEAD_DIMP
    kv = _dot(h, wkv_ref[...])
    for hh in range(SWA_KV_HEADS):
        sl = slice(hh * HEAD_DIMP, (hh + 1) * HEAD_DIMP)
        k_ref[:, sl] = _rope(kv[:, sl], c, s1, s2)
    v_ref[...] = kv[:, kw:]


def _tab_spec(n_prompt, blocks_per_seq):
    return pl.BlockSpec((TOKEN_TILE, HEAD_DIMP),
                        lambda i: (jnp.where(i < n_prompt, i % blocks_per_seq, blocks_per_seq), 0))


def _mem_kv_spec(layer, n_prompt, tiles_per_seq):
    return pl.BlockSpec(
        (1, 1, MEM_Q, MEM_TOKENS),
        lambda i: (layer, jnp.minimum(i, n_prompt - 1) // tiles_per_seq, 0, 0))


def _mem_scratch():
    return pltpu.VMEM((MEM_Q, MEM_HEADS * MEM_TOKENS), BF)


def _layer_a_out(x, tokp, toks, mq, mk_t, mv_t, mos, wo, g2, wgu, wd, gkv, wkv, tabs, casts,
                 *, seq):
    m = x.shape[0]
    grid = (m // TOKEN_TILE,)
    n_prompt = tokp.shape[0] // TOKEN_TILE
    tiles_per_seq = seq // TOKEN_TILE
    kw = SWA_KV_HEADS * HEAD_DIMP
    nt = GLA_HEADS * GLA_DVP
    tab = _tab_spec(n_prompt, tiles_per_seq)
    mem_spec = _mem_kv_spec(0, n_prompt, tiles_per_seq)
    jobs = [_cast_job(cw, layer, rows, grid) for cw, layer, rows in casts]
    return pl.pallas_call(
        functools.partial(_layer_a_out_kernel, n_prompt=n_prompt, tiles_per_seq=tiles_per_seq),
        grid=grid,
        in_specs=[_rows(D_MODEL), _prompt_rows(nt, n_prompt), _sample_rows(nt),
                  _rows(MEM_Q), mem_spec, mem_spec, _sample_rows(MEM_Q), _resident(wo.shape),
                  _layer_block(g2.shape, 0), _layer_block(wgu.shape, 0), _layer_block(wd.shape, 0),
                  _resident(gkv.shape), _resident(wkv.shape), tab, tab, tab]
                 + [j[0] for j in jobs],
        out_specs=[_rows(D_MODEL), _rows(kw), _rows(kw)] + [j[1] for j in jobs],
        out_shape=[jax.ShapeDtypeStruct((m, D_MODEL), F32),
                   jax.ShapeDtypeStruct((m, kw), F32), jax.ShapeDtypeStruct((m, kw), F32)]
                  + [j[2] for j in jobs],
        scratch_shapes=[_mem_scratch(), _mem_scratch()],
        compiler_params=_params("arbitrary"),
        name="layer_a_out",
    )(x, tokp, toks, mq, mk_t, mv_t, mos, wo, g2, wgu, wd, gkv, wkv, *tabs,
      *[c[0] for c in casts])


def _layer_b_in_kernel(x_ref, g1_ref, wgu_ref, wd_ref, gm_ref, w_ref, c_ref, s1_ref, s2_ref,
                       *refs):
    n = (len(refs) - 3) // 2
    xo_ref, q_ref, mq_ref = refs[n:n + 3]
    _run_cast_jobs(refs[:n] + refs[n + 3:])
    x = _ffn_half(x_ref[...], g1_ref, wgu_ref, wd_ref)
    xo_ref[...] = x
    h = _rms(x, gm_ref[0]).astype(BF)
    c, s1, s2 = c_ref[...], s1_ref[...], s2_ref[...]
    nq = SWA_Q_HEADS * HEAD_DIM
    qm = _dot(h, w_ref[...])
    for pp in range(SWA_Q_PAIRS):
        q = qm[:, pp * HEAD_DIMP:(pp + 1) * HEAD_DIMP]
        q_ref[pp] = (_rope(q, c, s1, s2) * (HEAD_DIM ** -0.5)).astype(BF)
    mq_ref[...] = (qm[:, nq:] * (MEM_HEAD_DIM ** -0.5)).astype(BF)


def _layer_b_in(x, g1, wgu, wd, gm, w, tabs, casts, *, n_prompt, seq):
    m = x.shape[0]
    grid = (m // TOKEN_TILE,)
    tab = _tab_spec(n_prompt, seq // TOKEN_TILE)
    q_spec = pl.BlockSpec((SWA_Q_PAIRS, TOKEN_TILE, HEAD_DIMP), lambda i: (0, i, 0))
    jobs = [_cast_job(cw, layer, rows, grid) for cw, layer, rows in casts]
    return pl.pallas_call(
        _layer_b_in_kernel,
        grid=grid,
        in_specs=[_rows(D_MODEL), _layer_block(g1.shape, 1), _layer_block(wgu.shape, 0),
                  _layer_block(wd.shape, 0), _layer_block(gm.shape, 1), _resident(w.shape),
                  tab, tab, tab] + [j[0] for j in jobs],
        out_specs=[_rows(D_MODEL), q_spec, _rows(MEM_Q)] + [j[1] for j in jobs],
        out_shape=[jax.ShapeDtypeStruct((m, D_MODEL), F32),
                   jax.ShapeDtypeStruct((SWA_Q_PAIRS, m, HEAD_DIMP), BF),
                   jax.ShapeDtypeStruct((m, MEM_Q), BF)] + [j[2] for j in jobs],
        compiler_params=_params("arbitrary"),
        name="layer_b_in",
    )(x, g1, wgu, wd, gm, w, *tabs, *[c[0] for c in casts])


def _layer_b_out_kernel(sink_ref, x_ref, q_ref, kprev_ref, kcur_ref, vprev_ref, vcur_ref,
                        mq_ref, mk_ref, mv_ref, toks_ref, mos_ref, wo_ref, g2_ref, wgu_ref,
                        wd_ref, gf_ref, yp_ref, ys_ref, kbd_scr, vbd_scr,
                        *, n_prompt, tiles_per_seq):
    i = pl.program_id(0)
    is_prompt = i < n_prompt
    first_of_seq = i % tiles_per_seq == 0

    @pl.when(first_of_seq)
    def _():
        _build_mem_block_diag(mk_ref, mv_ref, kbd_scr, vbd_scr)

    nt = SWA_Q_HEADS * HEAD_DIM
    mo = _mem_attn_tile(mq_ref[...], kbd_scr[...], vbd_scr[...]).astype(BF)
    mo = jnp.where(is_prompt, mo, mos_ref[...])
    tok = _swa_tile(q_ref, kprev_ref, kcur_ref, vprev_ref, vcur_ref, sink_ref, first_of_seq)
    toks = jnp.concatenate([toks_ref[pp] for pp in range(SWA_Q_PAIRS)], axis=1)
    tok = jnp.where(is_prompt, tok, toks)
    x = x_ref[...] + _dot(tok, wo_ref[:nt, :]) + _dot(mo, wo_ref[nt:, :])
    y = _rms(_ffn_half(x, g2_ref, wgu_ref, wd_ref), gf_ref[...])

    @pl.when(is_prompt)
    def _():
        yp_ref[...] = y

    @pl.when(jnp.logical_not(is_prompt))
    def _():
        ys_ref[...] = y


def _layer_b_out(sinks, x, qs, k_sh, v_sh, mq, mk_t, mv_t, toks, mos, wo, g2, wgu, wd, gf, *, seq):
    m = x.shape[0]
    n_tiles = m // TOKEN_TILE
    n_prompt = n_tiles - 1
    tiles_per_seq = seq // TOKEN_TILE
    kw = SWA_KV_HEADS * HEAD_DIMP
    blocks_per_tile = TOKEN_TILE // WINDOW
    q_spec = pl.BlockSpec((SWA_Q_PAIRS, TOKEN_TILE, HEAD_DIMP), lambda i: (0, i, 0))
    prev_spec = pl.BlockSpec((WINDOW, kw), lambda i: (jnp.maximum(i * blocks_per_tile - 1, 0), 0))
    toks_spec = pl.BlockSpec((SWA_Q_PAIRS, TOKEN_TILE, HEAD_DIMP), lambda i: (0, 0, 0),
                             pipeline_mode=pl.Buffered(1))
    mem_spec = _mem_kv_spec(1, n_prompt, tiles_per_seq)
    return pl.pallas_call(
        functools.partial(_layer_b_out_kernel, n_prompt=n_prompt, tiles_per_seq=tiles_per_seq),
        grid=(n_tiles,),
        in_specs=[pl.BlockSpec(memory_space=pltpu.SMEM), _rows(D_MODEL), q_spec,
                  prev_spec, _rows(kw), prev_spec, _rows(kw), _rows(MEM_Q), mem_spec, mem_spec,
                  toks_spec, _sample_rows(MEM_Q), _resident(wo.shape), _layer_block(g2.shape, 1),
                  _layer_block(wgu.shape, 0), _layer_block(wd.shape, 0), _resident(gf.shape)],
        out_specs=[_prompt_rows(D_MODEL, n_prompt),
                   pl.BlockSpec((TOKEN_TILE, D_MODEL), lambda i: (0, 0))],
        out_shape=[jax.ShapeDtypeStruct((n_prompt * TOKEN_TILE, D_MODEL), F32),
                   jax.ShapeDtypeStruct((TOKEN_TILE, D_MODEL), F32)],
        scratch_shapes=[_mem_scratch(), _mem_scratch()],
        compiler_params=_params("arbitrary"),
        name="layer_b_out",
    )(sinks, x, qs, k_sh, k_sh, v_sh, v_sh, mq, mk_t, mv_t, toks, mos, wo, g2, wgu, wd, gf)


def _mem_kv_kernel(x_ref, g_ref, wt_ref, *refs):
    n = (len(refs) - 2) // 2
    k_ref, v_ref = refs[n:n + 2]
    _run_cast_jobs(refs[:n] + refs[n + 2:])
    x = x_ref[...]
    xn = x * lax.rsqrt(jnp.mean(x * x, axis=-1, keepdims=True) + EPS)
    for l in range(2):
        h = (xn * g_ref[l]).astype(BF)
        kvt = _dot_nt(wt_ref[l], h)
        k_ref[l, 0] = kvt[:MEM_Q, :]
        v_ref[l, 0] = kvt[MEM_Q:, :]


def _mem_kv(mem, g, wt, casts, *, batch):
    out_spec = pl.BlockSpec((2, 1, MEM_Q, MEM_TOKENS), lambda b: (0, b, 0, 0))
    jobs = [_cast_job(cw, layer, rows, (batch,)) for cw, layer, rows in casts]
    return pl.pallas_call(
        _mem_kv_kernel,
        grid=(batch,),
        in_specs=[pl.BlockSpec((MEM_TOKENS, D_MODEL), lambda b: (b, 0)), _resident(g.shape),
                  _resident(wt.shape)] + [j[0] for j in jobs],
        out_specs=[out_spec, out_spec] + [j[1] for j in jobs],
        out_shape=[jax.ShapeDtypeStruct((2, batch, MEM_Q, MEM_TOKENS), F32)] * 2
                  + [j[2] for j in jobs],
        compiler_params=_params("arbitrary"),
        name="mem_kv",
    )(mem, g, wt, *[c[0] for c in casts])


def _gla_prompt_kernel(*refs):
    ns = GLA_SEQS
    seq_in = [refs[5 * i:5 * i + 5] for i in range(ns)]
    rest = refs[5 * ns:-1]
    s_scr = refs[-1]
    n = (len(rest) - 2) // 2
    tok_ref, st_ref = rest[n:n + 2]
    _run_cast_jobs(rest[:n] + rest[n + 2:])
    c = pl.program_id(1)
    C = GLA_CHUNK

    @pl.when(c == 0)
    def _():
        s_scr[...] = jnp.zeros(s_scr.shape, F32)

    row = lax.broadcasted_iota(jnp.int32, (C, C), 0)
    col = lax.broadcasted_iota(jnp.int32, (C, C), 1)
    causal = row >= col
    ltri = jnp.where(causal, 1.0, 0.0).astype(BF)
    _gla_chunks([(seq_in[i], tok_ref.at[i], s_scr.at[i], h)
                 for i in range(ns) for h in range(GLA_HEADS)], causal, ltri)

    @pl.when(c == pl.num_programs(1) - 1)
    def _():
        st_ref[...] = s_scr[...]


def _gla_chunks(chains, causal, ltri):
    C = GLA_CHUNK

    def sk(h):
        return slice(h * GLA_DKP, (h + 1) * GLA_DKP)

    def sv(h):
        return slice(h * GLA_DVP, (h + 1) * GLA_DVP)

    bs = []
    for (q_ref, k_ref, la_ref, v_ref, r_ref), _, _, h in chains:
        la = la_ref[:, sk(h)]
        hi = la.astype(BF)
        lo = (la - hi.astype(F32)).astype(BF)
        bb = _dot(ltri, jnp.concatenate([hi, lo], axis=1))
        bs.append(bb[:, :GLA_DKP] + bb[:, GLA_DKP:])
    ops = []
    for ((q_ref, k_ref, la_ref, v_ref, r_ref), _, _, h), b in zip(chains, bs):
        b_mid = b[C // 2 - 1:C // 2, :]
        b_last = b[C - 1:C, :]
        qe = q_ref[:, sk(h)] * jnp.exp(b - b_mid)
        ke = k_ref[:, sk(h)] * jnp.exp(b_mid - b)
        qb = (qe * jnp.exp(b_mid)).astype(BF)
        kd = (ke * jnp.exp(b_last - b_mid)).astype(BF)
        ops.append((qe.astype(BF), ke.astype(BF), qb, kd, jnp.exp(b_last)))
    As = [jnp.where(causal, _dot_nt(qe, ke), 0.0).astype(BF) for qe, ke, _, _, _ in ops]
    outs = []
    for ((q_ref, k_ref, la_ref, v_ref, r_ref), _, s_ref, h), a, (_, _, qb, kd, decay) in zip(
            chains, As, ops):
        v = v_ref[:, sv(h)].astype(BF)
        st = s_ref[h]
        outs.append(_dot(a, v) + _dot_nt(qb, st.astype(BF)))
        s_ref[h] = st * decay + _dot_tn(v, kd)
    for ((q_ref, k_ref, la_ref, v_ref, r_ref), tok_ref, _, h), o in zip(chains, outs):
        ms = jnp.sum(o * o, axis=1, keepdims=True) * (1.0 / GLA_DV)
        tok_ref[:, sv(h)] = (o * lax.rsqrt(ms + EPS) * r_ref[:, sv(h)]).astype(BF)


def _gla_prompt(q, k, la, v, r, casts, *, batch, seq):
    ns = GLA_SEQS
    m = batch * seq
    nc = seq // GLA_CHUNK
    qk = GLA_HEADS * GLA_DKP
    vr = GLA_HEADS * GLA_DVP
    grid = (batch // ns, nc)

    def tok_map(i):
        return lambda b, c: ((b * ns + i) * nc + c, 0)

    seq_specs = []
    for i in range(ns):
        seq_specs += [pl.BlockSpec((GLA_CHUNK, qk), tok_map(i))] * 3
        seq_specs += [pl.BlockSpec((GLA_CHUNK, vr), tok_map(i))] * 2
    jobs = [_cast_job(cw, layer, rows, grid) for cw, layer, rows in casts]
    res = pl.pallas_call(
        _gla_prompt_kernel,
        grid=grid,
        in_specs=seq_specs + [j[0] for j in jobs],
        out_specs=[pl.BlockSpec((ns, GLA_CHUNK, vr), lambda b, c: (b, c, 0)),
                   pl.BlockSpec((ns, GLA_HEADS, GLA_DVP, GLA_DKP), lambda b, c: (b, 0, 0, 0))]
                  + [j[1] for j in jobs],
        out_shape=[jax.ShapeDtypeStruct((batch, seq, vr), BF),
                   jax.ShapeDtypeStruct((batch, GLA_HEADS, GLA_DVP, GLA_DKP), F32)]
                  + [j[2] for j in jobs],
        scratch_shapes=[pltpu.VMEM((ns, GLA_HEADS, GLA_DVP, GLA_DKP), F32)],
        compiler_params=_params("arbitrary", "arbitrary"),
        name="gla_prompt",
    )(*([q, k, la, v, r] * ns), *[c[0] for c in casts])
    return (res[0].reshape(m, vr),) + tuple(res[1:])


_SAMPLE_BB = 16
_DEC_SEQ = 4
_DEC_BATCH = 128
_GLA_DK_BLK = 32
_NEW_ROWS = 16


def _gla_sample_kernel(s_ref, q_ref, k_ref, la_ref, v_ref, r_ref, so_ref, tok_ref, o_scr):
    j = pl.program_id(1)

    @pl.when(j == 0)
    def _():
        o_scr[...] = jnp.zeros(o_scr.shape, F32)

    def body(dk, carry):
        s = s_ref[0, 0, dk]
        for t in range(_DEC_SEQ):
            a = jnp.exp(la_ref[t, 0, pl.ds(dk, 1), :])
            s = a * s + k_ref[t, 0, pl.ds(dk, 1), :] * v_ref[t, 0, :GLA_DV, :]
            o_scr[t] = o_scr[t] + q_ref[t, 0, pl.ds(dk, 1), :] * s
        so_ref[0, 0, dk] = s
        return carry

    lax.fori_loop(0, _GLA_DK_BLK, body, 0, unroll=4)

    @pl.when(j == pl.num_programs(1) - 1)
    def _():
        tok_ref[...] = jnp.zeros(tok_ref.shape, F32)
        for t in range(_DEC_SEQ):
            o = o_scr[t]
            ms = jnp.sum(o * o, axis=0, keepdims=True) * (1.0 / GLA_DV)
            tok_ref[t, 0, :GLA_DV, :] = o * lax.rsqrt(ms + EPS) * r_ref[t, 0, :GLA_DV, :]


def _gla_sample(state, q, k, la, v, r):
    qk_spec = pl.BlockSpec((_DEC_SEQ, 1, _GLA_DK_BLK, _DEC_BATCH), lambda h, j: (0, h, j, 0))
    vr_spec = pl.BlockSpec((_DEC_SEQ, 1, GLA_DVP, _DEC_BATCH), lambda h, j: (0, h, 0, 0))
    s_spec = pl.BlockSpec((1, 1, _GLA_DK_BLK, GLA_DV, _DEC_BATCH), lambda h, j: (0, h, j, 0, 0))
    return pl.pallas_call(
        _gla_sample_kernel,
        grid=(GLA_HEADS, GLA_DK // _GLA_DK_BLK),
        in_specs=[s_spec, qk_spec, qk_spec, qk_spec, vr_spec, vr_spec],
        out_specs=[s_spec, vr_spec],
        out_shape=[jax.ShapeDtypeStruct(state.shape, F32),
                   jax.ShapeDtypeStruct(v.shape, F32)],
        scratch_shapes=[pltpu.VMEM((_DEC_SEQ, GLA_DV, _DEC_BATCH), F32)],
        compiler_params=_params("parallel", "arbitrary"),
        name="gla_sample",
    )(state, q, k, la, v, r)


def _mem_attn_sample_kernel(q_ref, mk_ref, mv_ref, o_ref):
    scores = [_dot(q_ref[bi], mk_ref[0, bi].astype(BF)) for bi in range(_SAMPLE_BB)]
    probs = []
    for s in scores:
        e = jnp.exp(s - jnp.max(s, axis=1, keepdims=True))
        probs.append((e * (1.0 / jnp.sum(e, axis=1, keepdims=True))).astype(BF))
    for bi, p in enumerate(probs):
        o_ref[bi] = _dot_nt(p, mv_ref[0, bi].astype(BF))


def _mem_attn_sample(qbd, mk_t, mv_t, layer):
    nb = qbd.shape[0]
    nr = MEM_HEADS * _DEC_SEQ
    kv_spec = pl.BlockSpec((1, _SAMPLE_BB, MEM_Q, MEM_TOKENS), lambda i: (layer, i, 0, 0))
    q_spec = pl.BlockSpec((_SAMPLE_BB, nr, MEM_Q), lambda i: (i, 0, 0))
    return pl.pallas_call(
        _mem_attn_sample_kernel,
        grid=(nb // _SAMPLE_BB,),
        in_specs=[q_spec, kv_spec, kv_spec],
        out_specs=q_spec,
        out_shape=jax.ShapeDtypeStruct((nb, nr, MEM_Q), F32),
        compiler_params=_params("parallel"),
        name="mem_attn_sample",
    )(qbd, mk_t, mv_t)


def _swa_sample_kernel(q_ref, sink_ref, kc_ref, vc_ref, kn_ref, vn_ref, o_ref, ko_ref, vo_ref):
    nq = SWA_Q_HEADS * _DEC_SEQ
    t = lax.broadcasted_iota(jnp.int32, (nq, WINDOW), 0) % _DEC_SEQ
    pos = lax.broadcasted_iota(jnp.int32, (nq, WINDOW), 1)
    bias_c = jnp.where(pos > t, 0.0, -jnp.inf)
    tn = lax.broadcasted_iota(jnp.int32, (nq, _NEW_ROWS), 0) % _DEC_SEQ
    new = lax.broadcasted_iota(jnp.int32, (nq, _NEW_ROWS), 1)
    bias_n = jnp.where(new <= tn, 0.0, -jnp.inf)
    sink = sink_ref[...]
    seqs = range(_SAMPLE_BB)
    scores = [(_dot(q_ref[bi], kc_ref[bi].astype(BF)) + bias_c,
               _dot_nt(q_ref[bi], kn_ref[bi].astype(BF)) + bias_n) for bi in seqs]
    probs = []
    for sc, sn in scores:
        m = jnp.maximum(jnp.maximum(jnp.max(sc, axis=1, keepdims=True),
                                    jnp.max(sn, axis=1, keepdims=True)), sink)
        ec = jnp.exp(sc - m)
        en = jnp.exp(sn - m)
        l = (jnp.sum(ec, axis=1, keepdims=True) + jnp.sum(en, axis=1, keepdims=True)
             + jnp.exp(sink - m))
        inv = 1.0 / l
        probs.append(((ec * inv).astype(BF), (en * inv).astype(BF)))
    for bi, (pc, pn) in zip(seqs, probs):
        o_ref[bi] = (_dot_nt(pc, vc_ref[bi].astype(BF)) + _dot(pn, vn_ref[bi].astype(BF)))
    new0 = WINDOW - _DEC_SEQ
    row = lax.broadcasted_iota(jnp.int32, (_NEW_ROWS, WINDOW), 0)
    col = lax.broadcasted_iota(jnp.int32, (_NEW_ROWS, WINDOW), 1)
    place = jnp.where((col == new0 + row) & (row < _DEC_SEQ), 1.0, 0.0)
    is_new = lax.broadcasted_iota(jnp.int32, (SWA_KV_HEADS * HEAD_DIM, WINDOW), 1) >= new0

    def placed(new_rows):
        return lax.dot_general(new_rows, place, (((0,), (0,)), ((), ())),
                               precision=lax.Precision.HIGHEST, preferred_element_type=F32)

    for bi in seqs:
        ko_ref[bi] = jnp.where(is_new, placed(kn_ref[bi]), pltpu.roll(kc_ref[bi], new0, 1))
        vo_ref[bi] = jnp.where(is_new, placed(vn_ref[bi]), pltpu.roll(vc_ref[bi], new0, 1))


def _swa_sample(qbd, sink_col, kc, vc, kn, vn):
    nb = qbd.shape[0]
    kw = SWA_KV_HEADS * HEAD_DIM
    nq = SWA_Q_HEADS * _DEC_SEQ
    kv_spec = pl.BlockSpec((_SAMPLE_BB, kw, WINDOW), lambda i: (i, 0, 0))
    new_spec = pl.BlockSpec((_SAMPLE_BB, _NEW_ROWS, kw), lambda i: (i, 0, 0))
    q_spec = pl.BlockSpec((_SAMPLE_BB, nq, kw), lambda i: (i, 0, 0))
    return pl.pallas_call(
        _swa_sample_kernel,
        grid=(nb // _SAMPLE_BB,),
        in_specs=[q_spec, _resident((nq, 1)), kv_spec, kv_spec, new_spec, new_spec],
        out_specs=[q_spec, kv_spec, kv_spec],
        out_shape=[jax.ShapeDtypeStruct((nb, nq, kw), F32),
                   jax.ShapeDtypeStruct(kc.shape, F32), jax.ShapeDtypeStruct(kc.shape, F32)],
        compiler_params=_params("parallel"),
        name="swa_sample",
    )(qbd, sink_col, kc, vc, kn, vn)


def _pad_heads(w, heads, dim, dim_p, axis):
    shape = w.shape
    w = w.reshape(shape[:axis] + (heads, dim) + shape[axis + 1:])
    pad = [(0, 0)] * w.ndim
    pad[axis + 1] = (0, dim_p - dim)
    w = jnp.pad(w, pad)
    return w.reshape(shape[:axis] + (heads * dim_p,) + shape[axis + 1:])


def _rope_tables(pos):
    half = ROT_DIM // 2
    inv_freq = np.exp(-math.log(ROPE_THETA) * np.arange(0, ROT_DIM, 2, dtype=np.float64) / ROT_DIM)
    ang = pos.astype(np.float64)[:, None] * inv_freq[None, :]
    cos, sin = np.cos(ang), np.sin(ang)
    n = pos.shape[0]
    rest = HEAD_DIM - ROT_DIM
    c = np.concatenate([cos, cos, np.ones((n, rest))], axis=1)
    s1 = np.concatenate([-sin, np.zeros((n, HEAD_DIM - half))], axis=1)
    s2 = np.concatenate([np.zeros((n, half)), sin, np.zeros((n, rest))], axis=1)
    return tuple(np.tile(a, (1, HEAD_DIMP // HEAD_DIM)).astype(np.float32) for a in (c, s1, s2))


def _prep_weights(p):
    w = {}
    for name in ("ffn1_norm", "ffn2_norm", "mix_norm"):
        w[name] = p[name][:, None, :]
    qk = GLA_HEADS * GLA_DK
    vv = GLA_HEADS * GLA_DV
    a_in = p["a_w_in"][0].astype(BF)
    w_qk = _pad_heads(a_in[:, :2 * qk], 2 * GLA_HEADS, GLA_DK, GLA_DKP, 1)
    w_vr = _pad_heads(a_in[:, 2 * qk:2 * (qk + vv)], 2 * GLA_HEADS, GLA_DV, GLA_DVP, 1)
    o = 2 * (qk + vv)
    w_g = jnp.pad(a_in[:, o:o + GLA_RANK], ((0, 0), (0, GLA_RANKP - GLA_RANK)))
    w["a_in"] = jnp.concatenate([w_qk, w_vr, w_g, a_in[:, o + GLA_RANK:]], axis=1)
    gate = _pad_heads(p["a_w_gate"][0], GLA_HEADS, GLA_DK, GLA_DKP, 1)
    w["a_gate"] = jnp.pad(gate, ((0, GLA_RANKP - GLA_RANK), (0, 0))).astype(BF)
    w["a_bgate"] = _pad_heads(p["a_b_gate"][0][None, :], GLA_HEADS, GLA_DK, GLA_DKP, 1)
    w["a_gn"] = jnp.tile(jnp.pad(p["a_out_norm"][0], (0, GLA_DVP - GLA_DV)), GLA_HEADS)[None, :]
    a_out = p["a_w_out"][0]
    w["a_out"] = jnp.concatenate(
        [_pad_heads(a_out[:vv], GLA_HEADS, GLA_DV, GLA_DVP, 0), a_out[vv:]], axis=0).astype(BF)
    w["b_in"] = p["b_w_in"][0].astype(BF)
    w["b_out"] = p["b_w_out"][0].astype(BF)
    w["kv"] = _pad_heads(p["w_kv"].astype(BF), 2 * SWA_KV_HEADS, HEAD_DIM, HEAD_DIMP, 1)
    w["mem_t"] = p["mem_w_kv"].transpose(0, 2, 1).astype(BF)
    return w


def _compact_kv(a, batch, seq):
    return a.reshape(batch, seq, SWA_KV_HEADS, HEAD_DIMP)[..., :HEAD_DIM]


def kernel(x_prompt, x_sample, state_gla, cache_swa_k, cache_swa_v, cache_mem_k, cache_mem_v,
           mem_prompt, ffn1_norm, ffn1_w_gu, ffn1_w_down, mix_norm, ffn2_norm, ffn2_w_gu,
           ffn2_w_down, mem_norm, mem_w_kv, a_w_in, a_w_gate, a_b_gate, a_out_norm, a_w_out,
           kv_norm, w_kv, b_w_in, b_sinks, b_w_out, final_norm):
    p = dict(ffn1_norm=ffn1_norm, ffn1_w_gu=ffn1_w_gu, ffn1_w_down=ffn1_w_down,
             mix_norm=mix_norm, ffn2_norm=ffn2_norm, ffn2_w_gu=ffn2_w_gu,
             ffn2_w_down=ffn2_w_down, mem_w_kv=mem_w_kv, a_w_in=a_w_in, a_w_gate=a_w_gate,
             a_b_gate=a_b_gate, a_out_norm=a_out_norm, a_w_out=a_w_out, w_kv=w_kv,
             b_w_in=b_w_in, b_w_out=b_w_out)
    w = _prep_weights(p)
    batch, seq, _ = x_prompt.shape
    nb, t, _ = x_sample.shape
    assert nb == _DEC_BATCH and t == _DEC_SEQ and nb * t == TOKEN_TILE
    assert seq % TOKEN_TILE == 0 and seq % GLA_CHUNK == 0
    mp = batch * seq
    kw = SWA_KV_HEADS * HEAD_DIM
    sinks = b_sinks[0]

    mem_k_t, mem_v_t, wgu_a1, wd_a1 = _mem_kv(
        mem_prompt.reshape(batch * MEM_TOKENS, D_MODEL), mem_norm[:, None, :], w["mem_t"],
        [(ffn1_w_gu, 0, D_MODEL // batch), (ffn1_w_down, 0, FFN_DIM // batch)], batch=batch)
    state_t = state_gla.transpose(0, 2, 3, 4, 1)
    kc_t = cache_swa_k.transpose(0, 2, 3, 1).reshape(nb, kw, WINDOW)
    vc_t = cache_swa_v.transpose(0, 2, 3, 1).reshape(nb, kw, WINDOW)
    cmk_t = cache_mem_k.transpose(0, 1, 3, 4, 2).reshape(2, nb, MEM_Q, MEM_TOKENS)
    cmv_t = cache_mem_v.transpose(0, 1, 3, 4, 2).reshape(2, nb, MEM_Q, MEM_TOKENS)

    tabs = tuple(
        jnp.asarray(np.concatenate([a, np.tile(b, (nb, 1))], axis=0))
        for a, b in zip(_rope_tables(np.arange(seq)), _rope_tables(PAST_LEN + np.arange(t))))

    mem_mask = (np.arange(MEM_Q) // MEM_HEAD_DIM)[None, :] == np.arange(MEM_HEADS)[:, None]
    kv_mask = (np.arange(kw) // HEAD_DIM)[None, :] == np.arange(SWA_KV_HEADS)[:, None]

    def mem_attn_sample(mq, layer):
        q4 = mq[mp:].reshape(nb, 1, t, MEM_Q)
        qbd = jnp.where(mem_mask[None, :, None, :], q4, 0).reshape(nb, MEM_HEADS * t, MEM_Q)
        o = _mem_attn_sample(qbd, cmk_t, cmv_t, layer).reshape(nb, MEM_HEADS, t, MEM_Q)
        o = jnp.sum(jnp.where(mem_mask[None, :, None, :], o, 0.0), axis=1)
        return o.reshape(nb * t, MEM_Q).astype(BF)

    def lanes(a, width):
        return a[mp:].reshape(nb, t, GLA_HEADS, width).transpose(1, 2, 3, 0)

    x, q, k, la, v, r, mq, wgu_a2, wd_a2 = _layer_a_in(
        x_prompt.reshape(mp, D_MODEL), x_sample.reshape(nb * t, D_MODEL), w["ffn1_norm"],
        wgu_a1, wd_a1, w["mix_norm"], w["a_in"], w["a_gate"], w["a_bgate"], w["a_gn"],
        [(ffn2_w_gu, 0, 32), (ffn2_w_down, 0, 128)])
    tok_p, st_p = _gla_prompt(q, k, la, v, r, [], batch=batch, seq=seq)
    st_s, tok_s = _gla_sample(state_t, lanes(q, GLA_DKP), lanes(k, GLA_DKP), lanes(la, GLA_DKP),
                              lanes(v, GLA_DVP), lanes(r, GLA_DVP))
    tok_s = tok_s.transpose(3, 0, 1, 2).reshape(nb * t, GLA_HEADS * GLA_DVP).astype(BF)
    mo_s = mem_attn_sample(mq, 0)
    x, k_sh, v_sh, wgu_b1, wd_b1 = _layer_a_out(
        x, tok_p, tok_s, mq, mem_k_t, mem_v_t, mo_s, w["a_out"], w["ffn2_norm"], wgu_a2, wd_a2,
        kv_norm[None, :], w["kv"], tabs, [(ffn1_w_gu, 1, 32), (ffn1_w_down, 1, 128)], seq=seq)

    x, qs, mq, wgu_b2, wd_b2 = _layer_b_in(
        x, w["ffn1_norm"], wgu_b1, wd_b1, w["mix_norm"], w["b_in"], tabs,
        [(ffn2_w_gu, 1, 32), (ffn2_w_down, 1, 128)], n_prompt=mp // TOKEN_TILE, seq=seq)

    k_new = _compact_kv(k_sh[mp:], nb, t)
    v_new = _compact_kv(v_sh[mp:], nb, t)

    def new_rows(a):
        return jnp.pad(a.reshape(nb, t, kw), ((0, 0), (0, _NEW_ROWS - t), (0, 0)))

    q5 = qs[:, mp:].reshape(SWA_Q_PAIRS, nb, t, 2, HEAD_DIM).transpose(1, 0, 3, 2, 4)
    q5 = q5.reshape(nb, SWA_KV_HEADS, SWA_GROUP, t, HEAD_DIM)
    qbd = jnp.where(kv_mask[None, :, None, None, :], jnp.tile(q5, (1, 1, 1, 1, SWA_KV_HEADS)), 0)
    qbd = qbd.reshape(nb, SWA_Q_HEADS * t, kw)
    o, k_s, v_s = _swa_sample(qbd, jnp.repeat(sinks, t)[:, None], kc_t, vc_t,
                              new_rows(k_new), new_rows(v_new))
    o = o.reshape(nb, SWA_KV_HEADS, SWA_GROUP, t, kw)
    o = jnp.where(kv_mask[None, :, None, None, :], o, 0.0)
    o = o.reshape(nb, SWA_KV_HEADS, SWA_GROUP, t, SWA_KV_HEADS, HEAD_DIM).sum(axis=4)
    tok_s = o.reshape(nb, SWA_Q_PAIRS, 2, t, HEAD_DIM).transpose(1, 0, 3, 2, 4)
    tok_s = tok_s.reshape(SWA_Q_PAIRS, nb * t, HEAD_DIMP).astype(BF)
    mo_s = mem_attn_sample(mq, 1)
    y_p, y_s = _layer_b_out(sinks, x, qs, k_sh, v_sh, mq, mem_k_t, mem_v_t, tok_s, mo_s,
                            w["b_out"], w["ffn2_norm"], wgu_b2, wd_b2, final_norm[None, :],
                            seq=seq)

    gla_prompt = st_p.transpose(0, 1, 3, 2)[None, :, :, :GLA_DK, :GLA_DV]
    gla_sample = st_s.transpose(0, 4, 1, 2, 3)
    def last_window(a):
        tiles_per_seq = seq // TOKEN_TILE
        a = a.reshape(-1, TOKEN_TILE, SWA_KV_HEADS * HEAD_DIMP)
        a = a[tiles_per_seq - 1:batch * tiles_per_seq:tiles_per_seq, TOKEN_TILE - WINDOW:]
        return a.reshape(batch, WINDOW, SWA_KV_HEADS, HEAD_DIMP)[..., :HEAD_DIM]

    swa_k_prompt = last_window(k_sh)
    swa_v_prompt = last_window(v_sh)
    swa_k_sample = k_s.reshape(nb, SWA_KV_HEADS, HEAD_DIM, WINDOW).transpose(0, 3, 1, 2)
    swa_v_sample = v_s.reshape(nb, SWA_KV_HEADS, HEAD_DIM, WINDOW).transpose(0, 3, 1, 2)
    mem_shape = (2, batch, MEM_HEADS, MEM_HEAD_DIM, MEM_TOKENS)
    mem_k_prompt = mem_k_t.reshape(mem_shape).transpose(0, 1, 4, 2, 3)
    mem_v_prompt = mem_v_t.reshape(mem_shape).transpose(0, 1, 4, 2, 3)
    return (y_p.reshape(batch, seq, D_MODEL), y_s.reshape(nb, t, D_MODEL), gla_prompt,
            gla_sample, swa_k_prompt, swa_v_prompt, swa_k_sample, swa_v_sample,
            mem_k_prompt, mem_v_prompt)
```

```python
import functools
import math

import jax
import jax.numpy as jnp
import numpy as np
from jax import lax
from jax.experimental import pallas as pl
from jax.experimental.pallas import tpu as pltpu

F32 = jnp.float32
BF = jnp.bfloat16

D_MODEL = 1024
FFN_DIM = 2816
EPS = 1e-6

GLA_HEADS = 4
GLA_DK = 96
GLA_DV = 192
GLA_DKP = 128
GLA_DVP = 256
GLA_RANK = 16
GLA_RANKP = 128
GLA_GATE_NORM = 16.0
GLA_CHUNK = 256
GLA_SEQS = 4

HEAD_DIM = 64
HEAD_DIMP = 128
SWA_Q_HEADS = 12
SWA_KV_HEADS = 3
SWA_GROUP = SWA_Q_HEADS // SWA_KV_HEADS
SWA_Q_PAIRS = SWA_Q_HEADS // 2
SWA_PAIRS_PER_KV = SWA_GROUP // 2
WINDOW = 128
ROT_DIM = 16
ROPE_THETA = 500000.0
PAST_LEN = 8192

MEM_TOKENS = 256
MEM_HEADS = 4
MEM_HEAD_DIM = 64
MEM_Q = MEM_HEADS * MEM_HEAD_DIM

FFN_TF = 256
FFN_CHUNKS = FFN_DIM // FFN_TF
TOKEN_TILE = 512

VMEM_LIMIT = 60 * 1024 * 1024


def _params(*sem):
    return pltpu.CompilerParams(dimension_semantics=sem, vmem_limit_bytes=VMEM_LIMIT)


def _resident(shape):
    nd = len(shape)
    return pl.BlockSpec(shape, lambda *_: (0,) * nd, pipeline_mode=pl.Buffered(1))


def _layer_block(shape, layer):
    nd = len(shape)
    return pl.BlockSpec((1,) + tuple(shape[1:]), lambda *_: (layer,) + (0,) * (nd - 1),
                        pipeline_mode=pl.Buffered(1))


def _rows(width):
    return pl.BlockSpec((TOKEN_TILE, width), lambda i: (i, 0))


def _prompt_rows(width, n_prompt):
    return pl.BlockSpec((TOKEN_TILE, width), lambda i: (jnp.minimum(i, n_prompt - 1), 0))


def _sample_rows(width):
    return pl.BlockSpec((TOKEN_TILE, width), lambda i: (0, 0), pipeline_mode=pl.Buffered(1))


def _rms(x, g):
    ms = jnp.mean(x * x, axis=-1, keepdims=True)
    return x * lax.rsqrt(ms + EPS) * g


def _silu(x):
    return x * (1.0 / (1.0 + jnp.exp(-x)))


def _dot(a, b):
    return jnp.dot(a, b, preferred_element_type=F32)


def _dot_nt(a, b):
    return lax.dot_general(a, b, (((1,), (1,)), ((), ())), preferred_element_type=F32)


def _dot_tn(a, b):
    return lax.dot_general(a, b, (((0,), (0,)), ((), ())), preferred_element_type=F32)


def _pick(is_prompt, p_ref, s_ref):
    return jnp.where(is_prompt, p_ref[...], s_ref[...])


def _cast_job(w, layer, rows, grid):
    _, r, c = w.shape
    assert r % rows == 0 and rows % 16 == 0
    nblk = r // rows
    total = math.prod(grid)
    assert nblk <= total
    steps_per_block = total // nblk

    def block(*idx):
        step = idx[0]
        for dim, i in zip(grid[1:], idx[1:]):
            step = step * dim + i
        return jnp.minimum(step // steps_per_block, nblk - 1)

    in_spec = pl.BlockSpec((1, rows, c), lambda *idx: (layer, block(*idx), 0))
    out_spec = pl.BlockSpec((1, rows, c), lambda *idx: (0, block(*idx), 0))
    return in_spec, out_spec, jax.ShapeDtypeStruct((1, r, c), BF)


def _run_cast_jobs(refs):
    n = len(refs) // 2
    for src, dst in zip(refs[:n], refs[n:]):
        dst[...] = src[...].astype(BF)


def _ffn_half(x, g_ref, wgu_ref, wd_ref):
    h = _rms(x, g_ref[0]).astype(BF)
    acc = jnp.zeros(x.shape, F32)
    for c in range(FFN_CHUNKS):
        lo, hi = c * FFN_TF, (c + 1) * FFN_TF
        gate = _dot(h, wgu_ref[0, :, lo:hi])
        up = _dot(h, wgu_ref[0, :, FFN_DIM + lo:FFN_DIM + hi])
        a = (_silu(gate) * up).astype(BF)
        acc = acc + _dot(a, wd_ref[0, lo:hi, :])
    return x + 0.5 * acc


def _rope(x, c, s1, s2):
    return (x * c + pltpu.roll(x, HEAD_DIMP - ROT_DIM // 2, 1) * s1
            + pltpu.roll(x, ROT_DIM // 2, 1) * s2)


_A_SECTIONS = (("q", GLA_HEADS * GLA_DKP), ("k", GLA_HEADS * GLA_DKP),
               ("v", GLA_HEADS * GLA_DVP), ("r", GLA_HEADS * GLA_DVP),
               ("g", GLA_RANKP), ("m", MEM_Q))
_A_COLS = {}
for _name, _width in _A_SECTIONS:
    _lo = max((hi for _, hi in _A_COLS.values()), default=0)
    _A_COLS[_name] = (_lo, _lo + _width)


def _layer_a_in_kernel(xp_ref, xs_ref, g1_ref, wgu_ref, wd_ref, gm_ref, w_ref, wg_ref, bg_ref,
                       gn_ref, *refs, n_prompt):
    n = (len(refs) - 7) // 2
    x_ref, q_ref, k_ref, la_ref, v_ref, r_ref, mq_ref = refs[n:n + 7]
    _run_cast_jobs(refs[:n] + refs[n + 7:])
    is_prompt = pl.program_id(0) < n_prompt
    x = _ffn_half(_pick(is_prompt, xp_ref, xs_ref), g1_ref, wgu_ref, wd_ref)
    x_ref[...] = x
    h = _rms(x, gm_ref[0]).astype(BF)

    proj = _dot(h, w_ref[...])

    def mm(name):
        lo, hi = _A_COLS[name]
        return proj[:, lo:hi]

    z = _dot(mm("g").astype(BF), wg_ref[...]) + bg_ref[...]
    la_ref[...] = (jnp.minimum(z, 0.0) - jnp.log1p(jnp.exp(-jnp.abs(z)))) * (1.0 / GLA_GATE_NORM)
    r_ref[...] = _silu(mm("r")) * gn_ref[...]
    q_ref[...] = mm("q") * (GLA_DK ** -0.5)
    k_ref[...] = mm("k")
    v_ref[...] = mm("v")
    mq_ref[...] = (mm("m") * (MEM_HEAD_DIM ** -0.5)).astype(BF)


def _layer_a_in(xp, xs, g1, wgu, wd, gm, w, wg, bg, gn, casts):
    n_prompt = xp.shape[0] // TOKEN_TILE
    m = xp.shape[0] + xs.shape[0]
    grid = (m // TOKEN_TILE,)
    qk = GLA_HEADS * GLA_DKP
    vr = GLA_HEADS * GLA_DVP
    outs = ((D_MODEL, F32), (qk, F32), (qk, F32), (qk, F32), (vr, F32), (vr, F32), (MEM_Q, BF))
    jobs = [_cast_job(cw, layer, rows, grid) for cw, layer, rows in casts]
    return pl.pallas_call(
        functools.partial(_layer_a_in_kernel, n_prompt=n_prompt),
        grid=grid,
        in_specs=[_prompt_rows(D_MODEL, n_prompt), _sample_rows(D_MODEL),
                  _layer_block(g1.shape, 0), _layer_block(wgu.shape, 0), _layer_block(wd.shape, 0),
                  _layer_block(gm.shape, 0), _resident(w.shape), _resident(wg.shape),
                  _resident(bg.shape), _resident(gn.shape)] + [j[0] for j in jobs],
        out_specs=[_rows(n) for n, _ in outs] + [j[1] for j in jobs],
        out_shape=[jax.ShapeDtypeStruct((m, n), dt) for n, dt in outs] + [j[2] for j in jobs],
        compiler_params=_params("arbitrary"),
        name="layer_a_in",
    )(xp, xs, g1, wgu, wd, gm, w, wg, bg, gn, *[c[0] for c in casts])


def _build_mem_block_diag(mk_ref, mv_ref, kbd_scr, vbd_scr):
    shape = (MEM_Q, MEM_HEADS * MEM_TOKENS)
    rh = lax.broadcasted_iota(jnp.int32, shape, 0) // MEM_HEAD_DIM
    ch = lax.broadcasted_iota(jnp.int32, shape, 1) // MEM_TOKENS
    diag = rh == ch
    kbd_scr[...] = jnp.where(diag, jnp.concatenate([mk_ref[0, 0]] * MEM_HEADS, axis=1),
                             0.0).astype(BF)
    vbd_scr[...] = jnp.where(diag, jnp.concatenate([mv_ref[0, 0]] * MEM_HEADS, axis=1),
                             0.0).astype(BF)


def _mem_attn_tile(q, kbd, vbd):
    tq = q.shape[0]
    s = _dot(q, kbd)
    sh = [s[:, h * MEM_TOKENS:(h + 1) * MEM_TOKENS] for h in range(MEM_HEADS)]
    ms = [jnp.max(x, axis=1, keepdims=True) for x in sh]
    es = [jnp.exp(x - m) for x, m in zip(sh, ms)]
    inv = [1.0 / jnp.sum(e, axis=1, keepdims=True) for e in es]
    o = _dot_nt(jnp.concatenate([e.astype(BF) for e in es], axis=1), vbd)
    lane_h = lax.broadcasted_iota(jnp.int32, (tq, MEM_Q), 1) // MEM_HEAD_DIM
    scale = jnp.where(lane_h == 0, inv[0],
                      jnp.where(lane_h == 1, inv[1], jnp.where(lane_h == 2, inv[2], inv[3])))
    return o * scale


def _swa_tile(q_ref, kprev_ref, kcur_ref, vprev_ref, vcur_ref, sink_ref, first_of_seq):
    blk = WINDOW
    nsub = TOKEN_TILE // blk
    qi = lax.broadcasted_iota(jnp.int32, (blk, 2 * blk), 0)
    kj = lax.broadcasted_iota(jnp.int32, (blk, 2 * blk), 1)
    d = blk + qi - kj
    band = (d >= 0) & (d < WINDOW)
    bias_mid = jnp.where(band, 0.0, -jnp.inf)
    bias_first = jnp.where(band & (kj >= blk), 0.0, -jnp.inf)
    bias0 = jnp.where(first_of_seq, bias_first, bias_mid)
    outs = [[None] * SWA_Q_PAIRS for _ in range(nsub)]
    for kh in range(SWA_KV_HEADS):
        sl = slice(kh * HEAD_DIMP, (kh + 1) * HEAD_DIMP)
        kblk = [kprev_ref[:, sl]] + [kcur_ref[s * blk:(s + 1) * blk, sl] for s in range(nsub)]
        vblk = [vprev_ref[:, sl]] + [vcur_ref[s * blk:(s + 1) * blk, sl] for s in range(nsub)]
        k2 = [(b.astype(BF), pltpu.roll(b, HEAD_DIM, 1).astype(BF)) for b in kblk]
        v2 = [(b.astype(BF), pltpu.roll(b, HEAD_DIM, 1).astype(BF)) for b in vblk]
        for sub in range(nsub):
            bias = bias0 if sub == 0 else bias_mid
            for pj in range(SWA_PAIRS_PER_KV):
                pp = kh * SWA_PAIRS_PER_KV + pj
                q = q_ref[pp, sub * blk:(sub + 1) * blk, :]
                o = None
                for half in range(2):
                    kb = jnp.concatenate([k2[sub][half], k2[sub + 1][half]], axis=0)
                    vb = jnp.concatenate([v2[sub][half], v2[sub + 1][half]], axis=0)
                    s = _dot_nt(q, kb) + bias
                    sink = sink_ref[2 * pp + half]
                    m = jnp.maximum(jnp.max(s, axis=1, keepdims=True), sink)
                    e = jnp.exp(s - m)
                    l = jnp.sum(e, axis=1, keepdims=True) + jnp.exp(sink - m)
                    oh = _dot(e.astype(BF), vb) * (1.0 / l)
                    o = oh if o is None else o + oh
                outs[sub][pp] = o.astype(BF)
    return jnp.concatenate([jnp.concatenate(row, axis=1) for row in outs], axis=0)


def _layer_a_out_kernel(x_ref, tokp_ref, toks_ref, mq_ref, mk_ref, mv_ref, mos_ref, wo_ref,
                        g2_ref, wgu_ref, wd_ref, gkv_ref, wkv_ref, c_ref, s1_ref, s2_ref,
                        *refs, n_prompt, tiles_per_seq):
    n = (len(refs) - 5) // 2
    xo_ref, k_ref, v_ref = refs[n:n + 3]
    kbd_scr, vbd_scr = refs[-2:]
    _run_cast_jobs(refs[:n] + refs[n + 3:-2])
    i = pl.program_id(0)
    is_prompt = i < n_prompt

    @pl.when(i % tiles_per_seq == 0)
    def _():
        _build_mem_block_diag(mk_ref, mv_ref, kbd_scr, vbd_scr)

    nt = GLA_HEADS * GLA_DVP
    mo = _mem_attn_tile(mq_ref[...], kbd_scr[...], vbd_scr[...]).astype(BF)
    mo = jnp.where(is_prompt, mo, mos_ref[...])
    x = (x_ref[...] + _dot(_pick(is_prompt, tokp_ref, toks_ref), wo_ref[:nt, :])
         + _dot(mo, wo_ref[nt:, :]))
    x = _ffn_half(x, g2_ref, wgu_ref, wd_ref)
    xo_ref[...] = x
    h = _rms(x, gkv_ref[...]).astype(BF)
    c, s1, s2 = c_ref[...], s1_ref[...], s2_ref[...]
    kw = SWA_KV_HEADS * HEAD_DIMP
    kv = _dot(h, wkv_ref[...])
    for hh in range(SWA_KV_HEADS):
        sl = slice(hh * HEAD_DIMP, (hh + 1) * HEAD_DIMP)
        k_ref[:, sl] = _rope(kv[:, sl], c, s1, s2)
    v_ref[...] = kv[:, kw:]


def _tab_spec(n_prompt, blocks_per_seq):
    return pl.BlockSpec((TOKEN_TILE, HEAD_DIMP),
                        lambda i: (jnp.where(i < n_prompt, i % blocks_per_seq, blocks_per_seq), 0))


def _mem_kv_spec(layer, n_prompt, tiles_per_seq):
    return pl.BlockSpec(
        (1, 1, MEM_Q, MEM_TOKENS),
        lambda i: (layer, jnp.minimum(i, n_prompt - 1) // tiles_per_seq, 0, 0))


def _mem_scratch():
    return pltpu.VMEM((MEM_Q, MEM_HEADS * MEM_TOKENS), BF)


def _layer_a_out(x, tokp, toks, mq, mk_t, mv_t, mos, wo, g2, wgu, wd, gkv, wkv, tabs, casts,
                 *, seq):
    m = x.shape[0]
    grid = (m // TOKEN_TILE,)
    n_prompt = tokp.shape[0] // TOKEN_TILE
    tiles_per_seq = seq // TOKEN_TILE
    kw = SWA_KV_HEADS * HEAD_DIMP
    nt = GLA_HEADS * GLA_DVP
    tab = _tab_spec(n_prompt, tiles_per_seq)
    mem_spec = _mem_kv_spec(0, n_prompt, tiles_per_seq)
    jobs = [_cast_job(cw, layer, rows, grid) for cw, layer, rows in casts]
    return pl.pallas_call(
        functools.partial(_layer_a_out_kernel, n_prompt=n_prompt, tiles_per_seq=tiles_per_seq),
        grid=grid,
        in_specs=[_rows(D_MODEL), _prompt_rows(nt, n_prompt), _sample_rows(nt),
                  _rows(MEM_Q), mem_spec, mem_spec, _sample_rows(MEM_Q), _resident(wo.shape),
                  _layer_block(g2.shape, 0), _layer_block(wgu.shape, 0), _layer_block(wd.shape, 0),
                  _resident(gkv.shape), _resident(wkv.shape), tab, tab, tab]
                 + [j[0] for j in jobs],
        out_specs=[_rows(D_MODEL), _rows(kw), _rows(kw)] + [j[1] for j in jobs],
        out_shape=[jax.ShapeDtypeStruct((m, D_MODEL), F32),
                   jax.ShapeDtypeStruct((m, kw), F32), jax.ShapeDtypeStruct((m, kw), F32)]
                  + [j[2] for j in jobs],
        scratch_shapes=[_mem_scratch(), _mem_scratch()],
        compiler_params=_params("arbitrary"),
        name="layer_a_out",
    )(x, tokp, toks, mq, mk_t, mv_t, mos, wo, g2, wgu, wd, gkv, wkv, *tabs,
      *[c[0] for c in casts])


def _layer_b_in_kernel(x_ref, g1_ref, wgu_ref, wd_ref, gm_ref, w_ref, c_ref, s1_ref, s2_ref,
                       *refs):
    n = (len(refs) - 3) // 2
    xo_ref, q_ref, mq_ref = refs[n:n + 3]
    _run_cast_jobs(refs[:n] + refs[n + 3:])
    x = _ffn_half(x_ref[...], g1_ref, wgu_ref, wd_ref)
    xo_ref[...] = x
    h = _rms(x, gm_ref[0]).astype(BF)
    c, s1, s2 = c_ref[...], s1_ref[...], s2_ref[...]
    nq = SWA_Q_HEADS * HEAD_DIM
    qm = _dot(h, w_ref[...])
    for pp in range(SWA_Q_PAIRS):
        q = qm[:, pp * HEAD_DIMP:(pp + 1) * HEAD_DIMP]
        q_ref[pp] = (_rope(q, c, s1, s2) * (HEAD_DIM ** -0.5)).astype(BF)
    mq_ref[...] = (qm[:, nq:] * (MEM_HEAD_DIM ** -0.5)).astype(BF)


def _layer_b_in(x, g1, wgu, wd, gm, w, tabs, casts, *, n_prompt, seq):
    m = x.shape[0]
    grid = (m // TOKEN_TILE,)
    tab = _tab_spec(n_prompt, seq // TOKEN_TILE)
    q_spec = pl.BlockSpec((SWA_Q_PAIRS, TOKEN_TILE, HEAD_DIMP), lambda i: (0, i, 0))
    jobs = [_cast_job(cw, layer, rows, grid) for cw, layer, rows in casts]
    return pl.pallas_call(
        _layer_b_in_kernel,
        grid=grid,
        in_specs=[_rows(D_MODEL), _layer_block(g1.shape, 1), _layer_block(wgu.shape, 0),
                  _layer_block(wd.shape, 0), _layer_block(gm.shape, 1), _resident(w.shape),
                  tab, tab, tab] + [j[0] for j in jobs],
        out_specs=[_rows(D_MODEL), q_spec, _rows(MEM_Q)] + [j[1] for j in jobs],
        out_shape=[jax.ShapeDtypeStruct((m, D_MODEL), F32),
                   jax.ShapeDtypeStruct((SWA_Q_PAIRS, m, HEAD_DIMP), BF),
                   jax.ShapeDtypeStruct((m, MEM_Q), BF)] + [j[2] for j in jobs],
        compiler_params=_params("arbitrary"),
        name="layer_b_in",
    )(x, g1, wgu, wd, gm, w, *tabs, *[c[0] for c in casts])


def _layer_b_out_kernel(sink_ref, x_ref, q_ref, kprev_ref, kcur_ref, vprev_ref, vcur_ref,
                        mq_ref, mk_ref, mv_ref, toks_ref, mos_ref, wo_ref, g2_ref, wgu_ref,
                        wd_ref, gf_ref, yp_ref, ys_ref, kbd_scr, vbd_scr,
                        *, n_prompt, tiles_per_seq):
    i = pl.program_id(0)
    is_prompt = i < n_prompt
    first_of_seq = i % tiles_per_seq == 0

    @pl.when(first_of_seq)
    def _():
        _build_mem_block_diag(mk_ref, mv_ref, kbd_scr, vbd_scr)

    nt = SWA_Q_HEADS * HEAD_DIM
    mo = _mem_attn_tile(mq_ref[...], kbd_scr[...], vbd_scr[...]).astype(BF)
    mo = jnp.where(is_prompt, mo, mos_ref[...])
    tok = _swa_tile(q_ref, kprev_ref, kcur_ref, vprev_ref, vcur_ref, sink_ref, first_of_seq)
    toks = jnp.concatenate([toks_ref[pp] for pp in range(SWA_Q_PAIRS)], axis=1)
    tok = jnp.where(is_prompt, tok, toks)
    x = x_ref[...] + _dot(tok, wo_ref[:nt, :]) + _dot(mo, wo_ref[nt:, :])
    y = _rms(_ffn_half(x, g2_ref, wgu_ref, wd_ref), gf_ref[...])

    @pl.when(is_prompt)
    def _():
        yp_ref[...] = y

    @pl.when(jnp.logical_not(is_prompt))
    def _():
        ys_ref[...] = y


def _layer_b_out(sinks, x, qs, k_sh, v_sh, mq, mk_t, mv_t, toks, mos, wo, g2, wgu, wd, gf, *, seq):
    m = x.shape[0]
    n_tiles = m // TOKEN_TILE
    n_prompt = n_tiles - 1
    tiles_per_seq = seq // TOKEN_TILE
    kw = SWA_KV_HEADS * HEAD_DIMP
    blocks_per_tile = TOKEN_TILE // WINDOW
    q_spec = pl.BlockSpec((SWA_Q_PAIRS, TOKEN_TILE, HEAD_DIMP), lambda i: (0, i, 0))
    prev_spec = pl.BlockSpec((WINDOW, kw), lambda i: (jnp.maximum(i * blocks_per_tile - 1, 0), 0))
    toks_spec = pl.BlockSpec((SWA_Q_PAIRS, TOKEN_TILE, HEAD_DIMP), lambda i: (0, 0, 0),
                             pipeline_mode=pl.Buffered(1))
    mem_spec = _mem_kv_spec(1, n_prompt, tiles_per_seq)
    return pl.pallas_call(
        functools.partial(_layer_b_out_kernel, n_prompt=n_prompt, tiles_per_seq=tiles_per_seq),
        grid=(n_tiles,),
        in_specs=[pl.BlockSpec(memory_space=pltpu.SMEM), _rows(D_MODEL), q_spec,
                  prev_spec, _rows(kw), prev_spec, _rows(kw), _rows(MEM_Q), mem_spec, mem_spec,
                  toks_spec, _sample_rows(MEM_Q), _resident(wo.shape), _layer_block(g2.shape, 1),
                  _layer_block(wgu.shape, 0), _layer_block(wd.shape, 0), _resident(gf.shape)],
        out_specs=[_prompt_rows(D_MODEL, n_prompt),
                   pl.BlockSpec((TOKEN_TILE, D_MODEL), lambda i: (0, 0))],
        out_shape=[jax.ShapeDtypeStruct((n_prompt * TOKEN_TILE, D_MODEL), F32),
                   jax.ShapeDtypeStruct((TOKEN_TILE, D_MODEL), F32)],
        scratch_shapes=[_mem_scratch(), _mem_scratch()],
        compiler_params=_params("arbitrary"),
        name="layer_b_out",
    )(sinks, x, qs, k_sh, k_sh, v_sh, v_sh, mq, mk_t, mv_t, toks, mos, wo, g2, wgu, wd, gf)


def _mem_kv_kernel(x_ref, g_ref, wt_ref, *refs):
    n = (len(refs) - 2) // 2
    k_ref, v_ref = refs[n:n + 2]
    _run_cast_jobs(refs[:n] + refs[n + 2:])
    x = x_ref[...]
    xn = x * lax.rsqrt(jnp.mean(x * x, axis=-1, keepdims=True) + EPS)
    for l in range(2):
        h = (xn * g_ref[l]).astype(BF)
        kvt = _dot_nt(wt_ref[l], h)
        k_ref[l, 0] = kvt[:MEM_Q, :]
        v_ref[l, 0] = kvt[MEM_Q:, :]


def _mem_kv(mem, g, wt, casts, *, batch):
    out_spec = pl.BlockSpec((2, 1, MEM_Q, MEM_TOKENS), lambda b: (0, b, 0, 0))
    jobs = [_cast_job(cw, layer, rows, (batch,)) for cw, layer, rows in casts]
    return pl.pallas_call(
        _mem_kv_kernel,
        grid=(batch,),
        in_specs=[pl.BlockSpec((MEM_TOKENS, D_MODEL), lambda b: (b, 0)), _resident(g.shape),
                  _resident(wt.shape)] + [j[0] for j in jobs],
        out_specs=[out_spec, out_spec] + [j[1] for j in jobs],
        out_shape=[jax.ShapeDtypeStruct((2, batch, MEM_Q, MEM_TOKENS), F32)] * 2
                  + [j[2] for j in jobs],
        compiler_params=_params("arbitrary"),
        name="mem_kv",
    )(mem, g, wt, *[c[0] for c in casts])


def _gla_prompt_kernel(*refs):
    ns = GLA_SEQS
    seq_in = [refs[5 * i:5 * i + 5] for i in range(ns)]
    rest = refs[5 * ns:-1]
    s_scr = refs[-1]
    n = (len(rest) - 2) // 2
    tok_ref, st_ref = rest[n:n + 2]
    _run_cast_jobs(rest[:n] + rest[n + 2:])
    c = pl.program_id(1)
    C = GLA_CHUNK

    @pl.when(c == 0)
    def _():
        s_scr[...] = jnp.zeros(s_scr.shape, F32)

    row = lax.broadcasted_iota(jnp.int32, (C, C), 0)
    col = lax.broadcasted_iota(jnp.int32, (C, C), 1)
    causal = row >= col
    ltri = jnp.where(causal, 1.0, 0.0).astype(BF)
    _gla_chunks([(seq_in[i], tok_ref.at[i], s_scr.at[i], h)
                 for i in range(ns) for h in range(GLA_HEADS)], causal, ltri)

    @pl.when(c == pl.num_programs(1) - 1)
    def _():
        st_ref[...] = s_scr[...]


def _gla_chunks(chains, causal, ltri):
    C = GLA_CHUNK

    def sk(h):
        return slice(h * GLA_DKP, (h + 1) * GLA_DKP)

    def sv(h):
        return slice(h * GLA_DVP, (h + 1) * GLA_DVP)

    bs = []
    for (q_ref, k_ref, la_ref, v_ref, r_ref), _, _, h in chains:
        la = la_ref[:, sk(h)]
        hi = la.astype(BF)
        lo = (la - hi.astype(F32)).astype(BF)
        bb = _dot(ltri, jnp.concatenate([hi, lo], axis=1))
        bs.append(bb[:, :GLA_DKP] + bb[:, GLA_DKP:])
    ops = []
    for ((q_ref, k_ref, la_ref, v_ref, r_ref), _, _, h), b in zip(chains, bs):
        b_mid = b[C // 2 - 1:C // 2, :]
        b_last = b[C - 1:C, :]
        qe = q_ref[:, sk(h)] * jnp.exp(b - b_mid)
        ke = k_ref[:, sk(h)] * jnp.exp(b_mid - b)
        qb = (qe * jnp.exp(b_mid)).astype(BF)
        kd = (ke * jnp.exp(b_last - b_mid)).astype(BF)
        ops.append((qe.astype(BF), ke.astype(BF), qb, kd, jnp.exp(b_last)))
    As = [jnp.where(causal, _dot_nt(qe, ke), 0.0).astype(BF) for qe, ke, _, _, _ in ops]
    outs = []
    for ((q_ref, k_ref, la_ref, v_ref, r_ref), _, s_ref, h), a, (_, _, qb, kd, decay) in zip(
            chains, As, ops):
        v = v_ref[:, sv(h)].astype(BF)
        st = s_ref[h]
        outs.append(_dot(a, v) + _dot_nt(qb, st.astype(BF)))
        s_ref[h] = st * decay + _dot_tn(v, kd)
    for ((q_ref, k_ref, la_ref, v_ref, r_ref), tok_ref, _, h), o in zip(chains, outs):
        ms = jnp.sum(o * o, axis=1, keepdims=True) * (1.0 / GLA_DV)
        tok_ref[:, sv(h)] = (o * lax.rsqrt(ms + EPS) * r_ref[:, sv(h)]).astype(BF)


def _gla_prompt(q, k, la, v, r, casts, *, batch, seq):
    ns = GLA_SEQS
    m = batch * seq
    nc = seq // GLA_CHUNK
    qk = GLA_HEADS * GLA_DKP
    vr = GLA_HEADS * GLA_DVP
    grid = (batch // ns, nc)

    def tok_map(i):
        return lambda b, c: ((b * ns + i) * nc + c, 0)

    seq_specs = []
    for i in range(ns):
        seq_specs += [pl.BlockSpec((GLA_CHUNK, qk), tok_map(i))] * 3
        seq_specs += [pl.BlockSpec((GLA_CHUNK, vr), tok_map(i))] * 2
    jobs = [_cast_job(cw, layer, rows, grid) for cw, layer, rows in casts]
    res = pl.pallas_call(
        _gla_prompt_kernel,
        grid=grid,
        in_specs=seq_specs + [j[0] for j in jobs],
        out_specs=[pl.BlockSpec((ns, GLA_CHUNK, vr), lambda b, c: (b, c, 0)),
                   pl.BlockSpec((ns, GLA_HEADS, GLA_DVP, GLA_DKP), lambda b, c: (b, 0, 0, 0))]
                  + [j[1] for j in jobs],
        out_shape=[jax.ShapeDtypeStruct((batch, seq, vr), BF),
                   jax.ShapeDtypeStruct((batch, GLA_HEADS, GLA_DVP, GLA_DKP), F32)]
                  + [j[2] for j in jobs],
        scratch_shapes=[pltpu.VMEM((ns, GLA_HEADS, GLA_DVP, GLA_DKP), F32)],
        compiler_params=_params("arbitrary", "arbitrary"),
        name="gla_prompt",
    )(*([q, k, la, v, r] * ns), *[c[0] for c in casts])
    return (res[0].reshape(m, vr),) + tuple(res[1:])


_SAMPLE_BB = 16
_DEC_SEQ = 4
_DEC_BATCH = 128
_GLA_DK_BLK = 48
_NEW_ROWS = 16


def _gla_sample_kernel(s_ref, q_ref, k_ref, la_ref, v_ref, r_ref, so_ref, tok_ref, o_scr):
    j = pl.program_id(1)

    @pl.when(j == 0)
    def _():
        o_scr[...] = jnp.zeros(o_scr.shape, F32)

    def body(dk, carry):
        s = s_ref[0, 0, dk]
        for t in range(_DEC_SEQ):
            a = jnp.exp(la_ref[t, 0, pl.ds(dk, 1), :])
            s = a * s + k_ref[t, 0, pl.ds(dk, 1), :] * v_ref[t, 0, :GLA_DV, :]
            o_scr[t] = o_scr[t] + q_ref[t, 0, pl.ds(dk, 1), :] * s
        so_ref[0, 0, dk] = s
        return carry

    lax.fori_loop(0, _GLA_DK_BLK, body, 0, unroll=4)

    @pl.when(j == pl.num_programs(1) - 1)
    def _():
        tok_ref[...] = jnp.zeros(tok_ref.shape, F32)
        for t in range(_DEC_SEQ):
            o = o_scr[t]
            ms = jnp.sum(o * o, axis=0, keepdims=True) * (1.0 / GLA_DV)
            tok_ref[t, 0, :GLA_DV, :] = o * lax.rsqrt(ms + EPS) * r_ref[t, 0, :GLA_DV, :]


def _gla_sample(state, q, k, la, v, r):
    qk_spec = pl.BlockSpec((_DEC_SEQ, 1, _GLA_DK_BLK, _DEC_BATCH), lambda h, j: (0, h, j, 0))
    vr_spec = pl.BlockSpec((_DEC_SEQ, 1, GLA_DVP, _DEC_BATCH), lambda h, j: (0, h, 0, 0))
    s_spec = pl.BlockSpec((1, 1, _GLA_DK_BLK, GLA_DV, _DEC_BATCH), lambda h, j: (0, h, j, 0, 0))
    return pl.pallas_call(
        _gla_sample_kernel,
        grid=(GLA_HEADS, GLA_DK // _GLA_DK_BLK),
        in_specs=[s_spec, qk_spec, qk_spec, qk_spec, vr_spec, vr_spec],
        out_specs=[s_spec, vr_spec],
        out_shape=[jax.ShapeDtypeStruct(state.shape, F32),
                   jax.ShapeDtypeStruct(v.shape, F32)],
        scratch_shapes=[pltpu.VMEM((_DEC_SEQ, GLA_DV, _DEC_BATCH), F32)],
        compiler_params=_params("parallel", "arbitrary"),
        name="gla_sample",
    )(state, q, k, la, v, r)


def _mem_attn_sample_kernel(q_ref, mk_ref, mv_ref, o_ref):
    scores = [_dot(q_ref[bi], mk_ref[0, bi].astype(BF)) for bi in range(_SAMPLE_BB)]
    probs = []
    for s in scores:
        e = jnp.exp(s - jnp.max(s, axis=1, keepdims=True))
        probs.append((e * (1.0 / jnp.sum(e, axis=1, keepdims=True))).astype(BF))
    for bi, p in enumerate(probs):
        o_ref[bi] = _dot_nt(p, mv_ref[0, bi].astype(BF))


def _mem_attn_sample(qbd, mk_t, mv_t, layer):
    nb = qbd.shape[0]
    nr = MEM_HEADS * _DEC_SEQ
    kv_spec = pl.BlockSpec((1, _SAMPLE_BB, MEM_Q, MEM_TOKENS), lambda i: (layer, i, 0, 0))
    q_spec = pl.BlockSpec((_SAMPLE_BB, nr, MEM_Q), lambda i: (i, 0, 0))
    return pl.pallas_call(
        _mem_attn_sample_kernel,
        grid=(nb // _SAMPLE_BB,),
        in_specs=[q_spec, kv_spec, kv_spec],
        out_specs=q_spec,
        out_shape=jax.ShapeDtypeStruct((nb, nr, MEM_Q), F32),
        compiler_params=_params("parallel"),
        name="mem_attn_sample",
    )(qbd, mk_t, mv_t)


def _swa_sample_kernel(q_ref, sink_ref, kc_ref, vc_ref, kn_ref, vn_ref, o_ref, ko_ref, vo_ref):
    nq = SWA_Q_HEADS * _DEC_SEQ
    t = lax.broadcasted_iota(jnp.int32, (nq, WINDOW), 0) % _DEC_SEQ
    pos = lax.broadcasted_iota(jnp.int32, (nq, WINDOW), 1)
    bias_c = jnp.where(pos > t, 0.0, -jnp.inf)
    tn = lax.broadcasted_iota(jnp.int32, (nq, _NEW_ROWS), 0) % _DEC_SEQ
    new = lax.broadcasted_iota(jnp.int32, (nq, _NEW_ROWS), 1)
    bias_n = jnp.where(new <= tn, 0.0, -jnp.inf)
    sink = sink_ref[...]
    seqs = range(_SAMPLE_BB)
    scores = [(_dot(q_ref[bi], kc_ref[bi].astype(BF)) + bias_c,
               _dot_nt(q_ref[bi], kn_ref[bi].astype(BF)) + bias_n) for bi in seqs]
    probs = []
    for sc, sn in scores:
        m = jnp.maximum(jnp.maximum(jnp.max(sc, axis=1, keepdims=True),
                                    jnp.max(sn, axis=1, keepdims=True)), sink)
        ec = jnp.exp(sc - m)
        en = jnp.exp(sn - m)
        l = (jnp.sum(ec, axis=1, keepdims=True) + jnp.sum(en, axis=1, keepdims=True)
             + jnp.exp(sink - m))
        inv = 1.0 / l
        probs.append(((ec * inv).astype(BF), (en * inv).astype(BF)))
    for bi, (pc, pn) in zip(seqs, probs):
        o_ref[bi] = (_dot_nt(pc, vc_ref[bi].astype(BF)) + _dot(pn, vn_ref[bi].astype(BF)))
    new0 = WINDOW - _DEC_SEQ
    row = lax.broadcasted_iota(jnp.int32, (_NEW_ROWS, WINDOW), 0)
    col = lax.broadcasted_iota(jnp.int32, (_NEW_ROWS, WINDOW), 1)
    place = jnp.where((col == new0 + row) & (row < _DEC_SEQ), 1.0, 0.0)
    is_new = lax.broadcasted_iota(jnp.int32, (SWA_KV_HEADS * HEAD_DIM, WINDOW), 1) >= new0

    def placed(new_rows):
        return lax.dot_general(new_rows, place, (((0,), (0,)), ((), ())),
                               precision=lax.Precision.HIGHEST, preferred_element_type=F32)

    for bi in seqs:
        ko_ref[bi] = jnp.where(is_new, placed(kn_ref[bi]), pltpu.roll(kc_ref[bi], new0, 1))
        vo_ref[bi] = jnp.where(is_new, placed(vn_ref[bi]), pltpu.roll(vc_ref[bi], new0, 1))


def _swa_sample(qbd, sink_col, kc, vc, kn, vn):
    nb = qbd.shape[0]
    kw = SWA_KV_HEADS * HEAD_DIM
    nq = SWA_Q_HEADS * _DEC_SEQ
    kv_spec = pl.BlockSpec((_SAMPLE_BB, kw, WINDOW), lambda i: (i, 0, 0))
    new_spec = pl.BlockSpec((_SAMPLE_BB, _NEW_ROWS, kw), lambda i: (i, 0, 0))
    q_spec = pl.BlockSpec((_SAMPLE_BB, nq, kw), lambda i: (i, 0, 0))
    return pl.pallas_call(
        _swa_sample_kernel,
        grid=(nb // _SAMPLE_BB,),
        in_specs=[q_spec, _resident((nq, 1)), kv_spec, kv_spec, new_spec, new_spec],
        out_specs=[q_spec, kv_spec, kv_spec],
        out_shape=[jax.ShapeDtypeStruct((nb, nq, kw), F32),
                   jax.ShapeDtypeStruct(kc.shape, F32), jax.ShapeDtypeStruct(kc.shape, F32)],
        compiler_params=_params("parallel"),
        name="swa_sample",
    )(qbd, sink_col, kc, vc, kn, vn)


def _pad_heads(w, heads, dim, dim_p, axis):
    shape = w.shape
    w = w.reshape(shape[:axis] + (heads, dim) + shape[axis + 1:])
    pad = [(0, 0)] * w.ndim
    pad[axis + 1] = (0, dim_p - dim)
    w = jnp.pad(w, pad)
    return w.reshape(shape[:axis] + (heads * dim_p,) + shape[axis + 1:])


def _rope_tables(pos):
    half = ROT_DIM // 2
    inv_freq = np.exp(-math.log(ROPE_THETA) * np.arange(0, ROT_DIM, 2, dtype=np.float64) / ROT_DIM)
    ang = pos.astype(np.float64)[:, None] * inv_freq[None, :]
    cos, sin = np.cos(ang), np.sin(ang)
    n = pos.shape[0]
    rest = HEAD_DIM - ROT_DIM
    c = np.concatenate([cos, cos, np.ones((n, rest))], axis=1)
    s1 = np.concatenate([-sin, np.zeros((n, HEAD_DIM - half))], axis=1)
    s2 = np.concatenate([np.zeros((n, half)), sin, np.zeros((n, rest))], axis=1)
    return tuple(np.tile(a, (1, HEAD_DIMP // HEAD_DIM)).astype(np.float32) for a in (c, s1, s2))


def _prep_weights(p):
    w = {}
    for name in ("ffn1_norm", "ffn2_norm", "mix_norm"):
        w[name] = p[name][:, None, :]
    qk = GLA_HEADS * GLA_DK
    vv = GLA_HEADS * GLA_DV
    a_in = p["a_w_in"][0].astype(BF)
    w_qk = _pad_heads(a_in[:, :2 * qk], 2 * GLA_HEADS, GLA_DK, GLA_DKP, 1)
    w_vr = _pad_heads(a_in[:, 2 * qk:2 * (qk + vv)], 2 * GLA_HEADS, GLA_DV, GLA_DVP, 1)
    o = 2 * (qk + vv)
    w_g = jnp.pad(a_in[:, o:o + GLA_RANK], ((0, 0), (0, GLA_RANKP - GLA_RANK)))
    w["a_in"] = jnp.concatenate([w_qk, w_vr, w_g, a_in[:, o + GLA_RANK:]], axis=1)
    gate = _pad_heads(p["a_w_gate"][0], GLA_HEADS, GLA_DK, GLA_DKP, 1)
    w["a_gate"] = jnp.pad(gate, ((0, GLA_RANKP - GLA_RANK), (0, 0))).astype(BF)
    w["a_bgate"] = _pad_heads(p["a_b_gate"][0][None, :], GLA_HEADS, GLA_DK, GLA_DKP, 1)
    w["a_gn"] = jnp.tile(jnp.pad(p["a_out_norm"][0], (0, GLA_DVP - GLA_DV)), GLA_HEADS)[None, :]
    a_out = p["a_w_out"][0]
    w["a_out"] = jnp.concatenate(
        [_pad_heads(a_out[:vv], GLA_HEADS, GLA_DV, GLA_DVP, 0), a_out[vv:]], axis=0).astype(BF)
    w["b_in"] = p["b_w_in"][0].astype(BF)
    w["b_out"] = p["b_w_out"][0].astype(BF)
    w["kv"] = _pad_heads(p["w_kv"].astype(BF), 2 * SWA_KV_HEADS, HEAD_DIM, HEAD_DIMP, 1)
    w["mem_t"] = p["mem_w_kv"].transpose(0, 2, 1).astype(BF)
    return w


def _compact_kv(a, batch, seq):
    return a.reshape(batch, seq, SWA_KV_HEADS, HEAD_DIMP)[..., :HEAD_DIM]


def kernel(x_prompt, x_sample, state_gla, cache_swa_k, cache_swa_v, cache_mem_k, cache_mem_v,
           mem_prompt, ffn1_norm, ffn1_w_gu, ffn1_w_down, mix_norm, ffn2_norm, ffn2_w_gu,
           ffn2_w_down, mem_norm, mem_w_kv, a_w_in, a_w_gate, a_b_gate, a_out_norm, a_w_out,
           kv_norm, w_kv, b_w_in, b_sinks, b_w_out, final_norm):
    p = dict(ffn1_norm=ffn1_norm, ffn1_w_gu=ffn1_w_gu, ffn1_w_down=ffn1_w_down,
             mix_norm=mix_norm, ffn2_norm=ffn2_norm, ffn2_w_gu=ffn2_w_gu,
             ffn2_w_down=ffn2_w_down, mem_w_kv=mem_w_kv, a_w_in=a_w_in, a_w_gate=a_w_gate,
             a_b_gate=a_b_gate, a_out_norm=a_out_norm, a_w_out=a_w_out, w_kv=w_kv,
             b_w_in=b_w_in, b_w_out=b_w_out)
    w = _prep_weights(p)
    batch, seq, _ = x_prompt.shape
    nb, t, _ = x_sample.shape
    assert nb == _DEC_BATCH and t == _DEC_SEQ and nb * t == TOKEN_TILE
    assert seq % TOKEN_TILE == 0 and seq % GLA_CHUNK == 0
    mp = batch * seq
    kw = SWA_KV_HEADS * HEAD_DIM
    sinks = b_sinks[0]

    mem_k_t, mem_v_t, wgu_a1, wd_a1 = _mem_kv(
        mem_prompt.reshape(batch * MEM_TOKENS, D_MODEL), mem_norm[:, None, :], w["mem_t"],
        [(ffn1_w_gu, 0, D_MODEL // batch), (ffn1_w_down, 0, FFN_DIM // batch)], batch=batch)
    state_t = state_gla.transpose(0, 2, 3, 4, 1)
    kc_t = cache_swa_k.transpose(0, 2, 3, 1).reshape(nb, kw, WINDOW)
    vc_t = cache_swa_v.transpose(0, 2, 3, 1).reshape(nb, kw, WINDOW)
    cmk_t = cache_mem_k.transpose(0, 1, 3, 4, 2).reshape(2, nb, MEM_Q, MEM_TOKENS)
    cmv_t = cache_mem_v.transpose(0, 1, 3, 4, 2).reshape(2, nb, MEM_Q, MEM_TOKENS)

    tabs = tuple(
        jnp.asarray(np.concatenate([a, np.tile(b, (nb, 1))], axis=0))
        for a, b in zip(_rope_tables(np.arange(seq)), _rope_tables(PAST_LEN + np.arange(t))))

    mem_mask = (np.arange(MEM_Q) // MEM_HEAD_DIM)[None, :] == np.arange(MEM_HEADS)[:, None]
    kv_mask = (np.arange(kw) // HEAD_DIM)[None, :] == np.arange(SWA_KV_HEADS)[:, None]

    def mem_attn_sample(mq, layer):
        q4 = mq[mp:].reshape(nb, 1, t, MEM_Q)
        qbd = jnp.where(mem_mask[None, :, None, :], q4, 0).reshape(nb, MEM_HEADS * t, MEM_Q)
        o = _mem_attn_sample(qbd, cmk_t, cmv_t, layer).reshape(nb, MEM_HEADS, t, MEM_Q)
        o = jnp.sum(jnp.where(mem_mask[None, :, None, :], o, 0.0), axis=1)
        return o.reshape(nb * t, MEM_Q).astype(BF)

    def lanes(a, width):
        return a[mp:].reshape(nb, t, GLA_HEADS, width).transpose(1, 2, 3, 0)

    x, q, k, la, v, r, mq, wgu_a2, wd_a2 = _layer_a_in(
        x_prompt.reshape(mp, D_MODEL), x_sample.reshape(nb * t, D_MODEL), w["ffn1_norm"],
        wgu_a1, wd_a1, w["mix_norm"], w["a_in"], w["a_gate"], w["a_bgate"], w["a_gn"],
        [(ffn2_w_gu, 0, 32), (ffn2_w_down, 0, 128)])
    tok_p, st_p = _gla_prompt(q, k, la, v, r, [], batch=batch, seq=seq)
    st_s, tok_s = _gla_sample(state_t, lanes(q, GLA_DKP), lanes(k, GLA_DKP), lanes(la, GLA_DKP),
                              lanes(v, GLA_DVP), lanes(r, GLA_DVP))
    tok_s = tok_s.transpose(3, 0, 1, 2).reshape(nb * t, GLA_HEADS * GLA_DVP).astype(BF)
    mo_s = mem_attn_sample(mq, 0)
    x, k_sh, v_sh, wgu_b1, wd_b1 = _layer_a_out(
        x, tok_p, tok_s, mq, mem_k_t, mem_v_t, mo_s, w["a_out"], w["ffn2_norm"], wgu_a2, wd_a2,
        kv_norm[None, :], w["kv"], tabs, [(ffn1_w_gu, 1, 32), (ffn1_w_down, 1, 128)], seq=seq)

    x, qs, mq, wgu_b2, wd_b2 = _layer_b_in(
        x, w["ffn1_norm"], wgu_b1, wd_b1, w["mix_norm"], w["b_in"], tabs,
        [(ffn2_w_gu, 1, 32), (ffn2_w_down, 1, 128)], n_prompt=mp // TOKEN_TILE, seq=seq)

    k_new = _compact_kv(k_sh[mp:], nb, t)
    v_new = _compact_kv(v_sh[mp:], nb, t)

    def new_rows(a):
        return jnp.pad(a.reshape(nb, t, kw), ((0, 0), (0, _NEW_ROWS - t), (0, 0)))

    q5 = qs[:, mp:].reshape(SWA_Q_PAIRS, nb, t, 2, HEAD_DIM).transpose(1, 0, 3, 2, 4)
    q5 = q5.reshape(nb, SWA_KV_HEADS, SWA_GROUP, t, HEAD_DIM)
    qbd = jnp.where(kv_mask[None, :, None, None, :], jnp.tile(q5, (1, 1, 1, 1, SWA_KV_HEADS)), 0)
    qbd = qbd.reshape(nb, SWA_Q_HEADS * t, kw)
    o, k_s, v_s = _swa_sample(qbd, jnp.repeat(sinks, t)[:, None], kc_t, vc_t,
                              new_rows(k_new), new_rows(v_new))
    o = o.reshape(nb, SWA_KV_HEADS, SWA_GROUP, t, kw)
    o = jnp.where(kv_mask[None, :, None, None, :], o, 0.0)
    o = o.reshape(nb, SWA_KV_HEADS, SWA_GROUP, t, SWA_KV_HEADS, HEAD_DIM).sum(axis=4)
    tok_s = o.reshape(nb, SWA_Q_PAIRS, 2, t, HEAD_DIM).transpose(1, 0, 3, 2, 4)
    tok_s = tok_s.reshape(SWA_Q_PAIRS, nb * t, HEAD_DIMP).astype(BF)
    mo_s = mem_attn_sample(mq, 1)
    y_p, y_s = _layer_b_out(sinks, x, qs, k_sh, v_sh, mq, mem_k_t, mem_v_t, tok_s, mo_s,
                            w["b_out"], w["ffn2_norm"], wgu_b2, wd_b2, final_norm[None, :],
                            seq=seq)

    gla_prompt = st_p.transpose(0, 1, 3, 2)[None, :, :, :GLA_DK, :GLA_DV]
    gla_sample = st_s.transpose(0, 4, 1, 2, 3)
    def last_window(a):
        tiles_per_seq = seq // TOKEN_TILE
        a = a.reshape(-1, TOKEN_TILE, SWA_KV_HEADS * HEAD_DIMP)
        a = a[tiles_per_seq - 1:batch * tiles_per_seq:tiles_per_seq, TOKEN_TILE - WINDOW:]
        return a.reshape(batch, WINDOW, SWA_KV_HEADS, HEAD_DIMP)[..., :HEAD_DIM]

    swa_k_prompt = last_window(k_sh)
    swa_v_prompt = last_window(v_sh)
    swa_k_sample = k_s.reshape(nb, SWA_KV_HEADS, HEAD_DIM, WINDOW).transpose(0, 3, 1, 2)
    swa_v_sample = v_s.reshape(nb, SWA_KV_HEADS, HEAD_DIM, WINDOW).transpose(0, 3, 1, 2)
    mem_shape = (2, batch, MEM_HEADS, MEM_HEAD_DIM, MEM_TOKENS)
    mem_k_prompt = mem_k_t.reshape(mem_shape).transpose(0, 1, 4, 2, 3)
    mem_v_prompt = mem_v_t.reshape(mem_shape).transpose(0, 1, 4, 2, 3)
    return (y_p.reshape(batch, seq, D_MODEL), y_s.reshape(nb, t, D_MODEL), gla_prompt,
            gla_sample, swa_k_prompt, swa_v_prompt, swa_k_sample, swa_v_sample,
            mem_k_prompt, mem_v_prompt)
```
